```python
import jax
import jax.numpy as jnp
from jax import lax
import numpy as np

D_MODEL = 4096
BATCH = 4
SEQ = 4096
DEPTH = 1

GRID_W = 64
CTX_LEN = 256
EPS = 1e-6
N_MOD = 6
A_HEADS = 16
A_DK = 128
A_DV = 128
A_K = A_HEADS * A_DK
A_WIDTH = A_HEADS * A_DV
B_HEADS = 8
B_DQK = 128
B_DV = 256
B_QK = B_HEADS * B_DQK
B_WIDTH = B_HEADS * B_DV
CONV_W = 3
CHUNK = 64
N_EXPERTS = 16
CAPACITY = 2
D_EXPERT = 2048
IN_LAYOUT = (("a_q", A_K), ("a_f_fwd", A_K), ("a_f_bwd", A_K), ("a_i", A_WIDTH), ("a_g", A_WIDTH),
             ("b_q", B_QK), ("b_k", B_QK), ("b_v", B_WIDTH), ("b_o", B_WIDTH), ("b_gates", 4 * B_HEADS),
             ("gate_a", D_MODEL), ("gate_b", D_MODEL))
N_IN = sum(size for _, size in IN_LAYOUT)

kernel_name = "hybrid_hgrn2_mlstm_ec_moe_diffusion"

F32 = jnp.float32


def _rmsnorm(x, w):
    xf = x.astype(F32)
    y = xf * lax.rsqrt(jnp.mean(jnp.square(xf), axis=-1, keepdims=True) + EPS)
    return (y * w.astype(F32)).astype(x.dtype)


def _modulate(u, shift, scale):
    return u * (1 + scale[:, None, :]) + shift[:, None, :]


def _split_in(p):
    pieces = {}
    off = 0
    for name, size in IN_LAYOUT:
        pieces[name] = p[..., off:off + size]
        off += size
    return pieces


def _row_conv(u, w, b, n_rows, row_len):
    Bn, T, C = u.shape
    half = CONV_W // 2
    up = jnp.pad(u.reshape(Bn, n_rows, row_len, C), ((0, 0), (0, 0), (half, half), (0, 0)))
    y = b.astype(u.dtype)
    for j in range(CONV_W):
        y = y + up[:, :, j:j + row_len, :] * w[j]
    return y.reshape(Bn, T, C)


def _to_chunks(a):
    Bn, T, H, d = a.shape
    return a.reshape(Bn, T // CHUNK, CHUNK, H, d).transpose(1, 0, 3, 2, 4)


def _from_chunks(a):
    n, Bn, H, L, d = a.shape
    return a.transpose(1, 0, 3, 2, 4).reshape(Bn, n * L, H, d)


def _hgrn2_scan(q, k, logf, v, s0):
    causal = jnp.tril(jnp.ones((CHUNK, CHUNK), bool))[:, :, None]

    def step(S, inp):
        qi, ki, fi, vi = inp
        b = jnp.cumsum(fi, axis=2)
        diff = b[:, :, :, None, :] - b[:, :, None, :, :]
        decay = jnp.where(causal, jnp.exp(jnp.minimum(diff, 0.0)), 0.0)
        attn = jnp.einsum('bhtd,bhsd,bhtsd->bhts', qi, ki, decay)
        o = jnp.einsum('bhts,bhsv->bhtv', attn, vi) + jnp.einsum('bhtd,bhdv->bhtv', qi * jnp.exp(b), S)
        b_last = b[:, :, -1:, :]
        S_new = jnp.exp(b_last[:, :, 0, :, None]) * S + jnp.einsum('bhsd,bhsv->bhdv', ki * jnp.exp(b_last - b), vi)
        return S_new, o

    S, o = lax.scan(step, s0, (_to_chunks(q), _to_chunks(k), _to_chunks(logf), _to_chunks(v)))
    return _from_chunks(o), S


def _mlstm_scan(q, k, v, ig, lf, state):
    causal = jnp.tril(jnp.ones((CHUNK, CHUNK), bool))
    chunk3 = lambda a: _to_chunks(a[..., None])[..., 0]

    def step(carry, inp):
        C, n, m = carry
        qi, ki, vi, ii, fi = inp
        b = jnp.cumsum(fi, axis=-1)
        logw = jnp.where(causal, b[..., :, None] - b[..., None, :] + ii[..., None, :], -jnp.inf)
        log_inter = b + m[..., None]
        m_t = jnp.maximum(jnp.max(logw, axis=-1), log_inter)
        w = jnp.exp(logw - m_t[..., None])
        a = jnp.exp(log_inter - m_t)
        qk = jnp.einsum('bhtd,bhsd->bhts', qi, ki) * w
        num = jnp.einsum('bhts,bhsv->bhtv', qk, vi) + a[..., None] * jnp.einsum('bhtd,bhdv->bhtv', qi, C)
        den = jnp.sum(qk, axis=-1) + a * jnp.einsum('bhtd,bhd->bht', qi, n)
        h = num / jnp.maximum(jnp.abs(den), jnp.exp(-m_t))[..., None]
        b_last = b[..., -1]
        logs = b_last[..., None] - b + ii
        m_new = jnp.maximum(b_last + m, jnp.max(logs, axis=-1))
        ws = jnp.exp(logs - m_new[..., None])
        dec = jnp.exp(b_last + m - m_new)
        C_new = dec[..., None, None] * C + jnp.einsum('bhs,bhsd,bhsv->bhdv', ws, ki, vi)
        n_new = dec[..., None] * n + jnp.einsum('bhs,bhsd->bhd', ws, ki)
        return (C_new, n_new, m_new), h

    st, h = lax.scan(step, state, (_to_chunks(q), _to_chunks(k), _to_chunks(v), chunk3(ig), chunk3(lf)))
    return _from_chunks(h), st


def _run_direction(scan_fn, ctx_in, lat_in, state0, reverse):
    if reverse:
        ctx_in = tuple(jnp.flip(a, 1) for a in ctx_in)
        lat_in = tuple(jnp.flip(a, 1) for a in lat_in)
    o_ctx, st = scan_fn(*ctx_in, state0)
    o_lat, _ = scan_fn(*lat_in, st)
    if reverse:
        o_ctx, o_lat = jnp.flip(o_ctx, 1), jnp.flip(o_lat, 1)
    return o_ctx, o_lat


def _hgrn2_feats(pp, f_bias, lb):
    Bn, T, _ = pp['a_q'].shape
    hd = lambda a, d: a.astype(F32).reshape(Bn, T, A_HEADS, d)
    q = jax.nn.silu(hd(pp['a_q'], A_DK))
    v = hd(pp['a_i'], A_DV)
    dirs = []
    for d, name in enumerate(('a_f_fwd', 'a_f_bwd')):
        z = pp[name].astype(F32) + f_bias[d]
        lbd = lb[d]
        logf = jnp.log(lbd + (1 - lbd) * jax.nn.sigmoid(z))
        k = (1 - lbd) * jax.nn.sigmoid(-z)
        dirs.append((hd(k, A_DK), hd(logf, A_DK)))
    return q, v, dirs


def _hgrn2_mixer(pl, pc, f_bias, lb, norm_w, need_ctx):
    ql, vl, dl = _hgrn2_feats(pl, f_bias, lb)
    qc, vc, dc = _hgrn2_feats(pc, f_bias, lb)
    s0 = jnp.zeros((qc.shape[0], A_HEADS, A_DK, A_DV), F32)
    outs_c, outs_l = [], []
    for d in range(2):
        (kc, fc), (kl, fl) = dc[d], dl[d]
        oc, ol = _run_direction(_hgrn2_scan, (qc, kc, fc, vc), (ql, kl, fl, vl), s0, reverse=(d == 1))
        outs_c.append(oc)
        outs_l.append(ol)

    def readout(o, pp):
        Bn, T = o.shape[:2]
        o = _rmsnorm(o, norm_w.reshape(A_HEADS, A_DV)).reshape(Bn, T, A_WIDTH)
        return o * jax.nn.silu(pp['a_g'].astype(F32))

    y_lat = readout(outs_l[0] + outs_l[1], pl)
    y_ctx = readout(outs_c[0] + outs_c[1], pc) if need_ctx else None
    return y_lat, y_ctx


def _mlstm_feats(pp, conv_w, conv_b, gate_b, n_rows, row_len):
    Bn, T, _ = pp['b_q'].shape
    qk = jnp.concatenate([pp['b_q'], pp['b_k']], axis=-1).astype(F32)
    qk = jax.nn.silu(_row_conv(qk, conv_w, conv_b, n_rows, row_len))
    q = qk[..., :B_QK].reshape(Bn, T, B_HEADS, B_DQK)
    k = qk[..., B_QK:].reshape(Bn, T, B_HEADS, B_DQK) * (B_DQK ** -0.5)
    v = pp['b_v'].astype(F32).reshape(Bn, T, B_HEADS, B_DV)
    g = pp['b_gates'].astype(F32).reshape(Bn, T, 4, B_HEADS) + gate_b
    dirs = [(g[:, :, 0], jax.nn.log_sigmoid(g[:, :, 1])), (g[:, :, 2], jax.nn.log_sigmoid(g[:, :, 3]))]
    return q, k, v, dirs


def _mlstm_mixer(pl, pc, conv_w, conv_b, gate_b, norm_w, n_rows, need_ctx):
    ql, kl, vl, dl = _mlstm_feats(pl, conv_w, conv_b, gate_b, n_rows, GRID_W)
    qc, kc, vc, dc = _mlstm_feats(pc, conv_w, conv_b, gate_b, 1, pc['b_q'].shape[1])
    Bn = qc.shape[0]
    st0 = (jnp.zeros((Bn, B_HEADS, B_DQK, B_DV), F32), jnp.zeros((Bn, B_HEADS, B_DQK), F32),
           jnp.zeros((Bn, B_HEADS), F32))
    outs_c, outs_l = [], []
    for d in range(2):
        (ic, fc), (il, fl) = dc[d], dl[d]
        oc, ol = _run_direction(_mlstm_scan, (qc, kc, vc, ic, fc), (ql, kl, vl, il, fl), st0, reverse=(d == 1))
        outs_c.append(oc)
        outs_l.append(ol)

    def readout(h, pp):
        Bn_, T = h.shape[:2]
        h = _rmsnorm(h, norm_w.reshape(B_HEADS, B_DV)).reshape(Bn_, T, B_WIDTH)
        return h * jax.nn.sigmoid(pp['b_o'].astype(F32))

    y_lat = readout(outs_l[0] + outs_l[1], pl)
    y_ctx = readout(outs_c[0] + outs_c[1], pc) if need_ctx else None
    return y_lat, y_ctx


def _merge(pp, ya, yb, wa, wb, wo):
    ga = jax.nn.sigmoid(pp['gate_a'].astype(F32))
    gb = jax.nn.sigmoid(pp['gate_b'].astype(F32))
    return (ga * (ya @ wa) + gb * (yb @ wb)) @ wo


def _ec_moe(u, w_router, wg, wu, wd):
    Bn, N, D = u.shape
    cap = CAPACITY * N // N_EXPERTS
    aff = jax.nn.softmax((u @ w_router).astype(F32), axis=-1)
    top_w, top_i = lax.top_k(jnp.swapaxes(aff, 1, 2), cap)
    bidx = jnp.arange(Bn)[:, None]
    out = jnp.zeros_like(u)
    for e in range(N_EXPERTS):
        idx = top_i[:, e]
        xe = u[bidx, idx]
        he = jax.nn.silu(xe @ wg[e]) * (xe @ wu[e])
        ye = (he @ wd[e]) * top_w[:, e, :, None].astype(u.dtype)
        out = out.at[bidx, idx].add(ye.astype(u.dtype))
    return out


def setup_inputs(seed: int = 0) -> dict:
    key = jax.random.key(seed)
    ks = jax.random.split(key, 24)
    nrm = lambda k, shape, s: jax.random.normal(k, shape, F32) * s
    D = D_MODEL
    gate_base = jnp.stack([jnp.zeros((B_HEADS,), F32), jnp.linspace(3.0, 6.0, B_HEADS, dtype=F32),
                           jnp.zeros((B_HEADS,), F32), jnp.linspace(3.0, 6.0, B_HEADS, dtype=F32)])
    return {
        "x": nrm(ks[0], (BATCH, SEQ, D), 1.0),
        "c": nrm(ks[1], (BATCH, D), 1.0),
        "ctx": nrm(ks[2], (BATCH, CTX_LEN, D), 1.0),
        "c_ctx": nrm(ks[3], (D,), 1.0),
        "w_ada": nrm(ks[4], (DEPTH, D, N_MOD * D), 0.5 * D ** -0.5),
        "b_ada": nrm(ks[5], (DEPTH, N_MOD * D), 0.02),
        "g_norm": 1.0 + nrm(ks[6], (DEPTH, 4, D), 0.02),
        "w_in": nrm(ks[7], (DEPTH, D, N_IN), D ** -0.5),
        "hgrn_f_bias": 2.0 + nrm(ks[8], (DEPTH, 2, A_K), 0.5),
        "hgrn_lb": nrm(ks[9], (2, DEPTH + 1, A_K), 0.1).at[:, DEPTH].add(2.0),
        "hgrn_norm": 1.0 + nrm(ks[10], (DEPTH, A_WIDTH), 0.02),
        "mlstm_conv_w": nrm(ks[11], (DEPTH, CONV_W, 2 * B_QK), CONV_W ** -0.5),
        "mlstm_conv_b": nrm(ks[12], (DEPTH, 2 * B_QK), 0.02),
        "mlstm_gate_b": gate_base[None] + nrm(ks[13], (DEPTH, 4, B_HEADS), 0.1),
        "mlstm_norm": 1.0 + nrm(ks[14], (DEPTH, B_WIDTH), 0.02),
        "w_branch_a": nrm(ks[15], (DEPTH, A_WIDTH, D), A_WIDTH ** -0.5),
        "w_branch_b": nrm(ks[16], (DEPTH, B_WIDTH, D), B_WIDTH ** -0.5),
        "w_out": nrm(ks[17], (DEPTH, D, D), D ** -0.5),
        "w_router": nrm(ks[18], (DEPTH, D, N_EXPERTS), D ** -0.5),
        "w_expert_gate": nrm(ks[19], (DEPTH, N_EXPERTS, D, D_EXPERT), D ** -0.5),
        "w_expert_up": nrm(ks[20], (DEPTH, N_EXPERTS, D, D_EXPERT), D ** -0.5),
        "w_expert_down": nrm(ks[21], (DEPTH, N_EXPERTS, D_EXPERT, D), D_EXPERT ** -0.5),
    }


def reference(x, c, ctx, c_ctx, w_ada, b_ada, g_norm, w_in, hgrn_f_bias, hgrn_lb, hgrn_norm,
              mlstm_conv_w, mlstm_conv_b, mlstm_gate_b, mlstm_norm, w_branch_a, w_branch_b, w_out,
              w_router, w_expert_gate, w_expert_up, w_expert_down):
    Bn, T, D = x.shape
    rows = T // GRID_W
    lb_all = jnp.cumsum(jax.nn.softmax(hgrn_lb.astype(F32), axis=1), axis=1)
    h_lat, h_ctx = x, ctx
    for l in range(DEPTH):
        need_ctx = l < DEPTH - 1
        mod_lat = (jax.nn.silu(c) @ w_ada[l] + b_ada[l]).reshape(Bn, N_MOD, D)
        mod_ctx = (jax.nn.silu(c_ctx)[None] @ w_ada[l] + b_ada[l]).reshape(1, N_MOD, D)

        u_lat = _modulate(_rmsnorm(h_lat, g_norm[l, 0]), mod_lat[:, 0], mod_lat[:, 1])
        u_ctx = _modulate(_rmsnorm(h_ctx, g_norm[l, 0]), mod_ctx[:, 0], mod_ctx[:, 1])
        p_lat = _split_in(u_lat @ w_in[l])
        p_ctx = _split_in(u_ctx @ w_in[l])
        ya_lat, ya_ctx = _hgrn2_mixer(p_lat, p_ctx, hgrn_f_bias[l], lb_all[:, l], hgrn_norm[l], need_ctx)
        yb_lat, yb_ctx = _mlstm_mixer(p_lat, p_ctx, mlstm_conv_w[l], mlstm_conv_b[l], mlstm_gate_b[l],
                                      mlstm_norm[l], rows, need_ctx)
        mix_lat = _merge(p_lat, ya_lat, yb_lat, w_branch_a[l], w_branch_b[l], w_out[l])
        h_lat = h_lat + (mod_lat[:, 2, None, :] * _rmsnorm(mix_lat, g_norm[l, 1])).astype(h_lat.dtype)
        if need_ctx:
            mix_ctx = _merge(p_ctx, ya_ctx, yb_ctx, w_branch_a[l], w_branch_b[l], w_out[l])
            h_ctx = h_ctx + (mod_ctx[:, 2, None, :] * _rmsnorm(mix_ctx, g_norm[l, 1])).astype(h_ctx.dtype)

        v_lat = _modulate(_rmsnorm(h_lat, g_norm[l, 2]), mod_lat[:, 3], mod_lat[:, 4])
        f_lat = _ec_moe(v_lat, w_router[l], w_expert_gate[l], w_expert_up[l], w_expert_down[l])
        h_lat = h_lat + (mod_lat[:, 5, None, :] * _rmsnorm(f_lat, g_norm[l, 3])).astype(h_lat.dtype)
        if need_ctx:
            v_ctx = _modulate(_rmsnorm(h_ctx, g_norm[l, 2]), mod_ctx[:, 3], mod_ctx[:, 4])
            f_ctx = _ec_moe(v_ctx, w_router[l], w_expert_gate[l], w_expert_up[l], w_expert_down[l])
            h_ctx = h_ctx + (mod_ctx[:, 5, None, :] * _rmsnorm(f_ctx, g_norm[l, 3])).astype(h_ctx.dtype)
    return h_lat.astype(x.dtype)
```

```python
import functools

import jax
import jax.numpy as jnp
from jax import lax
from jax.experimental import pallas as pl
from jax.experimental.pallas import tpu as pltpu

F32 = jnp.float32
BF16 = jnp.bfloat16

EPS = 1e-6
N_MOD = 6
A_HEADS = 16
A_D = 128
B_HEADS = 8
B_DQK = 128
B_DV = 256
N_EXPERTS = 16
CAPACITY = 2
CHUNK = 64
SUB = 16
TOK = 256
NCH = TOK // CHUNK
LANE = 128
VMEM_LIMIT = 56 * 1024 * 1024

_NT = (((1,), (1,)), ((), ()))


def _pick(n, pref):
    t = min(n, pref)
    while n % t:
        t //= 2
    return t


def _cparams(sem):
    return pltpu.CompilerParams(dimension_semantics=sem, vmem_limit_bytes=VMEM_LIMIT)


def _silu(x):
    return x / (1.0 + jnp.exp(-x))


def _sigmoid(x):
    return 1.0 / (1.0 + jnp.exp(-x))


def _sig_pair(z):
    t = jnp.exp(-jnp.abs(z))
    r = 1.0 / (1.0 + t)
    tr = t * r
    pos = z >= 0
    return jnp.where(pos, r, tr), jnp.where(pos, tr, r)


def _log_sigmoid(x):
    return jnp.minimum(x, 0.0) - jnp.log(1.0 + jnp.exp(-jnp.abs(x)))


def _rms(xf, w):
    return xf * lax.rsqrt(jnp.mean(xf * xf, axis=-1, keepdims=True) + EPS) * w


def _mod_kernel(c_ref, w_ref, b_ref, o_ref):
    a = _silu(c_ref[...]).astype(BF16)
    o_ref[...] = jnp.dot(a, w_ref[...].astype(BF16), preferred_element_type=F32) + b_ref[...]


def _modulation(c8, w, b):
    D, N = w.shape
    tn = _pick(N, 512)
    return pl.pallas_call(
        _mod_kernel,
        grid=(N // tn,),
        in_specs=[pl.BlockSpec((8, D), lambda j: (0, 0)),
                  pl.BlockSpec((D, tn), lambda j: (0, j)),
                  pl.BlockSpec((1, tn), lambda j: (0, j))],
        out_specs=pl.BlockSpec((8, tn), lambda j: (0, j)),
        out_shape=jax.ShapeDtypeStruct((8, N), F32),
        compiler_params=_cparams(("parallel",)),
        name="adaln_mod",
    )(c8, w, b.reshape(1, N))


def _prenorm_kernel(x_ref, ctx_ref, g_ref, sh_ref, sc_ref, o_ref, *, n_lat):
    i = pl.program_id(1)
    g = g_ref[...]
    sh = sh_ref[0]
    sc = sc_ref[0]

    @pl.when(i < n_lat)
    def _():
        o_ref[0] = (_rms(x_ref[0], g) * (1.0 + sc) + sh).astype(o_ref.dtype)

    @pl.when(i == n_lat)
    def _():
        o_ref[0] = (_rms(ctx_ref[0], g) * (1.0 + sc) + sh).astype(o_ref.dtype)


def _prenorm(x, ctx, g, mod3):
    Bn, T, D = x.shape
    n_lat = T // TOK
    row = lambda b, i: jnp.where(i == n_lat, Bn, b)
    return pl.pallas_call(
        functools.partial(_prenorm_kernel, n_lat=n_lat),
        grid=(Bn, n_lat + 1),
        in_specs=[pl.BlockSpec((1, TOK, D), lambda b, i: (b, jnp.minimum(i, n_lat - 1), 0)),
                  pl.BlockSpec((1, TOK, D), lambda b, i: (b, 0, 0)),
                  pl.BlockSpec((1, D), lambda b, i: (0, 0)),
                  pl.BlockSpec((1, 1, D), lambda b, i: (row(b, i), 0, 0)),
                  pl.BlockSpec((1, 1, D), lambda b, i: (row(b, i), 0, 1))],
        out_specs=pl.BlockSpec((1, TOK, D), lambda b, i: (b, i, 0)),
        out_shape=jax.ShapeDtypeStruct((Bn, T + TOK, D), BF16),
        compiler_params=_cparams(("parallel", "arbitrary")),
        name="prenorm",
    )(x, ctx, g.reshape(1, D), mod3, mod3)


def _mm_kernel(a_ref, b_ref, o_ref):
    o_ref[...] = jnp.dot(a_ref[...], b_ref[...].astype(BF16),
                         preferred_element_type=F32).astype(o_ref.dtype)


def _mm_bias_kernel(a_ref, b_ref, bias_ref, o_ref):
    o_ref[...] = (jnp.dot(a_ref[...], b_ref[...].astype(BF16), preferred_element_type=F32)
                  + bias_ref[...]).astype(o_ref.dtype)


def _matmul(a, b, out_dtype, tm=1024, tn=512, bias=None, name="matmul"):
    M, K = a.shape
    N = b.shape[1]
    tm = _pick(M, tm)
    tn = _pick(N, tn)
    in_specs = [pl.BlockSpec((tm, K), lambda i, j: (i, 0)),
                pl.BlockSpec((K, tn), lambda i, j: (0, j))]
    args = [a, b]
    kern = _mm_kernel
    if bias is not None:
        in_specs.append(pl.BlockSpec((1, tn), lambda i, j: (0, j)))
        args.append(bias.reshape(1, N))
        kern = _mm_bias_kernel
    return pl.pallas_call(
        kern,
        grid=(M // tm, N // tn),
        in_specs=in_specs,
        out_specs=pl.BlockSpec((tm, tn), lambda i, j: (i, j)),
        out_shape=jax.ShapeDtypeStruct((M, N), out_dtype),
        compiler_params=_cparams(("parallel", "arbitrary")),
        name=name,
    )(*args)


def _tok_block(d, j, n_lat):
    lat = jnp.where(d == 0, j - 1, n_lat - j)
    return jnp.where(j == 0, n_lat, lat)


def _out_block(d, j, n_lat):
    return jnp.clip(jnp.where(d == 0, j - 1, n_lat - j), 0, n_lat - 1)


def _hgrn_kernel(aq_ref, af_ref, ai_ref, fb_ref, lb_ref, o_ref, st_ref):
    d = pl.program_id(2)
    j = pl.program_id(3)
    sgn = 1 - 2 * d
    fwd = d == 0

    @pl.when(j == 0)
    def _():
        st_ref[...] = jnp.zeros_like(st_ref)

    row = lax.broadcasted_iota(jnp.int32, (CHUNK, CHUNK), 0)
    col = lax.broadcasted_iota(jnp.int32, (CHUNK, CHUNK), 1)
    seen = (row - col) * sgn >= 0
    rs = jnp.right_shift(row, SUB.bit_length() - 1)
    cs = jnp.right_shift(col, SUB.bit_length() - 1)
    prev_blk = (rs - cs) * sgn > 0
    diag_blk = (rs == cs) & seen
    seen_f = jnp.where(seen, 1.0, 0.0)
    lane_sub = col & (SUB - 1)

    fb = fb_ref[0]
    lb = lb_ref[0]

    def chunk(c, carry):
        ci = jnp.where(fwd, c, NCH - 1 - c)
        r0 = pl.multiple_of(ci * CHUNK, CHUNK)
        z = af_ref[0, pl.ds(r0, CHUNK), :] + fb
        sp, sn = _sig_pair(z)
        logf = jnp.log(lb + (1.0 - lb) * sp)
        k = (1.0 - lb) * sn
        q = _silu(aq_ref[0, pl.ds(r0, CHUNK), :])
        v = ai_ref[0, pl.ds(r0, CHUNK), :]
        vb = v.astype(BF16)

        b = jnp.dot(seen_f, logf, precision=lax.Precision.HIGHEST, preferred_element_type=F32)
        total = jnp.sum(logf, axis=0, keepdims=True)

        entry = []
        for i in range(CHUNK // SUB):
            lo, hi = i * SUB, i * SUB + SUB - 1
            e_f = b[lo:lo + 1] - logf[lo:lo + 1]
            e_b = b[hi:hi + 1] - logf[hi:hi + 1]
            entry.append(jnp.where(fwd, e_f, e_b))
        entry_rows = jnp.concatenate([jnp.broadcast_to(e, (SUB, A_D)) for e in entry], axis=0)

        qt = (q * jnp.exp(jnp.minimum(b - entry_rows, 0.0))).astype(BF16)
        blks = []
        for i in range(CHUNK // SUB):
            kt = (k * jnp.exp(jnp.minimum(entry[i] - b, 0.0))).astype(BF16)
            blks.append(lax.dot_general(qt[i * SUB:(i + 1) * SUB], kt, _NT,
                                        preferred_element_type=F32))
        attn_off = jnp.concatenate(blks, axis=0)

        acc = jnp.zeros((CHUNK, CHUNK), F32)
        for sl in range(SUB):
            bref = jnp.concatenate(
                [jnp.broadcast_to(b[i * SUB + sl:i * SUB + sl + 1], (SUB, A_D))
                 for i in range(CHUNK // SUB)], axis=0)
            kref = jnp.concatenate(
                [jnp.broadcast_to(k[i * SUB + sl:i * SUB + sl + 1], (SUB, A_D))
                 for i in range(CHUNK // SUB)], axis=0)
            p = (q * kref) * jnp.exp(jnp.minimum(b - bref, 0.0))
            a = jnp.sum(p, axis=1, keepdims=True)
            acc = jnp.where(lane_sub == sl, a, acc)

        attn = jnp.where(prev_blk, attn_off, 0.0) + jnp.where(diag_blk, acc, 0.0)
        st = st_ref[...]
        qhat = (q * jnp.exp(b)).astype(BF16)
        o = jnp.dot(attn.astype(BF16), vb, preferred_element_type=F32)
        o = o + lax.dot_general(qhat, st.astype(BF16), _NT, preferred_element_type=F32)
        o_ref[0, 0, pl.ds(r0, CHUNK), :] = o

        khat = (k * jnp.exp(total - b)).astype(BF16)
        vt = v.T.astype(BF16)
        st_ref[...] = st * jnp.exp(total) + jnp.dot(vt, khat, preferred_element_type=F32)
        return carry

    lax.fori_loop(0, NCH, chunk, 0)


def _hgrn_scan(p3, f_bias, lb, n_lat):
    Bn = p3.shape[0]
    A = A_HEADS
    tb = lambda d, j: _tok_block(d, j, n_lat)
    return pl.pallas_call(
        _hgrn_kernel,
        grid=(Bn, A, 2, n_lat + 1),
        in_specs=[pl.BlockSpec((1, TOK, A_D), lambda b, h, d, j: (b, tb(d, j), h)),
                  pl.BlockSpec((1, TOK, A_D), lambda b, h, d, j: (b, tb(d, j), A * (1 + d) + h)),
                  pl.BlockSpec((1, TOK, A_D), lambda b, h, d, j: (b, tb(d, j), 3 * A + h)),
                  pl.BlockSpec((1, 1, A_D), lambda b, h, d, j: (d, 0, h)),
                  pl.BlockSpec((1, 1, A_D), lambda b, h, d, j: (d, 0, h))],
        out_specs=pl.BlockSpec((1, 1, TOK, A_D),
                               lambda b, h, d, j: (d, b, _out_block(d, j, n_lat), h)),
        out_shape=jax.ShapeDtypeStruct((2, Bn, n_lat * TOK, A * A_D), F32),
        scratch_shapes=[pltpu.VMEM((A_D, A_D), F32)],
        compiler_params=_cparams(("parallel", "parallel", "parallel", "arbitrary")),
        name="hgrn2_scan",
    )(p3, p3, p3, f_bias.reshape(2, 1, A * A_D), lb.reshape(2, 1, A * A_D))


def _mlstm_kernel(bq_ref, bk_ref, bv_ref, gr_ref, gc_ref, cwq_ref, cwk_ref, o_ref,
                  q_s, k_s, c_s, n_s, m_s):
    d = pl.program_id(2)
    j = pl.program_id(3)
    sgn = 1 - 2 * d
    fwd = d == 0

    @pl.when(j == 0)
    def _():
        c_s[...] = jnp.zeros_like(c_s)
        n_s[...] = jnp.zeros_like(n_s)
        m_s[...] = jnp.zeros_like(m_s)

    t = lax.broadcasted_iota(jnp.int32, (TOK, 1), 0)
    row_mask = jnp.where(j == 0, TOK - 1, CHUNK - 1)
    pos = t & row_mask
    first = pos == 0
    last = pos == row_mask

    def conv(u, cw):
        up = jnp.where(first, 0.0, pltpu.roll(u, 1, 0))
        dn = jnp.where(last, 0.0, pltpu.roll(u, TOK - 1, 0))
        y = cw[3:4] + up * cw[0:1] + u * cw[1:2] + dn * cw[2:3]
        return _silu(y)

    q_s[...] = conv(bq_ref[0], cwq_ref[:, 0, :])
    k_s[...] = conv(bk_ref[0], cwk_ref[:, 0, :]) * (B_DQK ** -0.5)

    row = lax.broadcasted_iota(jnp.int32, (CHUNK, CHUNK), 0)
    col = lax.broadcasted_iota(jnp.int32, (CHUNK, CHUNK), 1)
    seen = (row - col) * sgn >= 0
    seen_t = (col - row) * sgn >= 0

    def chunk(c, carry):
        ci = jnp.where(fwd, c, NCH - 1 - c)
        r0 = pl.multiple_of(ci * CHUNK, CHUNK)
        q = q_s[pl.ds(r0, CHUNK), :]
        k = k_s[pl.ds(r0, CHUNK), :]
        vb = bv_ref[0, pl.ds(r0, CHUNK), :].astype(BF16)
        qb = q.astype(BF16)
        kb = k.astype(BF16)
        g_r = gr_ref[0, 0, 0, ci]
        g_c = gc_ref[0, 0, 0, ci]
        ii_r = g_r[0:1]
        lf_r = _log_sigmoid(g_r[1:2])
        ii_c = g_c[:, 0:1]
        lf_c = _log_sigmoid(g_c[:, 1:2])

        b_c = jnp.sum(jnp.where(seen, lf_r, 0.0), axis=1, keepdims=True)
        b_r = jnp.sum(jnp.where(seen_t, lf_c, 0.0), axis=0, keepdims=True)
        total = jnp.sum(lf_r, axis=1, keepdims=True)
        m = m_s[:, 0:1]

        logw = jnp.where(seen, b_c - b_r + ii_r, -jnp.inf)
        log_inter = b_c + m
        m_t = jnp.maximum(jnp.max(logw, axis=1, keepdims=True), log_inter)
        w = jnp.exp(logw - m_t)
        a = jnp.exp(log_inter - m_t)
        qk = lax.dot_general(qb, kb, _NT, preferred_element_type=F32) * w
        cmat = c_s[...]
        nvec = n_s[...]
        num = (jnp.dot(qk.astype(BF16), vb, preferred_element_type=F32)
               + a * jnp.dot(qb, cmat.astype(BF16), preferred_element_type=F32))
        den = jnp.sum(qk, axis=1, keepdims=True) + a * jnp.sum(q * nvec, axis=1, keepdims=True)
        o_ref[0, 0, pl.ds(r0, CHUNK), :] = num / jnp.maximum(jnp.abs(den), jnp.exp(-m_t))

        logs = total - b_c + ii_c
        m_new = jnp.maximum(total + m, jnp.max(logs, axis=0, keepdims=True))
        ws = jnp.exp(logs - m_new)
        dec = jnp.exp(total + m - m_new)
        kw = k * ws
        c_s[...] = dec * cmat + jnp.dot(kw.T.astype(BF16), vb, preferred_element_type=F32)
        n_s[...] = dec * nvec + jnp.sum(kw, axis=0, keepdims=True)
        m_s[...] = jnp.broadcast_to(m_new, m_s.shape)
        return carry

    lax.fori_loop(0, NCH, chunk, 0)


def _mlstm_scan(p3, g_rows, g_cols, conv4, n_lat):
    Bn = p3.shape[0]
    A, Bh = A_HEADS, B_HEADS
    q0 = 5 * A
    k0 = 5 * A + Bh
    v0 = (5 * A + 2 * Bh) // 2
    tb = lambda d, j: _tok_block(d, j, n_lat)
    return pl.pallas_call(
        _mlstm_kernel,
        grid=(Bn, Bh, 2, n_lat + 1),
        in_specs=[pl.BlockSpec((1, TOK, B_DQK), lambda b, h, d, j: (b, tb(d, j), q0 + h)),
                  pl.BlockSpec((1, TOK, B_DQK), lambda b, h, d, j: (b, tb(d, j), k0 + h)),
                  pl.BlockSpec((1, TOK, B_DV), lambda b, h, d, j: (b, tb(d, j), v0 + h)),
                  pl.BlockSpec((1, 1, 1, NCH, 2, CHUNK),
                               lambda b, h, d, j: (b, d, h, tb(d, j), 0, 0)),
                  pl.BlockSpec((1, 1, 1, NCH, CHUNK, 2),
                               lambda b, h, d, j: (b, d, h, tb(d, j), 0, 0)),
                  pl.BlockSpec((4, 1, B_DQK), lambda b, h, d, j: (0, 0, h)),
                  pl.BlockSpec((4, 1, B_DQK), lambda b, h, d, j: (0, 0, Bh + h))],
        out_specs=pl.BlockSpec((1, 1, TOK, B_DV),
                               lambda b, h, d, j: (d, b, _out_block(d, j, n_lat), h)),
        out_shape=jax.ShapeDtypeStruct((2, Bn, n_lat * TOK, Bh * B_DV), F32),
        scratch_shapes=[pltpu.VMEM((TOK, B_DQK), F32), pltpu.VMEM((TOK, B_DQK), F32),
                        pltpu.VMEM((B_DQK, B_DV), F32), pltpu.VMEM((1, B_DQK), F32),
                        pltpu.VMEM((1, LANE), F32)],
        compiler_params=_cparams(("parallel", "parallel", "parallel", "arbitrary")),
        name="mlstm_scan",
    )(p3, p3, p3, g_rows, g_cols, conv4, conv4)


def _readout_kernel(oa_ref, ob_ref, ag_ref, bo_ref, na_ref, nb_ref, ya_ref, yb_ref):
    for h in range(A_HEADS):
        sl = slice(h * A_D, (h + 1) * A_D)
        o = oa_ref[0, 0, :, sl] + oa_ref[1, 0, :, sl]
        ya_ref[0, :, sl] = (_rms(o, na_ref[:, sl]) * _silu(ag_ref[0, :, sl])).astype(ya_ref.dtype)
    for h in range(B_HEADS):
        sl = slice(h * B_DV, (h + 1) * B_DV)
        o = ob_ref[0, 0, :, sl] + ob_ref[1, 0, :, sl]
        yb_ref[0, :, sl] = (_rms(o, nb_ref[:, sl]) * _sigmoid(bo_ref[0, :, sl])).astype(yb_ref.dtype)


def _readout(oa, ob, p3, norm_a, norm_b, n_lat):
    Bn = p3.shape[0]
    T = n_lat * TOK
    wa = A_HEADS * A_D
    wb = B_HEADS * B_DV
    ag_blk = 4
    bo_blk = (5 * A_HEADS + 4 * B_HEADS) * LANE // wb
    return pl.pallas_call(
        _readout_kernel,
        grid=(Bn, n_lat),
        in_specs=[pl.BlockSpec((2, 1, TOK, wa), lambda b, i: (0, b, i, 0)),
                  pl.BlockSpec((2, 1, TOK, wb), lambda b, i: (0, b, i, 0)),
                  pl.BlockSpec((1, TOK, wa), lambda b, i: (b, i, ag_blk)),
                  pl.BlockSpec((1, TOK, wb), lambda b, i: (b, i, bo_blk)),
                  pl.BlockSpec((1, wa), lambda b, i: (0, 0)),
                  pl.BlockSpec((1, wb), lambda b, i: (0, 0))],
        out_specs=[pl.BlockSpec((1, TOK, wa), lambda b, i: (b, i, 0)),
                   pl.BlockSpec((1, TOK, wb), lambda b, i: (b, i, 0))],
        out_shape=[jax.ShapeDtypeStruct((Bn, T, wa), BF16),
                   jax.ShapeDtypeStruct((Bn, T, wb), BF16)],
        compiler_params=_cparams(("parallel", "parallel")),
        name="readout",
    )(oa, ob, p3, p3, norm_a.reshape(1, wa), norm_b.reshape(1, wb))


def _merge_kernel(ya_ref, yb_ref, wa_ref, wb_ref, ga_ref, gb_ref, o_ref):
    pa = jnp.dot(ya_ref[0], wa_ref[...], preferred_element_type=F32)
    pb = jnp.dot(yb_ref[0], wb_ref[...], preferred_element_type=F32)
    o_ref[0] = (_sigmoid(ga_ref[0]) * pa + _sigmoid(gb_ref[0]) * pb).astype(o_ref.dtype)


def _merge(ya, yb, wa, wb, p3, D):
    Bn, T, ka = ya.shape
    kb = yb.shape[2]
    tm = _pick(T, 1024)
    tn = _pick(D, 512)
    ga0 = (5 * A_HEADS + 6 * B_HEADS) * LANE // tn
    gb0 = ga0 + D // tn
    return pl.pallas_call(
        _merge_kernel,
        grid=(Bn, T // tm, D // tn),
        in_specs=[pl.BlockSpec((1, tm, ka), lambda b, i, j: (b, i, 0)),
                  pl.BlockSpec((1, tm, kb), lambda b, i, j: (b, i, 0)),
                  pl.BlockSpec((ka, tn), lambda b, i, j: (0, j)),
                  pl.BlockSpec((kb, tn), lambda b, i, j: (0, j)),
                  pl.BlockSpec((1, tm, tn), lambda b, i, j: (b, i, ga0 + j)),
                  pl.BlockSpec((1, tm, tn), lambda b, i, j: (b, i, gb0 + j))],
        out_specs=pl.BlockSpec((1, tm, tn), lambda b, i, j: (b, i, j)),
        out_shape=jax.ShapeDtypeStruct((Bn, T, D), BF16),
        compiler_params=_cparams(("parallel", "parallel", "arbitrary")),
        name="merge",
    )(ya, yb, wa, wb, p3, p3)


def _resid_router_kernel(x_ref, mix_ref, g1_ref, g2_ref, gate_ref, sh_ref, sc_ref, wr_ref,
                         h_ref, v_ref, aff_ref):
    h = x_ref[0] + gate_ref[0] * _rms(mix_ref[0], g1_ref[...])
    h_ref[0] = h
    v = _rms(h, g2_ref[...]) * (1.0 + sc_ref[0]) + sh_ref[0]
    v_ref[0] = v.astype(v_ref.dtype)
    logits = jnp.dot(v, wr_ref[...], precision=lax.Precision.HIGHEST, preferred_element_type=F32)
    lane = lax.broadcasted_iota(jnp.int32, logits.shape, 1)
    logits = jnp.where(lane < N_EXPERTS, logits, -jnp.inf)
    e = jnp.exp(logits - jnp.max(logits, axis=-1, keepdims=True))
    aff_ref[0] = e / jnp.sum(e, axis=-1, keepdims=True)


def _resid_router(x, mix, g1, g2, mod3, w_router_pad):
    Bn, T, D = x.shape
    mspec = lambda k: pl.BlockSpec((1, 1, D), lambda b, i: (b, 0, k))
    tok = pl.BlockSpec((1, TOK, D), lambda b, i: (b, i, 0))
    vec = pl.BlockSpec((1, D), lambda b, i: (0, 0))
    return pl.pallas_call(
        _resid_router_kernel,
        grid=(Bn, T // TOK),
        in_specs=[tok, tok, vec, vec, mspec(2), mspec(3), mspec(4),
                  pl.BlockSpec((D, LANE), lambda b, i: (0, 0))],
        out_specs=[tok, tok, pl.BlockSpec((1, TOK, LANE), lambda b, i: (b, i, 0))],
        out_shape=[jax.ShapeDtypeStruct((Bn, T, D), F32),
                   jax.ShapeDtypeStruct((Bn, T, D), BF16),
                   jax.ShapeDtypeStruct((Bn, T, LANE), F32)],
        compiler_params=_cparams(("parallel", "parallel")),
        name="resid_router",
    )(x, mix, g1.reshape(1, D), g2.reshape(1, D), mod3, mod3, mod3, w_router_pad)


def _rank_kernel(affc_ref, affr_ref, rank_ref, col_s, *, tt):
    e = pl.program_id(1)
    ti = pl.program_id(2)
    T = col_s.shape[0]

    @pl.when(ti == 0)
    def _():
        lane = lax.broadcasted_iota(jnp.int32, (T, LANE), 1)
        col_s[...] = jnp.sum(jnp.where(lane == e, affc_ref[0], 0.0), axis=1, keepdims=True)

    rowv = affr_ref[0, 0]
    t_idx = ti * tt + lax.broadcasted_iota(jnp.int32, (tt, tt), 1)
    s_loc = lax.broadcasted_iota(jnp.int32, (tt, tt), 0)

    def s_loop(si, acc):
        s0 = pl.multiple_of(si * tt, tt)
        cv = col_s[pl.ds(s0, tt), :]
        ahead = (cv > rowv) | ((cv == rowv) & (s_loc + s0 < t_idx))
        return acc + jnp.sum(jnp.where(ahead, 1.0, 0.0), axis=0, keepdims=True)

    rank_ref[0, 0] = lax.fori_loop(0, T // tt, s_loop, jnp.zeros((1, tt), F32))


def _rank(aff, aff_rows):
    Bn, T, _ = aff.shape
    tt = _pick(T, 512)
    return pl.pallas_call(
        functools.partial(_rank_kernel, tt=tt),
        grid=(Bn, N_EXPERTS, T // tt),
        in_specs=[pl.BlockSpec((1, T, LANE), lambda b, e, t: (b, 0, 0)),
                  pl.BlockSpec((1, 1, 1, tt), lambda b, e, t: (b, e, 0, t))],
        out_specs=pl.BlockSpec((1, 1, 1, tt), lambda b, e, t: (b, e, 0, t)),
        out_shape=jax.ShapeDtypeStruct((Bn, N_EXPERTS, 1, T), F32),
        scratch_shapes=[pltpu.VMEM((T, 1), F32)],
        compiler_params=_cparams(("parallel", "parallel", "arbitrary")),
        name="ec_rank",
    )(aff, aff_rows)


def _gather_kernel(rank_ref, v_ref, o_ref, acc_s, *, cap, tk):
    T = v_ref.shape[1]
    slot = lax.broadcasted_iota(jnp.int32, (cap, tk), 0).astype(F32)

    def k_loop(kc, carry):
        k0 = pl.multiple_of(kc * tk, tk)
        oh = jnp.where(rank_ref[0, 0, :, pl.ds(k0, tk)] == slot, 1.0, 0.0).astype(BF16)
        part = jnp.dot(oh, v_ref[0, pl.ds(k0, tk), :], preferred_element_type=F32)

        @pl.when(kc == 0)
        def _():
            acc_s[...] = part

        @pl.when(kc > 0)
        def _():
            acc_s[...] += part
        return carry

    lax.fori_loop(0, T // tk, k_loop, 0)
    o_ref[0, 0] = acc_s[...].astype(o_ref.dtype)


def _gather(rank, v, cap):
    Bn, T, D = v.shape
    ct = _pick(D, 1024)
    tk = _pick(T, 512)
    return pl.pallas_call(
        functools.partial(_gather_kernel, cap=cap, tk=tk),
        grid=(Bn, D // ct, N_EXPERTS),
        in_specs=[pl.BlockSpec((1, 1, 1, T), lambda b, c, e: (b, e, 0, 0)),
                  pl.BlockSpec((1, T, ct), lambda b, c, e: (b, 0, c))],
        out_specs=pl.BlockSpec((1, 1, cap, ct), lambda b, c, e: (e, b, 0, c)),
        out_shape=jax.ShapeDtypeStruct((N_EXPERTS, Bn, cap, D), BF16),
        scratch_shapes=[pltpu.VMEM((cap, ct), F32)],
        compiler_params=_cparams(("parallel", "parallel", "arbitrary")),
        name="ec_gather",
    )(rank, v)


def _ffn1_kernel(x_ref, wg_ref, wu_ref, o_ref):
    x = x_ref[0]
    g = jnp.dot(x, wg_ref[0].astype(BF16), preferred_element_type=F32)
    u = jnp.dot(x, wu_ref[0].astype(BF16), preferred_element_type=F32)
    o_ref[0] = (_silu(g) * u).astype(o_ref.dtype)


def _ffn1(xg, wg, wu):
    E, M, D = xg.shape
    F = wg.shape[2]
    tm = _pick(M, 1024)
    tn = _pick(F, 256)
    return pl.pallas_call(
        _ffn1_kernel,
        grid=(E, M // tm, F // tn),
        in_specs=[pl.BlockSpec((1, tm, D), lambda e, i, j: (e, i, 0)),
                  pl.BlockSpec((1, D, tn), lambda e, i, j: (e, 0, j)),
                  pl.BlockSpec((1, D, tn), lambda e, i, j: (e, 0, j))],
        out_specs=pl.BlockSpec((1, tm, tn), lambda e, i, j: (e, i, j)),
        out_shape=jax.ShapeDtypeStruct((E, M, F), BF16),
        compiler_params=_cparams(("parallel", "parallel", "arbitrary")),
        name="ec_ffn_up",
    )(xg, wg, wu)


def _ffn2_kernel(h_ref, wd_ref, o_ref):
    o_ref[0] = jnp.dot(h_ref[0], wd_ref[0].astype(BF16),
                       preferred_element_type=F32).astype(o_ref.dtype)


def _ffn2(hid, wd):
    E, M, F = hid.shape
    D = wd.shape[2]
    tn = _pick(D, 512)
    return pl.pallas_call(
        _ffn2_kernel,
        grid=(E, D // tn),
        in_specs=[pl.BlockSpec((1, M, F), lambda e, j: (e, 0, 0)),
                  pl.BlockSpec((1, F, tn), lambda e, j: (e, 0, j))],
        out_specs=pl.BlockSpec((1, M, tn), lambda e, j: (e, 0, j)),
        out_shape=jax.ShapeDtypeStruct((E, M, D), BF16),
        compiler_params=_cparams(("parallel", "arbitrary")),
        name="ec_ffn_down",
    )(hid, wd)


def _scatter_kernel(rank_ref, aff_ref, y_ref, o_ref, *, cap):
    e = pl.program_id(3)
    tq = o_ref.shape[1]

    @pl.when(e == 0)
    def _():
        o_ref[...] = jnp.zeros_like(o_ref)

    rk = jnp.broadcast_to(rank_ref[0, 0], (LANE, tq)).T
    af = jnp.broadcast_to(aff_ref[0, 0], (LANE, tq)).T
    reps = pl.cdiv(cap, LANE)
    rk = jnp.concatenate([rk] * reps, axis=1)[:, :cap]
    af = jnp.concatenate([af] * reps, axis=1)[:, :cap]
    slot = lax.broadcasted_iota(jnp.int32, (tq, cap), 1).astype(F32)
    ohw = jnp.where(rk == slot, af, 0.0).astype(BF16)
    o_ref[0] += jnp.dot(ohw, y_ref[0, 0], preferred_element_type=F32)


def _scatter(rank, aff_rows, y, cap):
    E, Bn, _, D = y.shape
    T = rank.shape[3]
    tq = _pick(T, 1024)
    ct = _pick(D, 1024)
    return pl.pallas_call(
        functools.partial(_scatter_kernel, cap=cap),
        grid=(Bn, T // tq, D // ct, E),
        in_specs=[pl.BlockSpec((1, 1, 1, tq), lambda b, t, c, e: (b, e, 0, t)),
                  pl.BlockSpec((1, 1, 1, tq), lambda b, t, c, e: (b, e, 0, t)),
                  pl.BlockSpec((1, 1, cap, ct), lambda b, t, c, e: (e, b, 0, c))],
        out_specs=pl.BlockSpec((1, tq, ct), lambda b, t, c, e: (b, t, c)),
        out_shape=jax.ShapeDtypeStruct((Bn, T, D), F32),
        compiler_params=_cparams(("parallel", "parallel", "parallel", "arbitrary")),
        name="ec_scatter",
    )(rank, aff_rows, y)


def _final_kernel(h_ref, f_ref, g_ref, gate_ref, o_ref):
    o_ref[0] = h_ref[0] + gate_ref[0] * _rms(f_ref[0], g_ref[...])


def _final(h, f, g3, mod3):
    Bn, T, D = h.shape
    tok = pl.BlockSpec((1, TOK, D), lambda b, i: (b, i, 0))
    return pl.pallas_call(
        _final_kernel,
        grid=(Bn, T // TOK),
        in_specs=[tok, tok, pl.BlockSpec((1, D), lambda b, i: (0, 0)),
                  pl.BlockSpec((1, 1, D), lambda b, i: (b, 0, 5))],
        out_specs=tok,
        out_shape=jax.ShapeDtypeStruct((Bn, T, D), F32),
        compiler_params=_cparams(("parallel", "parallel")),
        name="final_resid",
    )(h, f, g3.reshape(1, D), mod3)


def _layer(h_lat, ctx, c8, l, lb_l, w_ada, b_ada, g_norm, w_in, hgrn_f_bias, hgrn_norm,
           mlstm_conv_w, mlstm_conv_b, mlstm_gate_b, mlstm_norm, w_branch_a, w_branch_b, w_out,
           w_router, w_expert_gate, w_expert_up, w_expert_down):
    Bn, T, D = h_lat.shape
    n_lat = T // TOK
    n_tok = T + TOK
    A, Bh = A_HEADS, B_HEADS
    n_main = (5 * A + 6 * Bh) * LANE + 2 * D
    g0 = (5 * A + 6 * Bh) * LANE

    mod = _modulation(c8, w_ada[l], b_ada[l])
    mod3 = mod.reshape(8, 1, N_MOD * D)

    u = _prenorm(h_lat, ctx, g_norm[l, 0], mod3)

    w_l = w_in[l]
    w_main = jnp.concatenate([w_l[:, :g0], w_l[:, g0 + 4 * Bh:]], axis=1).astype(BF16)
    w_gate = jnp.pad(w_l[:, g0:g0 + 4 * Bh], ((0, 0), (0, LANE - 4 * Bh))).astype(BF16)
    gate_bias = jnp.pad(mlstm_gate_b[l].reshape(4 * Bh), (0, LANE - 4 * Bh))
    u2 = u.reshape(Bn * n_tok, D)
    p3 = _matmul(u2, w_main, F32, name="in_proj").reshape(Bn, n_tok, n_main)
    gates = _matmul(u2, w_gate, F32, tn=LANE, bias=gate_bias, name="in_proj_gates")

    g6 = gates[:, :4 * Bh].reshape(Bn, n_tok // CHUNK, CHUNK, 2, 2, Bh)
    g_rows = g6.transpose(0, 3, 5, 1, 4, 2)
    g_cols = g6.transpose(0, 3, 5, 1, 2, 4)
    conv4 = jnp.concatenate([mlstm_conv_w[l], mlstm_conv_b[l][None]], axis=0)
    conv4 = conv4.reshape(4, 1, 2 * Bh * B_DQK)

    oa = _hgrn_scan(p3, hgrn_f_bias[l], lb_l, n_lat)
    ob = _mlstm_scan(p3, g_rows, g_cols, conv4, n_lat)
    ya, yb = _readout(oa, ob, p3, hgrn_norm[l], mlstm_norm[l], n_lat)
    merged = _merge(ya, yb, w_branch_a[l].astype(BF16), w_branch_b[l].astype(BF16), p3, D)
    mix = _matmul(merged.reshape(Bn * T, D), w_out[l].astype(BF16), F32, name="out_proj")

    w_router_pad = jnp.pad(w_router[l], ((0, 0), (0, LANE - N_EXPERTS)))
    h_lat, v_lat, aff = _resid_router(h_lat, mix.reshape(Bn, T, D), g_norm[l, 1], g_norm[l, 2],
                                      mod3, w_router_pad)

    cap = CAPACITY * T // N_EXPERTS
    aff_rows = aff[:, :, :N_EXPERTS].transpose(0, 2, 1).reshape(Bn, N_EXPERTS, 1, T)
    rank = _rank(aff, aff_rows)
    xg = _gather(rank, v_lat, cap)
    hid = _ffn1(xg.reshape(N_EXPERTS, Bn * cap, D), w_expert_gate[l], w_expert_up[l])
    y = _ffn2(hid, w_expert_down[l]).reshape(N_EXPERTS, Bn, cap, D)
    f_lat = _scatter(rank, aff_rows, y, cap)
    return _final(h_lat, f_lat, g_norm[l, 3], mod3)


def kernel(x, c, ctx, c_ctx, w_ada, b_ada, g_norm, w_in, hgrn_f_bias, hgrn_lb, hgrn_norm,
           mlstm_conv_w, mlstm_conv_b, mlstm_gate_b, mlstm_norm, w_branch_a, w_branch_b, w_out,
           w_router, w_expert_gate, w_expert_up, w_expert_down):
    Bn, T, D = x.shape
    depth = w_ada.shape[0]
    assert depth == 1, "context outputs are only produced for the state hand-off (single layer)"
    assert ctx.shape[1] == TOK and T % TOK == 0 and Bn < 8
    lb_all = jnp.cumsum(jax.nn.softmax(hgrn_lb.astype(F32), axis=1), axis=1)
    c8 = jnp.zeros((8, D), F32).at[:Bn].set(c).at[Bn].set(c_ctx)
    h_lat = x
    for l in range(depth):
        h_lat = _layer(h_lat, ctx, c8, l, lb_all[:, l], w_ada, b_ada, g_norm, w_in, hgrn_f_bias,
                       hgrn_norm, mlstm_conv_w, mlstm_conv_b, mlstm_gate_b, mlstm_norm,
                       w_branch_a, w_branch_b, w_out, w_router, w_expert_gate, w_expert_up,
                       w_expert_down)
    return h_lat.astype(x.dtype)
```

```python
import functools

import jax
import jax.numpy as jnp
from jax import lax
from jax.experimental import pallas as pl
from jax.experimental.pallas import tpu as pltpu

F32 = jnp.float32
BF16 = jnp.bfloat16

EPS = 1e-6
N_MOD = 6
A_HEADS = 16
A_D = 128
B_HEADS = 8
B_DQK = 128
B_DV = 256
N_EXPERTS = 16
CAPACITY = 2
CHUNK = 64
SUB = 16
TOK = 256
NCH = TOK // CHUNK
NSUB = CHUNK // SUB
LANE = 128
VMEM_LIMIT = 56 * 1024 * 1024

_NT = (((1,), (1,)), ((), ()))


def _pick(n, pref):
    t = min(n, pref)
    while n % t:
        t //= 2
    return t


def _cparams(sem):
    return pltpu.CompilerParams(dimension_semantics=sem, vmem_limit_bytes=VMEM_LIMIT)


def _silu(x):
    return x / (1.0 + jnp.exp(-x))


def _sigmoid(x):
    return 1.0 / (1.0 + jnp.exp(-x))


def _sig_pair(z):
    t = jnp.exp(-jnp.abs(z))
    r = 1.0 / (1.0 + t)
    tr = t * r
    pos = z >= 0
    return jnp.where(pos, r, tr), jnp.where(pos, tr, r)


def _log_sigmoid(x):
    return jnp.minimum(x, 0.0) - jnp.log(1.0 + jnp.exp(-jnp.abs(x)))


def _rms(xf, w):
    return xf * lax.rsqrt(jnp.mean(xf * xf, axis=-1, keepdims=True) + EPS) * w


def _rows(x, r, n):
    return jnp.broadcast_to(x[r:r + 1], (n, x.shape[1]))


def _mod_kernel(c_ref, w_ref, b_ref, o_ref):
    a = _silu(c_ref[...]).astype(BF16)
    o_ref[...] = jnp.dot(a, w_ref[...].astype(BF16), preferred_element_type=F32) + b_ref[...]


def _modulation(c8, w, b):
    D, N = w.shape
    tn = _pick(N, 512)
    return pl.pallas_call(
        _mod_kernel,
        grid=(N // tn,),
        in_specs=[pl.BlockSpec((8, D), lambda j: (0, 0)),
                  pl.BlockSpec((D, tn), lambda j: (0, j)),
                  pl.BlockSpec((1, tn), lambda j: (0, j))],
        out_specs=pl.BlockSpec((8, tn), lambda j: (0, j)),
        out_shape=jax.ShapeDtypeStruct((8, N), F32),
        compiler_params=_cparams(("parallel",)),
        name="adaln_mod",
    )(c8, w, b.reshape(1, N))


def _prenorm_kernel(x_ref, ctx_ref, g_ref, sh_ref, sc_ref, o_ref, *, n_lat):
    i = pl.program_id(1)
    g = g_ref[...]
    sh = sh_ref[0]
    sc = sc_ref[0]

    @pl.when(i < n_lat)
    def _():
        o_ref[0] = (_rms(x_ref[0], g) * (1.0 + sc) + sh).astype(o_ref.dtype)

    @pl.when(i == n_lat)
    def _():
        o_ref[0] = (_rms(ctx_ref[0], g) * (1.0 + sc) + sh).astype(o_ref.dtype)


def _prenorm(x, ctx, g, mod3):
    Bn, T, D = x.shape
    n_lat = T // TOK
    row = lambda b, i: jnp.where(i == n_lat, Bn, b)
    return pl.pallas_call(
        functools.partial(_prenorm_kernel, n_lat=n_lat),
        grid=(Bn, n_lat + 1),
        in_specs=[pl.BlockSpec((1, TOK, D), lambda b, i: (b, jnp.minimum(i, n_lat - 1), 0)),
                  pl.BlockSpec((1, TOK, D), lambda b, i: (b, 0, 0)),
                  pl.BlockSpec((1, D), lambda b, i: (0, 0)),
                  pl.BlockSpec((1, 1, D), lambda b, i: (row(b, i), 0, 0)),
                  pl.BlockSpec((1, 1, D), lambda b, i: (row(b, i), 0, 1))],
        out_specs=pl.BlockSpec((1, TOK, D), lambda b, i: (b, i, 0)),
        out_shape=jax.ShapeDtypeStruct((Bn, T + TOK, D), BF16),
        compiler_params=_cparams(("parallel", "arbitrary")),
        name="prenorm",
    )(x, ctx, g.reshape(1, D), mod3, mod3)


def _mm_kernel(a_ref, b_ref, o_ref):
    o_ref[...] = jnp.dot(a_ref[...], b_ref[...].astype(BF16),
                         preferred_element_type=F32).astype(o_ref.dtype)


def _mm_bias_kernel(a_ref, b_ref, bias_ref, o_ref):
    o_ref[...] = (jnp.dot(a_ref[...], b_ref[...].astype(BF16), preferred_element_type=F32)
                  + bias_ref[...]).astype(o_ref.dtype)


def _matmul(a, b, out_dtype, tm=1024, tn=512, bias=None, name="matmul"):
    M, K = a.shape
    N = b.shape[1]
    tm = _pick(M, tm)
    tn = _pick(N, tn)
    in_specs = [pl.BlockSpec((tm, K), lambda i, j: (i, 0)),
                pl.BlockSpec((K, tn), lambda i, j: (0, j))]
    args = [a, b]
    kern = _mm_kernel
    if bias is not None:
        in_specs.append(pl.BlockSpec((1, tn), lambda i, j: (0, j)))
        args.append(bias.reshape(1, N))
        kern = _mm_bias_kernel
    return pl.pallas_call(
        kern,
        grid=(M // tm, N // tn),
        in_specs=in_specs,
        out_specs=pl.BlockSpec((tm, tn), lambda i, j: (i, j)),
        out_shape=jax.ShapeDtypeStruct((M, N), out_dtype),
        compiler_params=_cparams(("parallel", "arbitrary")),
        name=name,
    )(*args)


def _tok_block(rev, j, n_lat):
    lat = n_lat - j if rev else j - 1
    return jnp.where(j == 0, n_lat, lat)


def _out_block(rev, j, n_lat):
    return jnp.clip(n_lat - j if rev else j - 1, 0, n_lat - 1)


def _chunk_masks(rev):
    row = lax.broadcasted_iota(jnp.int32, (CHUNK, CHUNK), 0)
    col = lax.broadcasted_iota(jnp.int32, (CHUNK, CHUNK), 1)
    return (col >= row, col <= row) if rev else (col <= row, col >= row)


def _hgrn_consts():
    t = jnp.arange(TOK)
    same = (t[:, None] // CHUNK) == (t[None, :] // CHUNK)
    sub_r, sub_c = t[:, None] // SUB, t[None, :] // SUB
    fwd = jnp.concatenate([same & (t[None, :] <= t[:, None]), same & (sub_c < sub_r)], axis=0)
    bwd = jnp.concatenate([same & (t[None, :] >= t[:, None]), same & (sub_c > sub_r)], axis=0)
    masks = jnp.stack([fwd, bwd]).astype(BF16)
    r = jnp.arange(SUB * A_D)
    emat = ((r[:, None] // A_D) == (jnp.arange(LANE)[None, :] % SUB)).astype(BF16)
    return masks, emat


def _hgrn_dir(aq_ref, af_ref, ai_ref, fb, lb, mask_ref, emat_ref, st_ref, o_ref, need_out, rev):
    z = af_ref[0] + fb
    sp, sn = _sig_pair(z)
    k = (1.0 - lb) * sn
    lf2 = jnp.log2(lb + (1.0 - lb) * sp)
    lk2 = jnp.log2(k)
    v = ai_ref[0]

    hi = lf2.astype(BF16)
    r1 = lf2 - hi.astype(F32)
    mid = r1.astype(BF16)
    lo = (r1 - mid.astype(F32)).astype(BF16)
    cs = jnp.dot(mask_ref[0], jnp.concatenate([hi, mid, lo], axis=1), preferred_element_type=F32)
    cs = cs[:, :A_D] + cs[:, A_D:2 * A_D] + cs[:, 2 * A_D:]
    b = cs[:TOK]
    ent = cs[TOK:]
    c_all = lk2 - b

    last = [(c * CHUNK if rev else c * CHUNK + CHUNK - 1) for c in range(NCH)]
    tot_rows = jnp.concatenate([_rows(b, last[c], CHUNK) for c in range(NCH)], axis=0)
    khat = jnp.exp2(jnp.minimum(tot_rows + c_all, lk2)).astype(BF16)
    upd = [jnp.dot(v[c * CHUNK:(c + 1) * CHUNK].T.astype(BF16), khat[c * CHUNK:(c + 1) * CHUNK],
                   preferred_element_type=F32) for c in range(NCH)]

    order = range(NCH - 1, -1, -1) if rev else range(NCH)
    st = st_ref[...]
    st_in = [None] * NCH
    for c in order:
        st_in[c] = st
        st = st * jnp.exp2(b[last[c]:last[c] + 1]) + upd[c]
    st_ref[...] = st

    @pl.when(need_out)
    def _():
        q = _silu(aq_ref[0])
        vb = v.astype(BF16)
        qt = (q * jnp.exp2(b - ent)).astype(BF16)
        row = lax.broadcasted_iota(jnp.int32, (CHUNK, CHUNK), 0)
        col = lax.broadcasted_iota(jnp.int32, (CHUNK, CHUNK), 1)
        rs = jnp.right_shift(row, SUB.bit_length() - 1)
        cs_ = jnp.right_shift(col, SUB.bit_length() - 1)
        prev_blk = (cs_ > rs) if rev else (cs_ < rs)
        diag_blk = (rs == cs_) & ((col >= row) if rev else (col <= row))

        off = [[None] * NSUB for _ in range(NCH)]
        for i in range(NSUB):
            e_i = jnp.concatenate([_rows(ent, c * CHUNK + i * SUB, CHUNK) for c in range(NCH)], axis=0)
            kt = jnp.exp2(jnp.minimum(e_i + c_all, lk2)).astype(BF16)
            for c in range(NCH):
                r0 = c * CHUNK + i * SUB
                off[c][i] = lax.dot_general(qt[r0:r0 + SUB], kt[c * CHUNK:(c + 1) * CHUNK], _NT,
                                            preferred_element_type=F32)

        ps = []
        for sl in range(SUB):
            cref = jnp.concatenate([_rows(c_all, g * SUB + sl, SUB) for g in range(TOK // SUB)], axis=0)
            kref = jnp.concatenate([_rows(lk2, g * SUB + sl, SUB) for g in range(TOK // SUB)], axis=0)
            ps.append((q * jnp.exp2(jnp.minimum(b + cref, kref))).astype(BF16))
        acc = jnp.dot(jnp.concatenate(ps, axis=1), emat_ref[...], preferred_element_type=F32)

        qhat = (q * jnp.exp2(b)).astype(BF16)
        for c in range(NCH):
            sl_c = slice(c * CHUNK, (c + 1) * CHUNK)
            attn = (jnp.where(prev_blk, jnp.concatenate(off[c], axis=0), 0.0)
                    + jnp.where(diag_blk, acc[sl_c, :CHUNK], 0.0))
            o = jnp.dot(attn.astype(BF16), vb[sl_c], preferred_element_type=F32)
            o = o + lax.dot_general(qhat[sl_c], st_in[c].astype(BF16), _NT, preferred_element_type=F32)
            o_ref[0, sl_c, :] = o


def _hgrn_kernel(aqf, aff, aif, aqb, afb, aib, fb_ref, lb_ref, mask_ref, emat_ref, of_ref, ob_ref,
                 stf_ref, stb_ref):
    j = pl.program_id(2)

    @pl.when(j == 0)
    def _():
        stf_ref[...] = jnp.zeros_like(stf_ref)
        stb_ref[...] = jnp.zeros_like(stb_ref)

    need_out = j > 0
    _hgrn_dir(aqf, aff, aif, fb_ref[0], lb_ref[0], mask_ref.at[0:1], emat_ref, stf_ref, of_ref,
              need_out, False)
    _hgrn_dir(aqb, afb, aib, fb_ref[1], lb_ref[1], mask_ref.at[1:2], emat_ref, stb_ref, ob_ref,
              need_out, True)


def _hgrn_scan(p3, f_bias, lb, n_lat):
    Bn = p3.shape[0]
    A = A_HEADS
    masks, emat = _hgrn_consts()

    def feat(rev, colblk):
        return pl.BlockSpec((1, TOK, A_D), lambda b, h, j: (b, _tok_block(rev, j, n_lat), colblk(h)))

    def outp(rev):
        return pl.BlockSpec((1, TOK, A_D), lambda b, h, j: (b, _out_block(rev, j, n_lat), h))

    par = pl.BlockSpec((2, 1, A_D), lambda b, h, j: (0, 0, h))
    osh = jax.ShapeDtypeStruct((Bn, n_lat * TOK, A * A_D), F32)
    return pl.pallas_call(
        _hgrn_kernel,
        grid=(Bn, A, n_lat + 1),
        in_specs=[feat(False, lambda h: h), feat(False, lambda h: A + h), feat(False, lambda h: 3 * A + h),
                  feat(True, lambda h: h), feat(True, lambda h: 2 * A + h), feat(True, lambda h: 3 * A + h),
                  par, par,
                  pl.BlockSpec((2, 2 * TOK, TOK), lambda b, h, j: (0, 0, 0)),
                  pl.BlockSpec((SUB * A_D, LANE), lambda b, h, j: (0, 0))],
        out_specs=[outp(False), outp(True)],
        out_shape=[osh, osh],
        scratch_shapes=[pltpu.VMEM((A_D, A_D), F32), pltpu.VMEM((A_D, A_D), F32)],
        compiler_params=_cparams(("parallel", "parallel", "arbitrary")),
        name="hgrn2_scan",
    )(p3, p3, p3, p3, p3, p3, f_bias.reshape(2, 1, A * A_D), lb.reshape(2, 1, A * A_D), masks, emat)


def _mlstm_dir(bq_ref, bk_ref, bv_ref, gr_ref, gc_ref, cwq, cwk, first, last_, c_ref, n_ref, m_ref,
               o_ref, need_out, rev):
    def conv(u, cw):
        up = jnp.where(first, 0.0, pltpu.roll(u, 1, 0))
        dn = jnp.where(last_, 0.0, pltpu.roll(u, TOK - 1, 0))
        return _silu(cw[3:4] + up * cw[0:1] + u * cw[1:2] + dn * cw[2:3])

    q_all = conv(bq_ref[0], cwq)
    k_all = conv(bk_ref[0], cwk) * (B_DQK ** -0.5)
    seen, seen_t = _chunk_masks(rev)

    pre = []
    for c in range(NCH):
        sl_c = slice(c * CHUNK, (c + 1) * CHUNK)
        k = k_all[sl_c]
        vb = bv_ref[0, sl_c, :].astype(BF16)
        g_r = gr_ref[0, 0, 0, c]
        g_c = gc_ref[0, 0, 0, c]
        ii_r = g_r[0:1]
        lf_r = _log_sigmoid(g_r[1:2])
        ii_c = g_c[:, 0:1]
        lf_c = _log_sigmoid(g_c[:, 1:2])
        b_c = jnp.sum(jnp.where(seen, lf_r, 0.0), axis=1, keepdims=True)
        b_r = jnp.sum(jnp.where(seen_t, lf_c, 0.0), axis=0, keepdims=True)
        total = jnp.sum(lf_r, axis=1, keepdims=True)
        logs = total - b_c + ii_c
        ms = jnp.max(logs, axis=0, keepdims=True)
        kw = k * jnp.exp(logs - ms)
        upd = jnp.dot(kw.T.astype(BF16), vb, preferred_element_type=F32)
        nupd = jnp.sum(kw, axis=0, keepdims=True)
        pre.append((vb, ii_r, b_c, b_r, total, ms, upd, nupd))

    cmat, nvec, m = c_ref[...], n_ref[...], m_ref[:, 0:1]
    st_in = [None] * NCH
    for c in (range(NCH - 1, -1, -1) if rev else range(NCH)):
        _, _, _, _, total, ms, upd, nupd = pre[c]
        st_in[c] = (cmat, nvec, m)
        m_new = jnp.maximum(total + m, ms)
        dec = jnp.exp(total + m - m_new)
        sc = jnp.exp(ms - m_new)
        cmat = dec * cmat + sc * upd
        nvec = dec * nvec + sc * nupd
        m = m_new
    c_ref[...] = cmat
    n_ref[...] = nvec
    m_ref[...] = jnp.broadcast_to(m, m_ref.shape)

    @pl.when(need_out)
    def _():
        for c in range(NCH):
            sl_c = slice(c * CHUNK, (c + 1) * CHUNK)
            vb, ii_r, b_c, b_r, _, _, _, _ = pre[c]
            cm, nv, m0 = st_in[c]
            q = q_all[sl_c]
            qb = q.astype(BF16)
            kb = k_all[sl_c].astype(BF16)
            logw = jnp.where(seen, b_c - b_r + ii_r, -jnp.inf)
            mw = jnp.max(logw, axis=1, keepdims=True)
            qk = lax.dot_general(qb, kb, _NT, preferred_element_type=F32) * jnp.exp(logw - mw)
            num0 = jnp.dot(qk.astype(BF16), vb, preferred_element_type=F32)
            den0 = jnp.sum(qk, axis=1, keepdims=True)
            log_inter = b_c + m0
            m_t = jnp.maximum(mw, log_inter)
            r = jnp.exp(mw - m_t)
            a = jnp.exp(log_inter - m_t)
            num = r * num0 + a * jnp.dot(qb, cm.astype(BF16), preferred_element_type=F32)
            den = r * den0 + a * jnp.sum(q * nv, axis=1, keepdims=True)
            o_ref[0, sl_c, :] = num / jnp.maximum(jnp.abs(den), jnp.exp(-m_t))


def _mlstm_kernel(bqf, bkf, bvf, grf, gcf, bqb, bkb, bvb, grb, gcb, cwq_ref, cwk_ref, of_ref, ob_ref,
                  cf_ref, nf_ref, mf_ref, cb_ref, nb_ref, mb_ref):
    j = pl.program_id(2)

    @pl.when(j == 0)
    def _():
        for ref in (cf_ref, nf_ref, mf_ref, cb_ref, nb_ref, mb_ref):
            ref[...] = jnp.zeros_like(ref)

    t = lax.broadcasted_iota(jnp.int32, (TOK, 1), 0)
    row_mask = jnp.where(j == 0, TOK - 1, CHUNK - 1)
    pos = t & row_mask
    first = pos == 0
    last_ = pos == row_mask
    cwq = cwq_ref[:, 0, :]
    cwk = cwk_ref[:, 0, :]
    need_out = j > 0
    _mlstm_dir(bqf, bkf, bvf, grf, gcf, cwq, cwk, first, last_, cf_ref, nf_ref, mf_ref, of_ref,
               need_out, False)
    _mlstm_dir(bqb, bkb, bvb, grb, gcb, cwq, cwk, first, last_, cb_ref, nb_ref, mb_ref, ob_ref,
               need_out, True)


def _mlstm_scan(p3, g_rows, g_cols, conv4, n_lat):
    Bn = p3.shape[0]
    A, Bh = A_HEADS, B_HEADS
    q0 = 5 * A
    k0 = 5 * A + Bh
    v0 = (5 * A + 2 * Bh) // 2

    def specs(rev):
        d = int(rev)
        tb = lambda j: _tok_block(rev, j, n_lat)
        return [pl.BlockSpec((1, TOK, B_DQK), lambda b, h, j: (b, tb(j), q0 + h)),
                pl.BlockSpec((1, TOK, B_DQK), lambda b, h, j: (b, tb(j), k0 + h)),
                pl.BlockSpec((1, TOK, B_DV), lambda b, h, j: (b, tb(j), v0 + h)),
                pl.BlockSpec((1, 1, 1, NCH, 2, CHUNK), lambda b, h, j: (b, d, h, tb(j), 0, 0)),
                pl.BlockSpec((1, 1, 1, NCH, CHUNK, 2), lambda b, h, j: (b, d, h, tb(j), 0, 0))]

    def outp(rev):
        return pl.BlockSpec((1, TOK, B_DV), lambda b, h, j: (b, _out_block(rev, j, n_lat), h))

    osh = jax.ShapeDtypeStruct((Bn, n_lat * TOK, Bh * B_DV), F32)
    state = [pltpu.VMEM((B_DQK, B_DV), F32), pltpu.VMEM((1, B_DQK), F32), pltpu.VMEM((1, LANE), F32)]
    return pl.pallas_call(
        _mlstm_kernel,
        grid=(Bn, Bh, n_lat + 1),
        in_specs=specs(False) + specs(True) + [
            pl.BlockSpec((4, 1, B_DQK), lambda b, h, j: (0, 0, h)),
            pl.BlockSpec((4, 1, B_DQK), lambda b, h, j: (0, 0, Bh + h))],
        out_specs=[outp(False), outp(True)],
        out_shape=[osh, osh],
        scratch_shapes=state + state,
        compiler_params=_cparams(("parallel", "parallel", "arbitrary")),
        name="mlstm_scan",
    )(p3, p3, p3, g_rows, g_cols, p3, p3, p3, g_rows, g_cols, conv4, conv4)


def _readout_kernel(oaf_ref, oab_ref, obf_ref, obb_ref, ag_ref, bo_ref, na_ref, nb_ref, ya_ref, yb_ref):
    for h in range(A_HEADS):
        sl = slice(h * A_D, (h + 1) * A_D)
        o = oaf_ref[0, :, sl] + oab_ref[0, :, sl]
        ya_ref[0, :, sl] = (_rms(o, na_ref[:, sl]) * _silu(ag_ref[0, :, sl])).astype(ya_ref.dtype)
    for h in range(B_HEADS):
        sl = slice(h * B_DV, (h + 1) * B_DV)
        o = obf_ref[0, :, sl] + obb_ref[0, :, sl]
        yb_ref[0, :, sl] = (_rms(o, nb_ref[:, sl]) * _sigmoid(bo_ref[0, :, sl])).astype(yb_ref.dtype)


def _readout(oa, ob, p3, norm_a, norm_b, n_lat):
    Bn = p3.shape[0]
    T = n_lat * TOK
    wa = A_HEADS * A_D
    wb = B_HEADS * B_DV
    ag_blk = 4
    bo_blk = (5 * A_HEADS + 4 * B_HEADS) * LANE // wb
    sa = pl.BlockSpec((1, TOK, wa), lambda b, i: (b, i, 0))
    sb = pl.BlockSpec((1, TOK, wb), lambda b, i: (b, i, 0))
    return pl.pallas_call(
        _readout_kernel,
        grid=(Bn, n_lat),
        in_specs=[sa, sa, sb, sb,
                  pl.BlockSpec((1, TOK, wa), lambda b, i: (b, i, ag_blk)),
                  pl.BlockSpec((1, TOK, wb), lambda b, i: (b, i, bo_blk)),
                  pl.BlockSpec((1, wa), lambda b, i: (0, 0)),
                  pl.BlockSpec((1, wb), lambda b, i: (0, 0))],
        out_specs=[sa, sb],
        out_shape=[jax.ShapeDtypeStruct((Bn, T, wa), BF16),
                   jax.ShapeDtypeStruct((Bn, T, wb), BF16)],
        compiler_params=_cparams(("parallel", "parallel")),
        name="readout",
    )(oa[0], oa[1], ob[0], ob[1], p3, p3, norm_a.reshape(1, wa), norm_b.reshape(1, wb))


def _merge_kernel(ya_ref, yb_ref, wa_ref, wb_ref, ga_ref, gb_ref, o_ref):
    pa = jnp.dot(ya_ref[0], wa_ref[...], preferred_element_type=F32)
    pb = jnp.dot(yb_ref[0], wb_ref[...], preferred_element_type=F32)
    o_ref[0] = (_sigmoid(ga_ref[0]) * pa + _sigmoid(gb_ref[0]) * pb).astype(o_ref.dtype)


def _merge(ya, yb, wa, wb, p3, D):
    Bn, T, ka = ya.shape
    kb = yb.shape[2]
    tm = _pick(T, 1024)
    tn = _pick(D, 512)
    ga0 = (5 * A_HEADS + 6 * B_HEADS) * LANE // tn
    gb0 = ga0 + D // tn
    return pl.pallas_call(
        _merge_kernel,
        grid=(Bn, T // tm, D // tn),
        in_specs=[pl.BlockSpec((1, tm, ka), lambda b, i, j: (b, i, 0)),
                  pl.BlockSpec((1, tm, kb), lambda b, i, j: (b, i, 0)),
                  pl.BlockSpec((ka, tn), lambda b, i, j: (0, j)),
                  pl.BlockSpec((kb, tn), lambda b, i, j: (0, j)),
                  pl.BlockSpec((1, tm, tn), lambda b, i, j: (b, i, ga0 + j)),
                  pl.BlockSpec((1, tm, tn), lambda b, i, j: (b, i, gb0 + j))],
        out_specs=pl.BlockSpec((1, tm, tn), lambda b, i, j: (b, i, j)),
        out_shape=jax.ShapeDtypeStruct((Bn, T, D), BF16),
        compiler_params=_cparams(("parallel", "parallel", "arbitrary")),
        name="merge",
    )(ya, yb, wa, wb, p3, p3)


def _resid_router_kernel(x_ref, mix_ref, g1_ref, g2_ref, gate_ref, sh_ref, sc_ref, wr_ref,
                         h_ref, v_ref, aff_ref):
    h = x_ref[0] + gate_ref[0] * _rms(mix_ref[0], g1_ref[...])
    h_ref[0] = h
    v = _rms(h, g2_ref[...]) * (1.0 + sc_ref[0]) + sh_ref[0]
    v_ref[0] = v.astype(v_ref.dtype)
    logits = jnp.dot(v, wr_ref[...], precision=lax.Precision.HIGHEST, preferred_element_type=F32)
    lane = lax.broadcasted_iota(jnp.int32, logits.shape, 1)
    logits = jnp.where(lane < N_EXPERTS, logits, -jnp.inf)
    e = jnp.exp(logits - jnp.max(logits, axis=-1, keepdims=True))
    aff_ref[0] = e / jnp.sum(e, axis=-1, keepdims=True)


def _resid_router(x, mix, g1, g2, mod3, w_router_pad):
    Bn, T, D = x.shape
    mspec = lambda k: pl.BlockSpec((1, 1, D), lambda b, i: (b, 0, k))
    tok = pl.BlockSpec((1, TOK, D), lambda b, i: (b, i, 0))
    vec = pl.BlockSpec((1, D), lambda b, i: (0, 0))
    return pl.pallas_call(
        _resid_router_kernel,
        grid=(Bn, T // TOK),
        in_specs=[tok, tok, vec, vec, mspec(2), mspec(3), mspec(4),
                  pl.BlockSpec((D, LANE), lambda b, i: (0, 0))],
        out_specs=[tok, tok, pl.BlockSpec((1, TOK, LANE), lambda b, i: (b, i, 0))],
        out_shape=[jax.ShapeDtypeStruct((Bn, T, D), F32),
                   jax.ShapeDtypeStruct((Bn, T, D), BF16),
                   jax.ShapeDtypeStruct((Bn, T, LANE), F32)],
        compiler_params=_cparams(("parallel", "parallel")),
        name="resid_router",
    )(x, mix, g1.reshape(1, D), g2.reshape(1, D), mod3, mod3, mod3, w_router_pad)


def _rank_kernel(affc_ref, affr_ref, rank_ref, col_s, *, tt):
    e = pl.program_id(1)
    ti = pl.program_id(2)
    T = col_s.shape[0]

    @pl.when(ti == 0)
    def _():
        lane = lax.broadcasted_iota(jnp.int32, (T, LANE), 1)
        col_s[...] = jnp.sum(jnp.where(lane == e, affc_ref[0], 0.0), axis=1, keepdims=True)

    rowv = affr_ref[0, 0]
    t_idx = ti * tt + lax.broadcasted_iota(jnp.int32, (tt, tt), 1)
    s_loc = lax.broadcasted_iota(jnp.int32, (tt, tt), 0)

    def s_loop(si, acc):
        s0 = pl.multiple_of(si * tt, tt)
        cv = col_s[pl.ds(s0, tt), :]
        ahead = (cv > rowv) | ((cv == rowv) & (s_loc + s0 < t_idx))
        return acc + jnp.sum(jnp.where(ahead, 1.0, 0.0), axis=0, keepdims=True)

    rank_ref[0, 0] = lax.fori_loop(0, T // tt, s_loop, jnp.zeros((1, tt), F32))


def _rank(aff, aff_rows):
    Bn, T, _ = aff.shape
    tt = _pick(T, 512)
    return pl.pallas_call(
        functools.partial(_rank_kernel, tt=tt),
        grid=(Bn, N_EXPERTS, T // tt),
        in_specs=[pl.BlockSpec((1, T, LANE), lambda b, e, t: (b, 0, 0)),
                  pl.BlockSpec((1, 1, 1, tt), lambda b, e, t: (b, e, 0, t))],
        out_specs=pl.BlockSpec((1, 1, 1, tt), lambda b, e, t: (b, e, 0, t)),
        out_shape=jax.ShapeDtypeStruct((Bn, N_EXPERTS, 1, T), F32),
        scratch_shapes=[pltpu.VMEM((T, 1), F32)],
        compiler_params=_cparams(("parallel", "parallel", "arbitrary")),
        name="ec_rank",
    )(aff, aff_rows)


def _gather_kernel(rank_ref, v_ref, o_ref, acc_s, *, cap, tk):
    T = v_ref.shape[1]
    slot = lax.broadcasted_iota(jnp.int32, (cap, tk), 0).astype(F32)

    def k_loop(kc, carry):
        k0 = pl.multiple_of(kc * tk, tk)
        oh = jnp.where(rank_ref[0, 0, :, pl.ds(k0, tk)] == slot, 1.0, 0.0).astype(BF16)
        part = jnp.dot(oh, v_ref[0, pl.ds(k0, tk), :], preferred_element_type=F32)

        @pl.when(kc == 0)
        def _():
            acc_s[...] = part

        @pl.when(kc > 0)
        def _():
            acc_s[...] += part
        return carry

    lax.fori_loop(0, T // tk, k_loop, 0)
    o_ref[0, 0] = acc_s[...].astype(o_ref.dtype)


def _gather(rank, v, cap):
    Bn, T, D = v.shape
    ct = _pick(D, 1024)
    tk = _pick(T, 512)
    return pl.pallas_call(
        functools.partial(_gather_kernel, cap=cap, tk=tk),
        grid=(Bn, D // ct, N_EXPERTS),
        in_specs=[pl.BlockSpec((1, 1, 1, T), lambda b, c, e: (b, e, 0, 0)),
                  pl.BlockSpec((1, T, ct), lambda b, c, e: (b, 0, c))],
        out_specs=pl.BlockSpec((1, 1, cap, ct), lambda b, c, e: (e, b, 0, c)),
        out_shape=jax.ShapeDtypeStruct((N_EXPERTS, Bn, cap, D), BF16),
        scratch_shapes=[pltpu.VMEM((cap, ct), F32)],
        compiler_params=_cparams(("parallel", "parallel", "arbitrary")),
        name="ec_gather",
    )(rank, v)


def _ffn1_kernel(x_ref, wg_ref, wu_ref, o_ref):
    x = x_ref[0]
    g = jnp.dot(x, wg_ref[0].astype(BF16), preferred_element_type=F32)
    u = jnp.dot(x, wu_ref[0].astype(BF16), preferred_element_type=F32)
    o_ref[0] = (_silu(g) * u).astype(o_ref.dtype)


def _ffn1(xg, wg, wu):
    E, M, D = xg.shape
    F = wg.shape[2]
    tm = _pick(M, 1024)
    tn = _pick(F, 256)
    return pl.pallas_call(
        _ffn1_kernel,
        grid=(E, M // tm, F // tn),
        in_specs=[pl.BlockSpec((1, tm, D), lambda e, i, j: (e, i, 0)),
                  pl.BlockSpec((1, D, tn), lambda e, i, j: (e, 0, j)),
                  pl.BlockSpec((1, D, tn), lambda e, i, j: (e, 0, j))],
        out_specs=pl.BlockSpec((1, tm, tn), lambda e, i, j: (e, i, j)),
        out_shape=jax.ShapeDtypeStruct((E, M, F), BF16),
        compiler_params=_cparams(("parallel", "parallel", "arbitrary")),
        name="ec_ffn_up",
    )(xg, wg, wu)


def _ffn2_kernel(h_ref, wd_ref, o_ref):
    o_ref[0] = jnp.dot(h_ref[0], wd_ref[0].astype(BF16),
                       preferred_element_type=F32).astype(o_ref.dtype)


def _ffn2(hid, wd):
    E, M, F = hid.shape
    D = wd.shape[2]
    tn = _pick(D, 512)
    return pl.pallas_call(
        _ffn2_kernel,
        grid=(E, D // tn),
        in_specs=[pl.BlockSpec((1, M, F), lambda e, j: (e, 0, 0)),
                  pl.BlockSpec((1, F, tn), lambda e, j: (e, 0, j))],
        out_specs=pl.BlockSpec((1, M, tn), lambda e, j: (e, 0, j)),
        out_shape=jax.ShapeDtypeStruct((E, M, D), BF16),
        compiler_params=_cparams(("parallel", "arbitrary")),
        name="ec_ffn_down",
    )(hid, wd)


def _scatter_kernel(rank_ref, aff_ref, y_ref, o_ref, *, cap):
    e = pl.program_id(3)
    tq = o_ref.shape[1]

    @pl.when(e == 0)
    def _():
        o_ref[...] = jnp.zeros_like(o_ref)

    rk = jnp.broadcast_to(rank_ref[0, 0], (LANE, tq)).T
    af = jnp.broadcast_to(aff_ref[0, 0], (LANE, tq)).T
    reps = pl.cdiv(cap, LANE)
    rk = jnp.concatenate([rk] * reps, axis=1)[:, :cap]
    af = jnp.concatenate([af] * reps, axis=1)[:, :cap]
    slot = lax.broadcasted_iota(jnp.int32, (tq, cap), 1).astype(F32)
    ohw = jnp.where(rk == slot, af, 0.0).astype(BF16)
    o_ref[0] += jnp.dot(ohw, y_ref[0, 0], preferred_element_type=F32)


def _scatter(rank, aff_rows, y, cap):
    E, Bn, _, D = y.shape
    T = rank.shape[3]
    tq = _pick(T, 1024)
    ct = _pick(D, 1024)
    return pl.pallas_call(
        functools.partial(_scatter_kernel, cap=cap),
        grid=(Bn, T // tq, D // ct, E),
        in_specs=[pl.BlockSpec((1, 1, 1, tq), lambda b, t, c, e: (b, e, 0, t)),
                  pl.BlockSpec((1, 1, 1, tq), lambda b, t, c, e: (b, e, 0, t)),
                  pl.BlockSpec((1, 1, cap, ct), lambda b, t, c, e: (e, b, 0, c))],
        out_specs=pl.BlockSpec((1, tq, ct), lambda b, t, c, e: (b, t, c)),
        out_shape=jax.ShapeDtypeStruct((Bn, T, D), F32),
        compiler_params=_cparams(("parallel", "parallel", "parallel", "arbitrary")),
        name="ec_scatter",
    )(rank, aff_rows, y)


def _final_kernel(h_ref, f_ref, g_ref, gate_ref, o_ref):
    o_ref[0] = h_ref[0] + gate_ref[0] * _rms(f_ref[0], g_ref[...])


def _final(h, f, g3, mod3):
    Bn, T, D = h.shape
    tok = pl.BlockSpec((1, TOK, D), lambda b, i: (b, i, 0))
    return pl.pallas_call(
        _final_kernel,
        grid=(Bn, T // TOK),
        in_specs=[tok, tok, pl.BlockSpec((1, D), lambda b, i: (0, 0)),
                  pl.BlockSpec((1, 1, D), lambda b, i: (b, 0, 5))],
        out_specs=tok,
        out_shape=jax.ShapeDtypeStruct((Bn, T, D), F32),
        compiler_params=_cparams(("parallel", "parallel")),
        name="final_resid",
    )(h, f, g3.reshape(1, D), mod3)


def _layer(h_lat, ctx, c8, l, lb_l, w_ada, b_ada, g_norm, w_in, hgrn_f_bias, hgrn_norm,
           mlstm_conv_w, mlstm_conv_b, mlstm_gate_b, mlstm_norm, w_branch_a, w_branch_b, w_out,
           w_router, w_expert_gate, w_expert_up, w_expert_down):
    Bn, T, D = h_lat.shape
    n_lat = T // TOK
    n_tok = T + TOK
    A, Bh = A_HEADS, B_HEADS
    n_main = (5 * A + 6 * Bh) * LANE + 2 * D
    g0 = (5 * A + 6 * Bh) * LANE

    mod = _modulation(c8, w_ada[l], b_ada[l])
    mod3 = mod.reshape(8, 1, N_MOD * D)

    u = _prenorm(h_lat, ctx, g_norm[l, 0], mod3)

    w_l = w_in[l]
    w_main = jnp.concatenate([w_l[:, :g0], w_l[:, g0 + 4 * Bh:]], axis=1).astype(BF16)
    w_gate = jnp.pad(w_l[:, g0:g0 + 4 * Bh], ((0, 0), (0, LANE - 4 * Bh))).astype(BF16)
    gate_bias = jnp.pad(mlstm_gate_b[l].reshape(4 * Bh), (0, LANE - 4 * Bh))
    u2 = u.reshape(Bn * n_tok, D)
    p3 = _matmul(u2, w_main, F32, name="in_proj").reshape(Bn, n_tok, n_main)
    gates = _matmul(u2, w_gate, F32, tn=LANE, bias=gate_bias, name="in_proj_gates")

    g6 = gates[:, :4 * Bh].reshape(Bn, n_tok // CHUNK, CHUNK, 2, 2, Bh)
    g_rows = g6.transpose(0, 3, 5, 1, 4, 2)
    g_cols = g6.transpose(0, 3, 5, 1, 2, 4)
    conv4 = jnp.concatenate([mlstm_conv_w[l], mlstm_conv_b[l][None]], axis=0)
    conv4 = conv4.reshape(4, 1, 2 * Bh * B_DQK)

    oa = _hgrn_scan(p3, hgrn_f_bias[l], lb_l, n_lat)
    ob = _mlstm_scan(p3, g_rows, g_cols, conv4, n_lat)
    ya, yb = _readout(oa, ob, p3, hgrn_norm[l], mlstm_norm[l], n_lat)
    merged = _merge(ya, yb, w_branch_a[l].astype(BF16), w_branch_b[l].astype(BF16), p3, D)
    mix = _matmul(merged.reshape(Bn * T, D), w_out[l].astype(BF16), F32, name="out_proj")

    w_router_pad = jnp.pad(w_router[l], ((0, 0), (0, LANE - N_EXPERTS)))
    h_lat, v_lat, aff = _resid_router(h_lat, mix.reshape(Bn, T, D), g_norm[l, 1], g_norm[l, 2],
                                      mod3, w_router_pad)

    cap = CAPACITY * T // N_EXPERTS
    aff_rows = aff[:, :, :N_EXPERTS].transpose(0, 2, 1).reshape(Bn, N_EXPERTS, 1, T)
    rank = _rank(aff, aff_rows)
    xg = _gather(rank, v_lat, cap)
    hid = _ffn1(xg.reshape(N_EXPERTS, Bn * cap, D), w_expert_gate[l], w_expert_up[l])
    y = _ffn2(hid, w_expert_down[l]).reshape(N_EXPERTS, Bn, cap, D)
    f_lat = _scatter(rank, aff_rows, y, cap)
    return _final(h_lat, f_lat, g_norm[l, 3], mod3)


def kernel(x, c, ctx, c_ctx, w_ada, b_ada, g_norm, w_in, hgrn_f_bias, hgrn_lb, hgrn_norm,
           mlstm_conv_w, mlstm_conv_b, mlstm_gate_b, mlstm_norm, w_branch_a, w_branch_b, w_out,
           w_router, w_expert_gate, w_expert_up, w_expert_down):
    Bn, T, D = x.shape
    depth = w_ada.shape[0]
    assert depth == 1, "context outputs are only produced for the state hand-off (single layer)"
    assert ctx.shape[1] == TOK and T % TOK == 0 and Bn < 8
    lb_all = jnp.cumsum(jax.nn.softmax(hgrn_lb.astype(F32), axis=1), axis=1)
    c8 = jnp.zeros((8, D), F32).at[:Bn].set(c).at[Bn].set(c_ctx)
    h_lat = x
    for l in range(depth):
        h_lat = _layer(h_lat, ctx, c8, l, lb_all[:, l], w_ada, b_ada, g_norm, w_in, hgrn_f_bias,
                       hgrn_norm, mlstm_conv_w, mlstm_conv_b, mlstm_gate_b, mlstm_norm,
                       w_branch_a, w_branch_b, w_out, w_router, w_expert_gate, w_expert_up,
                       w_expert_down)
    return h_lat.astype(x.dtype)
```

```python
import functools

import jax
import jax.numpy as jnp
from jax import lax
from jax.experimental import pallas as pl
from jax.experimental.pallas import tpu as pltpu

F32 = jnp.float32
BF16 = jnp.bfloat16

EPS = 1e-6
N_MOD = 6
A_HEADS = 16
A_D = 128
B_HEADS = 8
B_DQK = 128
B_DV = 256
N_EXPERTS = 16
CAPACITY = 2
CHUNK = 64
SUB = 16
TOK = 256
NCH = TOK // CHUNK
NSUB = CHUNK // SUB
LANE = 128
VMEM_LIMIT = 56 * 1024 * 1024

_NT = (((1,), (1,)), ((), ()))


def _pick(n, pref):
    t = min(n, pref)
    while n % t:
        t //= 2
    return t


def _cparams(sem):
    return pltpu.CompilerParams(dimension_semantics=sem, vmem_limit_bytes=VMEM_LIMIT)


def _silu(x):
    return x / (1.0 + jnp.exp(-x))


def _sigmoid(x):
    return 1.0 / (1.0 + jnp.exp(-x))


def _sig_pair(z):
    t = jnp.exp(-jnp.abs(z))
    r = 1.0 / (1.0 + t)
    tr = t * r
    pos = z >= 0
    return jnp.where(pos, r, tr), jnp.where(pos, tr, r)


def _log_sigmoid(x):
    return jnp.minimum(x, 0.0) - jnp.log(1.0 + jnp.exp(-jnp.abs(x)))


def _rms(xf, w):
    return xf * lax.rsqrt(jnp.mean(xf * xf, axis=-1, keepdims=True) + EPS) * w


def _rows(x, r, n):
    return jnp.broadcast_to(x[r:r + 1], (n, x.shape[1]))


def _mod_kernel(c_ref, w_ref, b_ref, o_ref):
    a = _silu(c_ref[...]).astype(BF16)
    o_ref[...] = jnp.dot(a, w_ref[...].astype(BF16), preferred_element_type=F32) + b_ref[...]


def _modulation(c8, w, b):
    D, N = w.shape
    tn = _pick(N, 512)
    return pl.pallas_call(
        _mod_kernel,
        grid=(N // tn,),
        in_specs=[pl.BlockSpec((8, D), lambda j: (0, 0)),
                  pl.BlockSpec((D, tn), lambda j: (0, j)),
                  pl.BlockSpec((1, tn), lambda j: (0, j))],
        out_specs=pl.BlockSpec((8, tn), lambda j: (0, j)),
        out_shape=jax.ShapeDtypeStruct((8, N), F32),
        compiler_params=_cparams(("parallel",)),
        name="adaln_mod",
    )(c8, w, b.reshape(1, N))


def _prenorm_kernel(x_ref, ctx_ref, g_ref, sh_ref, sc_ref, o_ref, *, n_lat):
    i = pl.program_id(1)
    g = g_ref[...]
    sh = sh_ref[0]
    sc = sc_ref[0]

    @pl.when(i < n_lat)
    def _():
        o_ref[0] = (_rms(x_ref[0], g) * (1.0 + sc) + sh).astype(o_ref.dtype)

    @pl.when(i == n_lat)
    def _():
        o_ref[0] = (_rms(ctx_ref[0], g) * (1.0 + sc) + sh).astype(o_ref.dtype)


def _prenorm(x, ctx, g, mod3):
    Bn, T, D = x.shape
    n_lat = T // TOK
    row = lambda b, i: jnp.where(i == n_lat, Bn, b)
    return pl.pallas_call(
        functools.partial(_prenorm_kernel, n_lat=n_lat),
        grid=(Bn, n_lat + 1),
        in_specs=[pl.BlockSpec((1, TOK, D), lambda b, i: (b, jnp.minimum(i, n_lat - 1), 0)),
                  pl.BlockSpec((1, TOK, D), lambda b, i: (b, 0, 0)),
                  pl.BlockSpec((1, D), lambda b, i: (0, 0)),
                  pl.BlockSpec((1, 1, D), lambda b, i: (row(b, i), 0, 0)),
                  pl.BlockSpec((1, 1, D), lambda b, i: (row(b, i), 0, 1))],
        out_specs=pl.BlockSpec((1, TOK, D), lambda b, i: (b, i, 0)),
        out_shape=jax.ShapeDtypeStruct((Bn, T + TOK, D), BF16),
        compiler_params=_cparams(("parallel", "arbitrary")),
        name="prenorm",
    )(x, ctx, g.reshape(1, D), mod3, mod3)


def _mm_kernel(a_ref, b_ref, o_ref):
    o_ref[...] = jnp.dot(a_ref[...], b_ref[...].astype(BF16),
                         preferred_element_type=F32).astype(o_ref.dtype)


def _mm_bias_kernel(a_ref, b_ref, bias_ref, o_ref):
    o_ref[...] = (jnp.dot(a_ref[...], b_ref[...].astype(BF16), preferred_element_type=F32)
                  + bias_ref[...]).astype(o_ref.dtype)


def _matmul(a, b, out_dtype, tm=1024, tn=512, bias=None, name="matmul"):
    M, K = a.shape
    N = b.shape[1]
    tm = _pick(M, tm)
    tn = _pick(N, tn)
    in_specs = [pl.BlockSpec((tm, K), lambda i, j: (i, 0)),
                pl.BlockSpec((K, tn), lambda i, j: (0, j))]
    args = [a, b]
    kern = _mm_kernel
    if bias is not None:
        in_specs.append(pl.BlockSpec((1, tn), lambda i, j: (0, j)))
        args.append(bias.reshape(1, N))
        kern = _mm_bias_kernel
    return pl.pallas_call(
        kern,
        grid=(M // tm, N // tn),
        in_specs=in_specs,
        out_specs=pl.BlockSpec((tm, tn), lambda i, j: (i, j)),
        out_shape=jax.ShapeDtypeStruct((M, N), out_dtype),
        compiler_params=_cparams(("parallel", "arbitrary")),
        name=name,
    )(*args)


def _tok_block(rev, j, n_lat):
    lat = n_lat - j if rev else j - 1
    return jnp.where(j == 0, n_lat, lat)


def _out_block(rev, j, n_lat):
    return jnp.clip(n_lat - j if rev else j - 1, 0, n_lat - 1)


def _chunk_masks(rev):
    row = lax.broadcasted_iota(jnp.int32, (CHUNK, CHUNK), 0)
    col = lax.broadcasted_iota(jnp.int32, (CHUNK, CHUNK), 1)
    return (col >= row, col <= row) if rev else (col <= row, col >= row)


def _hgrn_consts():
    t = jnp.arange(TOK)
    same = (t[:, None] // CHUNK) == (t[None, :] // CHUNK)
    sub_r, sub_c = t[:, None] // SUB, t[None, :] // SUB
    fwd = jnp.concatenate([same & (t[None, :] <= t[:, None]), same & (sub_c < sub_r)], axis=0)
    bwd = jnp.concatenate([same & (t[None, :] >= t[:, None]), same & (sub_c > sub_r)], axis=0)
    masks = jnp.stack([fwd, bwd]).astype(BF16)
    r = jnp.arange(SUB * A_D)
    emat = ((r[:, None] // A_D) == (jnp.arange(LANE)[None, :] % SUB)).astype(BF16)
    return masks, emat


def _hgrn_dir(aq_ref, af_ref, ai_ref, fb, lb, mask_ref, emat_ref, st_ref, o_ref, need_out, rev):
    z = af_ref[0] + fb
    sp, sn = _sig_pair(z)
    k = (1.0 - lb) * sn
    lf2 = jnp.log2(lb + (1.0 - lb) * sp)
    lk2 = jnp.log2(k)
    v = ai_ref[0]

    hi = lf2.astype(BF16)
    r1 = lf2 - hi.astype(F32)
    mid = r1.astype(BF16)
    lo = (r1 - mid.astype(F32)).astype(BF16)
    cs = jnp.dot(mask_ref[0], jnp.concatenate([hi, mid, lo], axis=1), preferred_element_type=F32)
    cs = cs[:, :A_D] + cs[:, A_D:2 * A_D] + cs[:, 2 * A_D:]
    b = cs[:TOK]
    ent = cs[TOK:]
    c_all = lk2 - b

    last = [(c * CHUNK if rev else c * CHUNK + CHUNK - 1) for c in range(NCH)]
    tot_rows = jnp.concatenate([_rows(b, last[c], CHUNK) for c in range(NCH)], axis=0)
    khat = jnp.exp2(jnp.minimum(tot_rows + c_all, lk2)).astype(BF16)
    upd = [jnp.dot(v[c * CHUNK:(c + 1) * CHUNK].T.astype(BF16), khat[c * CHUNK:(c + 1) * CHUNK],
                   preferred_element_type=F32) for c in range(NCH)]

    order = range(NCH - 1, -1, -1) if rev else range(NCH)
    st = st_ref[...]
    st_in = [None] * NCH
    for c in order:
        st_in[c] = st
        st = st * jnp.exp2(b[last[c]:last[c] + 1]) + upd[c]
    st_ref[...] = st

    @pl.when(need_out)
    def _():
        q = _silu(aq_ref[0])
        vb = v.astype(BF16)
        qt = (q * jnp.exp2(b - ent)).astype(BF16)
        row = lax.broadcasted_iota(jnp.int32, (CHUNK, CHUNK), 0)
        col = lax.broadcasted_iota(jnp.int32, (CHUNK, CHUNK), 1)
        rs = jnp.right_shift(row, SUB.bit_length() - 1)
        cs_ = jnp.right_shift(col, SUB.bit_length() - 1)
        prev_blk = (cs_ > rs) if rev else (cs_ < rs)
        diag_blk = (rs == cs_) & ((col >= row) if rev else (col <= row))

        off = [[None] * NSUB for _ in range(NCH)]
        for i in range(NSUB):
            e_i = jnp.concatenate([_rows(ent, c * CHUNK + i * SUB, CHUNK) for c in range(NCH)], axis=0)
            kt = jnp.exp2(jnp.minimum(e_i + c_all, lk2)).astype(BF16)
            for c in range(NCH):
                r0 = c * CHUNK + i * SUB
                off[c][i] = lax.dot_general(qt[r0:r0 + SUB], kt[c * CHUNK:(c + 1) * CHUNK], _NT,
                                            preferred_element_type=F32)

        ps = []
        for sl in range(SUB):
            cref = jnp.concatenate([_rows(c_all, g * SUB + sl, SUB) for g in range(TOK // SUB)], axis=0)
            kref = jnp.concatenate([_rows(lk2, g * SUB + sl, SUB) for g in range(TOK // SUB)], axis=0)
            ps.append((q * jnp.exp2(jnp.minimum(b + cref, kref))).astype(BF16))
        acc = jnp.dot(jnp.concatenate(ps, axis=1), emat_ref[...], preferred_element_type=F32)

        qhat = (q * jnp.exp2(b)).astype(BF16)
        for c in range(NCH):
            sl_c = slice(c * CHUNK, (c + 1) * CHUNK)
            attn = (jnp.where(prev_blk, jnp.concatenate(off[c], axis=0), 0.0)
                    + jnp.where(diag_blk, acc[sl_c, :CHUNK], 0.0))
            o = jnp.dot(attn.astype(BF16), vb[sl_c], preferred_element_type=F32)
            o = o + lax.dot_general(qhat[sl_c], st_in[c].astype(BF16), _NT, preferred_element_type=F32)
            o_ref[0, sl_c, :] = o


def _hgrn_kernel(aqf, aff, aif, aqb, afb, aib, fb_ref, lb_ref, mask_ref, emat_ref, of_ref, ob_ref,
                 stf_ref, stb_ref):
    j = pl.program_id(2)

    @pl.when(j == 0)
    def _():
        stf_ref[...] = jnp.zeros_like(stf_ref)
        stb_ref[...] = jnp.zeros_like(stb_ref)

    need_out = j > 0
    _hgrn_dir(aqf, aff, aif, fb_ref[0], lb_ref[0], mask_ref.at[0:1], emat_ref, stf_ref, of_ref,
              need_out, False)
    _hgrn_dir(aqb, afb, aib, fb_ref[1], lb_ref[1], mask_ref.at[1:2], emat_ref, stb_ref, ob_ref,
              need_out, True)


def _hgrn_scan(p3, f_bias, lb, n_lat):
    Bn = p3.shape[0]
    A = A_HEADS
    masks, emat = _hgrn_consts()

    def feat(rev, colblk):
        return pl.BlockSpec((1, TOK, A_D), lambda b, h, j: (b, _tok_block(rev, j, n_lat), colblk(h)))

    def outp(rev):
        return pl.BlockSpec((1, TOK, A_D), lambda b, h, j: (b, _out_block(rev, j, n_lat), h))

    par = pl.BlockSpec((2, 1, A_D), lambda b, h, j: (0, 0, h))
    osh = jax.ShapeDtypeStruct((Bn, n_lat * TOK, A * A_D), F32)
    return pl.pallas_call(
        _hgrn_kernel,
        grid=(Bn, A, n_lat + 1),
        in_specs=[feat(False, lambda h: h), feat(False, lambda h: A + h), feat(False, lambda h: 3 * A + h),
                  feat(True, lambda h: h), feat(True, lambda h: 2 * A + h), feat(True, lambda h: 3 * A + h),
                  par, par,
                  pl.BlockSpec((2, 2 * TOK, TOK), lambda b, h, j: (0, 0, 0)),
                  pl.BlockSpec((SUB * A_D, LANE), lambda b, h, j: (0, 0))],
        out_specs=[outp(False), outp(True)],
        out_shape=[osh, osh],
        scratch_shapes=[pltpu.VMEM((A_D, A_D), F32), pltpu.VMEM((A_D, A_D), F32)],
        compiler_params=_cparams(("parallel", "parallel", "arbitrary")),
        name="hgrn2_scan",
    )(p3, p3, p3, p3, p3, p3, f_bias.reshape(2, 1, A * A_D), lb.reshape(2, 1, A * A_D), masks, emat)


def _mlstm_dir(bq_ref, bk_ref, bv_ref, gr_ref, gc_ref, cwq, cwk, first, last_, c_ref, n_ref, m_ref,
               o_ref, need_out, rev):
    def conv(u, cw):
        up = jnp.where(first, 0.0, pltpu.roll(u, 1, 0))
        dn = jnp.where(last_, 0.0, pltpu.roll(u, TOK - 1, 0))
        return _silu(cw[3:4] + up * cw[0:1] + u * cw[1:2] + dn * cw[2:3])

    q_all = conv(bq_ref[0], cwq)
    k_all = conv(bk_ref[0], cwk) * (B_DQK ** -0.5)
    seen, seen_t = _chunk_masks(rev)

    pre = []
    for c in range(NCH):
        sl_c = slice(c * CHUNK, (c + 1) * CHUNK)
        k = k_all[sl_c]
        vb = bv_ref[0, sl_c, :].astype(BF16)
        g_r = gr_ref[0, 0, 0, c]
        g_c = gc_ref[0, 0, 0, c]
        ii_r = g_r[0:1]
        lf_r = _log_sigmoid(g_r[1:2])
        ii_c = g_c[:, 0:1]
        lf_c = _log_sigmoid(g_c[:, 1:2])
        b_c = jnp.sum(jnp.where(seen, lf_r, 0.0), axis=1, keepdims=True)
        b_r = jnp.sum(jnp.where(seen_t, lf_c, 0.0), axis=0, keepdims=True)
        total = jnp.sum(lf_r, axis=1, keepdims=True)
        logs = total - b_c + ii_c
        ms = jnp.max(logs, axis=0, keepdims=True)
        kw = k * jnp.exp(logs - ms)
        upd = jnp.dot(kw.T.astype(BF16), vb, preferred_element_type=F32)
        nupd = jnp.sum(kw, axis=0, keepdims=True)
        pre.append((vb, ii_r, b_c, b_r, total, ms, upd, nupd))

    cmat, nvec, m = c_ref[...], n_ref[...], m_ref[:, 0:1]
    st_in = [None] * NCH
    for c in (range(NCH - 1, -1, -1) if rev else range(NCH)):
        _, _, _, _, total, ms, upd, nupd = pre[c]
        st_in[c] = (cmat, nvec, m)
        m_new = jnp.maximum(total + m, ms)
        dec = jnp.exp(total + m - m_new)
        sc = jnp.exp(ms - m_new)
        cmat = dec * cmat + sc * upd
        nvec = dec * nvec + sc * nupd
        m = m_new
    c_ref[...] = cmat
    n_ref[...] = nvec
    m_ref[...] = jnp.broadcast_to(m, m_ref.shape)

    @pl.when(need_out)
    def _():
        for c in range(NCH):
            sl_c = slice(c * CHUNK, (c + 1) * CHUNK)
            vb, ii_r, b_c, b_r, _, _, _, _ = pre[c]
            cm, nv, m0 = st_in[c]
            q = q_all[sl_c]
            qb = q.astype(BF16)
            kb = k_all[sl_c].astype(BF16)
            logw = jnp.where(seen, b_c - b_r + ii_r, -jnp.inf)
            mw = jnp.max(logw, axis=1, keepdims=True)
            qk = lax.dot_general(qb, kb, _NT, preferred_element_type=F32) * jnp.exp(logw - mw)
            num0 = jnp.dot(qk.astype(BF16), vb, preferred_element_type=F32)
            den0 = jnp.sum(qk, axis=1, keepdims=True)
            log_inter = b_c + m0
            m_t = jnp.maximum(mw, log_inter)
            r = jnp.exp(mw - m_t)
            a = jnp.exp(log_inter - m_t)
            num = r * num0 + a * jnp.dot(qb, cm.astype(BF16), preferred_element_type=F32)
            den = r * den0 + a * jnp.sum(q * nv, axis=1, keepdims=True)
            o_ref[0, sl_c, :] = num / jnp.maximum(jnp.abs(den), jnp.exp(-m_t))


def _mlstm_kernel(bqf, bkf, bvf, grf, gcf, bqb, bkb, bvb, grb, gcb, cwq_ref, cwk_ref, of_ref, ob_ref,
                  cf_ref, nf_ref, mf_ref, cb_ref, nb_ref, mb_ref):
    j = pl.program_id(2)

    @pl.when(j == 0)
    def _():
        for ref in (cf_ref, nf_ref, mf_ref, cb_ref, nb_ref, mb_ref):
            ref[...] = jnp.zeros_like(ref)

    t = lax.broadcasted_iota(jnp.int32, (TOK, 1), 0)
    row_mask = jnp.where(j == 0, TOK - 1, CHUNK - 1)
    pos = t & row_mask
    first = pos == 0
    last_ = pos == row_mask
    cwq = cwq_ref[:, 0, :]
    cwk = cwk_ref[:, 0, :]
    need_out = j > 0
    _mlstm_dir(bqf, bkf, bvf, grf, gcf, cwq, cwk, first, last_, cf_ref, nf_ref, mf_ref, of_ref,
               need_out, False)
    _mlstm_dir(bqb, bkb, bvb, grb, gcb, cwq, cwk, first, last_, cb_ref, nb_ref, mb_ref, ob_ref,
               need_out, True)


def _mlstm_scan(p3, g_rows, g_cols, conv4, n_lat):
    Bn = p3.shape[0]
    A, Bh = A_HEADS, B_HEADS
    q0 = 5 * A
    k0 = 5 * A + Bh
    v0 = (5 * A + 2 * Bh) // 2

    def specs(rev):
        d = int(rev)
        tb = lambda j: _tok_block(rev, j, n_lat)
        return [pl.BlockSpec((1, TOK, B_DQK), lambda b, h, j: (b, tb(j), q0 + h)),
                pl.BlockSpec((1, TOK, B_DQK), lambda b, h, j: (b, tb(j), k0 + h)),
                pl.BlockSpec((1, TOK, B_DV), lambda b, h, j: (b, tb(j), v0 + h)),
                pl.BlockSpec((1, 1, 1, NCH, 2, CHUNK), lambda b, h, j: (b, d, h, tb(j), 0, 0)),
                pl.BlockSpec((1, 1, 1, NCH, CHUNK, 2), lambda b, h, j: (b, d, h, tb(j), 0, 0))]

    def outp(rev):
        return pl.BlockSpec((1, TOK, B_DV), lambda b, h, j: (b, _out_block(rev, j, n_lat), h))

    osh = jax.ShapeDtypeStruct((Bn, n_lat * TOK, Bh * B_DV), F32)
    state = [pltpu.VMEM((B_DQK, B_DV), F32), pltpu.VMEM((1, B_DQK), F32), pltpu.VMEM((1, LANE), F32)]
    return pl.pallas_call(
        _mlstm_kernel,
        grid=(Bn, Bh, n_lat + 1),
        in_specs=specs(False) + specs(True) + [
            pl.BlockSpec((4, 1, B_DQK), lambda b, h, j: (0, 0, h)),
            pl.BlockSpec((4, 1, B_DQK), lambda b, h, j: (0, 0, Bh + h))],
        out_specs=[outp(False), outp(True)],
        out_shape=[osh, osh],
        scratch_shapes=state + state,
        compiler_params=_cparams(("parallel", "parallel", "arbitrary")),
        name="mlstm_scan",
    )(p3, p3, p3, g_rows, g_cols, p3, p3, p3, g_rows, g_cols, conv4, conv4)


def _readout_kernel(oaf_ref, oab_ref, obf_ref, obb_ref, ag_ref, bo_ref, na_ref, nb_ref, ya_ref, yb_ref):
    for h in range(A_HEADS):
        sl = slice(h * A_D, (h + 1) * A_D)
        o = oaf_ref[0, :, sl] + oab_ref[0, :, sl]
        ya_ref[0, :, sl] = (_rms(o, na_ref[:, sl]) * _silu(ag_ref[0, :, sl])).astype(ya_ref.dtype)
    for h in range(B_HEADS):
        sl = slice(h * B_DV, (h + 1) * B_DV)
        o = obf_ref[0, :, sl] + obb_ref[0, :, sl]
        yb_ref[0, :, sl] = (_rms(o, nb_ref[:, sl]) * _sigmoid(bo_ref[0, :, sl])).astype(yb_ref.dtype)


def _readout(oa, ob, p3, norm_a, norm_b, n_lat):
    Bn = p3.shape[0]
    T = n_lat * TOK
    wa = A_HEADS * A_D
    wb = B_HEADS * B_DV
    ag_blk = 4
    bo_blk = (5 * A_HEADS + 4 * B_HEADS) * LANE // wb
    sa = pl.BlockSpec((1, TOK, wa), lambda b, i: (b, i, 0))
    sb = pl.BlockSpec((1, TOK, wb), lambda b, i: (b, i, 0))
    return pl.pallas_call(
        _readout_kernel,
        grid=(Bn, n_lat),
        in_specs=[sa, sa, sb, sb,
                  pl.BlockSpec((1, TOK, wa), lambda b, i: (b, i, ag_blk)),
                  pl.BlockSpec((1, TOK, wb), lambda b, i: (b, i, bo_blk)),
                  pl.BlockSpec((1, wa), lambda b, i: (0, 0)),
                  pl.BlockSpec((1, wb), lambda b, i: (0, 0))],
        out_specs=[sa, sb],
        out_shape=[jax.ShapeDtypeStruct((Bn, T, wa), BF16),
                   jax.ShapeDtypeStruct((Bn, T, wb), BF16)],
        compiler_params=_cparams(("parallel", "parallel")),
        name="readout",
    )(oa[0], oa[1], ob[0], ob[1], p3, p3, norm_a.reshape(1, wa), norm_b.reshape(1, wb))


def _merge_kernel(ya_ref, yb_ref, wa_ref, wb_ref, ga_ref, gb_ref, o_ref):
    pa = jnp.dot(ya_ref[0], wa_ref[...], preferred_element_type=F32)
    pb = jnp.dot(yb_ref[0], wb_ref[...], preferred_element_type=F32)
    o_ref[0] = (_sigmoid(ga_ref[0]) * pa + _sigmoid(gb_ref[0]) * pb).astype(o_ref.dtype)


def _merge(ya, yb, wa, wb, p3, D):
    Bn, T, ka = ya.shape
    kb = yb.shape[2]
    tm = _pick(T, 1024)
    tn = _pick(D, 512)
    ga0 = (5 * A_HEADS + 6 * B_HEADS) * LANE // tn
    gb0 = ga0 + D // tn
    return pl.pallas_call(
        _merge_kernel,
        grid=(Bn, T // tm, D // tn),
        in_specs=[pl.BlockSpec((1, tm, ka), lambda b, i, j: (b, i, 0)),
                  pl.BlockSpec((1, tm, kb), lambda b, i, j: (b, i, 0)),
                  pl.BlockSpec((ka, tn), lambda b, i, j: (0, j)),
                  pl.BlockSpec((kb, tn), lambda b, i, j: (0, j)),
                  pl.BlockSpec((1, tm, tn), lambda b, i, j: (b, i, ga0 + j)),
                  pl.BlockSpec((1, tm, tn), lambda b, i, j: (b, i, gb0 + j))],
        out_specs=pl.BlockSpec((1, tm, tn), lambda b, i, j: (b, i, j)),
        out_shape=jax.ShapeDtypeStruct((Bn, T, D), BF16),
        compiler_params=_cparams(("parallel", "parallel", "arbitrary")),
        name="merge",
    )(ya, yb, wa, wb, p3, p3)


def _resid_router_kernel(x_ref, mix_ref, g1_ref, g2_ref, gate_ref, sh_ref, sc_ref, wr_ref,
                         h_ref, v_ref, aff_ref):
    h = x_ref[0] + gate_ref[0] * _rms(mix_ref[0], g1_ref[...])
    h_ref[0] = h
    v = _rms(h, g2_ref[...]) * (1.0 + sc_ref[0]) + sh_ref[0]
    for s in range(v.shape[1] // LANE):
        v_ref[0, :, s, :] = v[:, s * LANE:(s + 1) * LANE]
    logits = jnp.dot(v, wr_ref[...], precision=lax.Precision.HIGHEST, preferred_element_type=F32)
    lane = lax.broadcasted_iota(jnp.int32, logits.shape, 1)
    logits = jnp.where(lane < N_EXPERTS, logits, -jnp.inf)
    e = jnp.exp(logits - jnp.max(logits, axis=-1, keepdims=True))
    aff_ref[0] = e / jnp.sum(e, axis=-1, keepdims=True)


def _resid_router(x, mix, g1, g2, mod3, w_router_pad):
    Bn, T, D = x.shape
    mspec = lambda k: pl.BlockSpec((1, 1, D), lambda b, i: (b, 0, k))
    tok = pl.BlockSpec((1, TOK, D), lambda b, i: (b, i, 0))
    vec = pl.BlockSpec((1, D), lambda b, i: (0, 0))
    return pl.pallas_call(
        _resid_router_kernel,
        grid=(Bn, T // TOK),
        in_specs=[tok, tok, vec, vec, mspec(2), mspec(3), mspec(4),
                  pl.BlockSpec((D, LANE), lambda b, i: (0, 0))],
        out_specs=[tok, pl.BlockSpec((1, TOK, D // LANE, LANE), lambda b, i: (b, i, 0, 0)),
                   pl.BlockSpec((1, TOK, LANE), lambda b, i: (b, i, 0))],
        out_shape=[jax.ShapeDtypeStruct((Bn, T, D), F32),
                   jax.ShapeDtypeStruct((Bn, T, D // LANE, LANE), F32),
                   jax.ShapeDtypeStruct((Bn, T, LANE), F32)],
        compiler_params=_cparams(("parallel", "parallel")),
        name="resid_router",
    )(x, mix, g1.reshape(1, D), g2.reshape(1, D), mod3, mod3, mod3, w_router_pad)


def _select_kernel(aff_ref, slot_ref, idx_ref, offs_ref, *, cap, tk):
    E, T = aff_ref.shape[1], aff_ref.shape[2]
    nck = T // LANE

    def key():
        return lax.bitcast_convert_type(aff_ref[0], jnp.int32)

    def bit_step(i, tau):
        cand = tau | jnp.left_shift(jnp.int32(1), 30 - i)
        cnt = jnp.sum(jnp.where(key() >= cand, 1.0, 0.0), axis=1, keepdims=True)
        return jnp.where(cnt >= cap, cand, tau)

    tau = lax.fori_loop(0, 31, bit_step, jnp.zeros((E, 1), jnp.int32))
    kk = key()
    gt = kk > tau
    eq = kk == tau
    need = cap - jnp.sum(jnp.where(gt, 1.0, 0.0), axis=1, keepdims=True)

    r = lax.broadcasted_iota(jnp.int32, (LANE, LANE), 0)
    c = lax.broadcasted_iota(jnp.int32, (LANE, LANE), 1)
    upper = jnp.where(r < c, 1.0, 0.0).astype(BF16)
    tr = jnp.right_shift(lax.broadcasted_iota(jnp.int32, (T, LANE), 0), LANE.bit_length() - 1)
    member = jnp.where(tr == lax.broadcasted_iota(jnp.int32, (T, LANE), 1), 1.0, 0.0).astype(BF16)

    def prefix(flags):
        x = jnp.where(flags, 1.0, 0.0).astype(BF16)
        tot = jnp.dot(x, member, preferred_element_type=F32)
        offs = jnp.dot(tot.astype(BF16), upper, preferred_element_type=F32)
        parts = [jnp.dot(x[:, j * LANE:(j + 1) * LANE], upper, preferred_element_type=F32)
                 + offs[:, j:j + 1] for j in range(nck)]
        return jnp.concatenate(parts, axis=1), offs

    tie_rank, _ = prefix(eq)
    sel = gt | (eq & (tie_rank < need))
    pos, offs = prefix(sel)
    slot = jnp.where(sel, pos, -1.0)
    slot_ref[0] = slot
    offs_ref[0] = offs.astype(jnp.int32)

    tt = lax.broadcasted_iota(jnp.int32, (8, T), 1)
    rr = lax.broadcasted_iota(jnp.int32, (8, T), 0)
    digits = jnp.where(rr == 0, jnp.right_shift(tt, 6), jnp.where(rr == 1, tt & 63, 0))
    digits = digits.astype(F32).astype(BF16)
    slot_iota = lax.broadcasted_iota(jnp.int32, (cap, tk), 0).astype(F32)
    for e in range(E):
        acc = jnp.zeros((8, cap), F32)
        for kc in range(T // tk):
            ks = slice(kc * tk, (kc + 1) * tk)
            oh = jnp.where(slot[e:e + 1, ks] == slot_iota, 1.0, 0.0).astype(BF16)
            acc = acc + lax.dot_general(digits[:, ks], oh, _NT, preferred_element_type=F32)
        idx_ref[0, e:e + 1, :] = (acc[0:1] * 64.0 + acc[1:2]).astype(jnp.int32)


def _select(aff_rows, cap):
    Bn, E, T = aff_rows.shape
    assert T // LANE < LANE and T <= 4096
    tk = _pick(T, 1024)
    row = lambda n: pl.BlockSpec((1, E, n), lambda b: (b, 0, 0))
    return pl.pallas_call(
        functools.partial(_select_kernel, cap=cap, tk=tk),
        grid=(Bn,),
        in_specs=[row(T)],
        out_specs=[row(T), row(cap), row(LANE)],
        out_shape=[jax.ShapeDtypeStruct((Bn, E, T), F32),
                   jax.ShapeDtypeStruct((Bn, E, cap), jnp.int32),
                   jax.ShapeDtypeStruct((Bn, E, LANE), jnp.int32)],
        compiler_params=_cparams(("parallel",)),
        name="ec_select",
    )(aff_rows)


def _gather_kernel(idx_ref, v_hbm, o_ref, buf, sem, *, cap, n_tok):
    b = pl.program_id(0)
    e = pl.program_id(1)
    base = (b * N_EXPERTS + e) * cap

    def row_copy(r, tok):
        return pltpu.make_async_copy(v_hbm.at[b * n_tok + tok], buf.at[r], sem)

    def issue(r, carry):
        row_copy(r, idx_ref[base + r]).start()
        return carry

    def drain(r, carry):
        row_copy(r, 0).wait()
        return carry

    lax.fori_loop(0, cap, issue, 0)
    lax.fori_loop(0, cap, drain, 0)
    for s in range(buf.shape[1]):
        o_ref[0, 0, :, s * LANE:(s + 1) * LANE] = buf[:, s, :].astype(o_ref.dtype)


def _gather(idx, v_slabs, cap):
    Bn, T, S, _ = v_slabs.shape
    D = S * LANE
    return pl.pallas_call(
        functools.partial(_gather_kernel, cap=cap, n_tok=T),
        grid_spec=pltpu.PrefetchScalarGridSpec(
            num_scalar_prefetch=1,
            grid=(Bn, N_EXPERTS),
            in_specs=[pl.BlockSpec(memory_space=pl.ANY)],
            out_specs=pl.BlockSpec((1, 1, cap, D), lambda b, e, idx_ref: (e, b, 0, 0)),
            scratch_shapes=[pltpu.VMEM((cap, S, LANE), F32), pltpu.SemaphoreType.DMA(())]),
        out_shape=jax.ShapeDtypeStruct((N_EXPERTS, Bn, cap, D), BF16),
        compiler_params=_cparams(("arbitrary", "arbitrary")),
        name="ec_gather",
    )(idx.reshape(-1), v_slabs.reshape(Bn * T, S, LANE))


def _ffn1_kernel(x_ref, wg_ref, wu_ref, o_ref):
    x = x_ref[0]
    g = jnp.dot(x, wg_ref[0].astype(BF16), preferred_element_type=F32)
    u = jnp.dot(x, wu_ref[0].astype(BF16), preferred_element_type=F32)
    o_ref[0] = (_silu(g) * u).astype(o_ref.dtype)


def _ffn1(xg, wg, wu):
    E, M, D = xg.shape
    F = wg.shape[2]
    tm = _pick(M, 1024)
    tn = _pick(F, 256)
    return pl.pallas_call(
        _ffn1_kernel,
        grid=(E, M // tm, F // tn),
        in_specs=[pl.BlockSpec((1, tm, D), lambda e, i, j: (e, i, 0)),
                  pl.BlockSpec((1, D, tn), lambda e, i, j: (e, 0, j)),
                  pl.BlockSpec((1, D, tn), lambda e, i, j: (e, 0, j))],
        out_specs=pl.BlockSpec((1, tm, tn), lambda e, i, j: (e, i, j)),
        out_shape=jax.ShapeDtypeStruct((E, M, F), BF16),
        compiler_params=_cparams(("parallel", "parallel", "arbitrary")),
        name="ec_ffn_up",
    )(xg, wg, wu)


def _ffn2_kernel(h_ref, wd_ref, o_ref):
    o_ref[0] = jnp.dot(h_ref[0], wd_ref[0].astype(BF16),
                       preferred_element_type=F32).astype(o_ref.dtype)


def _ffn2(hid, wd):
    E, M, F = hid.shape
    D = wd.shape[2]
    tn = _pick(D, 512)
    return pl.pallas_call(
        _ffn2_kernel,
        grid=(E, D // tn),
        in_specs=[pl.BlockSpec((1, M, F), lambda e, j: (e, 0, 0)),
                  pl.BlockSpec((1, F, tn), lambda e, j: (e, 0, j))],
        out_specs=pl.BlockSpec((1, M, tn), lambda e, j: (e, 0, j)),
        out_shape=jax.ShapeDtypeStruct((E, M, D), BF16),
        compiler_params=_cparams(("parallel", "arbitrary")),
        name="ec_ffn_down",
    )(hid, wd)


def _scatter_kernel(offs_ref, slot_ref, aff_ref, y_ref, o_ref, *, cap, win):
    b = pl.program_id(0)
    t = pl.program_id(1)
    e = pl.program_id(2)
    tq = o_ref.shape[1]

    @pl.when(e == 0)
    def _():
        o_ref[...] = jnp.zeros_like(o_ref)

    base = (b * N_EXPERTS + e) * LANE
    lo = offs_ref[base + t * (tq // LANE)]
    hi = offs_ref[base + (t + 1) * (tq // LANE)]
    sl = jnp.broadcast_to(slot_ref[0, 0], (LANE, tq)).T
    af = jnp.broadcast_to(aff_ref[0, 0], (LANE, tq)).T
    sl = jnp.concatenate([sl] * (win // LANE), axis=1)
    af = jnp.concatenate([af] * (win // LANE), axis=1)
    lane = lax.broadcasted_iota(jnp.int32, (tq, win), 1).astype(F32)
    for w0 in range(0, cap, win):
        @pl.when((lo < w0 + win) & (hi > w0))
        def _():
            ohw = jnp.where(sl == lane + float(w0), af, 0.0).astype(BF16)
            o_ref[0] += jnp.dot(ohw, y_ref[0, 0, w0:w0 + win, :], preferred_element_type=F32)


def _scatter(offs, slot_rows, aff_rows, y, cap):
    E, Bn, _, D = y.shape
    T = slot_rows.shape[3]
    tq = _pick(T, 512)
    win = _pick(cap, 256)
    assert win % LANE == 0 and tq % LANE == 0
    row = pl.BlockSpec((1, 1, 1, tq), lambda b, t, e, offs_ref: (b, e, 0, t))
    return pl.pallas_call(
        functools.partial(_scatter_kernel, cap=cap, win=win),
        grid_spec=pltpu.PrefetchScalarGridSpec(
            num_scalar_prefetch=1,
            grid=(Bn, T // tq, E),
            in_specs=[row, row,
                      pl.BlockSpec((1, 1, cap, D), lambda b, t, e, offs_ref: (e, b, 0, 0))],
            out_specs=pl.BlockSpec((1, tq, D), lambda b, t, e, offs_ref: (b, t, 0))),
        out_shape=jax.ShapeDtypeStruct((Bn, T, D), F32),
        compiler_params=_cparams(("parallel", "parallel", "arbitrary")),
        name="ec_scatter",
    )(offs.reshape(-1), slot_rows, aff_rows, y)


def _final_kernel(h_ref, f_ref, g_ref, gate_ref, o_ref):
    o_ref[0] = h_ref[0] + gate_ref[0] * _rms(f_ref[0], g_ref[...])


def _final(h, f, g3, mod3):
    Bn, T, D = h.shape
    tok = pl.BlockSpec((1, TOK, D), lambda b, i: (b, i, 0))
    return pl.pallas_call(
        _final_kernel,
        grid=(Bn, T // TOK),
        in_specs=[tok, tok, pl.BlockSpec((1, D), lambda b, i: (0, 0)),
                  pl.BlockSpec((1, 1, D), lambda b, i: (b, 0, 5))],
        out_specs=tok,
        out_shape=jax.ShapeDtypeStruct((Bn, T, D), F32),
        compiler_params=_cparams(("parallel", "parallel")),
        name="final_resid",
    )(h, f, g3.reshape(1, D), mod3)


def _layer(h_lat, ctx, c8, l, lb_l, w_ada, b_ada, g_norm, w_in, hgrn_f_bias, hgrn_norm,
           mlstm_conv_w, mlstm_conv_b, mlstm_gate_b, mlstm_norm, w_branch_a, w_branch_b, w_out,
           w_router, w_expert_gate, w_expert_up, w_expert_down):
    Bn, T, D = h_lat.shape
    n_lat = T // TOK
    n_tok = T + TOK
    A, Bh = A_HEADS, B_HEADS
    n_main = (5 * A + 6 * Bh) * LANE + 2 * D
    g0 = (5 * A + 6 * Bh) * LANE

    mod = _modulation(c8, w_ada[l], b_ada[l])
    mod3 = mod.reshape(8, 1, N_MOD * D)

    u = _prenorm(h_lat, ctx, g_norm[l, 0], mod3)

    w_l = w_in[l]
    w_main = jnp.concatenate([w_l[:, :g0], w_l[:, g0 + 4 * Bh:]], axis=1).astype(BF16)
    w_gate = jnp.pad(w_l[:, g0:g0 + 4 * Bh], ((0, 0), (0, LANE - 4 * Bh))).astype(BF16)
    gate_bias = jnp.pad(mlstm_gate_b[l].reshape(4 * Bh), (0, LANE - 4 * Bh))
    u2 = u.reshape(Bn * n_tok, D)
    p3 = _matmul(u2, w_main, F32, name="in_proj").reshape(Bn, n_tok, n_main)
    gates = _matmul(u2, w_gate, F32, tn=LANE, bias=gate_bias, name="in_proj_gates")

    g6 = gates[:, :4 * Bh].reshape(Bn, n_tok // CHUNK, CHUNK, 2, 2, Bh)
    g_rows = g6.transpose(0, 3, 5, 1, 4, 2)
    g_cols = g6.transpose(0, 3, 5, 1, 2, 4)
    conv4 = jnp.concatenate([mlstm_conv_w[l], mlstm_conv_b[l][None]], axis=0)
    conv4 = conv4.reshape(4, 1, 2 * Bh * B_DQK)

    oa = _hgrn_scan(p3, hgrn_f_bias[l], lb_l, n_lat)
    ob = _mlstm_scan(p3, g_rows, g_cols, conv4, n_lat)
    ya, yb = _readout(oa, ob, p3, hgrn_norm[l], mlstm_norm[l], n_lat)
    merged = _merge(ya, yb, w_branch_a[l].astype(BF16), w_branch_b[l].astype(BF16), p3, D)
    mix = _matmul(merged.reshape(Bn * T, D), w_out[l].astype(BF16), F32, name="out_proj")

    w_router_pad = jnp.pad(w_router[l], ((0, 0), (0, LANE - N_EXPERTS)))
    h_lat, v_lat, aff = _resid_router(h_lat, mix.reshape(Bn, T, D), g_norm[l, 1], g_norm[l, 2],
                                      mod3, w_router_pad)

    cap = CAPACITY * T // N_EXPERTS
    aff_rows = aff[:, :, :N_EXPERTS].transpose(0, 2, 1)
    slot, idx, offs = _select(aff_rows, cap)
    xg = _gather(idx, v_lat, cap)
    hid = _ffn1(xg.reshape(N_EXPERTS, Bn * cap, D), w_expert_gate[l], w_expert_up[l])
    y = _ffn2(hid, w_expert_down[l]).reshape(N_EXPERTS, Bn, cap, D)
    f_lat = _scatter(offs, slot.reshape(Bn, N_EXPERTS, 1, T), aff_rows.reshape(Bn, N_EXPERTS, 1, T),
                     y, cap)
    return _final(h_lat, f_lat, g_norm[l, 3], mod3)


def kernel(x, c, ctx, c_ctx, w_ada, b_ada, g_norm, w_in, hgrn_f_bias, hgrn_lb, hgrn_norm,
           mlstm_conv_w, mlstm_conv_b, mlstm_gate_b, mlstm_norm, w_branch_a, w_branch_b, w_out,
           w_router, w_expert_gate, w_expert_up, w_expert_down):
    Bn, T, D = x.shape
    depth = w_ada.shape[0]
    assert depth == 1, "context outputs are only produced for the state hand-off (single layer)"
    assert ctx.shape[1] == TOK and T % TOK == 0 and Bn < 8
    lb_all = jnp.cumsum(jax.nn.softmax(hgrn_lb.astype(F32), axis=1), axis=1)
    c8 = jnp.zeros((8, D), F32).at[:Bn].set(c).at[Bn].set(c_ctx)
    h_lat = x
    for l in range(depth):
        h_lat = _layer(h_lat, ctx, c8, l, lb_all[:, l], w_ada, b_ada, g_norm, w_in, hgrn_f_bias,
                       hgrn_norm, mlstm_conv_w, mlstm_conv_b, mlstm_gate_b, mlstm_norm,
                       w_branch_a, w_branch_b, w_out, w_router, w_expert_gate, w_expert_up,
                       w_expert_down)
    return h_lat.astype(x.dtype)
```

```python
import functools

import jax
import jax.numpy as jnp
from jax import lax
from jax.experimental import pallas as pl
from jax.experimental.pallas import tpu as pltpu

F32 = jnp.float32
BF16 = jnp.bfloat16

EPS = 1e-6
N_MOD = 6
A_HEADS = 16
A_D = 128
B_HEADS = 8
B_DQK = 128
B_DV = 256
N_EXPERTS = 16
CAPACITY = 2
CHUNK = 64
SUB = 16
TOK = 256
NCH = TOK // CHUNK
NSUB = CHUNK // SUB
LANE = 128
VMEM_LIMIT = 56 * 1024 * 1024

_NT = (((1,), (1,)), ((), ()))


def _pick(n, pref):
    t = min(n, pref)
    while n % t:
        t //= 2
    return t


def _cparams(sem):
    return pltpu.CompilerParams(dimension_semantics=sem, vmem_limit_bytes=VMEM_LIMIT)


def _silu(x):
    return x / (1.0 + jnp.exp(-x))


def _sigmoid(x):
    return 1.0 / (1.0 + jnp.exp(-x))


def _sig_pair(z):
    t = jnp.exp(-jnp.abs(z))
    r = 1.0 / (1.0 + t)
    tr = t * r
    pos = z >= 0
    return jnp.where(pos, r, tr), jnp.where(pos, tr, r)


def _log_sigmoid(x):
    return jnp.minimum(x, 0.0) - jnp.log(1.0 + jnp.exp(-jnp.abs(x)))


def _rms(xf, w):
    return xf * lax.rsqrt(jnp.mean(xf * xf, axis=-1, keepdims=True) + EPS) * w


def _rows(x, r, n):
    return jnp.broadcast_to(x[r:r + 1], (n, x.shape[1]))


def _mod_kernel(c_ref, w_ref, b_ref, o_ref):
    a = _silu(c_ref[...]).astype(BF16)
    o_ref[...] = jnp.dot(a, w_ref[...].astype(BF16), preferred_element_type=F32) + b_ref[...]


def _modulation(c8, w, b):
    D, N = w.shape
    tn = _pick(N, 512)
    return pl.pallas_call(
        _mod_kernel,
        grid=(N // tn,),
        in_specs=[pl.BlockSpec((8, D), lambda j: (0, 0)),
                  pl.BlockSpec((D, tn), lambda j: (0, j)),
                  pl.BlockSpec((1, tn), lambda j: (0, j))],
        out_specs=pl.BlockSpec((8, tn), lambda j: (0, j)),
        out_shape=jax.ShapeDtypeStruct((8, N), F32),
        compiler_params=_cparams(("parallel",)),
        name="adaln_mod",
    )(c8, w, b.reshape(1, N))


def _prenorm_kernel(x_ref, ctx_ref, g_ref, sh_ref, sc_ref, o_ref, *, n_lat):
    i = pl.program_id(1)
    g = g_ref[...]
    sh = sh_ref[0]
    sc = sc_ref[0]

    @pl.when(i < n_lat)
    def _():
        o_ref[0] = (_rms(x_ref[0], g) * (1.0 + sc) + sh).astype(o_ref.dtype)

    @pl.when(i == n_lat)
    def _():
        o_ref[0] = (_rms(ctx_ref[0], g) * (1.0 + sc) + sh).astype(o_ref.dtype)


def _prenorm(x, ctx, g, mod3):
    Bn, T, D = x.shape
    n_lat = T // TOK
    row = lambda b, i: jnp.where(i == n_lat, Bn, b)
    return pl.pallas_call(
        functools.partial(_prenorm_kernel, n_lat=n_lat),
        grid=(Bn, n_lat + 1),
        in_specs=[pl.BlockSpec((1, TOK, D), lambda b, i: (b, jnp.minimum(i, n_lat - 1), 0)),
                  pl.BlockSpec((1, TOK, D), lambda b, i: (b, 0, 0)),
                  pl.BlockSpec((1, D), lambda b, i: (0, 0)),
                  pl.BlockSpec((1, 1, D), lambda b, i: (row(b, i), 0, 0)),
                  pl.BlockSpec((1, 1, D), lambda b, i: (row(b, i), 0, 1))],
        out_specs=pl.BlockSpec((1, TOK, D), lambda b, i: (b, i, 0)),
        out_shape=jax.ShapeDtypeStruct((Bn, T + TOK, D), BF16),
        compiler_params=_cparams(("parallel", "arbitrary")),
        name="prenorm",
    )(x, ctx, g.reshape(1, D), mod3, mod3)


def _mm_kernel(a_ref, b_ref, o_ref):
    o_ref[...] = jnp.dot(a_ref[...], b_ref[...].astype(BF16),
                         preferred_element_type=F32).astype(o_ref.dtype)


def _mm_bias_kernel(a_ref, b_ref, bias_ref, o_ref):
    o_ref[...] = (jnp.dot(a_ref[...], b_ref[...].astype(BF16), preferred_element_type=F32)
                  + bias_ref[...]).astype(o_ref.dtype)


def _matmul(a, b, out_dtype, tm=1024, tn=512, bias=None, name="matmul"):
    M, K = a.shape
    N = b.shape[1]
    tm = _pick(M, tm)
    tn = _pick(N, tn)
    in_specs = [pl.BlockSpec((tm, K), lambda i, j: (i, 0)),
                pl.BlockSpec((K, tn), lambda i, j: (0, j))]
    args = [a, b]
    kern = _mm_kernel
    if bias is not None:
        in_specs.append(pl.BlockSpec((1, tn), lambda i, j: (0, j)))
        args.append(bias.reshape(1, N))
        kern = _mm_bias_kernel
    return pl.pallas_call(
        kern,
        grid=(M // tm, N // tn),
        in_specs=in_specs,
        out_specs=pl.BlockSpec((tm, tn), lambda i, j: (i, j)),
        out_shape=jax.ShapeDtypeStruct((M, N), out_dtype),
        compiler_params=_cparams(("parallel", "arbitrary")),
        name=name,
    )(*args)


def _tok_block(rev, j, n_lat):
    lat = n_lat - j if rev else j - 1
    return jnp.where(j == 0, n_lat, lat)


def _out_block(rev, j, n_lat):
    return jnp.clip(n_lat - j if rev else j - 1, 0, n_lat - 1)


def _chunk_masks(rev):
    row = lax.broadcasted_iota(jnp.int32, (CHUNK, CHUNK), 0)
    col = lax.broadcasted_iota(jnp.int32, (CHUNK, CHUNK), 1)
    return (col >= row, col <= row) if rev else (col <= row, col >= row)


def _hgrn_consts():
    t = jnp.arange(TOK)
    same = (t[:, None] // CHUNK) == (t[None, :] // CHUNK)
    sub_r, sub_c = t[:, None] // SUB, t[None, :] // SUB
    fwd = jnp.concatenate([same & (t[None, :] <= t[:, None]), same & (sub_c < sub_r)], axis=0)
    bwd = jnp.concatenate([same & (t[None, :] >= t[:, None]), same & (sub_c > sub_r)], axis=0)
    masks = jnp.stack([fwd, bwd]).astype(BF16)
    r = jnp.arange(SUB * A_D)
    emat = ((r[:, None] // A_D) == (jnp.arange(LANE)[None, :] % SUB)).astype(BF16)
    return masks, emat


def _hgrn_state(af_ref, ai_ref, fb, lb, mask_ref, st_ref, rev):
    z = af_ref[0] + fb
    sp, sn = _sig_pair(z)
    k = (1.0 - lb) * sn
    lf2 = jnp.log2(lb + (1.0 - lb) * sp)
    lk2 = jnp.log2(k)
    v = ai_ref[0]

    hi = lf2.astype(BF16)
    r1 = lf2 - hi.astype(F32)
    mid = r1.astype(BF16)
    lo = (r1 - mid.astype(F32)).astype(BF16)
    cs = jnp.dot(mask_ref[0], jnp.concatenate([hi, mid, lo], axis=1), preferred_element_type=F32)
    cs = cs[:, :A_D] + cs[:, A_D:2 * A_D] + cs[:, 2 * A_D:]
    b = cs[:TOK]
    ent = cs[TOK:]
    c_all = lk2 - b

    last = [(c * CHUNK if rev else c * CHUNK + CHUNK - 1) for c in range(NCH)]
    tot_rows = jnp.concatenate([_rows(b, last[c], CHUNK) for c in range(NCH)], axis=0)
    khat = jnp.exp2(jnp.minimum(tot_rows + c_all, lk2)).astype(BF16)
    upd = [jnp.dot(v[c * CHUNK:(c + 1) * CHUNK].T.astype(BF16), khat[c * CHUNK:(c + 1) * CHUNK],
                   preferred_element_type=F32) for c in range(NCH)]

    order = range(NCH - 1, -1, -1) if rev else range(NCH)
    st = st_ref[...]
    st_in = [None] * NCH
    for c in order:
        st_in[c] = st
        st = st * jnp.exp2(b[last[c]:last[c] + 1]) + upd[c]
    st_ref[...] = st
    return v, b, ent, c_all, lk2, st_in


def _hgrn_output(aq_ref, feats, emat_ref, o_ref, rev):
    v, b, ent, c_all, lk2, st_in = feats
    q = _silu(aq_ref[0])
    qb = q.astype(BF16)
    vb = v.astype(BF16)
    qt = (q * jnp.exp2(b - ent)).astype(BF16)
    row = lax.broadcasted_iota(jnp.int32, (CHUNK, CHUNK), 0)
    col = lax.broadcasted_iota(jnp.int32, (CHUNK, CHUNK), 1)
    rs = jnp.right_shift(row, SUB.bit_length() - 1)
    cs_ = jnp.right_shift(col, SUB.bit_length() - 1)
    prev_blk = (cs_ > rs) if rev else (cs_ < rs)
    diag_blk = (rs == cs_) & ((col >= row) if rev else (col <= row))

    off = [[None] * NSUB for _ in range(NCH)]
    for i in range(NSUB):
        e_i = jnp.concatenate([_rows(ent, c * CHUNK + i * SUB, CHUNK) for c in range(NCH)], axis=0)
        kt = jnp.exp2(jnp.minimum(e_i + c_all, lk2)).astype(BF16)
        for c in range(NCH):
            r0 = c * CHUNK + i * SUB
            off[c][i] = lax.dot_general(qt[r0:r0 + SUB], kt[c * CHUNK:(c + 1) * CHUNK], _NT,
                                        preferred_element_type=F32)

    half = SUB // 2
    nsb = TOK // SUB
    zeros = jnp.zeros((half, A_D), F32)
    ps = []
    for sl in range(SUB):
        n = half if (sl < half if rev else sl >= half) else SUB
        r0 = SUB - n if not rev else 0
        bq = b if n == SUB else jnp.concatenate(
            [b[g * SUB + r0:g * SUB + r0 + n] for g in range(nsb)], axis=0)
        cref = jnp.concatenate([_rows(c_all, g * SUB + sl, n) for g in range(nsb)], axis=0)
        kref = jnp.concatenate([_rows(lk2, g * SUB + sl, n) for g in range(nsb)], axis=0)
        e = jnp.exp2(jnp.minimum(bq + cref, kref))
        if n != SUB:
            parts = []
            for g in range(nsb):
                piece = e[g * half:(g + 1) * half]
                parts += [piece, zeros] if rev else [zeros, piece]
            e = jnp.concatenate(parts, axis=0)
        ps.append(e.astype(BF16) * qb)
    acc = jnp.dot(jnp.concatenate(ps, axis=1), emat_ref[...], preferred_element_type=F32)

    qhat = (q * jnp.exp2(b)).astype(BF16)
    for c in range(NCH):
        sl_c = slice(c * CHUNK, (c + 1) * CHUNK)
        attn = (jnp.where(prev_blk, jnp.concatenate(off[c], axis=0), 0.0)
                + jnp.where(diag_blk, acc[sl_c, :CHUNK], 0.0))
        o = jnp.dot(attn.astype(BF16), vb[sl_c], preferred_element_type=F32)
        o = o + lax.dot_general(qhat[sl_c], st_in[c].astype(BF16), _NT, preferred_element_type=F32)
        o_ref[0, sl_c, :] = o


def _hgrn_kernel(aqf, aff, aif, aqb, afb, aib, fb_ref, lb_ref, mask_ref, emat_ref, of_ref, ob_ref,
                 stf_ref, stb_ref):
    j = pl.program_id(2)

    @pl.when(j == 0)
    def _():
        stf_ref[...] = jnp.zeros_like(stf_ref)
        stb_ref[...] = jnp.zeros_like(stb_ref)

    feats_f = _hgrn_state(aff, aif, fb_ref[0], lb_ref[0], mask_ref.at[0:1], stf_ref, False)
    feats_b = _hgrn_state(afb, aib, fb_ref[1], lb_ref[1], mask_ref.at[1:2], stb_ref, True)

    _hgrn_output(aqf, feats_f, emat_ref, of_ref, False)
    _hgrn_output(aqb, feats_b, emat_ref, ob_ref, True)


def _hgrn_scan(p3, f_bias, lb, n_lat):
    Bn = p3.shape[0]
    A = A_HEADS
    masks, emat = _hgrn_consts()

    def feat(rev, colblk):
        return pl.BlockSpec((1, TOK, A_D), lambda b, h, j: (b, _tok_block(rev, j, n_lat), colblk(h)))

    def outp(rev):
        return pl.BlockSpec((1, TOK, A_D), lambda b, h, j: (b, _out_block(rev, j, n_lat), h))

    par = pl.BlockSpec((2, 1, A_D), lambda b, h, j: (0, 0, h))
    osh = jax.ShapeDtypeStruct((Bn, n_lat * TOK, A * A_D), F32)
    return pl.pallas_call(
        _hgrn_kernel,
        grid=(Bn, A, n_lat + 1),
        in_specs=[feat(False, lambda h: h), feat(False, lambda h: A + h), feat(False, lambda h: 3 * A + h),
                  feat(True, lambda h: h), feat(True, lambda h: 2 * A + h), feat(True, lambda h: 3 * A + h),
                  par, par,
                  pl.BlockSpec((2, 2 * TOK, TOK), lambda b, h, j: (0, 0, 0)),
                  pl.BlockSpec((SUB * A_D, LANE), lambda b, h, j: (0, 0))],
        out_specs=[outp(False), outp(True)],
        out_shape=[osh, osh],
        scratch_shapes=[pltpu.VMEM((A_D, A_D), F32), pltpu.VMEM((A_D, A_D), F32)],
        compiler_params=_cparams(("parallel", "parallel", "arbitrary")),
        name="hgrn2_scan",
    )(p3, p3, p3, p3, p3, p3, f_bias.reshape(2, 1, A * A_D), lb.reshape(2, 1, A * A_D), masks, emat)


def _mlstm_state(bq_ref, bk_ref, bv_ref, gr_ref, gc_ref, cwq, cwk, first, last_, c_ref, n_ref, m_ref,
                 rev):
    def conv(u, cw):
        up = jnp.where(first, 0.0, pltpu.roll(u, 1, 0))
        dn = jnp.where(last_, 0.0, pltpu.roll(u, TOK - 1, 0))
        return _silu(cw[3:4] + up * cw[0:1] + u * cw[1:2] + dn * cw[2:3])

    q_all = conv(bq_ref[0], cwq)
    k_all = conv(bk_ref[0], cwk) * (B_DQK ** -0.5)
    seen, seen_t = _chunk_masks(rev)

    pre = []
    for c in range(NCH):
        sl_c = slice(c * CHUNK, (c + 1) * CHUNK)
        k = k_all[sl_c]
        vb = bv_ref[0, sl_c, :].astype(BF16)
        g_r = gr_ref[0, 0, 0, c]
        g_c = gc_ref[0, 0, 0, c]
        ii_r = g_r[0:1]
        lf_r = _log_sigmoid(g_r[1:2])
        ii_c = g_c[:, 0:1]
        lf_c = _log_sigmoid(g_c[:, 1:2])
        b_c = jnp.sum(jnp.where(seen, lf_r, 0.0), axis=1, keepdims=True)
        b_r = jnp.sum(jnp.where(seen_t, lf_c, 0.0), axis=0, keepdims=True)
        total = jnp.sum(lf_r, axis=1, keepdims=True)
        logs = total - b_c + ii_c
        ms = jnp.max(logs, axis=0, keepdims=True)
        kw = k * jnp.exp(logs - ms)
        upd = jnp.dot(kw.T.astype(BF16), vb, preferred_element_type=F32)
        nupd = jnp.sum(kw, axis=0, keepdims=True)
        pre.append((vb, ii_r, b_c, b_r, total, ms, upd, nupd))

    cmat, nvec, m = c_ref[...], n_ref[...], m_ref[:, 0:1]
    st_in = [None] * NCH
    for c in (range(NCH - 1, -1, -1) if rev else range(NCH)):
        _, _, _, _, total, ms, upd, nupd = pre[c]
        st_in[c] = (cmat, nvec, m)
        m_new = jnp.maximum(total + m, ms)
        dec = jnp.exp(total + m - m_new)
        sc = jnp.exp(ms - m_new)
        cmat = dec * cmat + sc * upd
        nvec = dec * nvec + sc * nupd
        m = m_new
    c_ref[...] = cmat
    n_ref[...] = nvec
    m_ref[...] = jnp.broadcast_to(m, m_ref.shape)
    return q_all, k_all, pre, st_in


def _mlstm_output(feats, o_ref, rev):
    q_all, k_all, pre, st_in = feats
    seen, _ = _chunk_masks(rev)
    for c in range(NCH):
        sl_c = slice(c * CHUNK, (c + 1) * CHUNK)
        vb, ii_r, b_c, b_r, _, _, _, _ = pre[c]
        cm, nv, m0 = st_in[c]
        q = q_all[sl_c]
        qb = q.astype(BF16)
        kb = k_all[sl_c].astype(BF16)
        logw = jnp.where(seen, b_c - b_r + ii_r, -jnp.inf)
        mw = jnp.max(logw, axis=1, keepdims=True)
        qk = lax.dot_general(qb, kb, _NT, preferred_element_type=F32) * jnp.exp(logw - mw)
        num0 = jnp.dot(qk.astype(BF16), vb, preferred_element_type=F32)
        den0 = jnp.sum(qk, axis=1, keepdims=True)
        log_inter = b_c + m0
        m_t = jnp.maximum(mw, log_inter)
        r = jnp.exp(mw - m_t)
        a = jnp.exp(log_inter - m_t)
        num = r * num0 + a * jnp.dot(qb, cm.astype(BF16), preferred_element_type=F32)
        den = r * den0 + a * jnp.sum(q * nv, axis=1, keepdims=True)
        o_ref[0, sl_c, :] = num / jnp.maximum(jnp.abs(den), jnp.exp(-m_t))


def _mlstm_kernel(bqf, bkf, bvf, grf, gcf, bqb, bkb, bvb, grb, gcb, cwq_ref, cwk_ref, of_ref, ob_ref,
                  cf_ref, nf_ref, mf_ref, cb_ref, nb_ref, mb_ref):
    j = pl.program_id(2)

    @pl.when(j == 0)
    def _():
        for ref in (cf_ref, nf_ref, mf_ref, cb_ref, nb_ref, mb_ref):
            ref[...] = jnp.zeros_like(ref)

    t = lax.broadcasted_iota(jnp.int32, (TOK, 1), 0)
    row_mask = jnp.where(j == 0, TOK - 1, CHUNK - 1)
    pos = t & row_mask
    first = pos == 0
    last_ = pos == row_mask
    cwq = cwq_ref[:, 0, :]
    cwk = cwk_ref[:, 0, :]
    feats_f = _mlstm_state(bqf, bkf, bvf, grf, gcf, cwq, cwk, first, last_, cf_ref, nf_ref, mf_ref, False)
    feats_b = _mlstm_state(bqb, bkb, bvb, grb, gcb, cwq, cwk, first, last_, cb_ref, nb_ref, mb_ref, True)

    _mlstm_output(feats_f, of_ref, False)
    _mlstm_output(feats_b, ob_ref, True)


def _mlstm_scan(p3, g_rows, g_cols, conv4, n_lat):
    Bn = p3.shape[0]
    A, Bh = A_HEADS, B_HEADS
    q0 = 5 * A
    k0 = 5 * A + Bh
    v0 = (5 * A + 2 * Bh) // 2

    def specs(rev):
        d = int(rev)
        tb = lambda j: _tok_block(rev, j, n_lat)
        return [pl.BlockSpec((1, TOK, B_DQK), lambda b, h, j: (b, tb(j), q0 + h)),
                pl.BlockSpec((1, TOK, B_DQK), lambda b, h, j: (b, tb(j), k0 + h)),
                pl.BlockSpec((1, TOK, B_DV), lambda b, h, j: (b, tb(j), v0 + h)),
                pl.BlockSpec((1, 1, 1, NCH, 2, CHUNK), lambda b, h, j: (b, d, h, tb(j), 0, 0)),
                pl.BlockSpec((1, 1, 1, NCH, CHUNK, 2), lambda b, h, j: (b, d, h, tb(j), 0, 0))]

    def outp(rev):
        return pl.BlockSpec((1, TOK, B_DV), lambda b, h, j: (b, _out_block(rev, j, n_lat), h))

    osh = jax.ShapeDtypeStruct((Bn, n_lat * TOK, Bh * B_DV), F32)
    state = [pltpu.VMEM((B_DQK, B_DV), F32), pltpu.VMEM((1, B_DQK), F32), pltpu.VMEM((1, LANE), F32)]
    return pl.pallas_call(
        _mlstm_kernel,
        grid=(Bn, Bh, n_lat + 1),
        in_specs=specs(False) + specs(True) + [
            pl.BlockSpec((4, 1, B_DQK), lambda b, h, j: (0, 0, h)),
            pl.BlockSpec((4, 1, B_DQK), lambda b, h, j: (0, 0, Bh + h))],
        out_specs=[outp(False), outp(True)],
        out_shape=[osh, osh],
        scratch_shapes=state + state,
        compiler_params=_cparams(("parallel", "parallel", "arbitrary")),
        name="mlstm_scan",
    )(p3, p3, p3, g_rows, g_cols, p3, p3, p3, g_rows, g_cols, conv4, conv4)


def _readout_kernel(oaf_ref, oab_ref, obf_ref, obb_ref, ag_ref, bo_ref, na_ref, nb_ref, ya_ref, yb_ref):
    for h in range(A_HEADS):
        sl = slice(h * A_D, (h + 1) * A_D)
        o = oaf_ref[0, :, sl] + oab_ref[0, :, sl]
        ya_ref[0, :, sl] = (_rms(o, na_ref[:, sl]) * _silu(ag_ref[0, :, sl])).astype(ya_ref.dtype)
    for h in range(B_HEADS):
        sl = slice(h * B_DV, (h + 1) * B_DV)
        o = obf_ref[0, :, sl] + obb_ref[0, :, sl]
        yb_ref[0, :, sl] = (_rms(o, nb_ref[:, sl]) * _sigmoid(bo_ref[0, :, sl])).astype(yb_ref.dtype)


def _readout(oa, ob, p3, norm_a, norm_b, n_lat):
    Bn = p3.shape[0]
    T = n_lat * TOK
    wa = A_HEADS * A_D
    wb = B_HEADS * B_DV
    ag_blk = 4
    bo_blk = (5 * A_HEADS + 4 * B_HEADS) * LANE // wb
    sa = pl.BlockSpec((1, TOK, wa), lambda b, i: (b, i, 0))
    sb = pl.BlockSpec((1, TOK, wb), lambda b, i: (b, i, 0))
    return pl.pallas_call(
        _readout_kernel,
        grid=(Bn, n_lat),
        in_specs=[sa, sa, sb, sb,
                  pl.BlockSpec((1, TOK, wa), lambda b, i: (b, i, ag_blk)),
                  pl.BlockSpec((1, TOK, wb), lambda b, i: (b, i, bo_blk)),
                  pl.BlockSpec((1, wa), lambda b, i: (0, 0)),
                  pl.BlockSpec((1, wb), lambda b, i: (0, 0))],
        out_specs=[sa, sb],
        out_shape=[jax.ShapeDtypeStruct((Bn, T, wa), BF16),
                   jax.ShapeDtypeStruct((Bn, T, wb), BF16)],
        compiler_params=_cparams(("parallel", "parallel")),
        name="readout",
    )(oa[0], oa[1], ob[0], ob[1], p3, p3, norm_a.reshape(1, wa), norm_b.reshape(1, wb))


def _merge_kernel(ya_ref, yb_ref, wa_ref, wb_ref, ga_ref, gb_ref, o_ref):
    pa = jnp.dot(ya_ref[0], wa_ref[...], preferred_element_type=F32)
    pb = jnp.dot(yb_ref[0], wb_ref[...], preferred_element_type=F32)
    o_ref[0] = (_sigmoid(ga_ref[0]) * pa + _sigmoid(gb_ref[0]) * pb).astype(o_ref.dtype)


def _merge(ya, yb, wa, wb, p3, D):
    Bn, T, ka = ya.shape
    kb = yb.shape[2]
    tm = _pick(T, 1024)
    tn = _pick(D, 512)
    ga0 = (5 * A_HEADS + 6 * B_HEADS) * LANE // tn
    gb0 = ga0 + D // tn
    return pl.pallas_call(
        _merge_kernel,
        grid=(Bn, T // tm, D // tn),
        in_specs=[pl.BlockSpec((1, tm, ka), lambda b, i, j: (b, i, 0)),
                  pl.BlockSpec((1, tm, kb), lambda b, i, j: (b, i, 0)),
                  pl.BlockSpec((ka, tn), lambda b, i, j: (0, j)),
                  pl.BlockSpec((kb, tn), lambda b, i, j: (0, j)),
                  pl.BlockSpec((1, tm, tn), lambda b, i, j: (b, i, ga0 + j)),
                  pl.BlockSpec((1, tm, tn), lambda b, i, j: (b, i, gb0 + j))],
        out_specs=pl.BlockSpec((1, tm, tn), lambda b, i, j: (b, i, j)),
        out_shape=jax.ShapeDtypeStruct((Bn, T, D), BF16),
        compiler_params=_cparams(("parallel", "parallel", "arbitrary")),
        name="merge",
    )(ya, yb, wa, wb, p3, p3)


def _resid_router_kernel(x_ref, mix_ref, g1_ref, g2_ref, gate_ref, sh_ref, sc_ref, wr_ref,
                         h_ref, v_ref, aff_ref):
    h = x_ref[0] + gate_ref[0] * _rms(mix_ref[0], g1_ref[...])
    h_ref[0] = h
    v = _rms(h, g2_ref[...]) * (1.0 + sc_ref[0]) + sh_ref[0]
    v_ref[0] = v
    logits = jnp.dot(v, wr_ref[...], precision=lax.Precision.HIGHEST, preferred_element_type=F32)
    lane = lax.broadcasted_iota(jnp.int32, logits.shape, 1)
    logits = jnp.where(lane < N_EXPERTS, logits, -jnp.inf)
    e = jnp.exp(logits - jnp.max(logits, axis=-1, keepdims=True))
    aff_ref[0] = e / jnp.sum(e, axis=-1, keepdims=True)


def _resid_router(x, mix, g1, g2, mod3, w_router_pad):
    Bn, T, D = x.shape
    mspec = lambda k: pl.BlockSpec((1, 1, D), lambda b, i: (b, 0, k))
    tok = pl.BlockSpec((1, TOK, D), lambda b, i: (b, i, 0))
    vec = pl.BlockSpec((1, D), lambda b, i: (0, 0))
    return pl.pallas_call(
        _resid_router_kernel,
        grid=(Bn, T // TOK),
        in_specs=[tok, tok, vec, vec, mspec(2), mspec(3), mspec(4),
                  pl.BlockSpec((D, LANE), lambda b, i: (0, 0))],
        out_specs=[tok, tok, pl.BlockSpec((1, TOK, LANE), lambda b, i: (b, i, 0))],
        out_shape=[jax.ShapeDtypeStruct((Bn, T, D), F32),
                   jax.ShapeDtypeStruct((Bn, T, D), F32),
                   jax.ShapeDtypeStruct((Bn, T, LANE), F32)],
        compiler_params=_cparams(("parallel", "parallel")),
        name="resid_router",
    )(x, mix, g1.reshape(1, D), g2.reshape(1, D), mod3, mod3, mod3, w_router_pad)


def _select_kernel(aff_ref, slot_ref, idx_ref, offs_ref, *, cap, tk):
    E, T = aff_ref.shape[1], aff_ref.shape[2]
    nck = T // LANE

    def key():
        return lax.bitcast_convert_type(aff_ref[0], jnp.int32)

    def bit_step(i, tau):
        cand = tau | jnp.left_shift(jnp.int32(1), 30 - i)
        cnt = jnp.sum(jnp.where(key() >= cand, 1.0, 0.0), axis=1, keepdims=True)
        return jnp.where(cnt >= cap, cand, tau)

    tau = lax.fori_loop(0, 31, bit_step, jnp.zeros((E, 1), jnp.int32))
    kk = key()
    gt = kk > tau
    eq = kk == tau
    need = cap - jnp.sum(jnp.where(gt, 1.0, 0.0), axis=1, keepdims=True)

    r = lax.broadcasted_iota(jnp.int32, (LANE, LANE), 0)
    c = lax.broadcasted_iota(jnp.int32, (LANE, LANE), 1)
    upper = jnp.where(r < c, 1.0, 0.0).astype(BF16)
    tr = jnp.right_shift(lax.broadcasted_iota(jnp.int32, (T, LANE), 0), LANE.bit_length() - 1)
    member = jnp.where(tr == lax.broadcasted_iota(jnp.int32, (T, LANE), 1), 1.0, 0.0).astype(BF16)

    def prefix(flags):
        x = jnp.where(flags, 1.0, 0.0).astype(BF16)
        tot = jnp.dot(x, member, preferred_element_type=F32)
        offs = jnp.dot(tot.astype(BF16), upper, preferred_element_type=F32)
        parts = [jnp.dot(x[:, j * LANE:(j + 1) * LANE], upper, preferred_element_type=F32)
                 + offs[:, j:j + 1] for j in range(nck)]
        return jnp.concatenate(parts, axis=1), offs

    tie_rank, _ = prefix(eq)
    sel = gt | (eq & (tie_rank < need))
    pos, offs = prefix(sel)
    slot = jnp.where(sel, pos, -1.0)
    slot_ref[0] = slot
    offs_ref[0] = offs.astype(jnp.int32)

    tt = lax.broadcasted_iota(jnp.int32, (8, T), 1)
    rr = lax.broadcasted_iota(jnp.int32, (8, T), 0)
    digits = jnp.where(rr == 0, jnp.right_shift(tt, 6), jnp.where(rr == 1, tt & 63, 0))
    digits = digits.astype(F32).astype(BF16)
    slot_iota = lax.broadcasted_iota(jnp.int32, (cap, tk), 0).astype(F32)
    for e in range(E):
        acc = jnp.zeros((8, cap), F32)
        for kc in range(T // tk):
            ks = slice(kc * tk, (kc + 1) * tk)
            oh = jnp.where(slot[e:e + 1, ks] == slot_iota, 1.0, 0.0).astype(BF16)
            acc = acc + lax.dot_general(digits[:, ks], oh, _NT, preferred_element_type=F32)
        idx_ref[0, e:e + 1, :] = (acc[0:1] * 64.0 + acc[1:2]).astype(jnp.int32)


def _select(aff_rows, cap):
    Bn, E, T = aff_rows.shape
    assert T // LANE < LANE and T <= 4096
    tk = _pick(T, 1024)
    row = lambda n: pl.BlockSpec((1, E, n), lambda b: (b, 0, 0))
    return pl.pallas_call(
        functools.partial(_select_kernel, cap=cap, tk=tk),
        grid=(Bn,),
        in_specs=[row(T)],
        out_specs=[row(T), row(cap), row(LANE)],
        out_shape=[jax.ShapeDtypeStruct((Bn, E, T), F32),
                   jax.ShapeDtypeStruct((Bn, E, cap), jnp.int32),
                   jax.ShapeDtypeStruct((Bn, E, LANE), jnp.int32)],
        compiler_params=_cparams(("parallel",)),
        name="ec_select",
    )(aff_rows)


def _gather_kernel(idx_ref, v_hbm, o_ref, buf, sem, *, cap, n_tok):
    b = pl.program_id(0)
    e = pl.program_id(1)
    base = (b * N_EXPERTS + e) * cap

    def row_copy(r, tok):
        return pltpu.make_async_copy(v_hbm.at[pl.ds(b * n_tok + tok, 1), :], buf.at[pl.ds(r, 1), :], sem)

    def issue(r, carry):
        row_copy(r, idx_ref[base + r]).start()
        return carry

    def drain(r, carry):
        row_copy(r, 0).wait()
        return carry

    lax.fori_loop(0, cap, issue, 0)
    lax.fori_loop(0, cap, drain, 0)
    o_ref[0, 0] = buf[...].astype(o_ref.dtype)


def _gather(idx, v, cap):
    Bn, T, D = v.shape
    return pl.pallas_call(
        functools.partial(_gather_kernel, cap=cap, n_tok=T),
        grid_spec=pltpu.PrefetchScalarGridSpec(
            num_scalar_prefetch=1,
            grid=(Bn, N_EXPERTS),
            in_specs=[pl.BlockSpec(memory_space=pl.ANY)],
            out_specs=pl.BlockSpec((1, 1, cap, D), lambda b, e, idx_ref: (e, b, 0, 0)),
            scratch_shapes=[pltpu.VMEM((cap, D), F32), pltpu.SemaphoreType.DMA(())]),
        out_shape=jax.ShapeDtypeStruct((N_EXPERTS, Bn, cap, D), BF16),
        compiler_params=_cparams(("arbitrary", "arbitrary")),
        name="ec_gather",
    )(idx.reshape(-1), v.reshape(Bn * T, D))


def _ffn1_kernel(x_ref, wg_ref, wu_ref, o_ref):
    x = x_ref[0]
    g = jnp.dot(x, wg_ref[0].astype(BF16), preferred_element_type=F32)
    u = jnp.dot(x, wu_ref[0].astype(BF16), preferred_element_type=F32)
    o_ref[0] = (_silu(g) * u).astype(o_ref.dtype)


def _ffn1(xg, wg, wu):
    E, M, D = xg.shape
    F = wg.shape[2]
    tm = _pick(M, 1024)
    tn = _pick(F, 256)
    return pl.pallas_call(
        _ffn1_kernel,
        grid=(E, M // tm, F // tn),
        in_specs=[pl.BlockSpec((1, tm, D), lambda e, i, j: (e, i, 0)),
                  pl.BlockSpec((1, D, tn), lambda e, i, j: (e, 0, j)),
                  pl.BlockSpec((1, D, tn), lambda e, i, j: (e, 0, j))],
        out_specs=pl.BlockSpec((1, tm, tn), lambda e, i, j: (e, i, j)),
        out_shape=jax.ShapeDtypeStruct((E, M, F), BF16),
        compiler_params=_cparams(("parallel", "parallel", "arbitrary")),
        name="ec_ffn_up",
    )(xg, wg, wu)


def _ffn2_kernel(h_ref, wd_ref, o_ref):
    o_ref[0] = jnp.dot(h_ref[0], wd_ref[0].astype(BF16),
                       preferred_element_type=F32).astype(o_ref.dtype)


def _ffn2(hid, wd):
    E, M, F = hid.shape
    D = wd.shape[2]
    tn = _pick(D, 512)
    return pl.pallas_call(
        _ffn2_kernel,
        grid=(E, D // tn),
        in_specs=[pl.BlockSpec((1, M, F), lambda e, j: (e, 0, 0)),
                  pl.BlockSpec((1, F, tn), lambda e, j: (e, 0, j))],
        out_specs=pl.BlockSpec((1, M, tn), lambda e, j: (e, 0, j)),
        out_shape=jax.ShapeDtypeStruct((E, M, D), BF16),
        compiler_params=_cparams(("parallel", "arbitrary")),
        name="ec_ffn_down",
    )(hid, wd)


def _scatter_kernel(offs_ref, slot_ref, aff_ref, y_ref, h_ref, g_ref, gate_ref, o_ref, *, cap, win):
    b = pl.program_id(0)
    t = pl.program_id(1)
    e = pl.program_id(2)
    tq = o_ref.shape[1]

    @pl.when(e == 0)
    def _():
        o_ref[...] = jnp.zeros_like(o_ref)

    base = (b * N_EXPERTS + e) * LANE
    lo = offs_ref[base + t * (tq // LANE)]
    hi = offs_ref[base + (t + 1) * (tq // LANE)]
    sl = jnp.broadcast_to(slot_ref[0, 0], (LANE, tq)).T
    af = jnp.broadcast_to(aff_ref[0, 0], (LANE, tq)).T
    sl = jnp.concatenate([sl] * (win // LANE), axis=1)
    af = jnp.concatenate([af] * (win // LANE), axis=1)
    lane = lax.broadcasted_iota(jnp.int32, (tq, win), 1).astype(F32)
    for w0 in range(0, cap, win):
        @pl.when((lo < w0 + win) & (hi > w0))
        def _():
            ohw = jnp.where(sl == lane + float(w0), af, 0.0).astype(BF16)
            o_ref[0] += jnp.dot(ohw, y_ref[0, 0, w0:w0 + win, :], preferred_element_type=F32)

    @pl.when(e == pl.num_programs(2) - 1)
    def _():
        o_ref[0] = h_ref[0] + gate_ref[0] * _rms(o_ref[0], g_ref[...])


def _scatter_combine(offs, slot_rows, aff_rows, y, cap, h, g3, mod3):
    E, Bn, _, D = y.shape
    T = slot_rows.shape[3]
    tq = _pick(T, 512)
    win = _pick(cap, 256)
    assert win % LANE == 0 and tq % LANE == 0
    row = pl.BlockSpec((1, 1, 1, tq), lambda b, t, e, offs_ref: (b, e, 0, t))
    return pl.pallas_call(
        functools.partial(_scatter_kernel, cap=cap, win=win),
        grid_spec=pltpu.PrefetchScalarGridSpec(
            num_scalar_prefetch=1,
            grid=(Bn, T // tq, E),
            in_specs=[row, row,
                      pl.BlockSpec((1, 1, cap, D), lambda b, t, e, offs_ref: (e, b, 0, 0)),
                      pl.BlockSpec((1, tq, D), lambda b, t, e, offs_ref: (b, t, 0)),
                      pl.BlockSpec((1, D), lambda b, t, e, offs_ref: (0, 0)),
                      pl.BlockSpec((1, 1, D), lambda b, t, e, offs_ref: (b, 0, 5))],
            out_specs=pl.BlockSpec((1, tq, D), lambda b, t, e, offs_ref: (b, t, 0))),
        out_shape=jax.ShapeDtypeStruct((Bn, T, D), F32),
        compiler_params=_cparams(("parallel", "parallel", "arbitrary")),
        name="ec_scatter",
    )(offs.reshape(-1), slot_rows, aff_rows, y, h, g3.reshape(1, D), mod3)


def _layer(h_lat, ctx, c8, l, lb_l, w_ada, b_ada, g_norm, w_in, hgrn_f_bias, hgrn_norm,
           mlstm_conv_w, mlstm_conv_b, mlstm_gate_b, mlstm_norm, w_branch_a, w_branch_b, w_out,
           w_router, w_expert_gate, w_expert_up, w_expert_down):
    Bn, T, D = h_lat.shape
    n_lat = T // TOK
    n_tok = T + TOK
    A, Bh = A_HEADS, B_HEADS
    n_main = (5 * A + 6 * Bh) * LANE + 2 * D
    g0 = (5 * A + 6 * Bh) * LANE

    mod = _modulation(c8, w_ada[l], b_ada[l])
    mod3 = mod.reshape(8, 1, N_MOD * D)

    u = _prenorm(h_lat, ctx, g_norm[l, 0], mod3)

    w_l = w_in[l]
    w_main = jnp.concatenate([w_l[:, :g0], w_l[:, g0 + 4 * Bh:]], axis=1).astype(BF16)
    w_gate = jnp.pad(w_l[:, g0:g0 + 4 * Bh], ((0, 0), (0, LANE - 4 * Bh))).astype(BF16)
    gate_bias = jnp.pad(mlstm_gate_b[l].reshape(4 * Bh), (0, LANE - 4 * Bh))
    u2 = u.reshape(Bn * n_tok, D)
    p3 = _matmul(u2, w_main, F32, name="in_proj").reshape(Bn, n_tok, n_main)
    gates = _matmul(u2, w_gate, F32, tn=LANE, bias=gate_bias, name="in_proj_gates")

    g6 = gates[:, :4 * Bh].reshape(Bn, n_tok // CHUNK, CHUNK, 2, 2, Bh)
    g_rows = g6.transpose(0, 3, 5, 1, 4, 2)
    g_cols = g6.transpose(0, 3, 5, 1, 2, 4)
    conv4 = jnp.concatenate([mlstm_conv_w[l], mlstm_conv_b[l][None]], axis=0)
    conv4 = conv4.reshape(4, 1, 2 * Bh * B_DQK)

    oa = _hgrn_scan(p3, hgrn_f_bias[l], lb_l, n_lat)
    ob = _mlstm_scan(p3, g_rows, g_cols, conv4, n_lat)
    ya, yb = _readout(oa, ob, p3, hgrn_norm[l], mlstm_norm[l], n_lat)
    merged = _merge(ya, yb, w_branch_a[l].astype(BF16), w_branch_b[l].astype(BF16), p3, D)
    mix = _matmul(merged.reshape(Bn * T, D), w_out[l].astype(BF16), F32, name="out_proj")

    w_router_pad = jnp.pad(w_router[l], ((0, 0), (0, LANE - N_EXPERTS)))
    h_lat, v_lat, aff = _resid_router(h_lat, mix.reshape(Bn, T, D), g_norm[l, 1], g_norm[l, 2],
                                      mod3, w_router_pad)

    cap = CAPACITY * T // N_EXPERTS
    aff_rows = aff[:, :, :N_EXPERTS].transpose(0, 2, 1)
    slot, idx, offs = _select(aff_rows, cap)
    xg = _gather(idx, v_lat, cap)
    hid = _ffn1(xg.reshape(N_EXPERTS, Bn * cap, D), w_expert_gate[l], w_expert_up[l])
    y = _ffn2(hid, w_expert_down[l]).reshape(N_EXPERTS, Bn, cap, D)
    return _scatter_combine(offs, slot.reshape(Bn, N_EXPERTS, 1, T),
                            aff_rows.reshape(Bn, N_EXPERTS, 1, T), y, cap, h_lat, g_norm[l, 3], mod3)


def kernel(x, c, ctx, c_ctx, w_ada, b_ada, g_norm, w_in, hgrn_f_bias, hgrn_lb, hgrn_norm,
           mlstm_conv_w, mlstm_conv_b, mlstm_gate_b, mlstm_norm, w_branch_a, w_branch_b, w_out,
           w_router, w_expert_gate, w_expert_up, w_expert_down):
    Bn, T, D = x.shape
    depth = w_ada.shape[0]
    assert depth == 1, "context outputs are only produced for the state hand-off (single layer)"
    assert ctx.shape[1] == TOK and T % TOK == 0 and Bn < 8
    lb_all = jnp.cumsum(jax.nn.softmax(hgrn_lb.astype(F32), axis=1), axis=1)
    c8 = jnp.zeros((8, D), F32).at[:Bn].set(c).at[Bn].set(c_ctx)
    h_lat = x
    for l in range(depth):
        h_lat = _layer(h_lat, ctx, c8, l, lb_all[:, l], w_ada, b_ada, g_norm, w_in, hgrn_f_bias,
                       hgrn_norm, mlstm_conv_w, mlstm_conv_b, mlstm_gate_b, mlstm_norm,
                       w_branch_a, w_branch_b, w_out, w_router, w_expert_gate, w_expert_up,
                       w_expert_down)
    return h_lat.astype(x.dtype)
```

```python
import functools

import jax
import jax.numpy as jnp
from jax import lax
from jax.experimental import pallas as pl
from jax.experimental.pallas import tpu as pltpu

F32 = jnp.float32
BF16 = jnp.bfloat16

EPS = 1e-6
N_MOD = 6
A_HEADS = 16
A_D = 128
B_HEADS = 8
B_DQK = 128
B_DV = 256
N_EXPERTS = 16
CAPACITY = 2
CHUNK = 64
SUB = 16
A_HPS = 4
B_HPS = 4
TOK = 256
NCH = TOK // CHUNK
NSUB = CHUNK // SUB
LANE = 128
VMEM_LIMIT = 56 * 1024 * 1024

_NT = (((1,), (1,)), ((), ()))


def _pick(n, pref):
    t = min(n, pref)
    while n % t:
        t //= 2
    return t


def _cparams(sem):
    return pltpu.CompilerParams(dimension_semantics=sem, vmem_limit_bytes=VMEM_LIMIT)


def _silu(x):
    return x / (1.0 + jnp.exp(-x))


def _sigmoid(x):
    return 1.0 / (1.0 + jnp.exp(-x))


def _sig_pair(z):
    t = jnp.exp(-jnp.abs(z))
    r = 1.0 / (1.0 + t)
    tr = t * r
    pos = z >= 0
    return jnp.where(pos, r, tr), jnp.where(pos, tr, r)


def _log_sigmoid(x):
    return jnp.minimum(x, 0.0) - jnp.log(1.0 + jnp.exp(-jnp.abs(x)))


def _rms(xf, w):
    return xf * lax.rsqrt(jnp.mean(xf * xf, axis=-1, keepdims=True) + EPS) * w


def _rows(x, r, n):
    return jnp.broadcast_to(x[r:r + 1], (n, x.shape[1]))


def _mod_kernel(c_ref, w_ref, b_ref, o_ref):
    a = _silu(c_ref[...]).astype(BF16)
    o_ref[...] = jnp.dot(a, w_ref[...].astype(BF16), preferred_element_type=F32) + b_ref[...]


def _modulation(c8, w, b):
    D, N = w.shape
    tn = _pick(N, 512)
    return pl.pallas_call(
        _mod_kernel,
        grid=(N // tn,),
        in_specs=[pl.BlockSpec((8, D), lambda j: (0, 0)),
                  pl.BlockSpec((D, tn), lambda j: (0, j)),
                  pl.BlockSpec((1, tn), lambda j: (0, j))],
        out_specs=pl.BlockSpec((8, tn), lambda j: (0, j)),
        out_shape=jax.ShapeDtypeStruct((8, N), F32),
        compiler_params=_cparams(("parallel",)),
        name="adaln_mod",
    )(c8, w, b.reshape(1, N))


def _prenorm_kernel(x_ref, ctx_ref, g_ref, sh_ref, sc_ref, o_ref, *, n_lat):
    i = pl.program_id(1)
    g = g_ref[...]
    sh = sh_ref[0]
    sc = sc_ref[0]

    @pl.when(i < n_lat)
    def _():
        o_ref[0] = (_rms(x_ref[0], g) * (1.0 + sc) + sh).astype(o_ref.dtype)

    @pl.when(i == n_lat)
    def _():
        o_ref[0] = (_rms(ctx_ref[0], g) * (1.0 + sc) + sh).astype(o_ref.dtype)


def _prenorm(x, ctx, g, mod3):
    Bn, T, D = x.shape
    n_lat = T // TOK
    row = lambda b, i: jnp.where(i == n_lat, Bn, b)
    return pl.pallas_call(
        functools.partial(_prenorm_kernel, n_lat=n_lat),
        grid=(Bn, n_lat + 1),
        in_specs=[pl.BlockSpec((1, TOK, D), lambda b, i: (b, jnp.minimum(i, n_lat - 1), 0)),
                  pl.BlockSpec((1, TOK, D), lambda b, i: (b, 0, 0)),
                  pl.BlockSpec((1, D), lambda b, i: (0, 0)),
                  pl.BlockSpec((1, 1, D), lambda b, i: (row(b, i), 0, 0)),
                  pl.BlockSpec((1, 1, D), lambda b, i: (row(b, i), 0, 1))],
        out_specs=pl.BlockSpec((1, TOK, D), lambda b, i: (b, i, 0)),
        out_shape=jax.ShapeDtypeStruct((Bn, T + TOK, D), BF16),
        compiler_params=_cparams(("parallel", "arbitrary")),
        name="prenorm",
    )(x, ctx, g.reshape(1, D), mod3, mod3)


def _mm_kernel(a_ref, b_ref, o_ref):
    o_ref[...] = jnp.dot(a_ref[...], b_ref[...].astype(BF16),
                         preferred_element_type=F32).astype(o_ref.dtype)


def _mm_bias_kernel(a_ref, b_ref, bias_ref, o_ref):
    o_ref[...] = (jnp.dot(a_ref[...], b_ref[...].astype(BF16), preferred_element_type=F32)
                  + bias_ref[...]).astype(o_ref.dtype)


def _matmul(a, b, out_dtype, tm=1024, tn=512, bias=None, n_cols=None, name="matmul"):
    M, K = a.shape
    N = b.shape[1] if n_cols is None else n_cols
    tm = _pick(M, tm)
    tn = _pick(N, tn)
    in_specs = [pl.BlockSpec((tm, K), lambda i, j: (i, 0)),
                pl.BlockSpec((K, tn), lambda i, j: (0, j))]
    args = [a, b]
    kern = _mm_kernel
    if bias is not None:
        in_specs.append(pl.BlockSpec((1, tn), lambda i, j: (0, j)))
        args.append(bias.reshape(1, N))
        kern = _mm_bias_kernel
    return pl.pallas_call(
        kern,
        grid=(M // tm, N // tn),
        in_specs=in_specs,
        out_specs=pl.BlockSpec((tm, tn), lambda i, j: (i, j)),
        out_shape=jax.ShapeDtypeStruct((M, N), out_dtype),
        compiler_params=_cparams(("parallel", "arbitrary")),
        name=name,
    )(*args)


def _tok_block(rev, j, n_lat):
    lat = n_lat - j if rev else j - 1
    return jnp.where(j == 0, n_lat, lat)


def _out_block(rev, j, n_lat):
    return jnp.clip(n_lat - j if rev else j - 1, 0, n_lat - 1)


def _chunk_masks(rev):
    row = lax.broadcasted_iota(jnp.int32, (CHUNK, CHUNK), 0)
    col = lax.broadcasted_iota(jnp.int32, (CHUNK, CHUNK), 1)
    return (col >= row, col <= row) if rev else (col <= row, col >= row)


def _hgrn_consts():
    t = jnp.arange(TOK)
    same = (t[:, None] // CHUNK) == (t[None, :] // CHUNK)
    sub_r, sub_c = t[:, None] // SUB, t[None, :] // SUB
    fwd = jnp.concatenate([same & (t[None, :] <= t[:, None]), same & (sub_c < sub_r)], axis=0)
    bwd = jnp.concatenate([same & (t[None, :] >= t[:, None]), same & (sub_c > sub_r)], axis=0)
    masks = jnp.stack([fwd, bwd]).astype(BF16)
    r = jnp.arange(SUB * A_D)
    emat = ((r[:, None] // A_D) == (jnp.arange(LANE)[None, :] % SUB)).astype(BF16)
    return masks, emat


def _hgrn_state(af, v, fb, lb, mask_ref, st_ref, rev):
    z = af + fb
    sp, sn = _sig_pair(z)
    k = (1.0 - lb) * sn
    lf2 = jnp.log2(lb + (1.0 - lb) * sp)
    lk2 = jnp.log2(k)

    hi = lf2.astype(BF16)
    r1 = lf2 - hi.astype(F32)
    mid = r1.astype(BF16)
    lo = (r1 - mid.astype(F32)).astype(BF16)
    cs = jnp.dot(mask_ref[0], jnp.concatenate([hi, mid, lo], axis=1), preferred_element_type=F32)
    cs = cs[:, :A_D] + cs[:, A_D:2 * A_D] + cs[:, 2 * A_D:]
    b = cs[:TOK]
    ent = cs[TOK:]
    c_all = lk2 - b

    last = [(c * CHUNK if rev else c * CHUNK + CHUNK - 1) for c in range(NCH)]
    tot_rows = jnp.concatenate([_rows(b, last[c], CHUNK) for c in range(NCH)], axis=0)
    khat = jnp.exp2(jnp.minimum(tot_rows + c_all, lk2)).astype(BF16)
    upd = [jnp.dot(v[c * CHUNK:(c + 1) * CHUNK].T.astype(BF16), khat[c * CHUNK:(c + 1) * CHUNK],
                   preferred_element_type=F32) for c in range(NCH)]

    order = range(NCH - 1, -1, -1) if rev else range(NCH)
    st = st_ref[...]
    st_in = [None] * NCH
    for c in order:
        st_in[c] = st
        st = st * jnp.exp2(b[last[c]:last[c] + 1]) + upd[c]
    st_ref[...] = st
    return v, b, ent, c_all, lk2, st_in


def _hgrn_output(aq, feats, emat_ref, o_ref, hs, rev):
    v, b, ent, c_all, lk2, st_in = feats
    q = _silu(aq)
    qb = q.astype(BF16)
    vb = v.astype(BF16)
    qt = (q * jnp.exp2(b - ent)).astype(BF16)
    row = lax.broadcasted_iota(jnp.int32, (CHUNK, CHUNK), 0)
    col = lax.broadcasted_iota(jnp.int32, (CHUNK, CHUNK), 1)
    rs = jnp.right_shift(row, SUB.bit_length() - 1)
    cs_ = jnp.right_shift(col, SUB.bit_length() - 1)
    prev_blk = (cs_ > rs) if rev else (cs_ < rs)
    diag_blk = (rs == cs_) & ((col >= row) if rev else (col <= row))

    off = [[None] * NSUB for _ in range(NCH)]
    for i in range(NSUB):
        e_i = jnp.concatenate([_rows(ent, c * CHUNK + i * SUB, CHUNK) for c in range(NCH)], axis=0)
        kt = jnp.exp2(jnp.minimum(e_i + c_all, lk2)).astype(BF16)
        for c in range(NCH):
            r0 = c * CHUNK + i * SUB
            off[c][i] = lax.dot_general(qt[r0:r0 + SUB], kt[c * CHUNK:(c + 1) * CHUNK], _NT,
                                        preferred_element_type=F32)

    half = SUB // 2
    nsb = TOK // SUB
    zeros = jnp.zeros((half, A_D), F32)
    ps = []
    for sl in range(SUB):
        n = half if (sl < half if rev else sl >= half) else SUB
        r0 = SUB - n if not rev else 0
        bq = b if n == SUB else jnp.concatenate(
            [b[g * SUB + r0:g * SUB + r0 + n] for g in range(nsb)], axis=0)
        cref = jnp.concatenate([_rows(c_all, g * SUB + sl, n) for g in range(nsb)], axis=0)
        kref = jnp.concatenate([_rows(lk2, g * SUB + sl, n) for g in range(nsb)], axis=0)
        e = jnp.exp2(jnp.minimum(bq + cref, kref))
        if n != SUB:
            parts = []
            for g in range(nsb):
                piece = e[g * half:(g + 1) * half]
                parts += [piece, zeros] if rev else [zeros, piece]
            e = jnp.concatenate(parts, axis=0)
        ps.append(e.astype(BF16) * qb)
    acc = jnp.dot(jnp.concatenate(ps, axis=1), emat_ref[...], preferred_element_type=F32)

    qhat = (q * jnp.exp2(b)).astype(BF16)
    for c in range(NCH):
        sl_c = slice(c * CHUNK, (c + 1) * CHUNK)
        attn = (jnp.where(prev_blk, jnp.concatenate(off[c], axis=0), 0.0)
                + jnp.where(diag_blk, acc[sl_c, :CHUNK], 0.0))
        o = jnp.dot(attn.astype(BF16), vb[sl_c], preferred_element_type=F32)
        o = o + lax.dot_general(qhat[sl_c], st_in[c].astype(BF16), _NT, preferred_element_type=F32)
        o_ref[0, sl_c, hs] = o


def _hgrn_kernel(aqf, aff, aif, aqb, afb, aib, fb_ref, lb_ref, mask_ref, emat_ref, of_ref, ob_ref,
                 stf_ref, stb_ref):
    j = pl.program_id(2)

    @pl.when(j == 0)
    def _():
        stf_ref[...] = jnp.zeros_like(stf_ref)
        stb_ref[...] = jnp.zeros_like(stb_ref)

    feats = []
    for hh in range(A_HPS):
        hs = slice(hh * A_D, (hh + 1) * A_D)
        feats.append((hs,
                      _hgrn_state(aff[0, :, hs], aif[0, :, hs], fb_ref[0][:, hs], lb_ref[0][:, hs],
                                  mask_ref.at[0:1], stf_ref.at[hh], False),
                      _hgrn_state(afb[0, :, hs], aib[0, :, hs], fb_ref[1][:, hs], lb_ref[1][:, hs],
                                  mask_ref.at[1:2], stb_ref.at[hh], True)))
    for hs, feats_f, feats_b in feats:
        _hgrn_output(aqf[0, :, hs], feats_f, emat_ref, of_ref, hs, False)
        _hgrn_output(aqb[0, :, hs], feats_b, emat_ref, ob_ref, hs, True)


def _hgrn_scan(p3, f_bias, lb, n_lat):
    Bn = p3.shape[0]
    A = A_HEADS
    masks, emat = _hgrn_consts()

    W = A_HPS * A_D
    G = A // A_HPS

    def feat(rev, grp):
        return pl.BlockSpec((1, TOK, W), lambda b, h, j: (b, _tok_block(rev, j, n_lat), grp * G + h))

    def outp(rev):
        return pl.BlockSpec((1, TOK, W), lambda b, h, j: (b, _out_block(rev, j, n_lat), h))

    par = pl.BlockSpec((2, 1, W), lambda b, h, j: (0, 0, h))
    osh = jax.ShapeDtypeStruct((Bn, n_lat * TOK, A * A_D), F32)
    return pl.pallas_call(
        _hgrn_kernel,
        grid=(Bn, G, n_lat + 1),
        in_specs=[feat(False, 0), feat(False, 1), feat(False, 3),
                  feat(True, 0), feat(True, 2), feat(True, 3),
                  par, par,
                  pl.BlockSpec((2, 2 * TOK, TOK), lambda b, h, j: (0, 0, 0)),
                  pl.BlockSpec((SUB * A_D, LANE), lambda b, h, j: (0, 0))],
        out_specs=[outp(False), outp(True)],
        out_shape=[osh, osh],
        scratch_shapes=[pltpu.VMEM((A_HPS, A_D, A_D), F32), pltpu.VMEM((A_HPS, A_D, A_D), F32)],
        compiler_params=_cparams(("parallel", "parallel", "arbitrary")),
        name="hgrn2_scan",
    )(p3, p3, p3, p3, p3, p3, f_bias.reshape(2, 1, A * A_D), lb.reshape(2, 1, A * A_D), masks, emat)


def _mlstm_state(bq_ref, bk_ref, bv_ref, gr_ref, gc_ref, cwq, cwk, first, last_, c_ref, n_ref, m_ref,
                 hh, rev):
    def conv(u, cw):
        up = jnp.where(first, 0.0, pltpu.roll(u, 1, 0))
        dn = jnp.where(last_, 0.0, pltpu.roll(u, TOK - 1, 0))
        return _silu(cw[3:4] + up * cw[0:1] + u * cw[1:2] + dn * cw[2:3])

    hs = slice(hh * B_DQK, (hh + 1) * B_DQK)
    vs = slice(hh * B_DV, (hh + 1) * B_DV)
    c_ref, n_ref, m_ref = c_ref.at[hh], n_ref.at[hh], m_ref.at[hh]
    q_all = conv(bq_ref[0, :, hs], cwq[:, hs])
    k_all = conv(bk_ref[0, :, hs], cwk[:, hs]) * (B_DQK ** -0.5)
    seen, seen_t = _chunk_masks(rev)

    pre = []
    for c in range(NCH):
        sl_c = slice(c * CHUNK, (c + 1) * CHUNK)
        k = k_all[sl_c]
        vb = bv_ref[0, sl_c, vs].astype(BF16)
        g_r = gr_ref[0, 0, hh, c]
        g_c = gc_ref[0, 0, hh, c]
        ii_r = g_r[0:1]
        lf_r = _log_sigmoid(g_r[1:2])
        ii_c = g_c[:, 0:1]
        lf_c = _log_sigmoid(g_c[:, 1:2])
        b_c = jnp.sum(jnp.where(seen, lf_r, 0.0), axis=1, keepdims=True)
        b_r = jnp.sum(jnp.where(seen_t, lf_c, 0.0), axis=0, keepdims=True)
        total = jnp.sum(lf_r, axis=1, keepdims=True)
        logs = total - b_c + ii_c
        ms = jnp.max(logs, axis=0, keepdims=True)
        kw = k * jnp.exp(logs - ms)
        upd = jnp.dot(kw.T.astype(BF16), vb, preferred_element_type=F32)
        nupd = jnp.sum(kw, axis=0, keepdims=True)
        pre.append((vb, ii_r, b_c, b_r, total, ms, upd, nupd))

    cmat, nvec, m = c_ref[...], n_ref[...], m_ref[:, 0:1]
    st_in = [None] * NCH
    for c in (range(NCH - 1, -1, -1) if rev else range(NCH)):
        _, _, _, _, total, ms, upd, nupd = pre[c]
        st_in[c] = (cmat, nvec, m)
        m_new = jnp.maximum(total + m, ms)
        dec = jnp.exp(total + m - m_new)
        sc = jnp.exp(ms - m_new)
        cmat = dec * cmat + sc * upd
        nvec = dec * nvec + sc * nupd
        m = m_new
    c_ref[...] = cmat
    n_ref[...] = nvec
    m_ref[...] = jnp.broadcast_to(m, m_ref.shape)
    return q_all, k_all, pre, st_in


def _mlstm_output(feats, o_ref, hh, rev):
    q_all, k_all, pre, st_in = feats
    seen, _ = _chunk_masks(rev)
    for c in range(NCH):
        sl_c = slice(c * CHUNK, (c + 1) * CHUNK)
        vb, ii_r, b_c, b_r, _, _, _, _ = pre[c]
        cm, nv, m0 = st_in[c]
        q = q_all[sl_c]
        qb = q.astype(BF16)
        kb = k_all[sl_c].astype(BF16)
        logw = jnp.where(seen, b_c - b_r + ii_r, -jnp.inf)
        mw = jnp.max(logw, axis=1, keepdims=True)
        qk = lax.dot_general(qb, kb, _NT, preferred_element_type=F32) * jnp.exp(logw - mw)
        num0 = jnp.dot(qk.astype(BF16), vb, preferred_element_type=F32)
        den0 = jnp.sum(qk, axis=1, keepdims=True)
        log_inter = b_c + m0
        m_t = jnp.maximum(mw, log_inter)
        r = jnp.exp(mw - m_t)
        a = jnp.exp(log_inter - m_t)
        num = r * num0 + a * jnp.dot(qb, cm.astype(BF16), preferred_element_type=F32)
        den = r * den0 + a * jnp.sum(q * nv, axis=1, keepdims=True)
        o_ref[0, sl_c, hh * B_DV:(hh + 1) * B_DV] = num / jnp.maximum(jnp.abs(den), jnp.exp(-m_t))


def _mlstm_kernel(bqf, bkf, bvf, grf, gcf, bqb, bkb, bvb, grb, gcb, cwq_ref, cwk_ref, of_ref, ob_ref,
                  cf_ref, nf_ref, mf_ref, cb_ref, nb_ref, mb_ref):
    j = pl.program_id(2)

    @pl.when(j == 0)
    def _():
        for ref in (cf_ref, nf_ref, mf_ref, cb_ref, nb_ref, mb_ref):
            ref[...] = jnp.zeros_like(ref)

    t = lax.broadcasted_iota(jnp.int32, (TOK, 1), 0)
    row_mask = jnp.where(j == 0, TOK - 1, CHUNK - 1)
    pos = t & row_mask
    first = pos == 0
    last_ = pos == row_mask
    cwq = cwq_ref[:, 0, :]
    cwk = cwk_ref[:, 0, :]
    feats = []
    for hh in range(B_HPS):
        feats.append((_mlstm_state(bqf, bkf, bvf, grf, gcf, cwq, cwk, first, last_, cf_ref, nf_ref, mf_ref,
                                   hh, False),
                      _mlstm_state(bqb, bkb, bvb, grb, gcb, cwq, cwk, first, last_, cb_ref, nb_ref, mb_ref,
                                   hh, True)))
    for hh, (feats_f, feats_b) in enumerate(feats):
        _mlstm_output(feats_f, of_ref, hh, False)
        _mlstm_output(feats_b, ob_ref, hh, True)


def _mlstm_scan(p3, g_rows, g_cols, conv4, n_lat):
    Bn = p3.shape[0]
    A, Bh = A_HEADS, B_HEADS
    P = B_HPS
    G = Bh // P
    q0 = 5 * A // P
    k0 = (5 * A + Bh) // P
    v0 = (5 * A + 2 * Bh) // (2 * P)

    def specs(rev):
        d = int(rev)
        tb = lambda j: _tok_block(rev, j, n_lat)
        return [pl.BlockSpec((1, TOK, P * B_DQK), lambda b, h, j: (b, tb(j), q0 + h)),
                pl.BlockSpec((1, TOK, P * B_DQK), lambda b, h, j: (b, tb(j), k0 + h)),
                pl.BlockSpec((1, TOK, P * B_DV), lambda b, h, j: (b, tb(j), v0 + h)),
                pl.BlockSpec((1, 1, P, NCH, 2, CHUNK), lambda b, h, j: (b, d, h, tb(j), 0, 0)),
                pl.BlockSpec((1, 1, P, NCH, CHUNK, 2), lambda b, h, j: (b, d, h, tb(j), 0, 0))]

    def outp(rev):
        return pl.BlockSpec((1, TOK, P * B_DV), lambda b, h, j: (b, _out_block(rev, j, n_lat), h))

    osh = jax.ShapeDtypeStruct((Bn, n_lat * TOK, Bh * B_DV), F32)
    state = [pltpu.VMEM((P, B_DQK, B_DV), F32), pltpu.VMEM((P, 1, B_DQK), F32),
             pltpu.VMEM((P, 1, LANE), F32)]
    return pl.pallas_call(
        _mlstm_kernel,
        grid=(Bn, G, n_lat + 1),
        in_specs=specs(False) + specs(True) + [
            pl.BlockSpec((4, 1, P * B_DQK), lambda b, h, j: (0, 0, h)),
            pl.BlockSpec((4, 1, P * B_DQK), lambda b, h, j: (0, 0, G + h))],
        out_specs=[outp(False), outp(True)],
        out_shape=[osh, osh],
        scratch_shapes=state + state,
        compiler_params=_cparams(("parallel", "parallel", "arbitrary")),
        name="mlstm_scan",
    )(p3, p3, p3, g_rows, g_cols, p3, p3, p3, g_rows, g_cols, conv4, conv4)


def _readout_kernel(oaf_ref, oab_ref, obf_ref, obb_ref, ag_ref, bo_ref, na_ref, nb_ref, ya_ref, yb_ref):
    for h in range(A_HEADS):
        sl = slice(h * A_D, (h + 1) * A_D)
        o = oaf_ref[0, :, sl] + oab_ref[0, :, sl]
        ya_ref[0, :, sl] = (_rms(o, na_ref[:, sl]) * _silu(ag_ref[0, :, sl])).astype(ya_ref.dtype)
    for h in range(B_HEADS):
        sl = slice(h * B_DV, (h + 1) * B_DV)
        o = obf_ref[0, :, sl] + obb_ref[0, :, sl]
        yb_ref[0, :, sl] = (_rms(o, nb_ref[:, sl]) * _sigmoid(bo_ref[0, :, sl])).astype(yb_ref.dtype)


def _readout(oa, ob, p3, norm_a, norm_b, n_lat):
    Bn = p3.shape[0]
    T = n_lat * TOK
    wa = A_HEADS * A_D
    wb = B_HEADS * B_DV
    ag_blk = 4
    bo_blk = (5 * A_HEADS + 4 * B_HEADS) * LANE // wb
    sa = pl.BlockSpec((1, TOK, wa), lambda b, i: (b, i, 0))
    sb = pl.BlockSpec((1, TOK, wb), lambda b, i: (b, i, 0))
    return pl.pallas_call(
        _readout_kernel,
        grid=(Bn, n_lat),
        in_specs=[sa, sa, sb, sb,
                  pl.BlockSpec((1, TOK, wa), lambda b, i: (b, i, ag_blk)),
                  pl.BlockSpec((1, TOK, wb), lambda b, i: (b, i, bo_blk)),
                  pl.BlockSpec((1, wa), lambda b, i: (0, 0)),
                  pl.BlockSpec((1, wb), lambda b, i: (0, 0))],
        out_specs=[sa, sb],
        out_shape=[jax.ShapeDtypeStruct((Bn, T, wa), BF16),
                   jax.ShapeDtypeStruct((Bn, T, wb), BF16)],
        compiler_params=_cparams(("parallel", "parallel")),
        name="readout",
    )(oa[0], oa[1], ob[0], ob[1], p3, p3, norm_a.reshape(1, wa), norm_b.reshape(1, wb))


def _merge_kernel(ya_ref, yb_ref, wa_ref, wb_ref, ga_ref, gb_ref, o_ref):
    pa = jnp.dot(ya_ref[0], wa_ref[...], preferred_element_type=F32)
    pb = jnp.dot(yb_ref[0], wb_ref[...], preferred_element_type=F32)
    o_ref[0] = (_sigmoid(ga_ref[0]) * pa + _sigmoid(gb_ref[0]) * pb).astype(o_ref.dtype)


def _merge(ya, yb, wa, wb, p3, D):
    Bn, T, ka = ya.shape
    kb = yb.shape[2]
    tm = _pick(T, 1024)
    tn = _pick(D, 512)
    ga0 = 0
    gb0 = D // tn
    return pl.pallas_call(
        _merge_kernel,
        grid=(Bn, T // tm, D // tn),
        in_specs=[pl.BlockSpec((1, tm, ka), lambda b, i, j: (b, i, 0)),
                  pl.BlockSpec((1, tm, kb), lambda b, i, j: (b, i, 0)),
                  pl.BlockSpec((ka, tn), lambda b, i, j: (0, j)),
                  pl.BlockSpec((kb, tn), lambda b, i, j: (0, j)),
                  pl.BlockSpec((1, tm, tn), lambda b, i, j: (b, i, ga0 + j)),
                  pl.BlockSpec((1, tm, tn), lambda b, i, j: (b, i, gb0 + j))],
        out_specs=pl.BlockSpec((1, tm, tn), lambda b, i, j: (b, i, j)),
        out_shape=jax.ShapeDtypeStruct((Bn, T, D), BF16),
        compiler_params=_cparams(("parallel", "parallel", "arbitrary")),
        name="merge",
    )(ya, yb, wa, wb, p3, p3)


def _resid_router_kernel(x_ref, mix_ref, g1_ref, g2_ref, gate_ref, sh_ref, sc_ref, wr_ref,
                         h_ref, v_ref, aff_ref):
    h = x_ref[0] + gate_ref[0] * _rms(mix_ref[0], g1_ref[...])
    h_ref[0] = h
    v = _rms(h, g2_ref[...]) * (1.0 + sc_ref[0]) + sh_ref[0]
    v_ref[0] = v
    logits = jnp.dot(v, wr_ref[...], precision=lax.Precision.HIGHEST, preferred_element_type=F32)
    lane = lax.broadcasted_iota(jnp.int32, logits.shape, 1)
    logits = jnp.where(lane < N_EXPERTS, logits, -jnp.inf)
    e = jnp.exp(logits - jnp.max(logits, axis=-1, keepdims=True))
    aff_ref[0] = e / jnp.sum(e, axis=-1, keepdims=True)


def _resid_router(x, mix, g1, g2, mod3, w_router_pad):
    Bn, T, D = x.shape
    mspec = lambda k: pl.BlockSpec((1, 1, D), lambda b, i: (b, 0, k))
    tok = pl.BlockSpec((1, TOK, D), lambda b, i: (b, i, 0))
    vec = pl.BlockSpec((1, D), lambda b, i: (0, 0))
    return pl.pallas_call(
        _resid_router_kernel,
        grid=(Bn, T // TOK),
        in_specs=[tok, tok, vec, vec, mspec(2), mspec(3), mspec(4),
                  pl.BlockSpec((D, LANE), lambda b, i: (0, 0))],
        out_specs=[tok, tok, pl.BlockSpec((1, TOK, LANE), lambda b, i: (b, i, 0))],
        out_shape=[jax.ShapeDtypeStruct((Bn, T, D), F32),
                   jax.ShapeDtypeStruct((Bn, T, D), F32),
                   jax.ShapeDtypeStruct((Bn, T, LANE), F32)],
        compiler_params=_cparams(("parallel", "parallel")),
        name="resid_router",
    )(x, mix, g1.reshape(1, D), g2.reshape(1, D), mod3, mod3, mod3, w_router_pad)


def _select_kernel(aff_ref, slot_ref, idx_ref, offs_ref, *, cap, tk):
    E, T = aff_ref.shape[1], aff_ref.shape[2]
    nck = T // LANE

    def key():
        return lax.bitcast_convert_type(aff_ref[0], jnp.int32)

    def bit_step(i, tau):
        cand = tau | jnp.left_shift(jnp.int32(1), 30 - i)
        cnt = jnp.sum(jnp.where(key() >= cand, 1.0, 0.0), axis=1, keepdims=True)
        return jnp.where(cnt >= cap, cand, tau)

    tau = lax.fori_loop(0, 31, bit_step, jnp.zeros((E, 1), jnp.int32))
    kk = key()
    gt = kk > tau
    eq = kk == tau
    need = cap - jnp.sum(jnp.where(gt, 1.0, 0.0), axis=1, keepdims=True)

    r = lax.broadcasted_iota(jnp.int32, (LANE, LANE), 0)
    c = lax.broadcasted_iota(jnp.int32, (LANE, LANE), 1)
    upper = jnp.where(r < c, 1.0, 0.0).astype(BF16)
    tr = jnp.right_shift(lax.broadcasted_iota(jnp.int32, (T, LANE), 0), LANE.bit_length() - 1)
    member = jnp.where(tr == lax.broadcasted_iota(jnp.int32, (T, LANE), 1), 1.0, 0.0).astype(BF16)

    def prefix(flags):
        x = jnp.where(flags, 1.0, 0.0).astype(BF16)
        tot = jnp.dot(x, member, preferred_element_type=F32)
        offs = jnp.dot(tot.astype(BF16), upper, preferred_element_type=F32)
        parts = [jnp.dot(x[:, j * LANE:(j + 1) * LANE], upper, preferred_element_type=F32)
                 + offs[:, j:j + 1] for j in range(nck)]
        return jnp.concatenate(parts, axis=1), offs

    tie_rank, _ = prefix(eq)
    sel = gt | (eq & (tie_rank < need))
    pos, offs = prefix(sel)
    slot = jnp.where(sel, pos, -1.0)
    slot_ref[0] = slot
    offs_ref[0] = offs.astype(jnp.int32)

    tt = lax.broadcasted_iota(jnp.int32, (8, T), 1)
    rr = lax.broadcasted_iota(jnp.int32, (8, T), 0)
    digits = jnp.where(rr == 0, jnp.right_shift(tt, 6), jnp.where(rr == 1, tt & 63, 0))
    digits = digits.astype(F32).astype(BF16)
    slot_iota = lax.broadcasted_iota(jnp.int32, (cap, tk), 0).astype(F32)
    for e in range(E):
        acc = jnp.zeros((8, cap), F32)
        for kc in range(T // tk):
            ks = slice(kc * tk, (kc + 1) * tk)
            oh = jnp.where(slot[e:e + 1, ks] == slot_iota, 1.0, 0.0).astype(BF16)
            acc = acc + lax.dot_general(digits[:, ks], oh, _NT, preferred_element_type=F32)
        idx_ref[0, e:e + 1, :] = (acc[0:1] * 64.0 + acc[1:2]).astype(jnp.int32)


def _select(aff_rows, cap):
    Bn, E, T = aff_rows.shape
    assert T // LANE < LANE and T <= 4096
    tk = _pick(T, 1024)
    row = lambda n: pl.BlockSpec((1, E, n), lambda b: (b, 0, 0))
    return pl.pallas_call(
        functools.partial(_select_kernel, cap=cap, tk=tk),
        grid=(Bn,),
        in_specs=[row(T)],
        out_specs=[row(T), row(cap), row(LANE)],
        out_shape=[jax.ShapeDtypeStruct((Bn, E, T), F32),
                   jax.ShapeDtypeStruct((Bn, E, cap), jnp.int32),
                   jax.ShapeDtypeStruct((Bn, E, LANE), jnp.int32)],
        compiler_params=_cparams(("parallel",)),
        name="ec_select",
    )(aff_rows)


def _gather_kernel(idx_ref, v_hbm, o_ref, buf, sem, *, cap, n_tok):
    s = pl.program_id(0) * N_EXPERTS + pl.program_id(1)
    n_steps = pl.num_programs(0) * N_EXPERTS

    def row_copy(step, r, tok):
        slot = step % 2
        row = (step // N_EXPERTS) * n_tok + tok
        return pltpu.make_async_copy(v_hbm.at[pl.ds(row, 1), :], buf.at[slot, pl.ds(r, 1), :],
                                     sem.at[slot])

    def issue_all(step):
        def issue(r, carry):
            row_copy(step, r, idx_ref[step * cap + r]).start()
            return carry
        lax.fori_loop(0, cap, issue, 0)

    @pl.when(s == 0)
    def _():
        issue_all(s)

    @pl.when(s + 1 < n_steps)
    def _():
        issue_all(s + 1)

    def drain(r, carry):
        row_copy(s, r, 0).wait()
        return carry

    lax.fori_loop(0, cap, drain, 0)
    o_ref[0, 0] = buf[s % 2].astype(o_ref.dtype)


def _gather(idx, v, cap):
    Bn, T, D = v.shape
    return pl.pallas_call(
        functools.partial(_gather_kernel, cap=cap, n_tok=T),
        grid_spec=pltpu.PrefetchScalarGridSpec(
            num_scalar_prefetch=1,
            grid=(Bn, N_EXPERTS),
            in_specs=[pl.BlockSpec(memory_space=pl.ANY)],
            out_specs=pl.BlockSpec((1, 1, cap, D), lambda b, e, idx_ref: (e, b, 0, 0)),
            scratch_shapes=[pltpu.VMEM((2, cap, D), F32), pltpu.SemaphoreType.DMA((2,))]),
        out_shape=jax.ShapeDtypeStruct((N_EXPERTS, Bn, cap, D), BF16),
        compiler_params=_cparams(("arbitrary", "arbitrary")),
        name="ec_gather",
    )(idx.reshape(-1), v.reshape(Bn * T, D))


def _ffn1_kernel(x_ref, wg_ref, wu_ref, o_ref):
    x = x_ref[0]
    g = jnp.dot(x, wg_ref[0].astype(BF16), preferred_element_type=F32)
    u = jnp.dot(x, wu_ref[0].astype(BF16), preferred_element_type=F32)
    o_ref[0] = (_silu(g) * u).astype(o_ref.dtype)


def _ffn1(xg, wg, wu):
    E, M, D = xg.shape
    F = wg.shape[2]
    tm = _pick(M, 1024)
    tn = _pick(F, 256)
    return pl.pallas_call(
        _ffn1_kernel,
        grid=(E, M // tm, F // tn),
        in_specs=[pl.BlockSpec((1, tm, D), lambda e, i, j: (e, i, 0)),
                  pl.BlockSpec((1, D, tn), lambda e, i, j: (e, 0, j)),
                  pl.BlockSpec((1, D, tn), lambda e, i, j: (e, 0, j))],
        out_specs=pl.BlockSpec((1, tm, tn), lambda e, i, j: (e, i, j)),
        out_shape=jax.ShapeDtypeStruct((E, M, F), BF16),
        compiler_params=_cparams(("parallel", "parallel", "arbitrary")),
        name="ec_ffn_up",
    )(xg, wg, wu)


def _ffn2_kernel(h_ref, wd_ref, o_ref):
    o_ref[0] = jnp.dot(h_ref[0], wd_ref[0].astype(BF16),
                       preferred_element_type=F32).astype(o_ref.dtype)


def _ffn2(hid, wd):
    E, M, F = hid.shape
    D = wd.shape[2]
    tn = _pick(D, 512)
    return pl.pallas_call(
        _ffn2_kernel,
        grid=(E, D // tn),
        in_specs=[pl.BlockSpec((1, M, F), lambda e, j: (e, 0, 0)),
                  pl.BlockSpec((1, F, tn), lambda e, j: (e, 0, j))],
        out_specs=pl.BlockSpec((1, M, tn), lambda e, j: (e, 0, j)),
        out_shape=jax.ShapeDtypeStruct((E, M, D), BF16),
        compiler_params=_cparams(("parallel", "arbitrary")),
        name="ec_ffn_down",
    )(hid, wd)


def _scatter_kernel(offs_ref, slot_ref, aff_ref, y_ref, h_ref, g_ref, gate_ref, o_ref, *, cap, win):
    b = pl.program_id(0)
    t = pl.program_id(1)
    e = pl.program_id(2)
    tq = o_ref.shape[1]

    @pl.when(e == 0)
    def _():
        o_ref[...] = jnp.zeros_like(o_ref)

    base = (b * N_EXPERTS + e) * LANE
    lo = offs_ref[base + t * (tq // LANE)]
    hi = offs_ref[base + (t + 1) * (tq // LANE)]
    sl = jnp.broadcast_to(slot_ref[0, 0], (LANE, tq)).T
    af = jnp.broadcast_to(aff_ref[0, 0], (LANE, tq)).T
    sl = jnp.concatenate([sl] * (win // LANE), axis=1)
    af = jnp.concatenate([af] * (win // LANE), axis=1)
    lane = lax.broadcasted_iota(jnp.int32, (tq, win), 1).astype(F32)
    for w0 in range(0, cap, win):
        @pl.when((lo < w0 + win) & (hi > w0))
        def _():
            ohw = jnp.where(sl == lane + float(w0), af, 0.0).astype(BF16)
            o_ref[0] += jnp.dot(ohw, y_ref[0, 0, w0:w0 + win, :], preferred_element_type=F32)

    @pl.when(e == pl.num_programs(2) - 1)
    def _():
        o_ref[0] = h_ref[0] + gate_ref[0] * _rms(o_ref[0], g_ref[...])


def _scatter_combine(offs, slot_rows, aff_rows, y, cap, h, g3, mod3):
    E, Bn, _, D = y.shape
    T = slot_rows.shape[3]
    tq = _pick(T, 512)
    win = _pick(cap, 256)
    assert win % LANE == 0 and tq % LANE == 0
    row = pl.BlockSpec((1, 1, 1, tq), lambda b, t, e, offs_ref: (b, e, 0, t))
    return pl.pallas_call(
        functools.partial(_scatter_kernel, cap=cap, win=win),
        grid_spec=pltpu.PrefetchScalarGridSpec(
            num_scalar_prefetch=1,
            grid=(Bn, T // tq, E),
            in_specs=[row, row,
                      pl.BlockSpec((1, 1, cap, D), lambda b, t, e, offs_ref: (e, b, 0, 0)),
                      pl.BlockSpec((1, tq, D), lambda b, t, e, offs_ref: (b, t, 0)),
                      pl.BlockSpec((1, D), lambda b, t, e, offs_ref: (0, 0)),
                      pl.BlockSpec((1, 1, D), lambda b, t, e, offs_ref: (b, 0, 5))],
            out_specs=pl.BlockSpec((1, tq, D), lambda b, t, e, offs_ref: (b, t, 0))),
        out_shape=jax.ShapeDtypeStruct((Bn, T, D), F32),
        compiler_params=_cparams(("parallel", "parallel", "arbitrary")),
        name="ec_scatter",
    )(offs.reshape(-1), slot_rows, aff_rows, y, h, g3.reshape(1, D), mod3)


def _layer(h_lat, ctx, c8, l, lb_l, w_ada, b_ada, g_norm, w_in, hgrn_f_bias, hgrn_norm,
           mlstm_conv_w, mlstm_conv_b, mlstm_gate_b, mlstm_norm, w_branch_a, w_branch_b, w_out,
           w_router, w_expert_gate, w_expert_up, w_expert_down):
    Bn, T, D = h_lat.shape
    n_lat = T // TOK
    n_tok = T + TOK
    A, Bh = A_HEADS, B_HEADS
    g0 = (5 * A + 6 * Bh) * LANE

    mod = _modulation(c8, w_ada[l], b_ada[l])
    mod3 = mod.reshape(8, 1, N_MOD * D)

    u = _prenorm(h_lat, ctx, g_norm[l, 0], mod3)

    w_l = w_in[l]
    w_gate = jnp.pad(w_l[:, g0:g0 + 4 * Bh], ((0, 0), (0, LANE - 4 * Bh))).astype(BF16)
    gate_bias = jnp.pad(mlstm_gate_b[l].reshape(4 * Bh), (0, LANE - 4 * Bh))
    u2 = u.reshape(Bn * n_tok, D)
    p3 = _matmul(u2, w_l, F32, n_cols=g0, name="in_proj").reshape(Bn, n_tok, g0)
    p_merge = _matmul(u2, w_l[:, g0 + 4 * Bh:], F32, name="in_proj_merge").reshape(Bn, n_tok, 2 * D)
    gates = _matmul(u2, w_gate, F32, tn=LANE, bias=gate_bias, name="in_proj_gates")

    g6 = gates[:, :4 * Bh].reshape(Bn, n_tok // CHUNK, CHUNK, 2, 2, Bh)
    g_rows = g6.transpose(0, 3, 5, 1, 4, 2)
    g_cols = g6.transpose(0, 3, 5, 1, 2, 4)
    conv4 = jnp.concatenate([mlstm_conv_w[l], mlstm_conv_b[l][None]], axis=0)
    conv4 = conv4.reshape(4, 1, 2 * Bh * B_DQK)

    oa = _hgrn_scan(p3, hgrn_f_bias[l], lb_l, n_lat)
    ob = _mlstm_scan(p3, g_rows, g_cols, conv4, n_lat)
    ya, yb = _readout(oa, ob, p3, hgrn_norm[l], mlstm_norm[l], n_lat)
    merged = _merge(ya, yb, w_branch_a[l].astype(BF16), w_branch_b[l].astype(BF16), p_merge, D)
    mix = _matmul(merged.reshape(Bn * T, D), w_out[l].astype(BF16), F32, name="out_proj")

    w_router_pad = jnp.pad(w_router[l], ((0, 0), (0, LANE - N_EXPERTS)))
    h_lat, v_lat, aff = _resid_router(h_lat, mix.reshape(Bn, T, D), g_norm[l, 1], g_norm[l, 2],
                                      mod3, w_router_pad)

    cap = CAPACITY * T // N_EXPERTS
    aff_rows = aff[:, :, :N_EXPERTS].transpose(0, 2, 1)
    slot, idx, offs = _select(aff_rows, cap)
    xg = _gather(idx, v_lat, cap)
    hid = _ffn1(xg.reshape(N_EXPERTS, Bn * cap, D), w_expert_gate[l], w_expert_up[l])
    y = _ffn2(hid, w_expert_down[l]).reshape(N_EXPERTS, Bn, cap, D)
    return _scatter_combine(offs, slot.reshape(Bn, N_EXPERTS, 1, T),
                            aff_rows.reshape(Bn, N_EXPERTS, 1, T), y, cap, h_lat, g_norm[l, 3], mod3)


def kernel(x, c, ctx, c_ctx, w_ada, b_ada, g_norm, w_in, hgrn_f_bias, hgrn_lb, hgrn_norm,
           mlstm_conv_w, mlstm_conv_b, mlstm_gate_b, mlstm_norm, w_branch_a, w_branch_b, w_out,
           w_router, w_expert_gate, w_expert_up, w_expert_down):
    Bn, T, D = x.shape
    depth = w_ada.shape[0]
    assert depth == 1, "context outputs are only produced for the state hand-off (single layer)"
    assert ctx.shape[1] == TOK and T % TOK == 0 and Bn < 8
    lb_all = jnp.cumsum(jax.nn.softmax(hgrn_lb.astype(F32), axis=1), axis=1)
    c8 = jnp.zeros((8, D), F32).at[:Bn].set(c).at[Bn].set(c_ctx)
    h_lat = x
    for l in range(depth):
        h_lat = _layer(h_lat, ctx, c8, l, lb_all[:, l], w_ada, b_ada, g_norm, w_in, hgrn_f_bias,
                       hgrn_norm, mlstm_conv_w, mlstm_conv_b, mlstm_gate_b, mlstm_norm,
                       w_branch_a, w_branch_b, w_out, w_router, w_expert_gate, w_expert_up,
                       w_expert_down)
    return h_lat.astype(x.dtype)
```

```python
import functools

import jax
import jax.numpy as jnp
from jax import lax
from jax.experimental import pallas as pl
from jax.experimental.pallas import tpu as pltpu

F32 = jnp.float32
BF16 = jnp.bfloat16

EPS = 1e-6
N_MOD = 6
A_HEADS = 16
A_D = 128
B_HEADS = 8
B_DQK = 128
B_DV = 256
N_EXPERTS = 16
CAPACITY = 2
CHUNK = 64
SUB = 16
A_HPS = 4
B_HPS = 4
TOK = 256
NCH = TOK // CHUNK
NSUB = CHUNK // SUB
LANE = 128
VMEM_LIMIT = 56 * 1024 * 1024

_NT = (((1,), (1,)), ((), ()))


def _pick(n, pref):
    t = min(n, pref)
    while n % t:
        t //= 2
    return t


def _cparams(sem):
    return pltpu.CompilerParams(dimension_semantics=sem, vmem_limit_bytes=VMEM_LIMIT)


def _silu(x):
    return x / (1.0 + jnp.exp(-x))


def _sigmoid(x):
    return 1.0 / (1.0 + jnp.exp(-x))


def _sig_pair(z):
    t = jnp.exp(-jnp.abs(z))
    r = 1.0 / (1.0 + t)
    tr = t * r
    pos = z >= 0
    return jnp.where(pos, r, tr), jnp.where(pos, tr, r)


def _log_sigmoid(x):
    return jnp.minimum(x, 0.0) - jnp.log(1.0 + jnp.exp(-jnp.abs(x)))


def _rms(xf, w):
    return xf * lax.rsqrt(jnp.mean(xf * xf, axis=-1, keepdims=True) + EPS) * w


def _rows(x, r, n):
    return jnp.broadcast_to(x[r:r + 1], (n, x.shape[1]))


def _mod_kernel(c_ref, w_ref, b_ref, o_ref):
    a = _silu(c_ref[...]).astype(BF16)
    o_ref[...] = jnp.dot(a, w_ref[...].astype(BF16), preferred_element_type=F32) + b_ref[...]


def _modulation(c8, w, b):
    D, N = w.shape
    tn = _pick(N, 512)
    return pl.pallas_call(
        _mod_kernel,
        grid=(N // tn,),
        in_specs=[pl.BlockSpec((8, D), lambda j: (0, 0)),
                  pl.BlockSpec((D, tn), lambda j: (0, j)),
                  pl.BlockSpec((1, tn), lambda j: (0, j))],
        out_specs=pl.BlockSpec((8, tn), lambda j: (0, j)),
        out_shape=jax.ShapeDtypeStruct((8, N), F32),
        compiler_params=_cparams(("parallel",)),
        name="adaln_mod",
    )(c8, w, b.reshape(1, N))


def _prenorm_kernel(x_ref, ctx_ref, g_ref, sh_ref, sc_ref, o_ref, *, n_lat):
    i = pl.program_id(1)
    g = g_ref[...]
    sh = sh_ref[0]
    sc = sc_ref[0]

    @pl.when(i < n_lat)
    def _():
        o_ref[0] = (_rms(x_ref[0], g) * (1.0 + sc) + sh).astype(o_ref.dtype)

    @pl.when(i == n_lat)
    def _():
        o_ref[0] = (_rms(ctx_ref[0], g) * (1.0 + sc) + sh).astype(o_ref.dtype)


def _prenorm(x, ctx, g, mod3):
    Bn, T, D = x.shape
    n_lat = T // TOK
    row = lambda b, i: jnp.where(i == n_lat, Bn, b)
    return pl.pallas_call(
        functools.partial(_prenorm_kernel, n_lat=n_lat),
        grid=(Bn, n_lat + 1),
        in_specs=[pl.BlockSpec((1, TOK, D), lambda b, i: (b, jnp.minimum(i, n_lat - 1), 0)),
                  pl.BlockSpec((1, TOK, D), lambda b, i: (b, 0, 0)),
                  pl.BlockSpec((1, D), lambda b, i: (0, 0)),
                  pl.BlockSpec((1, 1, D), lambda b, i: (row(b, i), 0, 0)),
                  pl.BlockSpec((1, 1, D), lambda b, i: (row(b, i), 0, 1))],
        out_specs=pl.BlockSpec((1, TOK, D), lambda b, i: (b, i, 0)),
        out_shape=jax.ShapeDtypeStruct((Bn, T + TOK, D), BF16),
        compiler_params=_cparams(("parallel", "arbitrary")),
        name="prenorm",
    )(x, ctx, g.reshape(1, D), mod3, mod3)


def _mm_kernel(a_ref, b_ref, o_ref):
    o_ref[...] = jnp.dot(a_ref[...], b_ref[...].astype(BF16),
                         preferred_element_type=F32).astype(o_ref.dtype)


def _mm_bias_kernel(a_ref, b_ref, bias_ref, o_ref):
    o_ref[...] = (jnp.dot(a_ref[...], b_ref[...].astype(BF16), preferred_element_type=F32)
                  + bias_ref[...]).astype(o_ref.dtype)


def _matmul(a, b, out_dtype, tm=1024, tn=512, bias=None, n_cols=None, name="matmul"):
    M, K = a.shape
    N = b.shape[1] if n_cols is None else n_cols
    tm = _pick(M, tm)
    tn = _pick(N, tn)
    in_specs = [pl.BlockSpec((tm, K), lambda i, j: (i, 0)),
                pl.BlockSpec((K, tn), lambda i, j: (0, j))]
    args = [a, b]
    kern = _mm_kernel
    if bias is not None:
        in_specs.append(pl.BlockSpec((1, tn), lambda i, j: (0, j)))
        args.append(bias.reshape(1, N))
        kern = _mm_bias_kernel
    return pl.pallas_call(
        kern,
        grid=(M // tm, N // tn),
        in_specs=in_specs,
        out_specs=pl.BlockSpec((tm, tn), lambda i, j: (i, j)),
        out_shape=jax.ShapeDtypeStruct((M, N), out_dtype),
        compiler_params=_cparams(("parallel", "arbitrary")),
        name=name,
    )(*args)


def _wsplit_kernel(a_ref, b_ref, o_ref, g_ref, *, shift):
    j = pl.program_id(1)
    tn = o_ref.shape[1]
    cat = jnp.concatenate([a_ref[...], b_ref[...]], axis=1)
    o_ref[...] = cat[:, shift:shift + tn].astype(o_ref.dtype)

    @pl.when(j == 0)
    def _():
        lane = lax.broadcasted_iota(jnp.int32, g_ref.shape, 1)
        g_ref[...] = jnp.where(lane < shift, cat[:, :LANE], 0.0).astype(g_ref.dtype)


def _wsplit(w, col0, shift, n):
    K = w.shape[0]
    tk = _pick(K, 1024)
    tn = _pick(n, 512)
    assert col0 % tn == 0 and 0 < shift < LANE and col0 + shift + n <= w.shape[1]
    c0 = col0 // tn
    return pl.pallas_call(
        functools.partial(_wsplit_kernel, shift=shift),
        grid=(K // tk, n // tn),
        in_specs=[pl.BlockSpec((tk, tn), lambda i, j: (i, c0 + j)),
                  pl.BlockSpec((tk, tn), lambda i, j: (i, c0 + j + 1))],
        out_specs=[pl.BlockSpec((tk, tn), lambda i, j: (i, j)),
                   pl.BlockSpec((tk, LANE), lambda i, j: (i, 0))],
        out_shape=[jax.ShapeDtypeStruct((K, n), BF16), jax.ShapeDtypeStruct((K, LANE), BF16)],
        compiler_params=_cparams(("parallel", "arbitrary")),
        name="w_split",
    )(w, w)


def _tok_block(rev, j, n_lat):
    lat = n_lat - j if rev else j - 1
    return jnp.where(j == 0, n_lat, lat)


def _out_block(rev, j, n_lat):
    return jnp.clip(n_lat - j if rev else j - 1, 0, n_lat - 1)


def _chunk_masks(rev):
    row = lax.broadcasted_iota(jnp.int32, (CHUNK, CHUNK), 0)
    col = lax.broadcasted_iota(jnp.int32, (CHUNK, CHUNK), 1)
    return (col >= row, col <= row) if rev else (col <= row, col >= row)


def _hgrn_consts():
    t = jnp.arange(TOK)
    same = (t[:, None] // CHUNK) == (t[None, :] // CHUNK)
    sub_r, sub_c = t[:, None] // SUB, t[None, :] // SUB
    fwd = jnp.concatenate([same & (t[None, :] <= t[:, None]), same & (sub_c < sub_r)], axis=0)
    bwd = jnp.concatenate([same & (t[None, :] >= t[:, None]), same & (sub_c > sub_r)], axis=0)
    masks = jnp.stack([fwd, bwd]).astype(BF16)
    r = jnp.arange(SUB * A_D)
    emat = ((r[:, None] // A_D) == (jnp.arange(LANE)[None, :] % SUB)).astype(BF16)
    return masks, emat


def _hgrn_state(af, v, fb, lb, mask_ref, st_ref, rev):
    z = af + fb
    sp, sn = _sig_pair(z)
    k = (1.0 - lb) * sn
    lf2 = jnp.log2(lb + (1.0 - lb) * sp)
    lk2 = jnp.log2(k)

    hi = lf2.astype(BF16)
    r1 = lf2 - hi.astype(F32)
    mid = r1.astype(BF16)
    lo = (r1 - mid.astype(F32)).astype(BF16)
    cs = jnp.dot(mask_ref[0], jnp.concatenate([hi, mid, lo], axis=1), preferred_element_type=F32)
    cs = cs[:, :A_D] + cs[:, A_D:2 * A_D] + cs[:, 2 * A_D:]
    b = cs[:TOK]
    ent = cs[TOK:]
    c_all = lk2 - b

    last = [(c * CHUNK if rev else c * CHUNK + CHUNK - 1) for c in range(NCH)]
    tot_rows = jnp.concatenate([_rows(b, last[c], CHUNK) for c in range(NCH)], axis=0)
    khat = jnp.exp2(jnp.minimum(tot_rows + c_all, lk2)).astype(BF16)
    upd = [jnp.dot(v[c * CHUNK:(c + 1) * CHUNK].T.astype(BF16), khat[c * CHUNK:(c + 1) * CHUNK],
                   preferred_element_type=F32) for c in range(NCH)]

    order = range(NCH - 1, -1, -1) if rev else range(NCH)
    st = st_ref[...]
    st_in = [None] * NCH
    for c in order:
        st_in[c] = st
        st = st * jnp.exp2(b[last[c]:last[c] + 1]) + upd[c]
    st_ref[...] = st
    return v, b, ent, c_all, lk2, st_in


def _hgrn_output(aq, feats, emat_ref, o_ref, hs, rev):
    v, b, ent, c_all, lk2, st_in = feats
    q = _silu(aq)
    qb = q.astype(BF16)
    vb = v.astype(BF16)
    qt = (q * jnp.exp2(b - ent)).astype(BF16)
    row = lax.broadcasted_iota(jnp.int32, (CHUNK, CHUNK), 0)
    col = lax.broadcasted_iota(jnp.int32, (CHUNK, CHUNK), 1)
    rs = jnp.right_shift(row, SUB.bit_length() - 1)
    cs_ = jnp.right_shift(col, SUB.bit_length() - 1)
    prev_blk = (cs_ > rs) if rev else (cs_ < rs)
    diag_blk = (rs == cs_) & ((col >= row) if rev else (col <= row))

    off = [[None] * NSUB for _ in range(NCH)]
    for i in range(NSUB):
        e_i = jnp.concatenate([_rows(ent, c * CHUNK + i * SUB, CHUNK) for c in range(NCH)], axis=0)
        kt = jnp.exp2(jnp.minimum(e_i + c_all, lk2)).astype(BF16)
        for c in range(NCH):
            r0 = c * CHUNK + i * SUB
            off[c][i] = lax.dot_general(qt[r0:r0 + SUB], kt[c * CHUNK:(c + 1) * CHUNK], _NT,
                                        preferred_element_type=F32)

    half = SUB // 2
    nsb = TOK // SUB
    zeros = jnp.zeros((half, A_D), F32)
    ps = []
    for sl in range(SUB):
        n = half if (sl < half if rev else sl >= half) else SUB
        r0 = SUB - n if not rev else 0
        bq = b if n == SUB else jnp.concatenate(
            [b[g * SUB + r0:g * SUB + r0 + n] for g in range(nsb)], axis=0)
        cref = jnp.concatenate([_rows(c_all, g * SUB + sl, n) for g in range(nsb)], axis=0)
        kref = jnp.concatenate([_rows(lk2, g * SUB + sl, n) for g in range(nsb)], axis=0)
        e = jnp.exp2(jnp.minimum(bq + cref, kref))
        if n != SUB:
            parts = []
            for g in range(nsb):
                piece = e[g * half:(g + 1) * half]
                parts += [piece, zeros] if rev else [zeros, piece]
            e = jnp.concatenate(parts, axis=0)
        ps.append(e.astype(BF16) * qb)
    acc = jnp.dot(jnp.concatenate(ps, axis=1), emat_ref[...], preferred_element_type=F32)

    qhat = (q * jnp.exp2(b)).astype(BF16)
    for c in range(NCH):
        sl_c = slice(c * CHUNK, (c + 1) * CHUNK)
        attn = (jnp.where(prev_blk, jnp.concatenate(off[c], axis=0), 0.0)
                + jnp.where(diag_blk, acc[sl_c, :CHUNK], 0.0))
        o = jnp.dot(attn.astype(BF16), vb[sl_c], preferred_element_type=F32)
        o = o + lax.dot_general(qhat[sl_c], st_in[c].astype(BF16), _NT, preferred_element_type=F32)
        o_ref[0, sl_c, hs] = o


def _hgrn_kernel(aqf, aff, aif, aqb, afb, aib, fb_ref, lb_ref, mask_ref, emat_ref, of_ref, ob_ref,
                 stf_ref, stb_ref):
    j = pl.program_id(2)

    @pl.when(j == 0)
    def _():
        stf_ref[...] = jnp.zeros_like(stf_ref)
        stb_ref[...] = jnp.zeros_like(stb_ref)

    feats = []
    for hh in range(A_HPS):
        hs = slice(hh * A_D, (hh + 1) * A_D)
        feats.append((hs,
                      _hgrn_state(aff[0, :, hs], aif[0, :, hs], fb_ref[0][:, hs], lb_ref[0][:, hs],
                                  mask_ref.at[0:1], stf_ref.at[hh], False),
                      _hgrn_state(afb[0, :, hs], aib[0, :, hs], fb_ref[1][:, hs], lb_ref[1][:, hs],
                                  mask_ref.at[1:2], stb_ref.at[hh], True)))
    for hs, feats_f, feats_b in feats:
        _hgrn_output(aqf[0, :, hs], feats_f, emat_ref, of_ref, hs, False)
        _hgrn_output(aqb[0, :, hs], feats_b, emat_ref, ob_ref, hs, True)


def _hgrn_scan(p3, f_bias, lb, n_lat):
    Bn = p3.shape[0]
    A = A_HEADS
    masks, emat = _hgrn_consts()

    W = A_HPS * A_D
    G = A // A_HPS

    def feat(rev, grp):
        return pl.BlockSpec((1, TOK, W), lambda b, h, j: (b, _tok_block(rev, j, n_lat), grp * G + h))

    def outp(rev):
        return pl.BlockSpec((1, TOK, W), lambda b, h, j: (b, _out_block(rev, j, n_lat), h))

    par = pl.BlockSpec((2, 1, W), lambda b, h, j: (0, 0, h))
    osh = jax.ShapeDtypeStruct((Bn, n_lat * TOK, A * A_D), F32)
    return pl.pallas_call(
        _hgrn_kernel,
        grid=(Bn, G, n_lat + 1),
        in_specs=[feat(False, 0), feat(False, 1), feat(False, 3),
                  feat(True, 0), feat(True, 2), feat(True, 3),
                  par, par,
                  pl.BlockSpec((2, 2 * TOK, TOK), lambda b, h, j: (0, 0, 0)),
                  pl.BlockSpec((SUB * A_D, LANE), lambda b, h, j: (0, 0))],
        out_specs=[outp(False), outp(True)],
        out_shape=[osh, osh],
        scratch_shapes=[pltpu.VMEM((A_HPS, A_D, A_D), F32), pltpu.VMEM((A_HPS, A_D, A_D), F32)],
        compiler_params=_cparams(("parallel", "parallel", "arbitrary")),
        name="hgrn2_scan",
    )(p3, p3, p3, p3, p3, p3, f_bias.reshape(2, 1, A * A_D), lb.reshape(2, 1, A * A_D), masks, emat)


def _mlstm_state(bq_ref, bk_ref, bv_ref, gr_ref, gc_ref, cwq, cwk, first, last_, c_ref, n_ref, m_ref,
                 hh, rev):
    def conv(u, cw):
        up = jnp.where(first, 0.0, pltpu.roll(u, 1, 0))
        dn = jnp.where(last_, 0.0, pltpu.roll(u, TOK - 1, 0))
        return _silu(cw[3:4] + up * cw[0:1] + u * cw[1:2] + dn * cw[2:3])

    hs = slice(hh * B_DQK, (hh + 1) * B_DQK)
    vs = slice(hh * B_DV, (hh + 1) * B_DV)
    c_ref, n_ref, m_ref = c_ref.at[hh], n_ref.at[hh], m_ref.at[hh]
    q_all = conv(bq_ref[0, :, hs], cwq[:, hs])
    k_all = conv(bk_ref[0, :, hs], cwk[:, hs]) * (B_DQK ** -0.5)
    seen, seen_t = _chunk_masks(rev)

    pre = []
    for c in range(NCH):
        sl_c = slice(c * CHUNK, (c + 1) * CHUNK)
        k = k_all[sl_c]
        vb = bv_ref[0, sl_c, vs].astype(BF16)
        g_r = gr_ref[0, 0, hh, c]
        g_c = gc_ref[0, 0, hh, c]
        ii_r = g_r[0:1]
        lf_r = _log_sigmoid(g_r[1:2])
        ii_c = g_c[:, 0:1]
        lf_c = _log_sigmoid(g_c[:, 1:2])
        b_c = jnp.sum(jnp.where(seen, lf_r, 0.0), axis=1, keepdims=True)
        b_r = jnp.sum(jnp.where(seen_t, lf_c, 0.0), axis=0, keepdims=True)
        total = jnp.sum(lf_r, axis=1, keepdims=True)
        logs = total - b_c + ii_c
        ms = jnp.max(logs, axis=0, keepdims=True)
        kw = k * jnp.exp(logs - ms)
        upd = jnp.dot(kw.T.astype(BF16), vb, preferred_element_type=F32)
        nupd = jnp.sum(kw, axis=0, keepdims=True)
        pre.append((vb, ii_r, b_c, b_r, total, ms, upd, nupd))

    cmat, nvec, m = c_ref[...], n_ref[...], m_ref[:, 0:1]
    st_in = [None] * NCH
    for c in (range(NCH - 1, -1, -1) if rev else range(NCH)):
        _, _, _, _, total, ms, upd, nupd = pre[c]
        st_in[c] = (cmat, nvec, m)
        m_new = jnp.maximum(total + m, ms)
        dec = jnp.exp(total + m - m_new)
        sc = jnp.exp(ms - m_new)
        cmat = dec * cmat + sc * upd
        nvec = dec * nvec + sc * nupd
        m = m_new
    c_ref[...] = cmat
    n_ref[...] = nvec
    m_ref[...] = jnp.broadcast_to(m, m_ref.shape)
    return q_all, k_all, pre, st_in


def _mlstm_output(feats, o_ref, hh, rev):
    q_all, k_all, pre, st_in = feats
    seen, _ = _chunk_masks(rev)
    for c in range(NCH):
        sl_c = slice(c * CHUNK, (c + 1) * CHUNK)
        vb, ii_r, b_c, b_r, _, _, _, _ = pre[c]
        cm, nv, m0 = st_in[c]
        q = q_all[sl_c]
        qb = q.astype(BF16)
        kb = k_all[sl_c].astype(BF16)
        logw = jnp.where(seen, b_c - b_r + ii_r, -jnp.inf)
        mw = jnp.max(logw, axis=1, keepdims=True)
        qk = lax.dot_general(qb, kb, _NT, preferred_element_type=F32) * jnp.exp(logw - mw)
        num0 = jnp.dot(qk.astype(BF16), vb, preferred_element_type=F32)
        den0 = jnp.sum(qk, axis=1, keepdims=True)
        log_inter = b_c + m0
        m_t = jnp.maximum(mw, log_inter)
        r = jnp.exp(mw - m_t)
        a = jnp.exp(log_inter - m_t)
        num = r * num0 + a * jnp.dot(qb, cm.astype(BF16), preferred_element_type=F32)
        den = r * den0 + a * jnp.sum(q * nv, axis=1, keepdims=True)
        o_ref[0, sl_c, hh * B_DV:(hh + 1) * B_DV] = num / jnp.maximum(jnp.abs(den), jnp.exp(-m_t))


def _mlstm_kernel(bqf, bkf, bvf, grf, gcf, bqb, bkb, bvb, grb, gcb, cwq_ref, cwk_ref, of_ref, ob_ref,
                  cf_ref, nf_ref, mf_ref, cb_ref, nb_ref, mb_ref):
    j = pl.program_id(2)

    @pl.when(j == 0)
    def _():
        for ref in (cf_ref, nf_ref, mf_ref, cb_ref, nb_ref, mb_ref):
            ref[...] = jnp.zeros_like(ref)

    t = lax.broadcasted_iota(jnp.int32, (TOK, 1), 0)
    row_mask = jnp.where(j == 0, TOK - 1, CHUNK - 1)
    pos = t & row_mask
    first = pos == 0
    last_ = pos == row_mask
    cwq = cwq_ref[:, 0, :]
    cwk = cwk_ref[:, 0, :]
    feats = []
    for hh in range(B_HPS):
        feats.append((_mlstm_state(bqf, bkf, bvf, grf, gcf, cwq, cwk, first, last_, cf_ref, nf_ref, mf_ref,
                                   hh, False),
                      _mlstm_state(bqb, bkb, bvb, grb, gcb, cwq, cwk, first, last_, cb_ref, nb_ref, mb_ref,
                                   hh, True)))
    for hh, (feats_f, feats_b) in enumerate(feats):
        _mlstm_output(feats_f, of_ref, hh, False)
        _mlstm_output(feats_b, ob_ref, hh, True)


def _mlstm_scan(p3, g_rows, g_cols, conv4, n_lat):
    Bn = p3.shape[0]
    A, Bh = A_HEADS, B_HEADS
    P = B_HPS
    G = Bh // P
    q0 = 5 * A // P
    k0 = (5 * A + Bh) // P
    v0 = (5 * A + 2 * Bh) // (2 * P)

    def specs(rev):
        d = int(rev)
        tb = lambda j: _tok_block(rev, j, n_lat)
        return [pl.BlockSpec((1, TOK, P * B_DQK), lambda b, h, j: (b, tb(j), q0 + h)),
                pl.BlockSpec((1, TOK, P * B_DQK), lambda b, h, j: (b, tb(j), k0 + h)),
                pl.BlockSpec((1, TOK, P * B_DV), lambda b, h, j: (b, tb(j), v0 + h)),
                pl.BlockSpec((1, 1, P, NCH, 2, CHUNK), lambda b, h, j: (b, d, h, tb(j), 0, 0)),
                pl.BlockSpec((1, 1, P, NCH, CHUNK, 2), lambda b, h, j: (b, d, h, tb(j), 0, 0))]

    def outp(rev):
        return pl.BlockSpec((1, TOK, P * B_DV), lambda b, h, j: (b, _out_block(rev, j, n_lat), h))

    osh = jax.ShapeDtypeStruct((Bn, n_lat * TOK, Bh * B_DV), F32)
    state = [pltpu.VMEM((P, B_DQK, B_DV), F32), pltpu.VMEM((P, 1, B_DQK), F32),
             pltpu.VMEM((P, 1, LANE), F32)]
    return pl.pallas_call(
        _mlstm_kernel,
        grid=(Bn, G, n_lat + 1),
        in_specs=specs(False) + specs(True) + [
            pl.BlockSpec((4, 1, P * B_DQK), lambda b, h, j: (0, 0, h)),
            pl.BlockSpec((4, 1, P * B_DQK), lambda b, h, j: (0, 0, G + h))],
        out_specs=[outp(False), outp(True)],
        out_shape=[osh, osh],
        scratch_shapes=state + state,
        compiler_params=_cparams(("parallel", "parallel", "arbitrary")),
        name="mlstm_scan",
    )(p3, p3, p3, g_rows, g_cols, p3, p3, p3, g_rows, g_cols, conv4, conv4)


def _readout_kernel(oaf_ref, oab_ref, obf_ref, obb_ref, ag_ref, bo_ref, na_ref, nb_ref, ya_ref, yb_ref):
    for h in range(A_HEADS):
        sl = slice(h * A_D, (h + 1) * A_D)
        o = oaf_ref[0, :, sl] + oab_ref[0, :, sl]
        ya_ref[0, :, sl] = (_rms(o, na_ref[:, sl]) * _silu(ag_ref[0, :, sl])).astype(ya_ref.dtype)
    for h in range(B_HEADS):
        sl = slice(h * B_DV, (h + 1) * B_DV)
        o = obf_ref[0, :, sl] + obb_ref[0, :, sl]
        yb_ref[0, :, sl] = (_rms(o, nb_ref[:, sl]) * _sigmoid(bo_ref[0, :, sl])).astype(yb_ref.dtype)


def _readout(oa, ob, p3, norm_a, norm_b, n_lat):
    Bn = p3.shape[0]
    T = n_lat * TOK
    wa = A_HEADS * A_D
    wb = B_HEADS * B_DV
    ag_blk = 4
    bo_blk = (5 * A_HEADS + 4 * B_HEADS) * LANE // wb
    sa = pl.BlockSpec((1, TOK, wa), lambda b, i: (b, i, 0))
    sb = pl.BlockSpec((1, TOK, wb), lambda b, i: (b, i, 0))
    return pl.pallas_call(
        _readout_kernel,
        grid=(Bn, n_lat),
        in_specs=[sa, sa, sb, sb,
                  pl.BlockSpec((1, TOK, wa), lambda b, i: (b, i, ag_blk)),
                  pl.BlockSpec((1, TOK, wb), lambda b, i: (b, i, bo_blk)),
                  pl.BlockSpec((1, wa), lambda b, i: (0, 0)),
                  pl.BlockSpec((1, wb), lambda b, i: (0, 0))],
        out_specs=[sa, sb],
        out_shape=[jax.ShapeDtypeStruct((Bn, T, wa), BF16),
                   jax.ShapeDtypeStruct((Bn, T, wb), BF16)],
        compiler_params=_cparams(("parallel", "parallel")),
        name="readout",
    )(oa[0], oa[1], ob[0], ob[1], p3, p3, norm_a.reshape(1, wa), norm_b.reshape(1, wb))


def _merge_kernel(ya_ref, yb_ref, wa_ref, wb_ref, ga_ref, gb_ref, o_ref):
    pa = jnp.dot(ya_ref[0], wa_ref[...], preferred_element_type=F32)
    pb = jnp.dot(yb_ref[0], wb_ref[...], preferred_element_type=F32)
    o_ref[0] = (_sigmoid(ga_ref[0]) * pa + _sigmoid(gb_ref[0]) * pb).astype(o_ref.dtype)


def _merge(ya, yb, wa, wb, p3, D):
    Bn, T, ka = ya.shape
    kb = yb.shape[2]
    tm = _pick(T, 1024)
    tn = _pick(D, 512)
    ga0 = 0
    gb0 = D // tn
    return pl.pallas_call(
        _merge_kernel,
        grid=(Bn, T // tm, D // tn),
        in_specs=[pl.BlockSpec((1, tm, ka), lambda b, i, j: (b, i, 0)),
                  pl.BlockSpec((1, tm, kb), lambda b, i, j: (b, i, 0)),
                  pl.BlockSpec((ka, tn), lambda b, i, j: (0, j)),
                  pl.BlockSpec((kb, tn), lambda b, i, j: (0, j)),
                  pl.BlockSpec((1, tm, tn), lambda b, i, j: (b, i, ga0 + j)),
                  pl.BlockSpec((1, tm, tn), lambda b, i, j: (b, i, gb0 + j))],
        out_specs=pl.BlockSpec((1, tm, tn), lambda b, i, j: (b, i, j)),
        out_shape=jax.ShapeDtypeStruct((Bn, T, D), BF16),
        compiler_params=_cparams(("parallel", "parallel", "arbitrary")),
        name="merge",
    )(ya, yb, wa, wb, p3, p3)


def _resid_router_kernel(x_ref, mix_ref, g1_ref, g2_ref, gate_ref, sh_ref, sc_ref, wr_ref,
                         h_ref, v_ref, aff_ref):
    h = x_ref[0] + gate_ref[0] * _rms(mix_ref[0], g1_ref[...])
    h_ref[0] = h
    v = _rms(h, g2_ref[...]) * (1.0 + sc_ref[0]) + sh_ref[0]
    v_ref[0] = _pack_bf16_pairs(v)
    logits = jnp.dot(v, wr_ref[...], precision=lax.Precision.HIGHEST, preferred_element_type=F32)
    lane = lax.broadcasted_iota(jnp.int32, logits.shape, 1)
    logits = jnp.where(lane < N_EXPERTS, logits, -jnp.inf)
    e = jnp.exp(logits - jnp.max(logits, axis=-1, keepdims=True))
    aff_ref[0] = e / jnp.sum(e, axis=-1, keepdims=True)


def _resid_router(x, mix, g1, g2, mod3, w_router_pad):
    Bn, T, D = x.shape
    mspec = lambda k: pl.BlockSpec((1, 1, D), lambda b, i: (b, 0, k))
    tok = pl.BlockSpec((1, TOK, D), lambda b, i: (b, i, 0))
    vec = pl.BlockSpec((1, D), lambda b, i: (0, 0))
    return pl.pallas_call(
        _resid_router_kernel,
        grid=(Bn, T // TOK),
        in_specs=[tok, tok, vec, vec, mspec(2), mspec(3), mspec(4),
                  pl.BlockSpec((D, LANE), lambda b, i: (0, 0))],
        out_specs=[tok, pl.BlockSpec((1, TOK, D // 2), lambda b, i: (b, i, 0)),
                   pl.BlockSpec((1, TOK, LANE), lambda b, i: (b, i, 0))],
        out_shape=[jax.ShapeDtypeStruct((Bn, T, D), F32),
                   jax.ShapeDtypeStruct((Bn, T, D // 2), jnp.uint32),
                   jax.ShapeDtypeStruct((Bn, T, LANE), F32)],
        compiler_params=_cparams(("parallel", "parallel")),
        name="resid_router",
    )(x, mix, g1.reshape(1, D), g2.reshape(1, D), mod3, mod3, mod3, w_router_pad)


def _select_kernel(aff_ref, slot_ref, idx_ref, offs_ref, *, cap, tk):
    E, T = aff_ref.shape[1], aff_ref.shape[2]
    nck = T // LANE

    def key():
        return lax.bitcast_convert_type(aff_ref[0], jnp.int32)

    def bit_step(i, tau):
        cand = tau | jnp.left_shift(jnp.int32(1), 30 - i)
        cnt = jnp.sum(jnp.where(key() >= cand, 1.0, 0.0), axis=1, keepdims=True)
        return jnp.where(cnt >= cap, cand, tau)

    tau = lax.fori_loop(0, 31, bit_step, jnp.zeros((E, 1), jnp.int32))
    kk = key()
    gt = kk > tau
    eq = kk == tau
    need = cap - jnp.sum(jnp.where(gt, 1.0, 0.0), axis=1, keepdims=True)

    r = lax.broadcasted_iota(jnp.int32, (LANE, LANE), 0)
    c = lax.broadcasted_iota(jnp.int32, (LANE, LANE), 1)
    upper = jnp.where(r < c, 1.0, 0.0).astype(BF16)
    tr = jnp.right_shift(lax.broadcasted_iota(jnp.int32, (T, LANE), 0), LANE.bit_length() - 1)
    member = jnp.where(tr == lax.broadcasted_iota(jnp.int32, (T, LANE), 1), 1.0, 0.0).astype(BF16)

    def prefix(flags):
        x = jnp.where(flags, 1.0, 0.0).astype(BF16)
        tot = jnp.dot(x, member, preferred_element_type=F32)
        offs = jnp.dot(tot.astype(BF16), upper, preferred_element_type=F32)
        parts = [jnp.dot(x[:, j * LANE:(j + 1) * LANE], upper, preferred_element_type=F32)
                 + offs[:, j:j + 1] for j in range(nck)]
        return jnp.concatenate(parts, axis=1), offs

    tie_rank, _ = prefix(eq)
    sel = gt | (eq & (tie_rank < need))
    pos, offs = prefix(sel)
    slot = jnp.where(sel, pos, -1.0)
    slot_ref[0] = slot
    offs_ref[0] = offs.astype(jnp.int32)

    tt = lax.broadcasted_iota(jnp.int32, (8, T), 1)
    rr = lax.broadcasted_iota(jnp.int32, (8, T), 0)
    digits = jnp.where(rr == 0, jnp.right_shift(tt, 6), jnp.where(rr == 1, tt & 63, 0))
    digits = digits.astype(F32).astype(BF16)
    slot_iota = lax.broadcasted_iota(jnp.int32, (cap, tk), 0).astype(F32)
    for e in range(E):
        acc = jnp.zeros((8, cap), F32)
        for kc in range(T // tk):
            ks = slice(kc * tk, (kc + 1) * tk)
            oh = jnp.where(slot[e:e + 1, ks] == slot_iota, 1.0, 0.0).astype(BF16)
            acc = acc + lax.dot_general(digits[:, ks], oh, _NT, preferred_element_type=F32)
        idx_ref[0, e:e + 1, :] = (acc[0:1] * 64.0 + acc[1:2]).astype(jnp.int32)


def _select(aff_rows, cap):
    Bn, E, T = aff_rows.shape
    assert T // LANE < LANE and T <= 4096
    tk = _pick(T, 1024)
    row = lambda n: pl.BlockSpec((1, E, n), lambda b: (b, 0, 0))
    return pl.pallas_call(
        functools.partial(_select_kernel, cap=cap, tk=tk),
        grid=(Bn,),
        in_specs=[row(T)],
        out_specs=[row(T), row(cap), row(LANE)],
        out_shape=[jax.ShapeDtypeStruct((Bn, E, T), F32),
                   jax.ShapeDtypeStruct((Bn, E, cap), jnp.int32),
                   jax.ShapeDtypeStruct((Bn, E, LANE), jnp.int32)],
        compiler_params=_cparams(("parallel",)),
        name="ec_select",
    )(aff_rows)


def _pack_bf16_pairs(x):
    half = x.shape[1] // 2
    bits = lax.bitcast_convert_type(x.astype(BF16).astype(F32), jnp.uint32)
    return bits[:, half:] | (bits[:, :half] >> 16)


def _unpack_bf16_pairs(p):
    lo = lax.bitcast_convert_type(p << 16, F32).astype(BF16)
    hi = lax.bitcast_convert_type(p & jnp.uint32(0xFFFF0000), F32).astype(BF16)
    return lo, hi


def _gather_kernel(idx_ref, v_hbm, o_ref, buf, sem, *, cap, n_tok):
    b = pl.program_id(0)
    e = pl.program_id(1)
    base = (b * N_EXPERTS + e) * cap

    def row_copy(r, tok):
        return pltpu.make_async_copy(v_hbm.at[pl.ds(b * n_tok + tok, 1), :], buf.at[pl.ds(r, 1), :], sem)

    def issue(r, carry):
        row_copy(r, idx_ref[base + r]).start()
        return carry

    def drain(r, carry):
        row_copy(r, 0).wait()
        return carry

    lax.fori_loop(0, cap, issue, 0)
    lax.fori_loop(0, cap, drain, 0)
    half = buf.shape[1]
    lo, hi = _unpack_bf16_pairs(buf[...])
    o_ref[0, 0, :, :half] = lo
    o_ref[0, 0, :, half:] = hi


def _gather(idx, v_pairs, cap):
    Bn, T, half = v_pairs.shape
    D = 2 * half
    return pl.pallas_call(
        functools.partial(_gather_kernel, cap=cap, n_tok=T),
        grid_spec=pltpu.PrefetchScalarGridSpec(
            num_scalar_prefetch=1,
            grid=(Bn, N_EXPERTS),
            in_specs=[pl.BlockSpec(memory_space=pl.ANY)],
            out_specs=pl.BlockSpec((1, 1, cap, D), lambda b, e, idx_ref: (e, b, 0, 0)),
            scratch_shapes=[pltpu.VMEM((cap, half), jnp.uint32), pltpu.SemaphoreType.DMA(())]),
        out_shape=jax.ShapeDtypeStruct((N_EXPERTS, Bn, cap, D), BF16),
        compiler_params=_cparams(("arbitrary", "arbitrary")),
        name="ec_gather",
    )(idx.reshape(-1), v_pairs.reshape(Bn * T, half))


def _ffn1_kernel(x_ref, wg_ref, wu_ref, o_ref):
    x = x_ref[0]
    g = jnp.dot(x, wg_ref[0].astype(BF16), preferred_element_type=F32)
    u = jnp.dot(x, wu_ref[0].astype(BF16), preferred_element_type=F32)
    o_ref[0] = (_silu(g) * u).astype(o_ref.dtype)


def _ffn1(xg, wg, wu):
    E, M, D = xg.shape
    F = wg.shape[2]
    tm = _pick(M, 1024)
    tn = _pick(F, 256)
    return pl.pallas_call(
        _ffn1_kernel,
        grid=(E, M // tm, F // tn),
        in_specs=[pl.BlockSpec((1, tm, D), lambda e, i, j: (e, i, 0)),
                  pl.BlockSpec((1, D, tn), lambda e, i, j: (e, 0, j)),
                  pl.BlockSpec((1, D, tn), lambda e, i, j: (e, 0, j))],
        out_specs=pl.BlockSpec((1, tm, tn), lambda e, i, j: (e, i, j)),
        out_shape=jax.ShapeDtypeStruct((E, M, F), BF16),
        compiler_params=_cparams(("parallel", "parallel", "arbitrary")),
        name="ec_ffn_up",
    )(xg, wg, wu)


def _ffn2_kernel(h_ref, wd_ref, o_ref):
    o_ref[0] = jnp.dot(h_ref[0], wd_ref[0].astype(BF16),
                       preferred_element_type=F32).astype(o_ref.dtype)


def _ffn2(hid, wd):
    E, M, F = hid.shape
    D = wd.shape[2]
    tn = _pick(D, 512)
    return pl.pallas_call(
        _ffn2_kernel,
        grid=(E, D // tn),
        in_specs=[pl.BlockSpec((1, M, F), lambda e, j: (e, 0, 0)),
                  pl.BlockSpec((1, F, tn), lambda e, j: (e, 0, j))],
        out_specs=pl.BlockSpec((1, M, tn), lambda e, j: (e, 0, j)),
        out_shape=jax.ShapeDtypeStruct((E, M, D), BF16),
        compiler_params=_cparams(("parallel", "arbitrary")),
        name="ec_ffn_down",
    )(hid, wd)


def _scatter_kernel(offs_ref, slot_ref, aff_ref, y_ref, h_ref, g_ref, gate_ref, o_ref, *, cap, win):
    b = pl.program_id(0)
    t = pl.program_id(1)
    e = pl.program_id(2)
    tq = o_ref.shape[1]

    @pl.when(e == 0)
    def _():
        o_ref[...] = jnp.zeros_like(o_ref)

    base = (b * N_EXPERTS + e) * LANE
    lo = offs_ref[base + t * (tq // LANE)]
    hi = offs_ref[base + (t + 1) * (tq // LANE)]
    sl = jnp.broadcast_to(slot_ref[0, 0], (LANE, tq)).T
    af = jnp.broadcast_to(aff_ref[0, 0], (LANE, tq)).T
    sl = jnp.concatenate([sl] * (win // LANE), axis=1)
    af = jnp.concatenate([af] * (win // LANE), axis=1)
    lane = lax.broadcasted_iota(jnp.int32, (tq, win), 1).astype(F32)
    for w0 in range(0, cap, win):
        @pl.when((lo < w0 + win) & (hi > w0))
        def _():
            ohw = jnp.where(sl == lane + float(w0), af, 0.0).astype(BF16)
            o_ref[0] += jnp.dot(ohw, y_ref[0, 0, w0:w0 + win, :], preferred_element_type=F32)

    @pl.when(e == pl.num_programs(2) - 1)
    def _():
        o_ref[0] = h_ref[0] + gate_ref[0] * _rms(o_ref[0], g_ref[...])


def _scatter_combine(offs, slot_rows, aff_rows, y, cap, h, g3, mod3):
    E, Bn, _, D = y.shape
    T = slot_rows.shape[3]
    tq = _pick(T, 512)
    win = _pick(cap, 256)
    assert win % LANE == 0 and tq % LANE == 0
    row = pl.BlockSpec((1, 1, 1, tq), lambda b, t, e, offs_ref: (b, e, 0, t))
    return pl.pallas_call(
        functools.partial(_scatter_kernel, cap=cap, win=win),
        grid_spec=pltpu.PrefetchScalarGridSpec(
            num_scalar_prefetch=1,
            grid=(Bn, T // tq, E),
            in_specs=[row, row,
                      pl.BlockSpec((1, 1, cap, D), lambda b, t, e, offs_ref: (e, b, 0, 0)),
                      pl.BlockSpec((1, tq, D), lambda b, t, e, offs_ref: (b, t, 0)),
                      pl.BlockSpec((1, D), lambda b, t, e, offs_ref: (0, 0)),
                      pl.BlockSpec((1, 1, D), lambda b, t, e, offs_ref: (b, 0, 5))],
            out_specs=pl.BlockSpec((1, tq, D), lambda b, t, e, offs_ref: (b, t, 0))),
        out_shape=jax.ShapeDtypeStruct((Bn, T, D), F32),
        compiler_params=_cparams(("parallel", "parallel", "arbitrary")),
        name="ec_scatter",
    )(offs.reshape(-1), slot_rows, aff_rows, y, h, g3.reshape(1, D), mod3)


def _layer(h_lat, ctx, c8, l, lb_l, w_ada, b_ada, g_norm, w_in, hgrn_f_bias, hgrn_norm,
           mlstm_conv_w, mlstm_conv_b, mlstm_gate_b, mlstm_norm, w_branch_a, w_branch_b, w_out,
           w_router, w_expert_gate, w_expert_up, w_expert_down):
    Bn, T, D = h_lat.shape
    n_lat = T // TOK
    n_tok = T + TOK
    A, Bh = A_HEADS, B_HEADS
    g0 = (5 * A + 6 * Bh) * LANE

    mod = _modulation(c8, w_ada[l], b_ada[l])
    mod3 = mod.reshape(8, 1, N_MOD * D)

    u = _prenorm(h_lat, ctx, g_norm[l, 0], mod3)

    w_l = w_in[l]
    w_merge, w_gate = _wsplit(w_l, g0, 4 * Bh, 2 * D)
    gate_bias = jnp.pad(mlstm_gate_b[l].reshape(4 * Bh), (0, LANE - 4 * Bh))
    u2 = u.reshape(Bn * n_tok, D)
    p3 = _matmul(u2, w_l, F32, n_cols=g0, name="in_proj").reshape(Bn, n_tok, g0)
    p_merge = _matmul(u2, w_merge, F32, name="in_proj_merge").reshape(Bn, n_tok, 2 * D)
    gates = _matmul(u2, w_gate, F32, tn=LANE, bias=gate_bias, name="in_proj_gates")

    g6 = gates[:, :4 * Bh].reshape(Bn, n_tok // CHUNK, CHUNK, 2, 2, Bh)
    g_rows = g6.transpose(0, 3, 5, 1, 4, 2)
    g_cols = g6.transpose(0, 3, 5, 1, 2, 4)
    conv4 = jnp.concatenate([mlstm_conv_w[l], mlstm_conv_b[l][None]], axis=0)
    conv4 = conv4.reshape(4, 1, 2 * Bh * B_DQK)

    oa = _hgrn_scan(p3, hgrn_f_bias[l], lb_l, n_lat)
    ob = _mlstm_scan(p3, g_rows, g_cols, conv4, n_lat)
    ya, yb = _readout(oa, ob, p3, hgrn_norm[l], mlstm_norm[l], n_lat)
    merged = _merge(ya, yb, w_branch_a[l].astype(BF16), w_branch_b[l].astype(BF16), p_merge, D)
    mix = _matmul(merged.reshape(Bn * T, D), w_out[l].astype(BF16), F32, name="out_proj")

    w_router_pad = jnp.pad(w_router[l], ((0, 0), (0, LANE - N_EXPERTS)))
    h_lat, v_lat, aff = _resid_router(h_lat, mix.reshape(Bn, T, D), g_norm[l, 1], g_norm[l, 2],
                                      mod3, w_router_pad)

    cap = CAPACITY * T // N_EXPERTS
    aff_rows = aff[:, :, :N_EXPERTS].transpose(0, 2, 1)
    slot, idx, offs = _select(aff_rows, cap)
    xg = _gather(idx, v_lat, cap)
    hid = _ffn1(xg.reshape(N_EXPERTS, Bn * cap, D), w_expert_gate[l], w_expert_up[l])
    y = _ffn2(hid, w_expert_down[l]).reshape(N_EXPERTS, Bn, cap, D)
    return _scatter_combine(offs, slot.reshape(Bn, N_EXPERTS, 1, T),
                            aff_rows.reshape(Bn, N_EXPERTS, 1, T), y, cap, h_lat, g_norm[l, 3], mod3)


def kernel(x, c, ctx, c_ctx, w_ada, b_ada, g_norm, w_in, hgrn_f_bias, hgrn_lb, hgrn_norm,
           mlstm_conv_w, mlstm_conv_b, mlstm_gate_b, mlstm_norm, w_branch_a, w_branch_b, w_out,
           w_router, w_expert_gate, w_expert_up, w_expert_down):
    Bn, T, D = x.shape
    depth = w_ada.shape[0]
    assert depth == 1, "context outputs are only produced for the state hand-off (single layer)"
    assert ctx.shape[1] == TOK and T % TOK == 0 and Bn < 8
    lb_all = jnp.cumsum(jax.nn.softmax(hgrn_lb.astype(F32), axis=1), axis=1)
    c8 = jnp.zeros((8, D), F32).at[:Bn].set(c).at[Bn].set(c_ctx)
    h_lat = x
    for l in range(depth):
        h_lat = _layer(h_lat, ctx, c8, l, lb_all[:, l], w_ada, b_ada, g_norm, w_in, hgrn_f_bias,
                       hgrn_norm, mlstm_conv_w, mlstm_conv_b, mlstm_gate_b, mlstm_norm,
                       w_branch_a, w_branch_b, w_out, w_router, w_expert_gate, w_expert_up,
                       w_expert_down)
    return h_lat.astype(x.dtype)
```

```python
import functools

import jax
import jax.numpy as jnp
from jax import lax
from jax.experimental import pallas as pl
from jax.experimental.pallas import tpu as pltpu

F32 = jnp.float32
BF16 = jnp.bfloat16

EPS = 1e-6
N_MOD = 6
A_HEADS = 16
A_D = 128
B_HEADS = 8
B_DQK = 128
B_DV = 256
N_EXPERTS = 16
CAPACITY = 2
CHUNK = 64
SUB = 16
A_HPS = 4
B_HPS = 4
TOK = 256
NCH = TOK // CHUNK
NSUB = CHUNK // SUB
LANE = 128
VMEM_LIMIT = 56 * 1024 * 1024

_NT = (((1,), (1,)), ((), ()))


def _pick(n, pref):
    t = min(n, pref)
    while n % t:
        t //= 2
    return t


def _cparams(sem):
    return pltpu.CompilerParams(dimension_semantics=sem, vmem_limit_bytes=VMEM_LIMIT)


def _silu(x):
    return x / (1.0 + jnp.exp(-x))


def _sigmoid(x):
    return 1.0 / (1.0 + jnp.exp(-x))


def _sig_pair(z):
    t = jnp.exp(-jnp.abs(z))
    r = 1.0 / (1.0 + t)
    tr = t * r
    pos = z >= 0
    return jnp.where(pos, r, tr), jnp.where(pos, tr, r)


def _log_sigmoid(x):
    return jnp.minimum(x, 0.0) - jnp.log(1.0 + jnp.exp(-jnp.abs(x)))


def _rms(xf, w):
    return xf * lax.rsqrt(jnp.mean(xf * xf, axis=-1, keepdims=True) + EPS) * w


def _rows(x, r, n):
    return jnp.broadcast_to(x[r:r + 1], (n, x.shape[1]))


def _mod_kernel(c_ref, w_ref, b_ref, o_ref):
    a = _silu(c_ref[...]).astype(BF16)
    o_ref[...] = jnp.dot(a, w_ref[...].astype(BF16), preferred_element_type=F32) + b_ref[...]


def _modulation(c8, w, b):
    D, N = w.shape
    tn = _pick(N, 512)
    return pl.pallas_call(
        _mod_kernel,
        grid=(N // tn,),
        in_specs=[pl.BlockSpec((8, D), lambda j: (0, 0)),
                  pl.BlockSpec((D, tn), lambda j: (0, j)),
                  pl.BlockSpec((1, tn), lambda j: (0, j))],
        out_specs=pl.BlockSpec((8, tn), lambda j: (0, j)),
        out_shape=jax.ShapeDtypeStruct((8, N), F32),
        compiler_params=_cparams(("parallel",)),
        name="adaln_mod",
    )(c8, w, b.reshape(1, N))


def _prenorm_kernel(x_ref, ctx_ref, g_ref, sh_ref, sc_ref, o_ref, *, n_lat):
    i = pl.program_id(1)
    g = g_ref[...]
    sh = sh_ref[0]
    sc = sc_ref[0]

    @pl.when(i < n_lat)
    def _():
        o_ref[0] = (_rms(x_ref[0], g) * (1.0 + sc) + sh).astype(o_ref.dtype)

    @pl.when(i == n_lat)
    def _():
        o_ref[0] = (_rms(ctx_ref[0], g) * (1.0 + sc) + sh).astype(o_ref.dtype)


def _prenorm(x, ctx, g, mod3):
    Bn, T, D = x.shape
    n_lat = T // TOK
    row = lambda b, i: jnp.where(i == n_lat, Bn, b)
    return pl.pallas_call(
        functools.partial(_prenorm_kernel, n_lat=n_lat),
        grid=(Bn, n_lat + 1),
        in_specs=[pl.BlockSpec((1, TOK, D), lambda b, i: (b, jnp.minimum(i, n_lat - 1), 0)),
                  pl.BlockSpec((1, TOK, D), lambda b, i: (b, 0, 0)),
                  pl.BlockSpec((1, D), lambda b, i: (0, 0)),
                  pl.BlockSpec((1, 1, D), lambda b, i: (row(b, i), 0, 0)),
                  pl.BlockSpec((1, 1, D), lambda b, i: (row(b, i), 0, 1))],
        out_specs=pl.BlockSpec((1, TOK, D), lambda b, i: (b, i, 0)),
        out_shape=jax.ShapeDtypeStruct((Bn, T + TOK, D), BF16),
        compiler_params=_cparams(("parallel", "arbitrary")),
        name="prenorm",
    )(x, ctx, g.reshape(1, D), mod3, mod3)


def _mm_kernel(a_ref, b_ref, o_ref):
    o_ref[...] = jnp.dot(a_ref[...], b_ref[...].astype(BF16),
                         preferred_element_type=F32).astype(o_ref.dtype)


def _mm_bias_kernel(a_ref, b_ref, bias_ref, o_ref):
    o_ref[...] = (jnp.dot(a_ref[...], b_ref[...].astype(BF16), preferred_element_type=F32)
                  + bias_ref[...]).astype(o_ref.dtype)


def _matmul(a, b, out_dtype, tm=1024, tn=512, bias=None, n_cols=None, layer=None, name="matmul"):
    M, K = a.shape
    N = b.shape[-1] if n_cols is None else n_cols
    tm = _pick(M, tm)
    tn = _pick(N, tn)
    if layer is None:
        b_spec = pl.BlockSpec((K, tn), lambda i, j: (0, j))
    else:
        b_spec = pl.BlockSpec((None, K, tn), lambda i, j: (layer, 0, j))
    in_specs = [pl.BlockSpec((tm, K), lambda i, j: (i, 0)), b_spec]
    args = [a, b]
    kern = _mm_kernel
    if bias is not None:
        in_specs.append(pl.BlockSpec((1, tn), lambda i, j: (0, j)))
        args.append(bias.reshape(1, N))
        kern = _mm_bias_kernel
    return pl.pallas_call(
        kern,
        grid=(M // tm, N // tn),
        in_specs=in_specs,
        out_specs=pl.BlockSpec((tm, tn), lambda i, j: (i, j)),
        out_shape=jax.ShapeDtypeStruct((M, N), out_dtype),
        compiler_params=_cparams(("parallel", "arbitrary")),
        name=name,
    )(*args)


def _wsplit_kernel(a_ref, b_ref, o_ref, g_ref, *, shift):
    j = pl.program_id(1)
    tn = o_ref.shape[1]
    cat = jnp.concatenate([a_ref[...], b_ref[...]], axis=1)
    o_ref[...] = cat[:, shift:shift + tn].astype(o_ref.dtype)

    @pl.when(j == 0)
    def _():
        lane = lax.broadcasted_iota(jnp.int32, g_ref.shape, 1)
        g_ref[...] = jnp.where(lane < shift, cat[:, :LANE], 0.0).astype(g_ref.dtype)


def _wsplit(w, layer, col0, shift, n):
    K = w.shape[1]
    tk = _pick(K, 1024)
    tn = _pick(n, 512)
    assert col0 % tn == 0 and 0 < shift < LANE and col0 + shift + n <= w.shape[2]
    c0 = col0 // tn
    return pl.pallas_call(
        functools.partial(_wsplit_kernel, shift=shift),
        grid=(K // tk, n // tn),
        in_specs=[pl.BlockSpec((None, tk, tn), lambda i, j: (layer, i, c0 + j)),
                  pl.BlockSpec((None, tk, tn), lambda i, j: (layer, i, c0 + j + 1))],
        out_specs=[pl.BlockSpec((tk, tn), lambda i, j: (i, j)),
                   pl.BlockSpec((tk, LANE), lambda i, j: (i, 0))],
        out_shape=[jax.ShapeDtypeStruct((K, n), BF16), jax.ShapeDtypeStruct((K, LANE), BF16)],
        compiler_params=_cparams(("parallel", "arbitrary")),
        name="w_split",
    )(w, w)


def _tok_block(rev, j, n_lat):
    lat = n_lat - j if rev else j - 1
    return jnp.where(j == 0, n_lat, lat)


def _out_block(rev, j, n_lat):
    return jnp.clip(n_lat - j if rev else j - 1, 0, n_lat - 1)


def _chunk_masks(rev):
    row = lax.broadcasted_iota(jnp.int32, (CHUNK, CHUNK), 0)
    col = lax.broadcasted_iota(jnp.int32, (CHUNK, CHUNK), 1)
    return (col >= row, col <= row) if rev else (col <= row, col >= row)


def _hgrn_consts():
    t = jnp.arange(TOK)
    same = (t[:, None] // CHUNK) == (t[None, :] // CHUNK)
    sub_r, sub_c = t[:, None] // SUB, t[None, :] // SUB
    fwd = jnp.concatenate([same & (t[None, :] <= t[:, None]), same & (sub_c < sub_r)], axis=0)
    bwd = jnp.concatenate([same & (t[None, :] >= t[:, None]), same & (sub_c > sub_r)], axis=0)
    masks = jnp.stack([fwd, bwd]).astype(BF16)
    r = jnp.arange(SUB * A_D)
    emat = ((r[:, None] // A_D) == (jnp.arange(LANE)[None, :] % SUB)).astype(BF16)
    return masks, emat


def _hgrn_state(af, v, fb, lb, mask_ref, st_ref, rev):
    z = af + fb
    sp, sn = _sig_pair(z)
    k = (1.0 - lb) * sn
    lf2 = jnp.log2(lb + (1.0 - lb) * sp)
    lk2 = jnp.log2(k)

    hi = lf2.astype(BF16)
    r1 = lf2 - hi.astype(F32)
    mid = r1.astype(BF16)
    lo = (r1 - mid.astype(F32)).astype(BF16)
    cs = jnp.dot(mask_ref[0], jnp.concatenate([hi, mid, lo], axis=1), preferred_element_type=F32)
    cs = cs[:, :A_D] + cs[:, A_D:2 * A_D] + cs[:, 2 * A_D:]
    b = cs[:TOK]
    ent = cs[TOK:]
    c_all = lk2 - b

    last = [(c * CHUNK if rev else c * CHUNK + CHUNK - 1) for c in range(NCH)]
    tot_rows = jnp.concatenate([_rows(b, last[c], CHUNK) for c in range(NCH)], axis=0)
    khat = jnp.exp2(jnp.minimum(tot_rows + c_all, lk2)).astype(BF16)
    upd = [jnp.dot(v[c * CHUNK:(c + 1) * CHUNK].T.astype(BF16), khat[c * CHUNK:(c + 1) * CHUNK],
                   preferred_element_type=F32) for c in range(NCH)]

    order = range(NCH - 1, -1, -1) if rev else range(NCH)
    st = st_ref[...]
    st_in = [None] * NCH
    for c in order:
        st_in[c] = st
        st = st * jnp.exp2(b[last[c]:last[c] + 1]) + upd[c]
    st_ref[...] = st
    return v, b, ent, c_all, lk2, st_in


def _hgrn_output(aq, feats, emat_ref, o_ref, hs, rev):
    v, b, ent, c_all, lk2, st_in = feats
    q = _silu(aq)
    qb = q.astype(BF16)
    vb = v.astype(BF16)
    qt = (q * jnp.exp2(b - ent)).astype(BF16)
    row = lax.broadcasted_iota(jnp.int32, (CHUNK, CHUNK), 0)
    col = lax.broadcasted_iota(jnp.int32, (CHUNK, CHUNK), 1)
    rs = jnp.right_shift(row, SUB.bit_length() - 1)
    cs_ = jnp.right_shift(col, SUB.bit_length() - 1)
    prev_blk = (cs_ > rs) if rev else (cs_ < rs)
    diag_blk = (rs == cs_) & ((col >= row) if rev else (col <= row))

    off = [[None] * NSUB for _ in range(NCH)]
    for i in range(NSUB):
        e_i = jnp.concatenate([_rows(ent, c * CHUNK + i * SUB, CHUNK) for c in range(NCH)], axis=0)
        kt = jnp.exp2(jnp.minimum(e_i + c_all, lk2)).astype(BF16)
        for c in range(NCH):
            r0 = c * CHUNK + i * SUB
            off[c][i] = lax.dot_general(qt[r0:r0 + SUB], kt[c * CHUNK:(c + 1) * CHUNK], _NT,
                                        preferred_element_type=F32)

    half = SUB // 2
    nsb = TOK // SUB
    zeros = jnp.zeros((half, A_D), F32)
    ps = []
    for sl in range(SUB):
        n = half if (sl < half if rev else sl >= half) else SUB
        r0 = SUB - n if not rev else 0
        bq = b if n == SUB else jnp.concatenate(
            [b[g * SUB + r0:g * SUB + r0 + n] for g in range(nsb)], axis=0)
        cref = jnp.concatenate([_rows(c_all, g * SUB + sl, n) for g in range(nsb)], axis=0)
        kref = jnp.concatenate([_rows(lk2, g * SUB + sl, n) for g in range(nsb)], axis=0)
        e = jnp.exp2(jnp.minimum(bq + cref, kref))
        if n != SUB:
            parts = []
            for g in range(nsb):
                piece = e[g * half:(g + 1) * half]
                parts += [piece, zeros] if rev else [zeros, piece]
            e = jnp.concatenate(parts, axis=0)
        ps.append(e.astype(BF16) * qb)
    acc = jnp.dot(jnp.concatenate(ps, axis=1), emat_ref[...], preferred_element_type=F32)

    qhat = (q * jnp.exp2(b)).astype(BF16)
    for c in range(NCH):
        sl_c = slice(c * CHUNK, (c + 1) * CHUNK)
        attn = (jnp.where(prev_blk, jnp.concatenate(off[c], axis=0), 0.0)
                + jnp.where(diag_blk, acc[sl_c, :CHUNK], 0.0))
        o = jnp.dot(attn.astype(BF16), vb[sl_c], preferred_element_type=F32)
        o = o + lax.dot_general(qhat[sl_c], st_in[c].astype(BF16), _NT, preferred_element_type=F32)
        o_ref[0, sl_c, hs] = o.astype(o_ref.dtype)


def _hgrn_kernel(aqf, aff, aif, aqb, afb, aib, fb_ref, lb_ref, mask_ref, emat_ref, of_ref, ob_ref,
                 stf_ref, stb_ref):
    j = pl.program_id(2)

    @pl.when(j == 0)
    def _():
        stf_ref[...] = jnp.zeros_like(stf_ref)
        stb_ref[...] = jnp.zeros_like(stb_ref)

    feats = []
    for hh in range(A_HPS):
        hs = slice(hh * A_D, (hh + 1) * A_D)
        feats.append((hs,
                      _hgrn_state(aff[0, :, hs], aif[0, :, hs], fb_ref[0][:, hs], lb_ref[0][:, hs],
                                  mask_ref.at[0:1], stf_ref.at[hh], False),
                      _hgrn_state(afb[0, :, hs], aib[0, :, hs], fb_ref[1][:, hs], lb_ref[1][:, hs],
                                  mask_ref.at[1:2], stb_ref.at[hh], True)))
    for hs, feats_f, feats_b in feats:
        _hgrn_output(aqf[0, :, hs], feats_f, emat_ref, of_ref, hs, False)
        _hgrn_output(aqb[0, :, hs], feats_b, emat_ref, ob_ref, hs, True)


def _hgrn_scan(p3, f_bias, lb, n_lat):
    Bn = p3.shape[0]
    A = A_HEADS
    masks, emat = _hgrn_consts()

    W = A_HPS * A_D
    G = A // A_HPS

    def feat(rev, grp):
        return pl.BlockSpec((1, TOK, W), lambda b, h, j: (b, _tok_block(rev, j, n_lat), grp * G + h))

    def outp(rev):
        return pl.BlockSpec((1, TOK, W), lambda b, h, j: (b, _out_block(rev, j, n_lat), h))

    par = pl.BlockSpec((2, 1, W), lambda b, h, j: (0, 0, h))
    osh = jax.ShapeDtypeStruct((Bn, n_lat * TOK, A * A_D), BF16)
    return pl.pallas_call(
        _hgrn_kernel,
        grid=(Bn, G, n_lat + 1),
        in_specs=[feat(False, 0), feat(False, 1), feat(False, 3),
                  feat(True, 0), feat(True, 2), feat(True, 3),
                  par, par,
                  pl.BlockSpec((2, 2 * TOK, TOK), lambda b, h, j: (0, 0, 0)),
                  pl.BlockSpec((SUB * A_D, LANE), lambda b, h, j: (0, 0))],
        out_specs=[outp(False), outp(True)],
        out_shape=[osh, osh],
        scratch_shapes=[pltpu.VMEM((A_HPS, A_D, A_D), F32), pltpu.VMEM((A_HPS, A_D, A_D), F32)],
        compiler_params=_cparams(("parallel", "parallel", "arbitrary")),
        name="hgrn2_scan",
    )(p3, p3, p3, p3, p3, p3, f_bias.reshape(2, 1, A * A_D), lb.reshape(2, 1, A * A_D), masks, emat)


def _mlstm_state(bq_ref, bk_ref, bv_ref, gr_ref, gc_ref, cwq, cwk, first, last_, c_ref, n_ref, m_ref,
                 hh, rev):
    def conv(u, cw):
        up = jnp.where(first, 0.0, pltpu.roll(u, 1, 0))
        dn = jnp.where(last_, 0.0, pltpu.roll(u, TOK - 1, 0))
        return _silu(cw[3:4] + up * cw[0:1] + u * cw[1:2] + dn * cw[2:3])

    hs = slice(hh * B_DQK, (hh + 1) * B_DQK)
    vs = slice(hh * B_DV, (hh + 1) * B_DV)
    c_ref, n_ref, m_ref = c_ref.at[hh], n_ref.at[hh], m_ref.at[hh]
    q_all = conv(bq_ref[0, :, hs], cwq[:, hs])
    k_all = conv(bk_ref[0, :, hs], cwk[:, hs]) * (B_DQK ** -0.5)
    seen, seen_t = _chunk_masks(rev)

    pre = []
    for c in range(NCH):
        sl_c = slice(c * CHUNK, (c + 1) * CHUNK)
        k = k_all[sl_c]
        vb = bv_ref[0, sl_c, vs].astype(BF16)
        g_r = gr_ref[0, 0, hh, c]
        g_c = gc_ref[0, 0, hh, c]
        ii_r = g_r[0:1]
        lf_r = _log_sigmoid(g_r[1:2])
        ii_c = g_c[:, 0:1]
        lf_c = _log_sigmoid(g_c[:, 1:2])
        b_c = jnp.sum(jnp.where(seen, lf_r, 0.0), axis=1, keepdims=True)
        b_r = jnp.sum(jnp.where(seen_t, lf_c, 0.0), axis=0, keepdims=True)
        total = jnp.sum(lf_r, axis=1, keepdims=True)
        logs = total - b_c + ii_c
        ms = jnp.max(logs, axis=0, keepdims=True)
        kw = k * jnp.exp(logs - ms)
        upd = jnp.dot(kw.T.astype(BF16), vb, preferred_element_type=F32)
        nupd = jnp.sum(kw, axis=0, keepdims=True)
        pre.append((vb, ii_r, b_c, b_r, total, ms, upd, nupd))

    cmat, nvec, m = c_ref[...], n_ref[...], m_ref[:, 0:1]
    st_in = [None] * NCH
    for c in (range(NCH - 1, -1, -1) if rev else range(NCH)):
        _, _, _, _, total, ms, upd, nupd = pre[c]
        st_in[c] = (cmat, nvec, m)
        m_new = jnp.maximum(total + m, ms)
        dec = jnp.exp(total + m - m_new)
        sc = jnp.exp(ms - m_new)
        cmat = dec * cmat + sc * upd
        nvec = dec * nvec + sc * nupd
        m = m_new
    c_ref[...] = cmat
    n_ref[...] = nvec
    m_ref[...] = jnp.broadcast_to(m, m_ref.shape)
    return q_all, k_all, pre, st_in


def _mlstm_output(feats, o_ref, hh, rev):
    q_all, k_all, pre, st_in = feats
    seen, _ = _chunk_masks(rev)
    for c in range(NCH):
        sl_c = slice(c * CHUNK, (c + 1) * CHUNK)
        vb, ii_r, b_c, b_r, _, _, _, _ = pre[c]
        cm, nv, m0 = st_in[c]
        q = q_all[sl_c]
        qb = q.astype(BF16)
        kb = k_all[sl_c].astype(BF16)
        logw = jnp.where(seen, b_c - b_r + ii_r, -jnp.inf)
        mw = jnp.max(logw, axis=1, keepdims=True)
        qk = lax.dot_general(qb, kb, _NT, preferred_element_type=F32) * jnp.exp(logw - mw)
        num0 = jnp.dot(qk.astype(BF16), vb, preferred_element_type=F32)
        den0 = jnp.sum(qk, axis=1, keepdims=True)
        log_inter = b_c + m0
        m_t = jnp.maximum(mw, log_inter)
        r = jnp.exp(mw - m_t)
        a = jnp.exp(log_inter - m_t)
        num = r * num0 + a * jnp.dot(qb, cm.astype(BF16), preferred_element_type=F32)
        den = r * den0 + a * jnp.sum(q * nv, axis=1, keepdims=True)
        h = num / jnp.maximum(jnp.abs(den), jnp.exp(-m_t))
        o_ref[0, sl_c, hh * B_DV:(hh + 1) * B_DV] = h.astype(o_ref.dtype)


def _mlstm_kernel(bqf, bkf, bvf, grf, gcf, bqb, bkb, bvb, grb, gcb, cwq_ref, cwk_ref, of_ref, ob_ref,
                  cf_ref, nf_ref, mf_ref, cb_ref, nb_ref, mb_ref):
    j = pl.program_id(2)

    @pl.when(j == 0)
    def _():
        for ref in (cf_ref, nf_ref, mf_ref, cb_ref, nb_ref, mb_ref):
            ref[...] = jnp.zeros_like(ref)

    t = lax.broadcasted_iota(jnp.int32, (TOK, 1), 0)
    row_mask = jnp.where(j == 0, TOK - 1, CHUNK - 1)
    pos = t & row_mask
    first = pos == 0
    last_ = pos == row_mask
    cwq = cwq_ref[:, 0, :]
    cwk = cwk_ref[:, 0, :]
    feats = []
    for hh in range(B_HPS):
        feats.append((_mlstm_state(bqf, bkf, bvf, grf, gcf, cwq, cwk, first, last_, cf_ref, nf_ref, mf_ref,
                                   hh, False),
                      _mlstm_state(bqb, bkb, bvb, grb, gcb, cwq, cwk, first, last_, cb_ref, nb_ref, mb_ref,
                                   hh, True)))
    for hh, (feats_f, feats_b) in enumerate(feats):
        _mlstm_output(feats_f, of_ref, hh, False)
        _mlstm_output(feats_b, ob_ref, hh, True)


def _mlstm_scan(p3, g_rows, g_cols, conv4, n_lat):
    Bn = p3.shape[0]
    A, Bh = A_HEADS, B_HEADS
    P = B_HPS
    G = Bh // P
    q0 = 5 * A // P
    k0 = (5 * A + Bh) // P
    v0 = (5 * A + 2 * Bh) // (2 * P)

    def specs(rev):
        d = int(rev)
        tb = lambda j: _tok_block(rev, j, n_lat)
        return [pl.BlockSpec((1, TOK, P * B_DQK), lambda b, h, j: (b, tb(j), q0 + h)),
                pl.BlockSpec((1, TOK, P * B_DQK), lambda b, h, j: (b, tb(j), k0 + h)),
                pl.BlockSpec((1, TOK, P * B_DV), lambda b, h, j: (b, tb(j), v0 + h)),
                pl.BlockSpec((1, 1, P, NCH, 2, CHUNK), lambda b, h, j: (b, d, h, tb(j), 0, 0)),
                pl.BlockSpec((1, 1, P, NCH, CHUNK, 2), lambda b, h, j: (b, d, h, tb(j), 0, 0))]

    def outp(rev):
        return pl.BlockSpec((1, TOK, P * B_DV), lambda b, h, j: (b, _out_block(rev, j, n_lat), h))

    osh = jax.ShapeDtypeStruct((Bn, n_lat * TOK, Bh * B_DV), BF16)
    state = [pltpu.VMEM((P, B_DQK, B_DV), F32), pltpu.VMEM((P, 1, B_DQK), F32),
             pltpu.VMEM((P, 1, LANE), F32)]
    return pl.pallas_call(
        _mlstm_kernel,
        grid=(Bn, G, n_lat + 1),
        in_specs=specs(False) + specs(True) + [
            pl.BlockSpec((4, 1, P * B_DQK), lambda b, h, j: (0, 0, h)),
            pl.BlockSpec((4, 1, P * B_DQK), lambda b, h, j: (0, 0, G + h))],
        out_specs=[outp(False), outp(True)],
        out_shape=[osh, osh],
        scratch_shapes=state + state,
        compiler_params=_cparams(("parallel", "parallel", "arbitrary")),
        name="mlstm_scan",
    )(p3, p3, p3, g_rows, g_cols, p3, p3, p3, g_rows, g_cols, conv4, conv4)


def _readout_kernel(oaf_ref, oab_ref, obf_ref, obb_ref, ag_ref, bo_ref, na_ref, nb_ref, ya_ref, yb_ref):
    for h in range(A_HEADS):
        sl = slice(h * A_D, (h + 1) * A_D)
        o = oaf_ref[0, :, sl].astype(F32) + oab_ref[0, :, sl].astype(F32)
        ya_ref[0, :, sl] = (_rms(o, na_ref[:, sl]) * _silu(ag_ref[0, :, sl])).astype(ya_ref.dtype)
    for h in range(B_HEADS):
        sl = slice(h * B_DV, (h + 1) * B_DV)
        o = obf_ref[0, :, sl].astype(F32) + obb_ref[0, :, sl].astype(F32)
        yb_ref[0, :, sl] = (_rms(o, nb_ref[:, sl]) * _sigmoid(bo_ref[0, :, sl])).astype(yb_ref.dtype)


def _readout(oa, ob, p3, norm_a, norm_b, n_lat):
    Bn = p3.shape[0]
    T = n_lat * TOK
    wa = A_HEADS * A_D
    wb = B_HEADS * B_DV
    ag_blk = 4
    bo_blk = (5 * A_HEADS + 4 * B_HEADS) * LANE // wb
    sa = pl.BlockSpec((1, TOK, wa), lambda b, i: (b, i, 0))
    sb = pl.BlockSpec((1, TOK, wb), lambda b, i: (b, i, 0))
    return pl.pallas_call(
        _readout_kernel,
        grid=(Bn, n_lat),
        in_specs=[sa, sa, sb, sb,
                  pl.BlockSpec((1, TOK, wa), lambda b, i: (b, i, ag_blk)),
                  pl.BlockSpec((1, TOK, wb), lambda b, i: (b, i, bo_blk)),
                  pl.BlockSpec((1, wa), lambda b, i: (0, 0)),
                  pl.BlockSpec((1, wb), lambda b, i: (0, 0))],
        out_specs=[sa, sb],
        out_shape=[jax.ShapeDtypeStruct((Bn, T, wa), BF16),
                   jax.ShapeDtypeStruct((Bn, T, wb), BF16)],
        compiler_params=_cparams(("parallel", "parallel")),
        name="readout",
    )(oa[0], oa[1], ob[0], ob[1], p3, p3, norm_a.reshape(1, wa), norm_b.reshape(1, wb))


def _merge_kernel(ya_ref, yb_ref, wa_ref, wb_ref, ga_ref, gb_ref, o_ref):
    pa = jnp.dot(ya_ref[0], wa_ref[...], preferred_element_type=F32)
    pb = jnp.dot(yb_ref[0], wb_ref[...], preferred_element_type=F32)
    o_ref[0] = (_sigmoid(ga_ref[0]) * pa + _sigmoid(gb_ref[0]) * pb).astype(o_ref.dtype)


def _merge(ya, yb, wa, wb, p3, D):
    Bn, T, ka = ya.shape
    kb = yb.shape[2]
    tm = _pick(T, 1024)
    tn = _pick(D, 512)
    ga0 = 0
    gb0 = D // tn
    return pl.pallas_call(
        _merge_kernel,
        grid=(Bn, T // tm, D // tn),
        in_specs=[pl.BlockSpec((1, tm, ka), lambda b, i, j: (b, i, 0)),
                  pl.BlockSpec((1, tm, kb), lambda b, i, j: (b, i, 0)),
                  pl.BlockSpec((ka, tn), lambda b, i, j: (0, j)),
                  pl.BlockSpec((kb, tn), lambda b, i, j: (0, j)),
                  pl.BlockSpec((1, tm, tn), lambda b, i, j: (b, i, ga0 + j)),
                  pl.BlockSpec((1, tm, tn), lambda b, i, j: (b, i, gb0 + j))],
        out_specs=pl.BlockSpec((1, tm, tn), lambda b, i, j: (b, i, j)),
        out_shape=jax.ShapeDtypeStruct((Bn, T, D), BF16),
        compiler_params=_cparams(("parallel", "parallel", "arbitrary")),
        name="merge",
    )(ya, yb, wa, wb, p3, p3)


def _resid_router_kernel(x_ref, mix_ref, g1_ref, g2_ref, gate_ref, sh_ref, sc_ref, wr_ref,
                         h_ref, v_ref, aff_ref):
    h = x_ref[0] + gate_ref[0] * _rms(mix_ref[0].astype(F32), g1_ref[...])
    h_ref[0] = h
    v = _rms(h, g2_ref[...]) * (1.0 + sc_ref[0]) + sh_ref[0]
    v_ref[0] = _pack_bf16_pairs(v)
    logits = jnp.dot(v, wr_ref[...], precision=lax.Precision.HIGHEST, preferred_element_type=F32)
    lane = lax.broadcasted_iota(jnp.int32, logits.shape, 1)
    logits = jnp.where(lane < N_EXPERTS, logits, -jnp.inf)
    e = jnp.exp(logits - jnp.max(logits, axis=-1, keepdims=True))
    aff_ref[0] = e / jnp.sum(e, axis=-1, keepdims=True)


def _resid_router(x, mix, g1, g2, mod3, w_router_pad):
    Bn, T, D = x.shape
    mspec = lambda k: pl.BlockSpec((1, 1, D), lambda b, i: (b, 0, k))
    tok = pl.BlockSpec((1, TOK, D), lambda b, i: (b, i, 0))
    vec = pl.BlockSpec((1, D), lambda b, i: (0, 0))
    return pl.pallas_call(
        _resid_router_kernel,
        grid=(Bn, T // TOK),
        in_specs=[tok, tok, vec, vec, mspec(2), mspec(3), mspec(4),
                  pl.BlockSpec((D, LANE), lambda b, i: (0, 0))],
        out_specs=[tok, pl.BlockSpec((1, TOK, D // 2), lambda b, i: (b, i, 0)),
                   pl.BlockSpec((1, TOK, LANE), lambda b, i: (b, i, 0))],
        out_shape=[jax.ShapeDtypeStruct((Bn, T, D), F32),
                   jax.ShapeDtypeStruct((Bn, T, D // 2), jnp.uint32),
                   jax.ShapeDtypeStruct((Bn, T, LANE), F32)],
        compiler_params=_cparams(("parallel", "parallel")),
        name="resid_router",
    )(x, mix, g1.reshape(1, D), g2.reshape(1, D), mod3, mod3, mod3, w_router_pad)


def _select_kernel(aff_ref, slot_ref, idx_ref, offs_ref, *, cap, tk):
    E, T = aff_ref.shape[1], aff_ref.shape[2]
    nck = T // LANE

    def key():
        return lax.bitcast_convert_type(aff_ref[0], jnp.int32)

    def bit_step(i, tau):
        cand = tau | jnp.left_shift(jnp.int32(1), 30 - i)
        cnt = jnp.sum(jnp.where(key() >= cand, 1.0, 0.0), axis=1, keepdims=True)
        return jnp.where(cnt >= cap, cand, tau)

    tau = lax.fori_loop(0, 31, bit_step, jnp.zeros((E, 1), jnp.int32))
    kk = key()
    gt = kk > tau
    eq = kk == tau
    need = cap - jnp.sum(jnp.where(gt, 1.0, 0.0), axis=1, keepdims=True)

    r = lax.broadcasted_iota(jnp.int32, (LANE, LANE), 0)
    c = lax.broadcasted_iota(jnp.int32, (LANE, LANE), 1)
    upper = jnp.where(r < c, 1.0, 0.0).astype(BF16)
    tr = jnp.right_shift(lax.broadcasted_iota(jnp.int32, (T, LANE), 0), LANE.bit_length() - 1)
    member = jnp.where(tr == lax.broadcasted_iota(jnp.int32, (T, LANE), 1), 1.0, 0.0).astype(BF16)

    def prefix(flags):
        x = jnp.where(flags, 1.0, 0.0).astype(BF16)
        tot = jnp.dot(x, member, preferred_element_type=F32)
        offs = jnp.dot(tot.astype(BF16), upper, preferred_element_type=F32)
        parts = [jnp.dot(x[:, j * LANE:(j + 1) * LANE], upper, preferred_element_type=F32)
                 + offs[:, j:j + 1] for j in range(nck)]
        return jnp.concatenate(parts, axis=1), offs

    tie_rank, _ = prefix(eq)
    sel = gt | (eq & (tie_rank < need))
    pos, offs = prefix(sel)
    slot = jnp.where(sel, pos, -1.0)
    slot_ref[0] = slot
    offs_ref[0] = offs.astype(jnp.int32)

    tt = lax.broadcasted_iota(jnp.int32, (8, T), 1)
    rr = lax.broadcasted_iota(jnp.int32, (8, T), 0)
    digits = jnp.where(rr == 0, jnp.right_shift(tt, 6), jnp.where(rr == 1, tt & 63, 0))
    digits = digits.astype(F32).astype(BF16)
    slot_iota = lax.broadcasted_iota(jnp.int32, (cap, tk), 0).astype(F32)
    for e in range(E):
        acc = jnp.zeros((8, cap), F32)
        for kc in range(T // tk):
            ks = slice(kc * tk, (kc + 1) * tk)
            oh = jnp.where(slot[e:e + 1, ks] == slot_iota, 1.0, 0.0).astype(BF16)
            acc = acc + lax.dot_general(digits[:, ks], oh, _NT, preferred_element_type=F32)
        idx_ref[0, e:e + 1, :] = (acc[0:1] * 64.0 + acc[1:2]).astype(jnp.int32)


def _select(aff_rows, cap):
    Bn, E, T = aff_rows.shape
    assert T // LANE < LANE and T <= 4096
    tk = _pick(T, 1024)
    row = lambda n: pl.BlockSpec((1, E, n), lambda b: (b, 0, 0))
    return pl.pallas_call(
        functools.partial(_select_kernel, cap=cap, tk=tk),
        grid=(Bn,),
        in_specs=[row(T)],
        out_specs=[row(T), row(cap), row(LANE)],
        out_shape=[jax.ShapeDtypeStruct((Bn, E, T), F32),
                   jax.ShapeDtypeStruct((Bn, E, cap), jnp.int32),
                   jax.ShapeDtypeStruct((Bn, E, LANE), jnp.int32)],
        compiler_params=_cparams(("parallel",)),
        name="ec_select",
    )(aff_rows)


def _pack_bf16_pairs(x):
    half = x.shape[1] // 2
    bits = lax.bitcast_convert_type(x.astype(BF16).astype(F32), jnp.uint32)
    return bits[:, half:] | (bits[:, :half] >> 16)


def _unpack_bf16_pairs(p):
    lo = lax.bitcast_convert_type(p << 16, F32).astype(BF16)
    hi = lax.bitcast_convert_type(p & jnp.uint32(0xFFFF0000), F32).astype(BF16)
    return lo, hi


def _gather_kernel(idx_ref, v_hbm, o_ref, buf, sem, *, cap, n_tok):
    b = pl.program_id(0)
    e = pl.program_id(1)
    base = (b * N_EXPERTS + e) * cap

    def row_copy(r, tok):
        return pltpu.make_async_copy(v_hbm.at[pl.ds(b * n_tok + tok, 1), :], buf.at[pl.ds(r, 1), :], sem)

    def issue(r, carry):
        row_copy(r, idx_ref[base + r]).start()
        return carry

    def drain(r, carry):
        row_copy(r, 0).wait()
        return carry

    lax.fori_loop(0, cap, issue, 0)
    lax.fori_loop(0, cap, drain, 0)
    half = buf.shape[1]
    lo, hi = _unpack_bf16_pairs(buf[...])
    o_ref[0, 0, :, :half] = lo
    o_ref[0, 0, :, half:] = hi


def _gather(idx, v_pairs, cap):
    Bn, T, half = v_pairs.shape
    D = 2 * half
    return pl.pallas_call(
        functools.partial(_gather_kernel, cap=cap, n_tok=T),
        grid_spec=pltpu.PrefetchScalarGridSpec(
            num_scalar_prefetch=1,
            grid=(Bn, N_EXPERTS),
            in_specs=[pl.BlockSpec(memory_space=pl.ANY)],
            out_specs=pl.BlockSpec((1, 1, cap, D), lambda b, e, idx_ref: (e, b, 0, 0)),
            scratch_shapes=[pltpu.VMEM((cap, half), jnp.uint32), pltpu.SemaphoreType.DMA(())]),
        out_shape=jax.ShapeDtypeStruct((N_EXPERTS, Bn, cap, D), BF16),
        compiler_params=_cparams(("arbitrary", "arbitrary")),
        name="ec_gather",
    )(idx.reshape(-1), v_pairs.reshape(Bn * T, half))


def _ffn1_kernel(x_ref, wg_ref, wu_ref, o_ref):
    x = x_ref[0]
    g = jnp.dot(x, wg_ref[0].astype(BF16), preferred_element_type=F32)
    u = jnp.dot(x, wu_ref[0].astype(BF16), preferred_element_type=F32)
    o_ref[0] = (_silu(g) * u).astype(o_ref.dtype)


def _ffn1(xg, wg, wu):
    E, M, D = xg.shape
    F = wg.shape[2]
    tm = _pick(M, 1024)
    tn = _pick(F, 256)
    return pl.pallas_call(
        _ffn1_kernel,
        grid=(E, M // tm, F // tn),
        in_specs=[pl.BlockSpec((1, tm, D), lambda e, i, j: (e, i, 0)),
                  pl.BlockSpec((1, D, tn), lambda e, i, j: (e, 0, j)),
                  pl.BlockSpec((1, D, tn), lambda e, i, j: (e, 0, j))],
        out_specs=pl.BlockSpec((1, tm, tn), lambda e, i, j: (e, i, j)),
        out_shape=jax.ShapeDtypeStruct((E, M, F), BF16),
        compiler_params=_cparams(("parallel", "parallel", "arbitrary")),
        name="ec_ffn_up",
    )(xg, wg, wu)


def _ffn2_kernel(h_ref, wd_ref, o_ref):
    o_ref[0] = jnp.dot(h_ref[0], wd_ref[0].astype(BF16),
                       preferred_element_type=F32).astype(o_ref.dtype)


def _ffn2(hid, wd):
    E, M, F = hid.shape
    D = wd.shape[2]
    tn = _pick(D, 512)
    return pl.pallas_call(
        _ffn2_kernel,
        grid=(E, D // tn),
        in_specs=[pl.BlockSpec((1, M, F), lambda e, j: (e, 0, 0)),
                  pl.BlockSpec((1, F, tn), lambda e, j: (e, 0, j))],
        out_specs=pl.BlockSpec((1, M, tn), lambda e, j: (e, 0, j)),
        out_shape=jax.ShapeDtypeStruct((E, M, D), BF16),
        compiler_params=_cparams(("parallel", "arbitrary")),
        name="ec_ffn_down",
    )(hid, wd)


def _scatter_kernel(offs_ref, slot_ref, aff_ref, y_ref, h_ref, g_ref, gate_ref, o_ref, *, cap, win):
    b = pl.program_id(0)
    t = pl.program_id(1)
    e = pl.program_id(2)
    tq = o_ref.shape[1]

    @pl.when(e == 0)
    def _():
        o_ref[...] = jnp.zeros_like(o_ref)

    base = (b * N_EXPERTS + e) * LANE
    lo = offs_ref[base + t * (tq // LANE)]
    hi = offs_ref[base + (t + 1) * (tq // LANE)]
    sl = jnp.broadcast_to(slot_ref[0, 0], (LANE, tq)).T
    af = jnp.broadcast_to(aff_ref[0, 0], (LANE, tq)).T
    sl = jnp.concatenate([sl] * (win // LANE), axis=1)
    af = jnp.concatenate([af] * (win // LANE), axis=1)
    lane = lax.broadcasted_iota(jnp.int32, (tq, win), 1).astype(F32)
    for w0 in range(0, cap, win):
        @pl.when((lo < w0 + win) & (hi > w0))
        def _():
            ohw = jnp.where(sl == lane + float(w0), af, 0.0).astype(BF16)
            o_ref[0] += jnp.dot(ohw, y_ref[0, 0, w0:w0 + win, :], preferred_element_type=F32)

    @pl.when(e == pl.num_programs(2) - 1)
    def _():
        o_ref[0] = h_ref[0] + gate_ref[0] * _rms(o_ref[0], g_ref[...])


def _scatter_combine(offs, slot_rows, aff_rows, y, cap, h, g3, mod3):
    E, Bn, _, D = y.shape
    T = slot_rows.shape[3]
    tq = _pick(T, 512)
    win = _pick(cap, 256)
    assert win % LANE == 0 and tq % LANE == 0
    row = pl.BlockSpec((1, 1, 1, tq), lambda b, t, e, offs_ref: (b, e, 0, t))
    return pl.pallas_call(
        functools.partial(_scatter_kernel, cap=cap, win=win),
        grid_spec=pltpu.PrefetchScalarGridSpec(
            num_scalar_prefetch=1,
            grid=(Bn, T // tq, E),
            in_specs=[row, row,
                      pl.BlockSpec((1, 1, cap, D), lambda b, t, e, offs_ref: (e, b, 0, 0)),
                      pl.BlockSpec((1, tq, D), lambda b, t, e, offs_ref: (b, t, 0)),
                      pl.BlockSpec((1, D), lambda b, t, e, offs_ref: (0, 0)),
                      pl.BlockSpec((1, 1, D), lambda b, t, e, offs_ref: (b, 0, 5))],
            out_specs=pl.BlockSpec((1, tq, D), lambda b, t, e, offs_ref: (b, t, 0))),
        out_shape=jax.ShapeDtypeStruct((Bn, T, D), F32),
        compiler_params=_cparams(("parallel", "parallel", "arbitrary")),
        name="ec_scatter",
    )(offs.reshape(-1), slot_rows, aff_rows, y, h, g3.reshape(1, D), mod3)


def _layer(h_lat, ctx, c8, l, lb_l, w_ada, b_ada, g_norm, w_in, hgrn_f_bias, hgrn_norm,
           mlstm_conv_w, mlstm_conv_b, mlstm_gate_b, mlstm_norm, w_branch_a, w_branch_b, w_out,
           w_router, w_expert_gate, w_expert_up, w_expert_down):
    Bn, T, D = h_lat.shape
    n_lat = T // TOK
    n_tok = T + TOK
    A, Bh = A_HEADS, B_HEADS
    g0 = (5 * A + 6 * Bh) * LANE

    mod = _modulation(c8, w_ada[l], b_ada[l])
    mod3 = mod.reshape(8, 1, N_MOD * D)

    u = _prenorm(h_lat, ctx, g_norm[l, 0], mod3)

    w_merge, w_gate = _wsplit(w_in, l, g0, 4 * Bh, 2 * D)
    gate_bias = jnp.pad(mlstm_gate_b[l].reshape(4 * Bh), (0, LANE - 4 * Bh))
    u2 = u.reshape(Bn * n_tok, D)
    p3 = _matmul(u2, w_in, F32, n_cols=g0, layer=l, name="in_proj").reshape(Bn, n_tok, g0)
    p_merge = _matmul(u2, w_merge, F32, name="in_proj_merge").reshape(Bn, n_tok, 2 * D)
    gates = _matmul(u2, w_gate, F32, tn=LANE, bias=gate_bias, name="in_proj_gates")

    g6 = gates[:, :4 * Bh].reshape(Bn, n_tok // CHUNK, CHUNK, 2, 2, Bh)
    g_rows = g6.transpose(0, 3, 5, 1, 4, 2)
    g_cols = g6.transpose(0, 3, 5, 1, 2, 4)
    conv4 = jnp.concatenate([mlstm_conv_w[l], mlstm_conv_b[l][None]], axis=0)
    conv4 = conv4.reshape(4, 1, 2 * Bh * B_DQK)

    oa = _hgrn_scan(p3, hgrn_f_bias[l], lb_l, n_lat)
    ob = _mlstm_scan(p3, g_rows, g_cols, conv4, n_lat)
    ya, yb = _readout(oa, ob, p3, hgrn_norm[l], mlstm_norm[l], n_lat)
    merged = _merge(ya, yb, w_branch_a[l].astype(BF16), w_branch_b[l].astype(BF16), p_merge, D)
    mix = _matmul(merged.reshape(Bn * T, D), w_out[l].astype(BF16), BF16, name="out_proj")

    w_router_pad = jnp.pad(w_router[l], ((0, 0), (0, LANE - N_EXPERTS)))
    h_lat, v_lat, aff = _resid_router(h_lat, mix.reshape(Bn, T, D), g_norm[l, 1], g_norm[l, 2],
                                      mod3, w_router_pad)

    cap = CAPACITY * T // N_EXPERTS
    aff_rows = aff[:, :, :N_EXPERTS].transpose(0, 2, 1)
    slot, idx, offs = _select(aff_rows, cap)
    xg = _gather(idx, v_lat, cap)
    hid = _ffn1(xg.reshape(N_EXPERTS, Bn * cap, D), w_expert_gate[l], w_expert_up[l])
    y = _ffn2(hid, w_expert_down[l]).reshape(N_EXPERTS, Bn, cap, D)
    return _scatter_combine(offs, slot.reshape(Bn, N_EXPERTS, 1, T),
                            aff_rows.reshape(Bn, N_EXPERTS, 1, T), y, cap, h_lat, g_norm[l, 3], mod3)


def kernel(x, c, ctx, c_ctx, w_ada, b_ada, g_norm, w_in, hgrn_f_bias, hgrn_lb, hgrn_norm,
           mlstm_conv_w, mlstm_conv_b, mlstm_gate_b, mlstm_norm, w_branch_a, w_branch_b, w_out,
           w_router, w_expert_gate, w_expert_up, w_expert_down):
    Bn, T, D = x.shape
    depth = w_ada.shape[0]
    assert depth == 1, "context outputs are only produced for the state hand-off (single layer)"
    assert ctx.shape[1] == TOK and T % TOK == 0 and Bn < 8
    lb_all = jnp.cumsum(jax.nn.softmax(hgrn_lb.astype(F32), axis=1), axis=1)
    c8 = jnp.zeros((8, D), F32).at[:Bn].set(c).at[Bn].set(c_ctx)
    h_lat = x
    for l in range(depth):
        h_lat = _layer(h_lat, ctx, c8, l, lb_all[:, l], w_ada, b_ada, g_norm, w_in, hgrn_f_bias,
                       hgrn_norm, mlstm_conv_w, mlstm_conv_b, mlstm_gate_b, mlstm_norm,
                       w_branch_a, w_branch_b, w_out, w_router, w_expert_gate, w_expert_up,
                       w_expert_down)
    return h_lat.astype(x.dtype)
```

```python
import functools

import jax
import jax.numpy as jnp
from jax import lax
from jax.experimental import pallas as pl
from jax.experimental.pallas import tpu as pltpu

F32 = jnp.float32
BF16 = jnp.bfloat16

EPS = 1e-6
N_MOD = 6
A_HEADS = 16
A_D = 128
B_HEADS = 8
B_DQK = 128
B_DV = 256
N_EXPERTS = 16
CAPACITY = 2
CHUNK = 64
SUB = 16
A_HPS = 4
B_HPS = 4
TOK = 256
NCH = TOK // CHUNK
NSUB = CHUNK // SUB
LANE = 128
VMEM_LIMIT = 56 * 1024 * 1024

_NT = (((1,), (1,)), ((), ()))


def _pick(n, pref):
    t = min(n, pref)
    while n % t:
        t //= 2
    return t


def _cparams(sem):
    return pltpu.CompilerParams(dimension_semantics=sem, vmem_limit_bytes=VMEM_LIMIT)


def _silu(x):
    return x / (1.0 + jnp.exp(-x))


def _sigmoid(x):
    return 1.0 / (1.0 + jnp.exp(-x))


def _sig_pair(z):
    t = jnp.exp(-jnp.abs(z))
    r = 1.0 / (1.0 + t)
    tr = t * r
    pos = z >= 0
    return jnp.where(pos, r, tr), jnp.where(pos, tr, r)


def _log_sigmoid(x):
    return jnp.minimum(x, 0.0) - jnp.log(1.0 + jnp.exp(-jnp.abs(x)))


def _rms(xf, w):
    return xf * lax.rsqrt(jnp.mean(xf * xf, axis=-1, keepdims=True) + EPS) * w


def _rows(x, r, n):
    return jnp.broadcast_to(x[r:r + 1], (n, x.shape[1]))


def _mod_kernel(c_ref, w_ref, b_ref, o_ref):
    a = _silu(c_ref[...]).astype(BF16)
    o_ref[...] = jnp.dot(a, w_ref[...].astype(BF16), preferred_element_type=F32) + b_ref[...]


def _modulation(c8, w, b):
    D, N = w.shape
    tn = _pick(N, 512)
    return pl.pallas_call(
        _mod_kernel,
        grid=(N // tn,),
        in_specs=[pl.BlockSpec((8, D), lambda j: (0, 0)),
                  pl.BlockSpec((D, tn), lambda j: (0, j)),
                  pl.BlockSpec((1, tn), lambda j: (0, j))],
        out_specs=pl.BlockSpec((8, tn), lambda j: (0, j)),
        out_shape=jax.ShapeDtypeStruct((8, N), F32),
        compiler_params=_cparams(("parallel",)),
        name="adaln_mod",
    )(c8, w, b.reshape(1, N))


def _prenorm_kernel(x_ref, ctx_ref, g_ref, sh_ref, sc_ref, o_ref, *, n_lat):
    i = pl.program_id(1)
    g = g_ref[...]
    sh = sh_ref[0]
    sc = sc_ref[0]

    @pl.when(i < n_lat)
    def _():
        o_ref[0] = (_rms(x_ref[0], g) * (1.0 + sc) + sh).astype(o_ref.dtype)

    @pl.when(i == n_lat)
    def _():
        o_ref[0] = (_rms(ctx_ref[0], g) * (1.0 + sc) + sh).astype(o_ref.dtype)


def _prenorm(x, ctx, g, mod3):
    Bn, T, D = x.shape
    n_lat = T // TOK
    row = lambda b, i: jnp.where(i == n_lat, Bn, b)
    return pl.pallas_call(
        functools.partial(_prenorm_kernel, n_lat=n_lat),
        grid=(Bn, n_lat + 1),
        in_specs=[pl.BlockSpec((1, TOK, D), lambda b, i: (b, jnp.minimum(i, n_lat - 1), 0)),
                  pl.BlockSpec((1, TOK, D), lambda b, i: (b, 0, 0)),
                  pl.BlockSpec((1, D), lambda b, i: (0, 0)),
                  pl.BlockSpec((1, 1, D), lambda b, i: (row(b, i), 0, 0)),
                  pl.BlockSpec((1, 1, D), lambda b, i: (row(b, i), 0, 1))],
        out_specs=pl.BlockSpec((1, TOK, D), lambda b, i: (b, i, 0)),
        out_shape=jax.ShapeDtypeStruct((Bn, T + TOK, D), BF16),
        compiler_params=_cparams(("parallel", "arbitrary")),
        name="prenorm",
    )(x, ctx, g.reshape(1, D), mod3, mod3)


_NN = (((1,), (0,)), ((), ()))


def _mm_kernel(a_ref, b_ref, o_ref, *, dims):
    o_ref[...] = lax.dot_general(a_ref[...], b_ref[...].astype(BF16), dims,
                                 preferred_element_type=F32).astype(o_ref.dtype)


def _mm_bias_kernel(a_ref, b_ref, bias_ref, o_ref, *, dims):
    o_ref[...] = (lax.dot_general(a_ref[...], b_ref[...].astype(BF16), dims, preferred_element_type=F32)
                  + bias_ref[...]).astype(o_ref.dtype)


def _matmul(a, b, out_dtype, tm=1024, tn=512, bias=None, name="matmul"):
    M, K = a.shape
    N = b.shape[1]
    tm = _pick(M, tm)
    tn = _pick(N, tn)
    return _matmul_call(a, b, pl.BlockSpec((K, tn), lambda i, j: (0, j)), _NN, N, out_dtype, tm, tn,
                        bias, name)


def _matmul_t(a, bt, out_dtype, tm=1024, tn=512, bias=None, n_cols=None, layer=None, name="matmul_t"):
    M, K = a.shape
    N = bt.shape[-2] if n_cols is None else n_cols
    tm = _pick(M, tm)
    tn = _pick(N, tn)
    if layer is None:
        b_spec = pl.BlockSpec((tn, K), lambda i, j: (j, 0))
    else:
        b_spec = pl.BlockSpec((None, tn, K), lambda i, j: (layer, j, 0))
    return _matmul_call(a, bt, b_spec, _NT, N, out_dtype, tm, tn, bias, name)


def _matmul_call(a, b, b_spec, dims, N, out_dtype, tm, tn, bias, name):
    M, K = a.shape
    in_specs = [pl.BlockSpec((tm, K), lambda i, j: (i, 0)), b_spec]
    args = [a, b]
    kern = functools.partial(_mm_kernel, dims=dims)
    if bias is not None:
        in_specs.append(pl.BlockSpec((1, tn), lambda i, j: (0, j)))
        args.append(bias.reshape(1, N))
        kern = functools.partial(_mm_bias_kernel, dims=dims)
    return pl.pallas_call(
        kern,
        grid=(M // tm, N // tn),
        in_specs=in_specs,
        out_specs=pl.BlockSpec((tm, tn), lambda i, j: (i, j)),
        out_shape=jax.ShapeDtypeStruct((M, N), out_dtype),
        compiler_params=_cparams(("parallel", "arbitrary")),
        name=name,
    )(*args)


def _wsplit_kernel(a_ref, b_ref, o_ref, g_ref, *, shift):
    j = pl.program_id(1)
    tn = o_ref.shape[0]
    cat = jnp.concatenate([a_ref[...], b_ref[...]], axis=0)
    o_ref[...] = cat[shift:shift + tn].astype(o_ref.dtype)

    @pl.when(j == 0)
    def _():
        row = lax.broadcasted_iota(jnp.int32, g_ref.shape, 0)
        g_ref[...] = jnp.where(row < shift, cat[:LANE], 0.0).astype(g_ref.dtype)


def _wsplit(wt, layer, row0, shift, n):
    K = wt.shape[2]
    tk = _pick(K, 1024)
    tn = _pick(n, 512)
    assert row0 % tn == 0 and 0 < shift < LANE and shift % 8 == 0 and row0 + shift + n <= wt.shape[1]
    r0 = row0 // tn
    return pl.pallas_call(
        functools.partial(_wsplit_kernel, shift=shift),
        grid=(K // tk, n // tn),
        in_specs=[pl.BlockSpec((None, tn, tk), lambda i, j: (layer, r0 + j, i)),
                  pl.BlockSpec((None, tn, tk), lambda i, j: (layer, r0 + j + 1, i))],
        out_specs=[pl.BlockSpec((tn, tk), lambda i, j: (j, i)),
                   pl.BlockSpec((LANE, tk), lambda i, j: (0, i))],
        out_shape=[jax.ShapeDtypeStruct((n, K), BF16), jax.ShapeDtypeStruct((LANE, K), BF16)],
        compiler_params=_cparams(("parallel", "arbitrary")),
        name="w_split",
    )(wt, wt)


def _tok_block(rev, j, n_lat):
    lat = n_lat - j if rev else j - 1
    return jnp.where(j == 0, n_lat, lat)


def _out_block(rev, j, n_lat):
    return jnp.clip(n_lat - j if rev else j - 1, 0, n_lat - 1)


def _chunk_masks(rev):
    row = lax.broadcasted_iota(jnp.int32, (CHUNK, CHUNK), 0)
    col = lax.broadcasted_iota(jnp.int32, (CHUNK, CHUNK), 1)
    return (col >= row, col <= row) if rev else (col <= row, col >= row)


def _hgrn_consts():
    t = jnp.arange(TOK)
    same = (t[:, None] // CHUNK) == (t[None, :] // CHUNK)
    sub_r, sub_c = t[:, None] // SUB, t[None, :] // SUB
    fwd = jnp.concatenate([same & (t[None, :] <= t[:, None]), same & (sub_c < sub_r)], axis=0)
    bwd = jnp.concatenate([same & (t[None, :] >= t[:, None]), same & (sub_c > sub_r)], axis=0)
    masks = jnp.stack([fwd, bwd]).astype(BF16)
    r = jnp.arange(SUB * A_D)
    emat = ((r[:, None] // A_D) == (jnp.arange(LANE)[None, :] % SUB)).astype(BF16)
    return masks, emat


def _hgrn_state(af, v, fb, lb, mask_ref, st_ref, rev):
    z = af + fb
    sp, sn = _sig_pair(z)
    k = (1.0 - lb) * sn
    lf2 = jnp.log2(lb + (1.0 - lb) * sp)
    lk2 = jnp.log2(k)

    hi = lf2.astype(BF16)
    r1 = lf2 - hi.astype(F32)
    mid = r1.astype(BF16)
    lo = (r1 - mid.astype(F32)).astype(BF16)
    cs = jnp.dot(mask_ref[0], jnp.concatenate([hi, mid, lo], axis=1), preferred_element_type=F32)
    cs = cs[:, :A_D] + cs[:, A_D:2 * A_D] + cs[:, 2 * A_D:]
    b = cs[:TOK]
    ent = cs[TOK:]
    c_all = lk2 - b

    last = [(c * CHUNK if rev else c * CHUNK + CHUNK - 1) for c in range(NCH)]
    tot_rows = jnp.concatenate([_rows(b, last[c], CHUNK) for c in range(NCH)], axis=0)
    khat = jnp.exp2(jnp.minimum(tot_rows + c_all, lk2)).astype(BF16)
    upd = [jnp.dot(v[c * CHUNK:(c + 1) * CHUNK].T.astype(BF16), khat[c * CHUNK:(c + 1) * CHUNK],
                   preferred_element_type=F32) for c in range(NCH)]

    order = range(NCH - 1, -1, -1) if rev else range(NCH)
    st = st_ref[...]
    st_in = [None] * NCH
    for c in order:
        st_in[c] = st
        st = st * jnp.exp2(b[last[c]:last[c] + 1]) + upd[c]
    st_ref[...] = st
    return v, b, ent, c_all, lk2, st_in


def _hgrn_output(aq, feats, emat_ref, o_ref, hs, rev):
    v, b, ent, c_all, lk2, st_in = feats
    q = _silu(aq)
    qb = q.astype(BF16)
    vb = v.astype(BF16)
    qt = (q * jnp.exp2(b - ent)).astype(BF16)
    row = lax.broadcasted_iota(jnp.int32, (CHUNK, CHUNK), 0)
    col = lax.broadcasted_iota(jnp.int32, (CHUNK, CHUNK), 1)
    rs = jnp.right_shift(row, SUB.bit_length() - 1)
    cs_ = jnp.right_shift(col, SUB.bit_length() - 1)
    prev_blk = (cs_ > rs) if rev else (cs_ < rs)
    diag_blk = (rs == cs_) & ((col >= row) if rev else (col <= row))

    off = [[None] * NSUB for _ in range(NCH)]
    for i in range(NSUB):
        e_i = jnp.concatenate([_rows(ent, c * CHUNK + i * SUB, CHUNK) for c in range(NCH)], axis=0)
        kt = jnp.exp2(jnp.minimum(e_i + c_all, lk2)).astype(BF16)
        for c in range(NCH):
            r0 = c * CHUNK + i * SUB
            off[c][i] = lax.dot_general(qt[r0:r0 + SUB], kt[c * CHUNK:(c + 1) * CHUNK], _NT,
                                        preferred_element_type=F32)

    half = SUB // 2
    nsb = TOK // SUB
    zeros = jnp.zeros((half, A_D), F32)
    ps = []
    for sl in range(SUB):
        n = half if (sl < half if rev else sl >= half) else SUB
        r0 = SUB - n if not rev else 0
        bq = b if n == SUB else jnp.concatenate(
            [b[g * SUB + r0:g * SUB + r0 + n] for g in range(nsb)], axis=0)
        cref = jnp.concatenate([_rows(c_all, g * SUB + sl, n) for g in range(nsb)], axis=0)
        kref = jnp.concatenate([_rows(lk2, g * SUB + sl, n) for g in range(nsb)], axis=0)
        e = jnp.exp2(jnp.minimum(bq + cref, kref))
        if n != SUB:
            parts = []
            for g in range(nsb):
                piece = e[g * half:(g + 1) * half]
                parts += [piece, zeros] if rev else [zeros, piece]
            e = jnp.concatenate(parts, axis=0)
        ps.append(e.astype(BF16) * qb)
    acc = jnp.dot(jnp.concatenate(ps, axis=1), emat_ref[...], preferred_element_type=F32)

    qhat = (q * jnp.exp2(b)).astype(BF16)
    for c in range(NCH):
        sl_c = slice(c * CHUNK, (c + 1) * CHUNK)
        attn = (jnp.where(prev_blk, jnp.concatenate(off[c], axis=0), 0.0)
                + jnp.where(diag_blk, acc[sl_c, :CHUNK], 0.0))
        o = jnp.dot(attn.astype(BF16), vb[sl_c], preferred_element_type=F32)
        o = o + lax.dot_general(qhat[sl_c], st_in[c].astype(BF16), _NT, preferred_element_type=F32)
        o_ref[0, sl_c, hs] = o.astype(o_ref.dtype)


def _hgrn_kernel(aqf, aff, aif, aqb, afb, aib, fb_ref, lb_ref, mask_ref, emat_ref, of_ref, ob_ref,
                 stf_ref, stb_ref):
    j = pl.program_id(2)

    @pl.when(j == 0)
    def _():
        stf_ref[...] = jnp.zeros_like(stf_ref)
        stb_ref[...] = jnp.zeros_like(stb_ref)

    feats = []
    for hh in range(A_HPS):
        hs = slice(hh * A_D, (hh + 1) * A_D)
        feats.append((hs,
                      _hgrn_state(aff[0, :, hs], aif[0, :, hs], fb_ref[0][:, hs], lb_ref[0][:, hs],
                                  mask_ref.at[0:1], stf_ref.at[hh], False),
                      _hgrn_state(afb[0, :, hs], aib[0, :, hs], fb_ref[1][:, hs], lb_ref[1][:, hs],
                                  mask_ref.at[1:2], stb_ref.at[hh], True)))
    for hs, feats_f, feats_b in feats:
        _hgrn_output(aqf[0, :, hs], feats_f, emat_ref, of_ref, hs, False)
        _hgrn_output(aqb[0, :, hs], feats_b, emat_ref, ob_ref, hs, True)


def _hgrn_scan(p3, f_bias, lb, n_lat):
    Bn = p3.shape[0]
    A = A_HEADS
    masks, emat = _hgrn_consts()

    W = A_HPS * A_D
    G = A // A_HPS

    def feat(rev, grp):
        return pl.BlockSpec((1, TOK, W), lambda b, h, j: (b, _tok_block(rev, j, n_lat), grp * G + h))

    def outp(rev):
        return pl.BlockSpec((1, TOK, W), lambda b, h, j: (b, _out_block(rev, j, n_lat), h))

    par = pl.BlockSpec((2, 1, W), lambda b, h, j: (0, 0, h))
    osh = jax.ShapeDtypeStruct((Bn, n_lat * TOK, A * A_D), BF16)
    return pl.pallas_call(
        _hgrn_kernel,
        grid=(Bn, G, n_lat + 1),
        in_specs=[feat(False, 0), feat(False, 1), feat(False, 3),
                  feat(True, 0), feat(True, 2), feat(True, 3),
                  par, par,
                  pl.BlockSpec((2, 2 * TOK, TOK), lambda b, h, j: (0, 0, 0)),
                  pl.BlockSpec((SUB * A_D, LANE), lambda b, h, j: (0, 0))],
        out_specs=[outp(False), outp(True)],
        out_shape=[osh, osh],
        scratch_shapes=[pltpu.VMEM((A_HPS, A_D, A_D), F32), pltpu.VMEM((A_HPS, A_D, A_D), F32)],
        compiler_params=_cparams(("parallel", "parallel", "arbitrary")),
        name="hgrn2_scan",
    )(p3, p3, p3, p3, p3, p3, f_bias.reshape(2, 1, A * A_D), lb.reshape(2, 1, A * A_D), masks, emat)


def _mlstm_state(bq_ref, bk_ref, bv_ref, gr_ref, gc_ref, cwq, cwk, first, last_, c_ref, n_ref, m_ref,
                 hh, rev):
    def conv(u, cw):
        up = jnp.where(first, 0.0, pltpu.roll(u, 1, 0))
        dn = jnp.where(last_, 0.0, pltpu.roll(u, TOK - 1, 0))
        return _silu(cw[3:4] + up * cw[0:1] + u * cw[1:2] + dn * cw[2:3])

    hs = slice(hh * B_DQK, (hh + 1) * B_DQK)
    vs = slice(hh * B_DV, (hh + 1) * B_DV)
    c_ref, n_ref, m_ref = c_ref.at[hh], n_ref.at[hh], m_ref.at[hh]
    q_all = conv(bq_ref[0, :, hs], cwq[:, hs])
    k_all = conv(bk_ref[0, :, hs], cwk[:, hs]) * (B_DQK ** -0.5)
    seen, seen_t = _chunk_masks(rev)

    pre = []
    for c in range(NCH):
        sl_c = slice(c * CHUNK, (c + 1) * CHUNK)
        k = k_all[sl_c]
        vb = bv_ref[0, sl_c, vs].astype(BF16)
        g_r = gr_ref[0, 0, hh, c]
        g_c = gc_ref[0, 0, hh, c]
        ii_r = g_r[0:1]
        lf_r = _log_sigmoid(g_r[1:2])
        ii_c = g_c[:, 0:1]
        lf_c = _log_sigmoid(g_c[:, 1:2])
        b_c = jnp.sum(jnp.where(seen, lf_r, 0.0), axis=1, keepdims=True)
        b_r = jnp.sum(jnp.where(seen_t, lf_c, 0.0), axis=0, keepdims=True)
        total = jnp.sum(lf_r, axis=1, keepdims=True)
        logs = total - b_c + ii_c
        ms = jnp.max(logs, axis=0, keepdims=True)
        kw = k * jnp.exp(logs - ms)
        upd = jnp.dot(kw.T.astype(BF16), vb, preferred_element_type=F32)
        nupd = jnp.sum(kw, axis=0, keepdims=True)
        pre.append((vb, ii_r, b_c, b_r, total, ms, upd, nupd))

    cmat, nvec, m = c_ref[...], n_ref[...], m_ref[:, 0:1]
    st_in = [None] * NCH
    for c in (range(NCH - 1, -1, -1) if rev else range(NCH)):
        _, _, _, _, total, ms, upd, nupd = pre[c]
        st_in[c] = (cmat, nvec, m)
        m_new = jnp.maximum(total + m, ms)
        dec = jnp.exp(total + m - m_new)
        sc = jnp.exp(ms - m_new)
        cmat = dec * cmat + sc * upd
        nvec = dec * nvec + sc * nupd
        m = m_new
    c_ref[...] = cmat
    n_ref[...] = nvec
    m_ref[...] = jnp.broadcast_to(m, m_ref.shape)
    return q_all, k_all, pre, st_in


def _mlstm_output(feats, o_ref, hh, rev):
    q_all, k_all, pre, st_in = feats
    seen, _ = _chunk_masks(rev)
    for c in range(NCH):
        sl_c = slice(c * CHUNK, (c + 1) * CHUNK)
        vb, ii_r, b_c, b_r, _, _, _, _ = pre[c]
        cm, nv, m0 = st_in[c]
        q = q_all[sl_c]
        qb = q.astype(BF16)
        kb = k_all[sl_c].astype(BF16)
        logw = jnp.where(seen, b_c - b_r + ii_r, -jnp.inf)
        mw = jnp.max(logw, axis=1, keepdims=True)
        qk = lax.dot_general(qb, kb, _NT, preferred_element_type=F32) * jnp.exp(logw - mw)
        num0 = jnp.dot(qk.astype(BF16), vb, preferred_element_type=F32)
        den0 = jnp.sum(qk, axis=1, keepdims=True)
        log_inter = b_c + m0
        m_t = jnp.maximum(mw, log_inter)
        r = jnp.exp(mw - m_t)
        a = jnp.exp(log_inter - m_t)
        num = r * num0 + a * jnp.dot(qb, cm.astype(BF16), preferred_element_type=F32)
        den = r * den0 + a * jnp.sum(q * nv, axis=1, keepdims=True)
        h = num / jnp.maximum(jnp.abs(den), jnp.exp(-m_t))
        o_ref[0, sl_c, hh * B_DV:(hh + 1) * B_DV] = h.astype(o_ref.dtype)


def _mlstm_kernel(bqf, bkf, bvf, grf, gcf, bqb, bkb, bvb, grb, gcb, cwq_ref, cwk_ref, of_ref, ob_ref,
                  cf_ref, nf_ref, mf_ref, cb_ref, nb_ref, mb_ref):
    j = pl.program_id(2)

    @pl.when(j == 0)
    def _():
        for ref in (cf_ref, nf_ref, mf_ref, cb_ref, nb_ref, mb_ref):
            ref[...] = jnp.zeros_like(ref)

    t = lax.broadcasted_iota(jnp.int32, (TOK, 1), 0)
    row_mask = jnp.where(j == 0, TOK - 1, CHUNK - 1)
    pos = t & row_mask
    first = pos == 0
    last_ = pos == row_mask
    cwq = cwq_ref[:, 0, :]
    cwk = cwk_ref[:, 0, :]
    feats = []
    for hh in range(B_HPS):
        feats.append((_mlstm_state(bqf, bkf, bvf, grf, gcf, cwq, cwk, first, last_, cf_ref, nf_ref, mf_ref,
                                   hh, False),
                      _mlstm_state(bqb, bkb, bvb, grb, gcb, cwq, cwk, first, last_, cb_ref, nb_ref, mb_ref,
                                   hh, True)))
    for hh, (feats_f, feats_b) in enumerate(feats):
        _mlstm_output(feats_f, of_ref, hh, False)
        _mlstm_output(feats_b, ob_ref, hh, True)


def _mlstm_scan(p3, g_rows, g_cols, conv4, n_lat):
    Bn = p3.shape[0]
    A, Bh = A_HEADS, B_HEADS
    P = B_HPS
    G = Bh // P
    q0 = 5 * A // P
    k0 = (5 * A + Bh) // P
    v0 = (5 * A + 2 * Bh) // (2 * P)

    def specs(rev):
        d = int(rev)
        tb = lambda j: _tok_block(rev, j, n_lat)
        return [pl.BlockSpec((1, TOK, P * B_DQK), lambda b, h, j: (b, tb(j), q0 + h)),
                pl.BlockSpec((1, TOK, P * B_DQK), lambda b, h, j: (b, tb(j), k0 + h)),
                pl.BlockSpec((1, TOK, P * B_DV), lambda b, h, j: (b, tb(j), v0 + h)),
                pl.BlockSpec((1, 1, P, NCH, 2, CHUNK), lambda b, h, j: (b, d, h, tb(j), 0, 0)),
                pl.BlockSpec((1, 1, P, NCH, CHUNK, 2), lambda b, h, j: (b, d, h, tb(j), 0, 0))]

    def outp(rev):
        return pl.BlockSpec((1, TOK, P * B_DV), lambda b, h, j: (b, _out_block(rev, j, n_lat), h))

    osh = jax.ShapeDtypeStruct((Bn, n_lat * TOK, Bh * B_DV), BF16)
    state = [pltpu.VMEM((P, B_DQK, B_DV), F32), pltpu.VMEM((P, 1, B_DQK), F32),
             pltpu.VMEM((P, 1, LANE), F32)]
    return pl.pallas_call(
        _mlstm_kernel,
        grid=(Bn, G, n_lat + 1),
        in_specs=specs(False) + specs(True) + [
            pl.BlockSpec((4, 1, P * B_DQK), lambda b, h, j: (0, 0, h)),
            pl.BlockSpec((4, 1, P * B_DQK), lambda b, h, j: (0, 0, G + h))],
        out_specs=[outp(False), outp(True)],
        out_shape=[osh, osh],
        scratch_shapes=state + state,
        compiler_params=_cparams(("parallel", "parallel", "arbitrary")),
        name="mlstm_scan",
    )(p3, p3, p3, g_rows, g_cols, p3, p3, p3, g_rows, g_cols, conv4, conv4)


def _readout_kernel(oaf_ref, oab_ref, obf_ref, obb_ref, ag_ref, bo_ref, na_ref, nb_ref, ya_ref, yb_ref):
    for h in range(A_HEADS):
        sl = slice(h * A_D, (h + 1) * A_D)
        o = oaf_ref[0, :, sl].astype(F32) + oab_ref[0, :, sl].astype(F32)
        ya_ref[0, :, sl] = (_rms(o, na_ref[:, sl]) * _silu(ag_ref[0, :, sl])).astype(ya_ref.dtype)
    for h in range(B_HEADS):
        sl = slice(h * B_DV, (h + 1) * B_DV)
        o = obf_ref[0, :, sl].astype(F32) + obb_ref[0, :, sl].astype(F32)
        yb_ref[0, :, sl] = (_rms(o, nb_ref[:, sl]) * _sigmoid(bo_ref[0, :, sl])).astype(yb_ref.dtype)


def _readout(oa, ob, p3, norm_a, norm_b, n_lat):
    Bn = p3.shape[0]
    T = n_lat * TOK
    wa = A_HEADS * A_D
    wb = B_HEADS * B_DV
    ag_blk = 4
    bo_blk = (5 * A_HEADS + 4 * B_HEADS) * LANE // wb
    sa = pl.BlockSpec((1, TOK, wa), lambda b, i: (b, i, 0))
    sb = pl.BlockSpec((1, TOK, wb), lambda b, i: (b, i, 0))
    return pl.pallas_call(
        _readout_kernel,
        grid=(Bn, n_lat),
        in_specs=[sa, sa, sb, sb,
                  pl.BlockSpec((1, TOK, wa), lambda b, i: (b, i, ag_blk)),
                  pl.BlockSpec((1, TOK, wb), lambda b, i: (b, i, bo_blk)),
                  pl.BlockSpec((1, wa), lambda b, i: (0, 0)),
                  pl.BlockSpec((1, wb), lambda b, i: (0, 0))],
        out_specs=[sa, sb],
        out_shape=[jax.ShapeDtypeStruct((Bn, T, wa), BF16),
                   jax.ShapeDtypeStruct((Bn, T, wb), BF16)],
        compiler_params=_cparams(("parallel", "parallel")),
        name="readout",
    )(oa[0], oa[1], ob[0], ob[1], p3, p3, norm_a.reshape(1, wa), norm_b.reshape(1, wb))


def _merge_kernel(ya_ref, yb_ref, wa_ref, wb_ref, ga_ref, gb_ref, o_ref):
    pa = jnp.dot(ya_ref[0], wa_ref[...], preferred_element_type=F32)
    pb = jnp.dot(yb_ref[0], wb_ref[...], preferred_element_type=F32)
    o_ref[0] = (_sigmoid(ga_ref[0]) * pa + _sigmoid(gb_ref[0]) * pb).astype(o_ref.dtype)


def _merge(ya, yb, wa, wb, p3, D):
    Bn, T, ka = ya.shape
    kb = yb.shape[2]
    tm = _pick(T, 1024)
    tn = _pick(D, 512)
    ga0 = 0
    gb0 = D // tn
    return pl.pallas_call(
        _merge_kernel,
        grid=(Bn, T // tm, D // tn),
        in_specs=[pl.BlockSpec((1, tm, ka), lambda b, i, j: (b, i, 0)),
                  pl.BlockSpec((1, tm, kb), lambda b, i, j: (b, i, 0)),
                  pl.BlockSpec((ka, tn), lambda b, i, j: (0, j)),
                  pl.BlockSpec((kb, tn), lambda b, i, j: (0, j)),
                  pl.BlockSpec((1, tm, tn), lambda b, i, j: (b, i, ga0 + j)),
                  pl.BlockSpec((1, tm, tn), lambda b, i, j: (b, i, gb0 + j))],
        out_specs=pl.BlockSpec((1, tm, tn), lambda b, i, j: (b, i, j)),
        out_shape=jax.ShapeDtypeStruct((Bn, T, D), BF16),
        compiler_params=_cparams(("parallel", "parallel", "arbitrary")),
        name="merge",
    )(ya, yb, wa, wb, p3, p3)


def _resid_router_kernel(x_ref, mix_ref, g1_ref, g2_ref, gate_ref, sh_ref, sc_ref, wr_ref,
                         h_ref, v_ref, aff_ref):
    h = x_ref[0] + gate_ref[0] * _rms(mix_ref[0].astype(F32), g1_ref[...])
    h_ref[0] = h
    v = _rms(h, g2_ref[...]) * (1.0 + sc_ref[0]) + sh_ref[0]
    v_ref[0] = _pack_bf16_pairs(v)
    logits = jnp.dot(v, wr_ref[...], precision=lax.Precision.HIGHEST, preferred_element_type=F32)
    lane = lax.broadcasted_iota(jnp.int32, logits.shape, 1)
    logits = jnp.where(lane < N_EXPERTS, logits, -jnp.inf)
    e = jnp.exp(logits - jnp.max(logits, axis=-1, keepdims=True))
    aff_ref[0] = e / jnp.sum(e, axis=-1, keepdims=True)


def _resid_router(x, mix, g1, g2, mod3, w_router_pad):
    Bn, T, D = x.shape
    mspec = lambda k: pl.BlockSpec((1, 1, D), lambda b, i: (b, 0, k))
    tok = pl.BlockSpec((1, TOK, D), lambda b, i: (b, i, 0))
    vec = pl.BlockSpec((1, D), lambda b, i: (0, 0))
    return pl.pallas_call(
        _resid_router_kernel,
        grid=(Bn, T // TOK),
        in_specs=[tok, tok, vec, vec, mspec(2), mspec(3), mspec(4),
                  pl.BlockSpec((D, LANE), lambda b, i: (0, 0))],
        out_specs=[tok, pl.BlockSpec((1, TOK, D // 2), lambda b, i: (b, i, 0)),
                   pl.BlockSpec((1, TOK, LANE), lambda b, i: (b, i, 0))],
        out_shape=[jax.ShapeDtypeStruct((Bn, T, D), F32),
                   jax.ShapeDtypeStruct((Bn, T, D // 2), jnp.uint32),
                   jax.ShapeDtypeStruct((Bn, T, LANE), F32)],
        compiler_params=_cparams(("parallel", "parallel")),
        name="resid_router",
    )(x, mix, g1.reshape(1, D), g2.reshape(1, D), mod3, mod3, mod3, w_router_pad)


def _select_kernel(aff_ref, slot_ref, idx_ref, offs_ref, *, cap, tk):
    E, T = aff_ref.shape[1], aff_ref.shape[2]
    nck = T // LANE

    def key():
        return lax.bitcast_convert_type(aff_ref[0], jnp.int32)

    def bit_step(i, tau):
        cand = tau | jnp.left_shift(jnp.int32(1), 30 - i)
        cnt = jnp.sum(jnp.where(key() >= cand, 1.0, 0.0), axis=1, keepdims=True)
        return jnp.where(cnt >= cap, cand, tau)

    tau = lax.fori_loop(0, 31, bit_step, jnp.zeros((E, 1), jnp.int32))
    kk = key()
    gt = kk > tau
    eq = kk == tau
    need = cap - jnp.sum(jnp.where(gt, 1.0, 0.0), axis=1, keepdims=True)

    r = lax.broadcasted_iota(jnp.int32, (LANE, LANE), 0)
    c = lax.broadcasted_iota(jnp.int32, (LANE, LANE), 1)
    upper = jnp.where(r < c, 1.0, 0.0).astype(BF16)
    tr = jnp.right_shift(lax.broadcasted_iota(jnp.int32, (T, LANE), 0), LANE.bit_length() - 1)
    member = jnp.where(tr == lax.broadcasted_iota(jnp.int32, (T, LANE), 1), 1.0, 0.0).astype(BF16)

    def prefix(flags):
        x = jnp.where(flags, 1.0, 0.0).astype(BF16)
        tot = jnp.dot(x, member, preferred_element_type=F32)
        offs = jnp.dot(tot.astype(BF16), upper, preferred_element_type=F32)
        parts = [jnp.dot(x[:, j * LANE:(j + 1) * LANE], upper, preferred_element_type=F32)
                 + offs[:, j:j + 1] for j in range(nck)]
        return jnp.concatenate(parts, axis=1), offs

    tie_rank, _ = prefix(eq)
    sel = gt | (eq & (tie_rank < need))
    pos, offs = prefix(sel)
    slot = jnp.where(sel, pos, -1.0)
    slot_ref[0] = slot
    offs_ref[0] = offs.astype(jnp.int32)

    tt = lax.broadcasted_iota(jnp.int32, (8, T), 1)
    rr = lax.broadcasted_iota(jnp.int32, (8, T), 0)
    digits = jnp.where(rr == 0, jnp.right_shift(tt, 6), jnp.where(rr == 1, tt & 63, 0))
    digits = digits.astype(F32).astype(BF16)
    slot_iota = lax.broadcasted_iota(jnp.int32, (cap, tk), 0).astype(F32)
    for e in range(E):
        acc = jnp.zeros((8, cap), F32)
        for kc in range(T // tk):
            ks = slice(kc * tk, (kc + 1) * tk)
            oh = jnp.where(slot[e:e + 1, ks] == slot_iota, 1.0, 0.0).astype(BF16)
            acc = acc + lax.dot_general(digits[:, ks], oh, _NT, preferred_element_type=F32)
        idx_ref[0, e:e + 1, :] = (acc[0:1] * 64.0 + acc[1:2]).astype(jnp.int32)


def _select(aff_rows, cap):
    Bn, E, T = aff_rows.shape
    assert T // LANE < LANE and T <= 4096
    tk = _pick(T, 1024)
    row = lambda n: pl.BlockSpec((1, E, n), lambda b: (b, 0, 0))
    return pl.pallas_call(
        functools.partial(_select_kernel, cap=cap, tk=tk),
        grid=(Bn,),
        in_specs=[row(T)],
        out_specs=[row(T), row(cap), row(LANE)],
        out_shape=[jax.ShapeDtypeStruct((Bn, E, T), F32),
                   jax.ShapeDtypeStruct((Bn, E, cap), jnp.int32),
                   jax.ShapeDtypeStruct((Bn, E, LANE), jnp.int32)],
        compiler_params=_cparams(("parallel",)),
        name="ec_select",
    )(aff_rows)


def _pack_bf16_pairs(x):
    half = x.shape[1] // 2
    bits = lax.bitcast_convert_type(x.astype(BF16).astype(F32), jnp.uint32)
    return bits[:, half:] | (bits[:, :half] >> 16)


def _unpack_bf16_pairs(p):
    lo = lax.bitcast_convert_type(p << 16, F32).astype(BF16)
    hi = lax.bitcast_convert_type(p & jnp.uint32(0xFFFF0000), F32).astype(BF16)
    return lo, hi


def _gather_kernel(idx_ref, v_hbm, o_ref, buf, sem, *, cap, n_tok):
    b = pl.program_id(0)
    e = pl.program_id(1)
    base = (b * N_EXPERTS + e) * cap

    def row_copy(r, tok):
        return pltpu.make_async_copy(v_hbm.at[pl.ds(b * n_tok + tok, 1), :], buf.at[pl.ds(r, 1), :], sem)

    def issue(r, carry):
        row_copy(r, idx_ref[base + r]).start()
        return carry

    def drain(r, carry):
        row_copy(r, 0).wait()
        return carry

    lax.fori_loop(0, cap, issue, 0)
    lax.fori_loop(0, cap, drain, 0)
    half = buf.shape[1]
    lo, hi = _unpack_bf16_pairs(buf[...])
    o_ref[0, 0, :, :half] = lo
    o_ref[0, 0, :, half:] = hi


def _gather(idx, v_pairs, cap):
    Bn, T, half = v_pairs.shape
    D = 2 * half
    return pl.pallas_call(
        functools.partial(_gather_kernel, cap=cap, n_tok=T),
        grid_spec=pltpu.PrefetchScalarGridSpec(
            num_scalar_prefetch=1,
            grid=(Bn, N_EXPERTS),
            in_specs=[pl.BlockSpec(memory_space=pl.ANY)],
            out_specs=pl.BlockSpec((1, 1, cap, D), lambda b, e, idx_ref: (e, b, 0, 0)),
            scratch_shapes=[pltpu.VMEM((cap, half), jnp.uint32), pltpu.SemaphoreType.DMA(())]),
        out_shape=jax.ShapeDtypeStruct((N_EXPERTS, Bn, cap, D), BF16),
        compiler_params=_cparams(("arbitrary", "arbitrary")),
        name="ec_gather",
    )(idx.reshape(-1), v_pairs.reshape(Bn * T, half))


def _ffn1_kernel(x_ref, wg_ref, wu_ref, o_ref):
    x = x_ref[0]
    g = jnp.dot(x, wg_ref[0].astype(BF16), preferred_element_type=F32)
    u = jnp.dot(x, wu_ref[0].astype(BF16), preferred_element_type=F32)
    o_ref[0] = (_silu(g) * u).astype(o_ref.dtype)


def _ffn1(xg, wg, wu):
    E, M, D = xg.shape
    F = wg.shape[2]
    tm = _pick(M, 1024)
    tn = _pick(F, 256)
    return pl.pallas_call(
        _ffn1_kernel,
        grid=(E, M // tm, F // tn),
        in_specs=[pl.BlockSpec((1, tm, D), lambda e, i, j: (e, i, 0)),
                  pl.BlockSpec((1, D, tn), lambda e, i, j: (e, 0, j)),
                  pl.BlockSpec((1, D, tn), lambda e, i, j: (e, 0, j))],
        out_specs=pl.BlockSpec((1, tm, tn), lambda e, i, j: (e, i, j)),
        out_shape=jax.ShapeDtypeStruct((E, M, F), BF16),
        compiler_params=_cparams(("parallel", "parallel", "arbitrary")),
        name="ec_ffn_up",
    )(xg, wg, wu)


def _ffn2_kernel(h_ref, wd_ref, o_ref):
    o_ref[0] = jnp.dot(h_ref[0], wd_ref[0].astype(BF16),
                       preferred_element_type=F32).astype(o_ref.dtype)


def _ffn2(hid, wd):
    E, M, F = hid.shape
    D = wd.shape[2]
    tn = _pick(D, 512)
    return pl.pallas_call(
        _ffn2_kernel,
        grid=(E, D // tn),
        in_specs=[pl.BlockSpec((1, M, F), lambda e, j: (e, 0, 0)),
                  pl.BlockSpec((1, F, tn), lambda e, j: (e, 0, j))],
        out_specs=pl.BlockSpec((1, M, tn), lambda e, j: (e, 0, j)),
        out_shape=jax.ShapeDtypeStruct((E, M, D), BF16),
        compiler_params=_cparams(("parallel", "arbitrary")),
        name="ec_ffn_down",
    )(hid, wd)


def _scatter_kernel(offs_ref, slot_ref, aff_ref, y_ref, h_ref, g_ref, gate_ref, o_ref, *, cap, win):
    b = pl.program_id(0)
    t = pl.program_id(1)
    e = pl.program_id(2)
    tq = o_ref.shape[1]

    @pl.when(e == 0)
    def _():
        o_ref[...] = jnp.zeros_like(o_ref)

    base = (b * N_EXPERTS + e) * LANE
    lo = offs_ref[base + t * (tq // LANE)]
    hi = offs_ref[base + (t + 1) * (tq // LANE)]
    sl = jnp.broadcast_to(slot_ref[0, 0], (LANE, tq)).T
    af = jnp.broadcast_to(aff_ref[0, 0], (LANE, tq)).T
    sl = jnp.concatenate([sl] * (win // LANE), axis=1)
    af = jnp.concatenate([af] * (win // LANE), axis=1)
    lane = lax.broadcasted_iota(jnp.int32, (tq, win), 1).astype(F32)
    for w0 in range(0, cap, win):
        @pl.when((lo < w0 + win) & (hi > w0))
        def _():
            ohw = jnp.where(sl == lane + float(w0), af, 0.0).astype(BF16)
            o_ref[0] += jnp.dot(ohw, y_ref[0, 0, w0:w0 + win, :], preferred_element_type=F32)

    @pl.when(e == pl.num_programs(2) - 1)
    def _():
        o_ref[0] = h_ref[0] + gate_ref[0] * _rms(o_ref[0], g_ref[...])


def _scatter_combine(offs, slot_rows, aff_rows, y, cap, h, g3, mod3):
    E, Bn, _, D = y.shape
    T = slot_rows.shape[3]
    tq = _pick(T, 512)
    win = _pick(cap, 256)
    assert win % LANE == 0 and tq % LANE == 0
    row = pl.BlockSpec((1, 1, 1, tq), lambda b, t, e, offs_ref: (b, e, 0, t))
    return pl.pallas_call(
        functools.partial(_scatter_kernel, cap=cap, win=win),
        grid_spec=pltpu.PrefetchScalarGridSpec(
            num_scalar_prefetch=1,
            grid=(Bn, T // tq, E),
            in_specs=[row, row,
                      pl.BlockSpec((1, 1, cap, D), lambda b, t, e, offs_ref: (e, b, 0, 0)),
                      pl.BlockSpec((1, tq, D), lambda b, t, e, offs_ref: (b, t, 0)),
                      pl.BlockSpec((1, D), lambda b, t, e, offs_ref: (0, 0)),
                      pl.BlockSpec((1, 1, D), lambda b, t, e, offs_ref: (b, 0, 5))],
            out_specs=pl.BlockSpec((1, tq, D), lambda b, t, e, offs_ref: (b, t, 0))),
        out_shape=jax.ShapeDtypeStruct((Bn, T, D), F32),
        compiler_params=_cparams(("parallel", "parallel", "arbitrary")),
        name="ec_scatter",
    )(offs.reshape(-1), slot_rows, aff_rows, y, h, g3.reshape(1, D), mod3)


def _layer(h_lat, ctx, c8, l, lb_l, w_ada, b_ada, g_norm, w_in, hgrn_f_bias, hgrn_norm,
           mlstm_conv_w, mlstm_conv_b, mlstm_gate_b, mlstm_norm, w_branch_a, w_branch_b, w_out,
           w_router, w_expert_gate, w_expert_up, w_expert_down):
    Bn, T, D = h_lat.shape
    n_lat = T // TOK
    n_tok = T + TOK
    A, Bh = A_HEADS, B_HEADS
    g0 = (5 * A + 6 * Bh) * LANE

    mod = _modulation(c8, w_ada[l], b_ada[l])
    mod3 = mod.reshape(8, 1, N_MOD * D)

    u = _prenorm(h_lat, ctx, g_norm[l, 0], mod3)

    w_in_t = jnp.swapaxes(w_in, 1, 2)
    w_merge_t, w_gate_t = _wsplit(w_in_t, l, g0, 4 * Bh, 2 * D)
    gate_bias = jnp.pad(mlstm_gate_b[l].reshape(4 * Bh), (0, LANE - 4 * Bh))
    u2 = u.reshape(Bn * n_tok, D)
    p3 = _matmul_t(u2, w_in_t, F32, n_cols=g0, layer=l, name="in_proj").reshape(Bn, n_tok, g0)
    p_merge = _matmul_t(u2, w_merge_t, F32, name="in_proj_merge").reshape(Bn, n_tok, 2 * D)
    gates = _matmul_t(u2, w_gate_t, F32, tn=LANE, bias=gate_bias, name="in_proj_gates")

    g6 = gates[:, :4 * Bh].reshape(Bn, n_tok // CHUNK, CHUNK, 2, 2, Bh)
    g_rows = g6.transpose(0, 3, 5, 1, 4, 2)
    g_cols = g6.transpose(0, 3, 5, 1, 2, 4)
    conv4 = jnp.concatenate([mlstm_conv_w[l], mlstm_conv_b[l][None]], axis=0)
    conv4 = conv4.reshape(4, 1, 2 * Bh * B_DQK)

    oa = _hgrn_scan(p3, hgrn_f_bias[l], lb_l, n_lat)
    ob = _mlstm_scan(p3, g_rows, g_cols, conv4, n_lat)
    ya, yb = _readout(oa, ob, p3, hgrn_norm[l], mlstm_norm[l], n_lat)
    merged = _merge(ya, yb, w_branch_a[l].astype(BF16), w_branch_b[l].astype(BF16), p_merge, D)
    mix = _matmul(merged.reshape(Bn * T, D), w_out[l].astype(BF16), BF16, name="out_proj")

    w_router_pad = jnp.pad(w_router[l], ((0, 0), (0, LANE - N_EXPERTS)))
    h_lat, v_lat, aff = _resid_router(h_lat, mix.reshape(Bn, T, D), g_norm[l, 1], g_norm[l, 2],
                                      mod3, w_router_pad)

    cap = CAPACITY * T // N_EXPERTS
    aff_rows = aff[:, :, :N_EXPERTS].transpose(0, 2, 1)
    slot, idx, offs = _select(aff_rows, cap)
    xg = _gather(idx, v_lat, cap)
    hid = _ffn1(xg.reshape(N_EXPERTS, Bn * cap, D), w_expert_gate[l], w_expert_up[l])
    y = _ffn2(hid, w_expert_down[l]).reshape(N_EXPERTS, Bn, cap, D)
    return _scatter_combine(offs, slot.reshape(Bn, N_EXPERTS, 1, T),
                            aff_rows.reshape(Bn, N_EXPERTS, 1, T), y, cap, h_lat, g_norm[l, 3], mod3)


def kernel(x, c, ctx, c_ctx, w_ada, b_ada, g_norm, w_in, hgrn_f_bias, hgrn_lb, hgrn_norm,
           mlstm_conv_w, mlstm_conv_b, mlstm_gate_b, mlstm_norm, w_branch_a, w_branch_b, w_out,
           w_router, w_expert_gate, w_expert_up, w_expert_down):
    Bn, T, D = x.shape
    depth = w_ada.shape[0]
    assert depth == 1, "context outputs are only produced for the state hand-off (single layer)"
    assert ctx.shape[1] == TOK and T % TOK == 0 and Bn < 8
    lb_all = jnp.cumsum(jax.nn.softmax(hgrn_lb.astype(F32), axis=1), axis=1)
    c8 = jnp.zeros((8, D), F32).at[:Bn].set(c).at[Bn].set(c_ctx)
    h_lat = x
    for l in range(depth):
        h_lat = _layer(h_lat, ctx, c8, l, lb_all[:, l], w_ada, b_ada, g_norm, w_in, hgrn_f_bias,
                       hgrn_norm, mlstm_conv_w, mlstm_conv_b, mlstm_gate_b, mlstm_norm,
                       w_branch_a, w_branch_b, w_out, w_router, w_expert_gate, w_expert_up,
                       w_expert_down)
    return h_lat.astype(x.dtype)
```

```python
import functools

import jax
import jax.numpy as jnp
from jax import lax
from jax.experimental import pallas as pl
from jax.experimental.pallas import tpu as pltpu

F32 = jnp.float32
BF16 = jnp.bfloat16

EPS = 1e-6
N_MOD = 6
A_HEADS = 16
A_D = 128
B_HEADS = 8
B_DQK = 128
B_DV = 256
N_EXPERTS = 16
CAPACITY = 2
CHUNK = 64
SUB = 16
A_HPS = 4
B_HPS = 4
TOK = 256
NCH = TOK // CHUNK
NSUB = CHUNK // SUB
LANE = 128
VMEM_LIMIT = 56 * 1024 * 1024

_NT = (((1,), (1,)), ((), ()))


def _pick(n, pref):
    t = min(n, pref)
    while n % t:
        t //= 2
    return t


def _cparams(sem):
    return pltpu.CompilerParams(dimension_semantics=sem, vmem_limit_bytes=VMEM_LIMIT)


def _silu(x):
    return x / (1.0 + jnp.exp(-x))


def _sigmoid(x):
    return 1.0 / (1.0 + jnp.exp(-x))


def _sig_pair(z):
    t = jnp.exp(-jnp.abs(z))
    r = 1.0 / (1.0 + t)
    tr = t * r
    pos = z >= 0
    return jnp.where(pos, r, tr), jnp.where(pos, tr, r)


def _log_sigmoid(x):
    return jnp.minimum(x, 0.0) - jnp.log(1.0 + jnp.exp(-jnp.abs(x)))


def _rms(xf, w):
    return xf * lax.rsqrt(jnp.mean(xf * xf, axis=-1, keepdims=True) + EPS) * w


def _rows(x, r, n):
    return jnp.broadcast_to(x[r:r + 1], (n, x.shape[1]))


def _mod_kernel(c_ref, w_ref, b_ref, o_ref):
    a = _silu(c_ref[...]).astype(BF16)
    o_ref[...] = jnp.dot(a, w_ref[...].astype(BF16), preferred_element_type=F32) + b_ref[...]


def _modulation(c8, w, b):
    D, N = w.shape
    tn = _pick(N, 512)
    return pl.pallas_call(
        _mod_kernel,
        grid=(N // tn,),
        in_specs=[pl.BlockSpec((8, D), lambda j: (0, 0)),
                  pl.BlockSpec((D, tn), lambda j: (0, j)),
                  pl.BlockSpec((1, tn), lambda j: (0, j))],
        out_specs=pl.BlockSpec((8, tn), lambda j: (0, j)),
        out_shape=jax.ShapeDtypeStruct((8, N), F32),
        compiler_params=_cparams(("parallel",)),
        name="adaln_mod",
    )(c8, w, b.reshape(1, N))


def _prenorm_kernel(x_ref, ctx_ref, g_ref, sh_ref, sc_ref, o_ref, *, n_lat):
    i = pl.program_id(1)
    g = g_ref[...]
    sh = sh_ref[0]
    sc = sc_ref[0]

    @pl.when(i < n_lat)
    def _():
        o_ref[0] = (_rms(x_ref[0], g) * (1.0 + sc) + sh).astype(o_ref.dtype)

    @pl.when(i == n_lat)
    def _():
        o_ref[0] = (_rms(ctx_ref[0], g) * (1.0 + sc) + sh).astype(o_ref.dtype)


def _prenorm(x, ctx, g, mod3):
    Bn, T, D = x.shape
    n_lat = T // TOK
    row = lambda b, i: jnp.where(i == n_lat, Bn, b)
    return pl.pallas_call(
        functools.partial(_prenorm_kernel, n_lat=n_lat),
        grid=(Bn, n_lat + 1),
        in_specs=[pl.BlockSpec((1, TOK, D), lambda b, i: (b, jnp.minimum(i, n_lat - 1), 0)),
                  pl.BlockSpec((1, TOK, D), lambda b, i: (b, 0, 0)),
                  pl.BlockSpec((1, D), lambda b, i: (0, 0)),
                  pl.BlockSpec((1, 1, D), lambda b, i: (row(b, i), 0, 0)),
                  pl.BlockSpec((1, 1, D), lambda b, i: (row(b, i), 0, 1))],
        out_specs=pl.BlockSpec((1, TOK, D), lambda b, i: (b, i, 0)),
        out_shape=jax.ShapeDtypeStruct((Bn, T + TOK, D), BF16),
        compiler_params=_cparams(("parallel", "arbitrary")),
        name="prenorm",
    )(x, ctx, g.reshape(1, D), mod3, mod3)


_NN = (((1,), (0,)), ((), ()))


def _mm_kernel(a_ref, b_ref, o_ref, *, dims):
    o_ref[...] = lax.dot_general(a_ref[...], b_ref[...].astype(BF16), dims,
                                 preferred_element_type=F32).astype(o_ref.dtype)


def _mm_bias_kernel(a_ref, b_ref, bias_ref, o_ref, *, dims):
    o_ref[...] = (lax.dot_general(a_ref[...], b_ref[...].astype(BF16), dims, preferred_element_type=F32)
                  + bias_ref[...]).astype(o_ref.dtype)


def _matmul(a, b, out_dtype, tm=1024, tn=512, bias=None, name="matmul"):
    M, K = a.shape
    N = b.shape[1]
    tm = _pick(M, tm)
    tn = _pick(N, tn)
    return _matmul_call(a, b, pl.BlockSpec((K, tn), lambda i, j: (0, j)), _NN, N, out_dtype, tm, tn,
                        bias, name)


def _matmul_t(a, bt, out_dtype, tm=1024, tn=512, bias=None, n_cols=None, layer=None, name="matmul_t"):
    M, K = a.shape
    N = bt.shape[-2] if n_cols is None else n_cols
    tm = _pick(M, tm)
    tn = _pick(N, tn)
    if layer is None:
        b_spec = pl.BlockSpec((tn, K), lambda i, j: (j, 0))
    else:
        b_spec = pl.BlockSpec((None, tn, K), lambda i, j: (layer, j, 0))
    return _matmul_call(a, bt, b_spec, _NT, N, out_dtype, tm, tn, bias, name)


def _matmul_call(a, b, b_spec, dims, N, out_dtype, tm, tn, bias, name):
    M, K = a.shape
    in_specs = [pl.BlockSpec((tm, K), lambda i, j: (i, 0)), b_spec]
    args = [a, b]
    kern = functools.partial(_mm_kernel, dims=dims)
    if bias is not None:
        in_specs.append(pl.BlockSpec((1, tn), lambda i, j: (0, j)))
        args.append(bias.reshape(1, N))
        kern = functools.partial(_mm_bias_kernel, dims=dims)
    return pl.pallas_call(
        kern,
        grid=(M // tm, N // tn),
        in_specs=in_specs,
        out_specs=pl.BlockSpec((tm, tn), lambda i, j: (i, j)),
        out_shape=jax.ShapeDtypeStruct((M, N), out_dtype),
        compiler_params=_cparams(("parallel", "arbitrary")),
        name=name,
    )(*args)


def _wsplit_kernel(a_ref, b_ref, o_ref, g_ref, *, shift):
    j = pl.program_id(1)
    tn = o_ref.shape[0]
    cat = jnp.concatenate([a_ref[...], b_ref[...]], axis=0)
    o_ref[...] = cat[shift:shift + tn].astype(o_ref.dtype)

    @pl.when(j == 0)
    def _():
        row = lax.broadcasted_iota(jnp.int32, g_ref.shape, 0)
        g_ref[...] = jnp.where(row < shift, cat[:LANE], 0.0).astype(g_ref.dtype)


def _wsplit(wt, layer, row0, shift, n):
    K = wt.shape[2]
    tk = _pick(K, 1024)
    tn = _pick(n, 512)
    assert row0 % tn == 0 and 0 < shift < LANE and shift % 8 == 0 and row0 + shift + n <= wt.shape[1]
    r0 = row0 // tn
    return pl.pallas_call(
        functools.partial(_wsplit_kernel, shift=shift),
        grid=(K // tk, n // tn),
        in_specs=[pl.BlockSpec((None, tn, tk), lambda i, j: (layer, r0 + j, i)),
                  pl.BlockSpec((None, tn, tk), lambda i, j: (layer, r0 + j + 1, i))],
        out_specs=[pl.BlockSpec((tn, tk), lambda i, j: (j, i)),
                   pl.BlockSpec((LANE, tk), lambda i, j: (0, i))],
        out_shape=[jax.ShapeDtypeStruct((n, K), BF16), jax.ShapeDtypeStruct((LANE, K), BF16)],
        compiler_params=_cparams(("parallel", "arbitrary")),
        name="w_split",
    )(wt, wt)


def _tok_block(rev, j, n_lat):
    lat = n_lat - j if rev else j - 1
    return jnp.where(j == 0, n_lat, lat)


def _out_block(rev, j, n_lat):
    return jnp.clip(n_lat - j if rev else j - 1, 0, n_lat - 1)


def _chunk_masks(rev):
    row = lax.broadcasted_iota(jnp.int32, (CHUNK, CHUNK), 0)
    col = lax.broadcasted_iota(jnp.int32, (CHUNK, CHUNK), 1)
    return (col >= row, col <= row) if rev else (col <= row, col >= row)


def _hgrn_consts():
    t = jnp.arange(TOK)
    same = (t[:, None] // CHUNK) == (t[None, :] // CHUNK)
    sub_r, sub_c = t[:, None] // SUB, t[None, :] // SUB
    fwd = jnp.concatenate([same & (t[None, :] <= t[:, None]), same & (sub_c < sub_r)], axis=0)
    bwd = jnp.concatenate([same & (t[None, :] >= t[:, None]), same & (sub_c > sub_r)], axis=0)
    masks = jnp.stack([fwd, bwd]).astype(BF16)
    r = jnp.arange(SUB * A_D)
    emat = ((r[:, None] // A_D) == (jnp.arange(LANE)[None, :] % SUB)).astype(BF16)
    return masks, emat


def _hgrn_state(af, v, fb, lb, mask_ref, st_ref, rev):
    z = af + fb
    sp, sn = _sig_pair(z)
    k = (1.0 - lb) * sn
    lf2 = jnp.log2(lb + (1.0 - lb) * sp)
    lk2 = jnp.log2(k)

    hi = lf2.astype(BF16)
    r1 = lf2 - hi.astype(F32)
    mid = r1.astype(BF16)
    lo = (r1 - mid.astype(F32)).astype(BF16)
    cs = jnp.dot(mask_ref[0], jnp.concatenate([hi, mid, lo], axis=1), preferred_element_type=F32)
    cs = cs[:, :A_D] + cs[:, A_D:2 * A_D] + cs[:, 2 * A_D:]
    b = cs[:TOK]
    ent = cs[TOK:]
    c_all = lk2 - b

    last = [(c * CHUNK if rev else c * CHUNK + CHUNK - 1) for c in range(NCH)]
    tot_rows = jnp.concatenate([_rows(b, last[c], CHUNK) for c in range(NCH)], axis=0)
    khat = jnp.exp2(jnp.minimum(tot_rows + c_all, lk2)).astype(BF16)
    upd = [jnp.dot(v[c * CHUNK:(c + 1) * CHUNK].T.astype(BF16), khat[c * CHUNK:(c + 1) * CHUNK],
                   preferred_element_type=F32) for c in range(NCH)]

    order = range(NCH - 1, -1, -1) if rev else range(NCH)
    st = st_ref[...]
    st_in = [None] * NCH
    for c in order:
        st_in[c] = st
        st = st * jnp.exp2(b[last[c]:last[c] + 1]) + upd[c]
    st_ref[...] = st
    return v, b, ent, c_all, lk2, st_in


def _hgrn_output(aq, feats, emat_ref, o_ref, hs, rev):
    v, b, ent, c_all, lk2, st_in = feats
    q = _silu(aq)
    qb = q.astype(BF16)
    vb = v.astype(BF16)
    qt = (q * jnp.exp2(b - ent)).astype(BF16)
    row = lax.broadcasted_iota(jnp.int32, (CHUNK, CHUNK), 0)
    col = lax.broadcasted_iota(jnp.int32, (CHUNK, CHUNK), 1)
    rs = jnp.right_shift(row, SUB.bit_length() - 1)
    cs_ = jnp.right_shift(col, SUB.bit_length() - 1)
    prev_blk = (cs_ > rs) if rev else (cs_ < rs)
    diag_blk = (rs == cs_) & ((col >= row) if rev else (col <= row))

    off = [[None] * NSUB for _ in range(NCH)]
    for i in range(NSUB):
        e_i = jnp.concatenate([_rows(ent, c * CHUNK + i * SUB, CHUNK) for c in range(NCH)], axis=0)
        kt = jnp.exp2(jnp.minimum(e_i + c_all, lk2)).astype(BF16)
        for c in range(NCH):
            r0 = c * CHUNK + i * SUB
            off[c][i] = lax.dot_general(qt[r0:r0 + SUB], kt[c * CHUNK:(c + 1) * CHUNK], _NT,
                                        preferred_element_type=F32)

    half = SUB // 2
    nsb = TOK // SUB
    zeros = jnp.zeros((half, A_D), F32)
    ps = []
    for sl in range(SUB):
        n = half if (sl < half if rev else sl >= half) else SUB
        r0 = SUB - n if not rev else 0
        bq = b if n == SUB else jnp.concatenate(
            [b[g * SUB + r0:g * SUB + r0 + n] for g in range(nsb)], axis=0)
        cref = jnp.concatenate([_rows(c_all, g * SUB + sl, n) for g in range(nsb)], axis=0)
        kref = jnp.concatenate([_rows(lk2, g * SUB + sl, n) for g in range(nsb)], axis=0)
        e = jnp.exp2(jnp.minimum(bq + cref, kref))
        if n != SUB:
            parts = []
            for g in range(nsb):
                piece = e[g * half:(g + 1) * half]
                parts += [piece, zeros] if rev else [zeros, piece]
            e = jnp.concatenate(parts, axis=0)
        ps.append(e.astype(BF16) * qb)
    acc = jnp.dot(jnp.concatenate(ps, axis=1), emat_ref[...], preferred_element_type=F32)

    qhat = (q * jnp.exp2(b)).astype(BF16)
    for c in range(NCH):
        sl_c = slice(c * CHUNK, (c + 1) * CHUNK)
        attn = (jnp.where(prev_blk, jnp.concatenate(off[c], axis=0), 0.0)
                + jnp.where(diag_blk, acc[sl_c, :CHUNK], 0.0))
        o = jnp.dot(attn.astype(BF16), vb[sl_c], preferred_element_type=F32)
        o = o + lax.dot_general(qhat[sl_c], st_in[c].astype(BF16), _NT, preferred_element_type=F32)
        o_ref[0, sl_c, hs] = o.astype(o_ref.dtype)


def _hgrn_kernel(aqf, aff, aif, aqb, afb, aib, fb_ref, lb_ref, mask_ref, emat_ref, of_ref, ob_ref,
                 stf_ref, stb_ref):
    j = pl.program_id(2)

    @pl.when(j == 0)
    def _():
        stf_ref[...] = jnp.zeros_like(stf_ref)
        stb_ref[...] = jnp.zeros_like(stb_ref)

    feats = []
    for hh in range(A_HPS):
        hs = slice(hh * A_D, (hh + 1) * A_D)
        feats.append((hs,
                      _hgrn_state(aff[0, :, hs], aif[0, :, hs], fb_ref[0][:, hs], lb_ref[0][:, hs],
                                  mask_ref.at[0:1], stf_ref.at[hh], False),
                      _hgrn_state(afb[0, :, hs], aib[0, :, hs], fb_ref[1][:, hs], lb_ref[1][:, hs],
                                  mask_ref.at[1:2], stb_ref.at[hh], True)))
    for hs, feats_f, feats_b in feats:
        _hgrn_output(aqf[0, :, hs], feats_f, emat_ref, of_ref, hs, False)
        _hgrn_output(aqb[0, :, hs], feats_b, emat_ref, ob_ref, hs, True)


def _hgrn_scan(p3, f_bias, lb, n_lat):
    Bn = p3.shape[0]
    A = A_HEADS
    masks, emat = _hgrn_consts()

    W = A_HPS * A_D
    G = A // A_HPS

    def feat(rev, grp):
        return pl.BlockSpec((1, TOK, W), lambda b, h, j: (b, _tok_block(rev, j, n_lat), grp * G + h))

    def outp(rev):
        return pl.BlockSpec((1, TOK, W), lambda b, h, j: (b, _out_block(rev, j, n_lat), h))

    par = pl.BlockSpec((2, 1, W), lambda b, h, j: (0, 0, h))
    osh = jax.ShapeDtypeStruct((Bn, n_lat * TOK, A * A_D), BF16)
    return pl.pallas_call(
        _hgrn_kernel,
        grid=(Bn, G, n_lat + 1),
        in_specs=[feat(False, 0), feat(False, 1), feat(False, 3),
                  feat(True, 0), feat(True, 2), feat(True, 3),
                  par, par,
                  pl.BlockSpec((2, 2 * TOK, TOK), lambda b, h, j: (0, 0, 0)),
                  pl.BlockSpec((SUB * A_D, LANE), lambda b, h, j: (0, 0))],
        out_specs=[outp(False), outp(True)],
        out_shape=[osh, osh],
        scratch_shapes=[pltpu.VMEM((A_HPS, A_D, A_D), F32), pltpu.VMEM((A_HPS, A_D, A_D), F32)],
        compiler_params=_cparams(("parallel", "parallel", "arbitrary")),
        name="hgrn2_scan",
    )(p3, p3, p3, p3, p3, p3, f_bias.reshape(2, 1, A * A_D), lb.reshape(2, 1, A * A_D), masks, emat)


def _mlstm_state(bq_ref, bk_ref, bv_ref, gr_ref, gc_ref, cwq, cwk, first, last_, c_ref, n_ref, m_ref,
                 hh, rev):
    def conv(u, cw):
        up = jnp.where(first, 0.0, pltpu.roll(u, 1, 0))
        dn = jnp.where(last_, 0.0, pltpu.roll(u, TOK - 1, 0))
        return _silu(cw[3:4] + up * cw[0:1] + u * cw[1:2] + dn * cw[2:3])

    hs = slice(hh * B_DQK, (hh + 1) * B_DQK)
    vs = slice(hh * B_DV, (hh + 1) * B_DV)
    c_ref, n_ref, m_ref = c_ref.at[hh], n_ref.at[hh], m_ref.at[hh]
    q_all = conv(bq_ref[0, :, hs], cwq[:, hs])
    k_all = conv(bk_ref[0, :, hs], cwk[:, hs]) * (B_DQK ** -0.5)
    seen, seen_t = _chunk_masks(rev)

    pre = []
    for c in range(NCH):
        sl_c = slice(c * CHUNK, (c + 1) * CHUNK)
        k = k_all[sl_c]
        vb = bv_ref[0, sl_c, vs].astype(BF16)
        g_r = gr_ref[0, 0, hh, c]
        g_c = gc_ref[0, 0, hh, c]
        ii_r = g_r[0:1]
        lf_r = _log_sigmoid(g_r[1:2])
        ii_c = g_c[:, 0:1]
        lf_c = _log_sigmoid(g_c[:, 1:2])
        b_c = jnp.sum(jnp.where(seen, lf_r, 0.0), axis=1, keepdims=True)
        b_r = jnp.sum(jnp.where(seen_t, lf_c, 0.0), axis=0, keepdims=True)
        total = jnp.sum(lf_r, axis=1, keepdims=True)
        logs = total - b_c + ii_c
        ms = jnp.max(logs, axis=0, keepdims=True)
        kw = k * jnp.exp(logs - ms)
        upd = jnp.dot(kw.T.astype(BF16), vb, preferred_element_type=F32)
        nupd = jnp.sum(kw, axis=0, keepdims=True)
        pre.append((vb, ii_r, b_c, b_r, total, ms, upd, nupd))

    cmat, nvec, m = c_ref[...], n_ref[...], m_ref[:, 0:1]
    st_in = [None] * NCH
    for c in (range(NCH - 1, -1, -1) if rev else range(NCH)):
        _, _, _, _, total, ms, upd, nupd = pre[c]
        st_in[c] = (cmat, nvec, m)
        m_new = jnp.maximum(total + m, ms)
        dec = jnp.exp(total + m - m_new)
        sc = jnp.exp(ms - m_new)
        cmat = dec * cmat + sc * upd
        nvec = dec * nvec + sc * nupd
        m = m_new
    c_ref[...] = cmat
    n_ref[...] = nvec
    m_ref[...] = jnp.broadcast_to(m, m_ref.shape)
    return q_all, k_all, pre, st_in


def _mlstm_output(feats, o_ref, hh, rev):
    q_all, k_all, pre, st_in = feats
    seen, _ = _chunk_masks(rev)
    for c in range(NCH):
        sl_c = slice(c * CHUNK, (c + 1) * CHUNK)
        vb, ii_r, b_c, b_r, _, _, _, _ = pre[c]
        cm, nv, m0 = st_in[c]
        q = q_all[sl_c]
        qb = q.astype(BF16)
        kb = k_all[sl_c].astype(BF16)
        logw = jnp.where(seen, b_c - b_r + ii_r, -jnp.inf)
        mw = jnp.max(logw, axis=1, keepdims=True)
        qk = lax.dot_general(qb, kb, _NT, preferred_element_type=F32) * jnp.exp(logw - mw)
        num0 = jnp.dot(qk.astype(BF16), vb, preferred_element_type=F32)
        den0 = jnp.sum(qk, axis=1, keepdims=True)
        log_inter = b_c + m0
        m_t = jnp.maximum(mw, log_inter)
        r = jnp.exp(mw - m_t)
        a = jnp.exp(log_inter - m_t)
        num = r * num0 + a * jnp.dot(qb, cm.astype(BF16), preferred_element_type=F32)
        den = r * den0 + a * jnp.sum(q * nv, axis=1, keepdims=True)
        h = num / jnp.maximum(jnp.abs(den), jnp.exp(-m_t))
        o_ref[0, sl_c, hh * B_DV:(hh + 1) * B_DV] = h.astype(o_ref.dtype)


def _mlstm_kernel(bqf, bkf, bvf, grf, gcf, bqb, bkb, bvb, grb, gcb, cwq_ref, cwk_ref, of_ref, ob_ref,
                  cf_ref, nf_ref, mf_ref, cb_ref, nb_ref, mb_ref):
    j = pl.program_id(2)

    @pl.when(j == 0)
    def _():
        for ref in (cf_ref, nf_ref, mf_ref, cb_ref, nb_ref, mb_ref):
            ref[...] = jnp.zeros_like(ref)

    t = lax.broadcasted_iota(jnp.int32, (TOK, 1), 0)
    row_mask = jnp.where(j == 0, TOK - 1, CHUNK - 1)
    pos = t & row_mask
    first = pos == 0
    last_ = pos == row_mask
    cwq = cwq_ref[:, 0, :]
    cwk = cwk_ref[:, 0, :]
    feats = []
    for hh in range(B_HPS):
        feats.append((_mlstm_state(bqf, bkf, bvf, grf, gcf, cwq, cwk, first, last_, cf_ref, nf_ref, mf_ref,
                                   hh, False),
                      _mlstm_state(bqb, bkb, bvb, grb, gcb, cwq, cwk, first, last_, cb_ref, nb_ref, mb_ref,
                                   hh, True)))
    for hh, (feats_f, feats_b) in enumerate(feats):
        _mlstm_output(feats_f, of_ref, hh, False)
        _mlstm_output(feats_b, ob_ref, hh, True)


def _mlstm_scan(p3, g_rows, g_cols, conv4, n_lat):
    Bn = p3.shape[0]
    A, Bh = A_HEADS, B_HEADS
    P = B_HPS
    G = Bh // P
    q0 = 5 * A // P
    k0 = (5 * A + Bh) // P
    v0 = (5 * A + 2 * Bh) // (2 * P)

    def specs(rev):
        d = int(rev)
        tb = lambda j: _tok_block(rev, j, n_lat)
        return [pl.BlockSpec((1, TOK, P * B_DQK), lambda b, h, j: (b, tb(j), q0 + h)),
                pl.BlockSpec((1, TOK, P * B_DQK), lambda b, h, j: (b, tb(j), k0 + h)),
                pl.BlockSpec((1, TOK, P * B_DV), lambda b, h, j: (b, tb(j), v0 + h)),
                pl.BlockSpec((1, 1, P, NCH, 2, CHUNK), lambda b, h, j: (b, d, h, tb(j), 0, 0)),
                pl.BlockSpec((1, 1, P, NCH, CHUNK, 2), lambda b, h, j: (b, d, h, tb(j), 0, 0))]

    def outp(rev):
        return pl.BlockSpec((1, TOK, P * B_DV), lambda b, h, j: (b, _out_block(rev, j, n_lat), h))

    osh = jax.ShapeDtypeStruct((Bn, n_lat * TOK, Bh * B_DV), BF16)
    state = [pltpu.VMEM((P, B_DQK, B_DV), F32), pltpu.VMEM((P, 1, B_DQK), F32),
             pltpu.VMEM((P, 1, LANE), F32)]
    return pl.pallas_call(
        _mlstm_kernel,
        grid=(Bn, G, n_lat + 1),
        in_specs=specs(False) + specs(True) + [
            pl.BlockSpec((4, 1, P * B_DQK), lambda b, h, j: (0, 0, h)),
            pl.BlockSpec((4, 1, P * B_DQK), lambda b, h, j: (0, 0, G + h))],
        out_specs=[outp(False), outp(True)],
        out_shape=[osh, osh],
        scratch_shapes=state + state,
        compiler_params=_cparams(("parallel", "parallel", "arbitrary")),
        name="mlstm_scan",
    )(p3, p3, p3, g_rows, g_cols, p3, p3, p3, g_rows, g_cols, conv4, conv4)


def _readout_kernel(oaf_ref, oab_ref, obf_ref, obb_ref, ag_ref, bo_ref, na_ref, nb_ref, ya_ref, yb_ref):
    for h in range(A_HEADS):
        sl = slice(h * A_D, (h + 1) * A_D)
        o = oaf_ref[0, :, sl].astype(F32) + oab_ref[0, :, sl].astype(F32)
        ya_ref[0, :, sl] = (_rms(o, na_ref[:, sl]) * _silu(ag_ref[0, :, sl])).astype(ya_ref.dtype)
    for h in range(B_HEADS):
        sl = slice(h * B_DV, (h + 1) * B_DV)
        o = obf_ref[0, :, sl].astype(F32) + obb_ref[0, :, sl].astype(F32)
        yb_ref[0, :, sl] = (_rms(o, nb_ref[:, sl]) * _sigmoid(bo_ref[0, :, sl])).astype(yb_ref.dtype)


def _readout(oa, ob, p3, norm_a, norm_b, n_lat):
    Bn = p3.shape[0]
    T = n_lat * TOK
    wa = A_HEADS * A_D
    wb = B_HEADS * B_DV
    ag_blk = 4
    bo_blk = (5 * A_HEADS + 4 * B_HEADS) * LANE // wb
    sa = pl.BlockSpec((1, TOK, wa), lambda b, i: (b, i, 0))
    sb = pl.BlockSpec((1, TOK, wb), lambda b, i: (b, i, 0))
    return pl.pallas_call(
        _readout_kernel,
        grid=(Bn, n_lat),
        in_specs=[sa, sa, sb, sb,
                  pl.BlockSpec((1, TOK, wa), lambda b, i: (b, i, ag_blk)),
                  pl.BlockSpec((1, TOK, wb), lambda b, i: (b, i, bo_blk)),
                  pl.BlockSpec((1, wa), lambda b, i: (0, 0)),
                  pl.BlockSpec((1, wb), lambda b, i: (0, 0))],
        out_specs=[sa, sb],
        out_shape=[jax.ShapeDtypeStruct((Bn, T, wa), BF16),
                   jax.ShapeDtypeStruct((Bn, T, wb), BF16)],
        compiler_params=_cparams(("parallel", "parallel")),
        name="readout",
    )(oa[0], oa[1], ob[0], ob[1], p3, p3, norm_a.reshape(1, wa), norm_b.reshape(1, wb))


def _merge_kernel(ya_ref, yb_ref, wa_ref, wb_ref, ga_ref, gb_ref, o_ref):
    pa = jnp.dot(ya_ref[0], wa_ref[...], preferred_element_type=F32)
    pb = jnp.dot(yb_ref[0], wb_ref[...], preferred_element_type=F32)
    o_ref[0] = (_sigmoid(ga_ref[0]) * pa + _sigmoid(gb_ref[0]) * pb).astype(o_ref.dtype)


def _merge(ya, yb, wa, wb, p3, D):
    Bn, T, ka = ya.shape
    kb = yb.shape[2]
    tm = _pick(T, 1024)
    tn = _pick(D, 512)
    ga0 = 0
    gb0 = D // tn
    return pl.pallas_call(
        _merge_kernel,
        grid=(Bn, T // tm, D // tn),
        in_specs=[pl.BlockSpec((1, tm, ka), lambda b, i, j: (b, i, 0)),
                  pl.BlockSpec((1, tm, kb), lambda b, i, j: (b, i, 0)),
                  pl.BlockSpec((ka, tn), lambda b, i, j: (0, j)),
                  pl.BlockSpec((kb, tn), lambda b, i, j: (0, j)),
                  pl.BlockSpec((1, tm, tn), lambda b, i, j: (b, i, ga0 + j)),
                  pl.BlockSpec((1, tm, tn), lambda b, i, j: (b, i, gb0 + j))],
        out_specs=pl.BlockSpec((1, tm, tn), lambda b, i, j: (b, i, j)),
        out_shape=jax.ShapeDtypeStruct((Bn, T, D), BF16),
        compiler_params=_cparams(("parallel", "parallel", "arbitrary")),
        name="merge",
    )(ya, yb, wa, wb, p3, p3)


def _resid_router_kernel(x_ref, mix_ref, g1_ref, g2_ref, gate_ref, sh_ref, sc_ref, wr_ref,
                         h_ref, v_ref, aff_ref):
    h = x_ref[0] + gate_ref[0] * _rms(mix_ref[0].astype(F32), g1_ref[...])
    h_ref[0] = h
    v = _rms(h, g2_ref[...]) * (1.0 + sc_ref[0]) + sh_ref[0]
    v_ref[0] = _pack_bf16_pairs(v)
    logits = jnp.dot(v, wr_ref[...], precision=lax.Precision.HIGHEST, preferred_element_type=F32)
    lane = lax.broadcasted_iota(jnp.int32, logits.shape, 1)
    logits = jnp.where(lane < N_EXPERTS, logits, -jnp.inf)
    e = jnp.exp(logits - jnp.max(logits, axis=-1, keepdims=True))
    aff_ref[0] = e / jnp.sum(e, axis=-1, keepdims=True)


def _resid_router(x, mix, g1, g2, mod3, w_router_pad):
    Bn, T, D = x.shape
    mspec = lambda k: pl.BlockSpec((1, 1, D), lambda b, i: (b, 0, k))
    tok = pl.BlockSpec((1, TOK, D), lambda b, i: (b, i, 0))
    vec = pl.BlockSpec((1, D), lambda b, i: (0, 0))
    return pl.pallas_call(
        _resid_router_kernel,
        grid=(Bn, T // TOK),
        in_specs=[tok, tok, vec, vec, mspec(2), mspec(3), mspec(4),
                  pl.BlockSpec((D, LANE), lambda b, i: (0, 0))],
        out_specs=[tok, pl.BlockSpec((1, TOK, D // 2), lambda b, i: (b, i, 0)),
                   pl.BlockSpec((1, TOK, LANE), lambda b, i: (b, i, 0))],
        out_shape=[jax.ShapeDtypeStruct((Bn, T, D), F32),
                   jax.ShapeDtypeStruct((Bn, T, D // 2), jnp.uint32),
                   jax.ShapeDtypeStruct((Bn, T, LANE), F32)],
        compiler_params=_cparams(("parallel", "parallel")),
        name="resid_router",
    )(x, mix, g1.reshape(1, D), g2.reshape(1, D), mod3, mod3, mod3, w_router_pad)


def _select_kernel(aff_ref, slot_ref, idx_ref, offs_ref, *, cap, tk):
    E, T = aff_ref.shape[1], aff_ref.shape[2]
    nck = T // LANE

    def key():
        return lax.bitcast_convert_type(aff_ref[0], jnp.int32)

    def bit_step(i, tau):
        cand = tau | jnp.left_shift(jnp.int32(1), 30 - i)
        cnt = jnp.sum(jnp.where(key() >= cand, 1.0, 0.0), axis=1, keepdims=True)
        return jnp.where(cnt >= cap, cand, tau)

    tau = lax.fori_loop(0, 31, bit_step, jnp.zeros((E, 1), jnp.int32))
    kk = key()
    gt = kk > tau
    eq = kk == tau
    need = cap - jnp.sum(jnp.where(gt, 1.0, 0.0), axis=1, keepdims=True)

    r = lax.broadcasted_iota(jnp.int32, (LANE, LANE), 0)
    c = lax.broadcasted_iota(jnp.int32, (LANE, LANE), 1)
    upper = jnp.where(r < c, 1.0, 0.0).astype(BF16)
    tr = jnp.right_shift(lax.broadcasted_iota(jnp.int32, (T, LANE), 0), LANE.bit_length() - 1)
    member = jnp.where(tr == lax.broadcasted_iota(jnp.int32, (T, LANE), 1), 1.0, 0.0).astype(BF16)

    def prefix(flags):
        x = jnp.where(flags, 1.0, 0.0).astype(BF16)
        tot = jnp.dot(x, member, preferred_element_type=F32)
        offs = jnp.dot(tot.astype(BF16), upper, preferred_element_type=F32)
        parts = [jnp.dot(x[:, j * LANE:(j + 1) * LANE], upper, preferred_element_type=F32)
                 + offs[:, j:j + 1] for j in range(nck)]
        return jnp.concatenate(parts, axis=1), offs

    tie_rank, _ = prefix(eq)
    sel = gt | (eq & (tie_rank < need))
    pos, offs = prefix(sel)
    slot = jnp.where(sel, pos, -1.0)
    slot_ref[0] = slot
    offs_ref[0] = offs.astype(jnp.int32)

    tt = lax.broadcasted_iota(jnp.int32, (8, T), 1)
    rr = lax.broadcasted_iota(jnp.int32, (8, T), 0)
    digits = jnp.where(rr == 0, jnp.right_shift(tt, 6), jnp.where(rr == 1, tt & 63, 0))
    digits = digits.astype(F32).astype(BF16)
    slot_iota = lax.broadcasted_iota(jnp.int32, (cap, tk), 0).astype(F32)
    for e in range(E):
        acc = jnp.zeros((8, cap), F32)
        for kc in range(T // tk):
            ks = slice(kc * tk, (kc + 1) * tk)
            oh = jnp.where(slot[e:e + 1, ks] == slot_iota, 1.0, 0.0).astype(BF16)
            acc = acc + lax.dot_general(digits[:, ks], oh, _NT, preferred_element_type=F32)
        idx_ref[0, e:e + 1, :] = (acc[0:1] * 64.0 + acc[1:2]).astype(jnp.int32)


def _select(aff_rows, cap):
    Bn, E, T = aff_rows.shape
    assert T // LANE < LANE and T <= 4096
    tk = _pick(T, 1024)
    row = lambda n: pl.BlockSpec((1, E, n), lambda b: (b, 0, 0))
    return pl.pallas_call(
        functools.partial(_select_kernel, cap=cap, tk=tk),
        grid=(Bn,),
        in_specs=[row(T)],
        out_specs=[row(T), row(cap), row(LANE)],
        out_shape=[jax.ShapeDtypeStruct((Bn, E, T), F32),
                   jax.ShapeDtypeStruct((Bn, E, cap), jnp.int32),
                   jax.ShapeDtypeStruct((Bn, E, LANE), jnp.int32)],
        compiler_params=_cparams(("parallel",)),
        name="ec_select",
    )(aff_rows)


def _pack_bf16_pairs(x):
    half = x.shape[1] // 2
    bits = lax.bitcast_convert_type(x.astype(BF16).astype(F32), jnp.uint32)
    return bits[:, half:] | (bits[:, :half] >> 16)


def _unpack_bf16_pairs(p):
    lo = lax.bitcast_convert_type(p << 16, F32).astype(BF16)
    hi = lax.bitcast_convert_type(p & jnp.uint32(0xFFFF0000), F32).astype(BF16)
    return lo, hi


def _ffn1_kernel(idx_ref, v_hbm, wg_ref, wu_ref, o_ref, land, xb, sem, *, cap, n_tok):
    e = pl.program_id(0)
    i = pl.program_id(1)
    j = pl.program_id(2)
    n_e, n_i, n_j = pl.num_programs(0), pl.num_programs(1), pl.num_programs(2)
    tm = xb.shape[0]
    t = e * n_i + i
    per = tm // n_j

    def row_copy(tile, r, src_row):
        return pltpu.make_async_copy(v_hbm.at[pl.ds(src_row, 1), :],
                                     land.at[tile % 2, pl.ds(r, 1), :], sem.at[tile % 2])

    def issue(tile, r0, n):
        te = tile // n_i
        m0 = (tile % n_i) * tm

        def body(r, carry):
            m = m0 + r
            b = m // cap
            tok = idx_ref[(b * n_e + te) * cap + m % cap]
            row_copy(tile, r, b * n_tok + tok).start()
            return carry

        lax.fori_loop(r0, r0 + n, body, 0)

    @pl.when((t == 0) & (j == 0))
    def _():
        issue(t, 0, tm)

    @pl.when(t + 1 < n_e * n_i)
    def _():
        issue(t + 1, j * per, per)

    @pl.when(j == 0)
    def _():
        def drain(r, carry):
            row_copy(t, r, 0).wait()
            return carry

        lax.fori_loop(0, tm, drain, 0)
        half = land.shape[2]
        lo, hi = _unpack_bf16_pairs(land[t % 2])
        xb[:, :half] = lo
        xb[:, half:] = hi

    x = xb[...]
    g = jnp.dot(x, wg_ref[0].astype(BF16), preferred_element_type=F32)
    u = jnp.dot(x, wu_ref[0].astype(BF16), preferred_element_type=F32)
    o_ref[0] = (_silu(g) * u).astype(o_ref.dtype)


def _ffn1(idx, v_pairs, wg, wu, cap):
    Bn, T, half = v_pairs.shape
    E, D, F = wg.shape
    M = Bn * cap
    tm = _pick(M, 1024)
    tn = _pick(F, 256)
    assert tm % cap == 0 and tm % (F // tn) == 0 and D == 2 * half
    return pl.pallas_call(
        functools.partial(_ffn1_kernel, cap=cap, n_tok=T),
        grid_spec=pltpu.PrefetchScalarGridSpec(
            num_scalar_prefetch=1,
            grid=(E, M // tm, F // tn),
            in_specs=[pl.BlockSpec(memory_space=pl.ANY),
                      pl.BlockSpec((1, D, tn), lambda e, i, j, idx_ref: (e, 0, j)),
                      pl.BlockSpec((1, D, tn), lambda e, i, j, idx_ref: (e, 0, j))],
            out_specs=pl.BlockSpec((1, tm, tn), lambda e, i, j, idx_ref: (e, i, j)),
            scratch_shapes=[pltpu.VMEM((2, tm, half), jnp.uint32), pltpu.VMEM((tm, D), BF16),
                            pltpu.SemaphoreType.DMA((2,))]),
        out_shape=jax.ShapeDtypeStruct((E, M, F), BF16),
        compiler_params=_cparams(("arbitrary", "arbitrary", "arbitrary")),
        name="ec_ffn_up",
    )(idx.reshape(-1), v_pairs.reshape(Bn * T, half), wg, wu)


def _ffn2_kernel(h_ref, wd_ref, o_ref):
    o_ref[0] = jnp.dot(h_ref[0], wd_ref[0].astype(BF16),
                       preferred_element_type=F32).astype(o_ref.dtype)


def _ffn2(hid, wd):
    E, M, F = hid.shape
    D = wd.shape[2]
    tn = _pick(D, 512)
    return pl.pallas_call(
        _ffn2_kernel,
        grid=(E, D // tn),
        in_specs=[pl.BlockSpec((1, M, F), lambda e, j: (e, 0, 0)),
                  pl.BlockSpec((1, F, tn), lambda e, j: (e, 0, j))],
        out_specs=pl.BlockSpec((1, M, tn), lambda e, j: (e, 0, j)),
        out_shape=jax.ShapeDtypeStruct((E, M, D), BF16),
        compiler_params=_cparams(("parallel", "arbitrary")),
        name="ec_ffn_down",
    )(hid, wd)


def _scatter_kernel(offs_ref, slot_ref, aff_ref, y_ref, h_ref, g_ref, gate_ref, o_ref, *, cap, win):
    b = pl.program_id(0)
    t = pl.program_id(1)
    e = pl.program_id(2)
    tq = o_ref.shape[1]

    @pl.when(e == 0)
    def _():
        o_ref[...] = jnp.zeros_like(o_ref)

    base = (b * N_EXPERTS + e) * LANE
    lo = offs_ref[base + t * (tq // LANE)]
    hi = offs_ref[base + (t + 1) * (tq // LANE)]
    sl = jnp.broadcast_to(slot_ref[0, 0], (LANE, tq)).T
    af = jnp.broadcast_to(aff_ref[0, 0], (LANE, tq)).T
    sl = jnp.concatenate([sl] * (win // LANE), axis=1)
    af = jnp.concatenate([af] * (win // LANE), axis=1)
    lane = lax.broadcasted_iota(jnp.int32, (tq, win), 1).astype(F32)
    for w0 in range(0, cap, win):
        @pl.when((lo < w0 + win) & (hi > w0))
        def _():
            ohw = jnp.where(sl == lane + float(w0), af, 0.0).astype(BF16)
            o_ref[0] += jnp.dot(ohw, y_ref[0, 0, w0:w0 + win, :], preferred_element_type=F32)

    @pl.when(e == pl.num_programs(2) - 1)
    def _():
        o_ref[0] = h_ref[0] + gate_ref[0] * _rms(o_ref[0], g_ref[...])


def _scatter_combine(offs, slot_rows, aff_rows, y, cap, h, g3, mod3):
    E, Bn, _, D = y.shape
    T = slot_rows.shape[3]
    tq = _pick(T, 512)
    win = _pick(cap, 256)
    assert win % LANE == 0 and tq % LANE == 0
    row = pl.BlockSpec((1, 1, 1, tq), lambda b, t, e, offs_ref: (b, e, 0, t))
    return pl.pallas_call(
        functools.partial(_scatter_kernel, cap=cap, win=win),
        grid_spec=pltpu.PrefetchScalarGridSpec(
            num_scalar_prefetch=1,
            grid=(Bn, T // tq, E),
            in_specs=[row, row,
                      pl.BlockSpec((1, 1, cap, D), lambda b, t, e, offs_ref: (e, b, 0, 0)),
                      pl.BlockSpec((1, tq, D), lambda b, t, e, offs_ref: (b, t, 0)),
                      pl.BlockSpec((1, D), lambda b, t, e, offs_ref: (0, 0)),
                      pl.BlockSpec((1, 1, D), lambda b, t, e, offs_ref: (b, 0, 5))],
            out_specs=pl.BlockSpec((1, tq, D), lambda b, t, e, offs_ref: (b, t, 0))),
        out_shape=jax.ShapeDtypeStruct((Bn, T, D), F32),
        compiler_params=_cparams(("parallel", "parallel", "arbitrary")),
        name="ec_scatter",
    )(offs.reshape(-1), slot_rows, aff_rows, y, h, g3.reshape(1, D), mod3)


def _layer(h_lat, ctx, c8, l, lb_l, w_ada, b_ada, g_norm, w_in, hgrn_f_bias, hgrn_norm,
           mlstm_conv_w, mlstm_conv_b, mlstm_gate_b, mlstm_norm, w_branch_a, w_branch_b, w_out,
           w_router, w_expert_gate, w_expert_up, w_expert_down):
    Bn, T, D = h_lat.shape
    n_lat = T // TOK
    n_tok = T + TOK
    A, Bh = A_HEADS, B_HEADS
    g0 = (5 * A + 6 * Bh) * LANE

    mod = _modulation(c8, w_ada[l], b_ada[l])
    mod3 = mod.reshape(8, 1, N_MOD * D)

    u = _prenorm(h_lat, ctx, g_norm[l, 0], mod3)

    w_in_t = jnp.swapaxes(w_in, 1, 2)
    w_merge_t, w_gate_t = _wsplit(w_in_t, l, g0, 4 * Bh, 2 * D)
    gate_bias = jnp.pad(mlstm_gate_b[l].reshape(4 * Bh), (0, LANE - 4 * Bh))
    u2 = u.reshape(Bn * n_tok, D)
    p3 = _matmul_t(u2, w_in_t, F32, n_cols=g0, layer=l, name="in_proj").reshape(Bn, n_tok, g0)
    p_merge = _matmul_t(u2, w_merge_t, F32, name="in_proj_merge").reshape(Bn, n_tok, 2 * D)
    gates = _matmul_t(u2, w_gate_t, F32, tn=LANE, bias=gate_bias, name="in_proj_gates")

    g6 = gates[:, :4 * Bh].reshape(Bn, n_tok // CHUNK, CHUNK, 2, 2, Bh)
    g_rows = g6.transpose(0, 3, 5, 1, 4, 2)
    g_cols = g6.transpose(0, 3, 5, 1, 2, 4)
    conv4 = jnp.concatenate([mlstm_conv_w[l], mlstm_conv_b[l][None]], axis=0)
    conv4 = conv4.reshape(4, 1, 2 * Bh * B_DQK)

    oa = _hgrn_scan(p3, hgrn_f_bias[l], lb_l, n_lat)
    ob = _mlstm_scan(p3, g_rows, g_cols, conv4, n_lat)
    ya, yb = _readout(oa, ob, p3, hgrn_norm[l], mlstm_norm[l], n_lat)
    merged = _merge(ya, yb, w_branch_a[l].astype(BF16), w_branch_b[l].astype(BF16), p_merge, D)
    mix = _matmul(merged.reshape(Bn * T, D), w_out[l].astype(BF16), BF16, name="out_proj")

    w_router_pad = jnp.pad(w_router[l], ((0, 0), (0, LANE - N_EXPERTS)))
    h_lat, v_lat, aff = _resid_router(h_lat, mix.reshape(Bn, T, D), g_norm[l, 1], g_norm[l, 2],
                                      mod3, w_router_pad)

    cap = CAPACITY * T // N_EXPERTS
    aff_rows = aff[:, :, :N_EXPERTS].transpose(0, 2, 1)
    slot, idx, offs = _select(aff_rows, cap)
    hid = _ffn1(idx, v_lat, w_expert_gate[l], w_expert_up[l], cap)
    y = _ffn2(hid, w_expert_down[l]).reshape(N_EXPERTS, Bn, cap, D)
    return _scatter_combine(offs, slot.reshape(Bn, N_EXPERTS, 1, T),
                            aff_rows.reshape(Bn, N_EXPERTS, 1, T), y, cap, h_lat, g_norm[l, 3], mod3)


def kernel(x, c, ctx, c_ctx, w_ada, b_ada, g_norm, w_in, hgrn_f_bias, hgrn_lb, hgrn_norm,
           mlstm_conv_w, mlstm_conv_b, mlstm_gate_b, mlstm_norm, w_branch_a, w_branch_b, w_out,
           w_router, w_expert_gate, w_expert_up, w_expert_down):
    Bn, T, D = x.shape
    depth = w_ada.shape[0]
    assert depth == 1, "context outputs are only produced for the state hand-off (single layer)"
    assert ctx.shape[1] == TOK and T % TOK == 0 and Bn < 8
    lb_all = jnp.cumsum(jax.nn.softmax(hgrn_lb.astype(F32), axis=1), axis=1)
    c8 = jnp.zeros((8, D), F32).at[:Bn].set(c).at[Bn].set(c_ctx)
    h_lat = x
    for l in range(depth):
        h_lat = _layer(h_lat, ctx, c8, l, lb_all[:, l], w_ada, b_ada, g_norm, w_in, hgrn_f_bias,
                       hgrn_norm, mlstm_conv_w, mlstm_conv_b, mlstm_gate_b, mlstm_norm,
                       w_branch_a, w_branch_b, w_out, w_router, w_expert_gate, w_expert_up,
                       w_expert_down)
    return h_lat.astype(x.dtype)
```

```python
import functools

import jax
import jax.numpy as jnp
from jax import lax
from jax.experimental import pallas as pl
from jax.experimental.pallas import tpu as pltpu

F32 = jnp.float32
BF16 = jnp.bfloat16

EPS = 1e-6
N_MOD = 6
A_HEADS = 16
A_D = 128
B_HEADS = 8
B_DQK = 128
B_DV = 256
N_EXPERTS = 16
CAPACITY = 2
CHUNK = 64
SUB = 16
A_HPS = 4
B_HPS = 4
TOK = 256
NCH = TOK // CHUNK
NSUB = CHUNK // SUB
LANE = 128
VMEM_LIMIT = 56 * 1024 * 1024

_NT = (((1,), (1,)), ((), ()))


def _pick(n, pref):
    t = min(n, pref)
    while n % t:
        t //= 2
    return t


def _cparams(sem):
    return pltpu.CompilerParams(dimension_semantics=sem, vmem_limit_bytes=VMEM_LIMIT)


def _silu(x):
    return x / (1.0 + jnp.exp(-x))


def _sigmoid(x):
    return 1.0 / (1.0 + jnp.exp(-x))


def _sig_pair(z):
    t = jnp.exp(-jnp.abs(z))
    r = 1.0 / (1.0 + t)
    tr = t * r
    pos = z >= 0
    return jnp.where(pos, r, tr), jnp.where(pos, tr, r)


def _log_sigmoid(x):
    return jnp.minimum(x, 0.0) - jnp.log(1.0 + jnp.exp(-jnp.abs(x)))


def _rms(xf, w):
    return xf * lax.rsqrt(jnp.mean(xf * xf, axis=-1, keepdims=True) + EPS) * w


def _rows(x, r, n):
    return jnp.broadcast_to(x[r:r + 1], (n, x.shape[1]))


def _mod_kernel(c_ref, w_ref, b_ref, o_ref):
    a = _silu(c_ref[...]).astype(BF16)
    o_ref[...] = jnp.dot(a, w_ref[...].astype(BF16), preferred_element_type=F32) + b_ref[...]


def _modulation(c8, w, b):
    D, N = w.shape
    tn = _pick(N, 512)
    return pl.pallas_call(
        _mod_kernel,
        grid=(N // tn,),
        in_specs=[pl.BlockSpec((8, D), lambda j: (0, 0)),
                  pl.BlockSpec((D, tn), lambda j: (0, j)),
                  pl.BlockSpec((1, tn), lambda j: (0, j))],
        out_specs=pl.BlockSpec((8, tn), lambda j: (0, j)),
        out_shape=jax.ShapeDtypeStruct((8, N), F32),
        compiler_params=_cparams(("parallel",)),
        name="adaln_mod",
    )(c8, w, b.reshape(1, N))


def _prenorm_kernel(x_ref, ctx_ref, g_ref, sh_ref, sc_ref, o_ref, *, n_lat):
    i = pl.program_id(1)
    g = g_ref[...]
    sh = sh_ref[0]
    sc = sc_ref[0]

    @pl.when(i < n_lat)
    def _():
        o_ref[0] = (_rms(x_ref[0], g) * (1.0 + sc) + sh).astype(o_ref.dtype)

    @pl.when(i == n_lat)
    def _():
        o_ref[0] = (_rms(ctx_ref[0], g) * (1.0 + sc) + sh).astype(o_ref.dtype)


def _prenorm(x, ctx, g, mod3):
    Bn, T, D = x.shape
    n_lat = T // TOK
    row = lambda b, i: jnp.where(i == n_lat, Bn, b)
    return pl.pallas_call(
        functools.partial(_prenorm_kernel, n_lat=n_lat),
        grid=(Bn, n_lat + 1),
        in_specs=[pl.BlockSpec((1, TOK, D), lambda b, i: (b, jnp.minimum(i, n_lat - 1), 0)),
                  pl.BlockSpec((1, TOK, D), lambda b, i: (b, 0, 0)),
                  pl.BlockSpec((1, D), lambda b, i: (0, 0)),
                  pl.BlockSpec((1, 1, D), lambda b, i: (row(b, i), 0, 0)),
                  pl.BlockSpec((1, 1, D), lambda b, i: (row(b, i), 0, 1))],
        out_specs=pl.BlockSpec((1, TOK, D), lambda b, i: (b, i, 0)),
        out_shape=jax.ShapeDtypeStruct((Bn, T + TOK, D), BF16),
        compiler_params=_cparams(("parallel", "arbitrary")),
        name="prenorm",
    )(x, ctx, g.reshape(1, D), mod3, mod3)


_NN = (((1,), (0,)), ((), ()))


def _mm_kernel(a_ref, b_ref, o_ref, *, dims):
    o_ref[...] = lax.dot_general(a_ref[...], b_ref[...].astype(BF16), dims,
                                 preferred_element_type=F32).astype(o_ref.dtype)


def _mm_bias_kernel(a_ref, b_ref, bias_ref, o_ref, *, dims):
    o_ref[...] = (lax.dot_general(a_ref[...], b_ref[...].astype(BF16), dims, preferred_element_type=F32)
                  + bias_ref[...]).astype(o_ref.dtype)


def _matmul(a, b, out_dtype, tm=1024, tn=512, bias=None, name="matmul"):
    M, K = a.shape
    N = b.shape[1]
    tm = _pick(M, tm)
    tn = _pick(N, tn)
    return _matmul_call(a, b, pl.BlockSpec((K, tn), lambda i, j: (0, j)), _NN, N, out_dtype, tm, tn,
                        bias, name)


def _matmul_t(a, bt, out_dtype, tm=1024, tn=512, bias=None, n_cols=None, layer=None, name="matmul_t"):
    M, K = a.shape
    N = bt.shape[-2] if n_cols is None else n_cols
    tm = _pick(M, tm)
    tn = _pick(N, tn)
    if layer is None:
        b_spec = pl.BlockSpec((tn, K), lambda i, j: (j, 0))
    else:
        b_spec = pl.BlockSpec((None, tn, K), lambda i, j: (layer, j, 0))
    return _matmul_call(a, bt, b_spec, _NT, N, out_dtype, tm, tn, bias, name)


def _matmul_call(a, b, b_spec, dims, N, out_dtype, tm, tn, bias, name):
    M, K = a.shape
    in_specs = [pl.BlockSpec((tm, K), lambda i, j: (i, 0)), b_spec]
    args = [a, b]
    kern = functools.partial(_mm_kernel, dims=dims)
    if bias is not None:
        in_specs.append(pl.BlockSpec((1, tn), lambda i, j: (0, j)))
        args.append(bias.reshape(1, N))
        kern = functools.partial(_mm_bias_kernel, dims=dims)
    return pl.pallas_call(
        kern,
        grid=(M // tm, N // tn),
        in_specs=in_specs,
        out_specs=pl.BlockSpec((tm, tn), lambda i, j: (i, j)),
        out_shape=jax.ShapeDtypeStruct((M, N), out_dtype),
        compiler_params=_cparams(("parallel", "arbitrary")),
        name=name,
    )(*args)


def _wsplit_kernel(a_ref, b_ref, o_ref, g_ref, *, shift):
    j = pl.program_id(1)
    tn = o_ref.shape[0]
    cat = jnp.concatenate([a_ref[...], b_ref[...]], axis=0)
    o_ref[...] = cat[shift:shift + tn].astype(o_ref.dtype)

    @pl.when(j == 0)
    def _():
        row = lax.broadcasted_iota(jnp.int32, g_ref.shape, 0)
        g_ref[...] = jnp.where(row < shift, cat[:LANE], 0.0).astype(g_ref.dtype)


def _wsplit(wt, layer, row0, shift, n):
    K = wt.shape[2]
    tk = _pick(K, 1024)
    tn = _pick(n, 512)
    assert row0 % tn == 0 and 0 < shift < LANE and shift % 8 == 0 and row0 + shift + n <= wt.shape[1]
    r0 = row0 // tn
    return pl.pallas_call(
        functools.partial(_wsplit_kernel, shift=shift),
        grid=(K // tk, n // tn),
        in_specs=[pl.BlockSpec((None, tn, tk), lambda i, j: (layer, r0 + j, i)),
                  pl.BlockSpec((None, tn, tk), lambda i, j: (layer, r0 + j + 1, i))],
        out_specs=[pl.BlockSpec((tn, tk), lambda i, j: (j, i)),
                   pl.BlockSpec((LANE, tk), lambda i, j: (0, i))],
        out_shape=[jax.ShapeDtypeStruct((n, K), BF16), jax.ShapeDtypeStruct((LANE, K), BF16)],
        compiler_params=_cparams(("parallel", "arbitrary")),
        name="w_split",
    )(wt, wt)


def _tok_block(rev, j, n_lat):
    lat = n_lat - j if rev else j - 1
    return jnp.where(j == 0, n_lat, lat)


def _out_block(rev, j, n_lat):
    return jnp.clip(n_lat - j if rev else j - 1, 0, n_lat - 1)


def _chunk_masks(rev):
    row = lax.broadcasted_iota(jnp.int32, (CHUNK, CHUNK), 0)
    col = lax.broadcasted_iota(jnp.int32, (CHUNK, CHUNK), 1)
    return (col >= row, col <= row) if rev else (col <= row, col >= row)


def _hgrn_consts():
    t = jnp.arange(TOK)
    same = (t[:, None] // CHUNK) == (t[None, :] // CHUNK)
    sub_r, sub_c = t[:, None] // SUB, t[None, :] // SUB
    fwd = jnp.concatenate([same & (t[None, :] <= t[:, None]), same & (sub_c < sub_r)], axis=0)
    bwd = jnp.concatenate([same & (t[None, :] >= t[:, None]), same & (sub_c > sub_r)], axis=0)
    masks = jnp.stack([fwd, bwd]).astype(BF16)
    r = jnp.arange(SUB * A_D)
    emat = ((r[:, None] // A_D) == (jnp.arange(LANE)[None, :] % SUB)).astype(BF16)
    return masks, emat


def _hgrn_state(af, v, fb, lb, mask_ref, st_ref, rev):
    z = af + fb
    sp, sn = _sig_pair(z)
    k = (1.0 - lb) * sn
    lf2 = jnp.log2(lb + (1.0 - lb) * sp)
    lk2 = jnp.log2(k)

    hi = lf2.astype(BF16)
    r1 = lf2 - hi.astype(F32)
    mid = r1.astype(BF16)
    lo = (r1 - mid.astype(F32)).astype(BF16)
    cs = jnp.dot(mask_ref[0], jnp.concatenate([hi, mid, lo], axis=1), preferred_element_type=F32)
    cs = cs[:, :A_D] + cs[:, A_D:2 * A_D] + cs[:, 2 * A_D:]
    b = cs[:TOK]
    ent = cs[TOK:]
    c_all = lk2 - b

    last = [(c * CHUNK if rev else c * CHUNK + CHUNK - 1) for c in range(NCH)]
    tot_rows = jnp.concatenate([_rows(b, last[c], CHUNK) for c in range(NCH)], axis=0)
    khat = jnp.exp2(jnp.minimum(tot_rows + c_all, lk2)).astype(BF16)
    upd = [jnp.dot(v[c * CHUNK:(c + 1) * CHUNK].T.astype(BF16), khat[c * CHUNK:(c + 1) * CHUNK],
                   preferred_element_type=F32) for c in range(NCH)]

    order = range(NCH - 1, -1, -1) if rev else range(NCH)
    st = st_ref[...]
    st_in = [None] * NCH
    for c in order:
        st_in[c] = st
        st = st * jnp.exp2(b[last[c]:last[c] + 1]) + upd[c]
    st_ref[...] = st
    return v, b, ent, c_all, lk2, st_in


def _hgrn_output(aq, feats, emat_ref, o_ref, hs, rev):
    v, b, ent, c_all, lk2, st_in = feats
    q = _silu(aq)
    qb = q.astype(BF16)
    vb = v.astype(BF16)
    qt = (q * jnp.exp2(b - ent)).astype(BF16)
    row = lax.broadcasted_iota(jnp.int32, (CHUNK, CHUNK), 0)
    col = lax.broadcasted_iota(jnp.int32, (CHUNK, CHUNK), 1)
    rs = jnp.right_shift(row, SUB.bit_length() - 1)
    cs_ = jnp.right_shift(col, SUB.bit_length() - 1)
    prev_blk = (cs_ > rs) if rev else (cs_ < rs)
    diag_blk = (rs == cs_) & ((col >= row) if rev else (col <= row))

    off = [[None] * NSUB for _ in range(NCH)]
    for i in range(NSUB):
        e_i = jnp.concatenate([_rows(ent, c * CHUNK + i * SUB, CHUNK) for c in range(NCH)], axis=0)
        kt = jnp.exp2(jnp.minimum(e_i + c_all, lk2)).astype(BF16)
        for c in range(NCH):
            r0 = c * CHUNK + i * SUB
            off[c][i] = lax.dot_general(qt[r0:r0 + SUB], kt[c * CHUNK:(c + 1) * CHUNK], _NT,
                                        preferred_element_type=F32)

    half = SUB // 2
    nsb = TOK // SUB
    zeros = jnp.zeros((half, A_D), F32)
    ps = []
    for sl in range(SUB):
        n = half if (sl < half if rev else sl >= half) else SUB
        r0 = SUB - n if not rev else 0
        bq = b if n == SUB else jnp.concatenate(
            [b[g * SUB + r0:g * SUB + r0 + n] for g in range(nsb)], axis=0)
        cref = jnp.concatenate([_rows(c_all, g * SUB + sl, n) for g in range(nsb)], axis=0)
        kref = jnp.concatenate([_rows(lk2, g * SUB + sl, n) for g in range(nsb)], axis=0)
        e = jnp.exp2(jnp.minimum(bq + cref, kref))
        if n != SUB:
            parts = []
            for g in range(nsb):
                piece = e[g * half:(g + 1) * half]
                parts += [piece, zeros] if rev else [zeros, piece]
            e = jnp.concatenate(parts, axis=0)
        ps.append(e.astype(BF16) * qb)
    acc = jnp.dot(jnp.concatenate(ps, axis=1), emat_ref[...], preferred_element_type=F32)

    qhat = (q * jnp.exp2(b)).astype(BF16)
    for c in range(NCH):
        sl_c = slice(c * CHUNK, (c + 1) * CHUNK)
        attn = (jnp.where(prev_blk, jnp.concatenate(off[c], axis=0), 0.0)
                + jnp.where(diag_blk, acc[sl_c, :CHUNK], 0.0))
        o = jnp.dot(attn.astype(BF16), vb[sl_c], preferred_element_type=F32)
        o = o + lax.dot_general(qhat[sl_c], st_in[c].astype(BF16), _NT, preferred_element_type=F32)
        o_ref[0, sl_c, hs] = o.astype(o_ref.dtype)


def _hgrn_kernel(aqf, aff, aif, aqb, afb, aib, fb_ref, lb_ref, mask_ref, emat_ref, of_ref, ob_ref,
                 stf_ref, stb_ref):
    j = pl.program_id(2)

    @pl.when(j == 0)
    def _():
        stf_ref[...] = jnp.zeros_like(stf_ref)
        stb_ref[...] = jnp.zeros_like(stb_ref)

    feats = []
    for hh in range(A_HPS):
        hs = slice(hh * A_D, (hh + 1) * A_D)
        feats.append((hs,
                      _hgrn_state(aff[0, :, hs], aif[0, :, hs], fb_ref[0][:, hs], lb_ref[0][:, hs],
                                  mask_ref.at[0:1], stf_ref.at[hh], False),
                      _hgrn_state(afb[0, :, hs], aib[0, :, hs], fb_ref[1][:, hs], lb_ref[1][:, hs],
                                  mask_ref.at[1:2], stb_ref.at[hh], True)))
    for hs, feats_f, feats_b in feats:
        _hgrn_output(aqf[0, :, hs], feats_f, emat_ref, of_ref, hs, False)
        _hgrn_output(aqb[0, :, hs], feats_b, emat_ref, ob_ref, hs, True)


def _hgrn_scan(p3, f_bias, lb, n_lat):
    Bn = p3.shape[0]
    A = A_HEADS
    masks, emat = _hgrn_consts()

    W = A_HPS * A_D
    G = A // A_HPS

    def feat(rev, grp):
        return pl.BlockSpec((1, TOK, W), lambda b, h, j: (b, _tok_block(rev, j, n_lat), grp * G + h))

    def outp(rev):
        return pl.BlockSpec((1, TOK, W), lambda b, h, j: (b, _out_block(rev, j, n_lat), h))

    par = pl.BlockSpec((2, 1, W), lambda b, h, j: (0, 0, h))
    osh = jax.ShapeDtypeStruct((Bn, n_lat * TOK, A * A_D), BF16)
    return pl.pallas_call(
        _hgrn_kernel,
        grid=(Bn, G, n_lat + 1),
        in_specs=[feat(False, 0), feat(False, 1), feat(False, 3),
                  feat(True, 0), feat(True, 2), feat(True, 3),
                  par, par,
                  pl.BlockSpec((2, 2 * TOK, TOK), lambda b, h, j: (0, 0, 0)),
                  pl.BlockSpec((SUB * A_D, LANE), lambda b, h, j: (0, 0))],
        out_specs=[outp(False), outp(True)],
        out_shape=[osh, osh],
        scratch_shapes=[pltpu.VMEM((A_HPS, A_D, A_D), F32), pltpu.VMEM((A_HPS, A_D, A_D), F32)],
        compiler_params=_cparams(("parallel", "parallel", "arbitrary")),
        name="hgrn2_scan",
    )(p3, p3, p3, p3, p3, p3, f_bias.reshape(2, 1, A * A_D), lb.reshape(2, 1, A * A_D), masks, emat)


def _mlstm_state(bq_ref, bk_ref, bv_ref, gr_ref, gc_ref, cwq, cwk, first, last_, c_ref, n_ref, m_ref,
                 hh, rev):
    def conv(u, cw):
        up = jnp.where(first, 0.0, pltpu.roll(u, 1, 0))
        dn = jnp.where(last_, 0.0, pltpu.roll(u, TOK - 1, 0))
        return _silu(cw[3:4] + up * cw[0:1] + u * cw[1:2] + dn * cw[2:3])

    hs = slice(hh * B_DQK, (hh + 1) * B_DQK)
    vs = slice(hh * B_DV, (hh + 1) * B_DV)
    c_ref, n_ref, m_ref = c_ref.at[hh], n_ref.at[hh], m_ref.at[hh]
    q_all = conv(bq_ref[0, :, hs], cwq[:, hs])
    k_all = conv(bk_ref[0, :, hs], cwk[:, hs]) * (B_DQK ** -0.5)
    seen, seen_t = _chunk_masks(rev)

    pre = []
    for c in range(NCH):
        sl_c = slice(c * CHUNK, (c + 1) * CHUNK)
        k = k_all[sl_c]
        vb = bv_ref[0, sl_c, vs].astype(BF16)
        g_r = gr_ref[0, 0, hh, c]
        g_c = gc_ref[0, 0, hh, c]
        ii_r = g_r[0:1]
        lf_r = _log_sigmoid(g_r[1:2])
        ii_c = g_c[:, 0:1]
        lf_c = _log_sigmoid(g_c[:, 1:2])
        b_c = jnp.sum(jnp.where(seen, lf_r, 0.0), axis=1, keepdims=True)
        b_r = jnp.sum(jnp.where(seen_t, lf_c, 0.0), axis=0, keepdims=True)
        total = jnp.sum(lf_r, axis=1, keepdims=True)
        logs = total - b_c + ii_c
        ms = jnp.max(logs, axis=0, keepdims=True)
        kw = k * jnp.exp(logs - ms)
        upd = jnp.dot(kw.T.astype(BF16), vb, preferred_element_type=F32)
        nupd = jnp.sum(kw, axis=0, keepdims=True)
        pre.append((vb, ii_r, b_c, b_r, total, ms, upd, nupd))

    cmat, nvec, m = c_ref[...], n_ref[...], m_ref[:, 0:1]
    st_in = [None] * NCH
    for c in (range(NCH - 1, -1, -1) if rev else range(NCH)):
        _, _, _, _, total, ms, upd, nupd = pre[c]
        st_in[c] = (cmat, nvec, m)
        m_new = jnp.maximum(total + m, ms)
        dec = jnp.exp(total + m - m_new)
        sc = jnp.exp(ms - m_new)
        cmat = dec * cmat + sc * upd
        nvec = dec * nvec + sc * nupd
        m = m_new
    c_ref[...] = cmat
    n_ref[...] = nvec
    m_ref[...] = jnp.broadcast_to(m, m_ref.shape)
    return q_all, k_all, pre, st_in


def _mlstm_output(feats, o_ref, hh, rev):
    q_all, k_all, pre, st_in = feats
    seen, _ = _chunk_masks(rev)
    for c in range(NCH):
        sl_c = slice(c * CHUNK, (c + 1) * CHUNK)
        vb, ii_r, b_c, b_r, _, _, _, _ = pre[c]
        cm, nv, m0 = st_in[c]
        q = q_all[sl_c]
        qb = q.astype(BF16)
        kb = k_all[sl_c].astype(BF16)
        logw = jnp.where(seen, b_c - b_r + ii_r, -jnp.inf)
        mw = jnp.max(logw, axis=1, keepdims=True)
        qk = lax.dot_general(qb, kb, _NT, preferred_element_type=F32) * jnp.exp(logw - mw)
        num0 = jnp.dot(qk.astype(BF16), vb, preferred_element_type=F32)
        den0 = jnp.sum(qk, axis=1, keepdims=True)
        log_inter = b_c + m0
        m_t = jnp.maximum(mw, log_inter)
        r = jnp.exp(mw - m_t)
        a = jnp.exp(log_inter - m_t)
        num = r * num0 + a * jnp.dot(qb, cm.astype(BF16), preferred_element_type=F32)
        den = r * den0 + a * jnp.sum(q * nv, axis=1, keepdims=True)
        h = num / jnp.maximum(jnp.abs(den), jnp.exp(-m_t))
        o_ref[0, sl_c, hh * B_DV:(hh + 1) * B_DV] = h.astype(o_ref.dtype)


def _mlstm_kernel(bqf, bkf, bvf, grf, gcf, bqb, bkb, bvb, grb, gcb, cwq_ref, cwk_ref, of_ref, ob_ref,
                  cf_ref, nf_ref, mf_ref, cb_ref, nb_ref, mb_ref):
    j = pl.program_id(2)

    @pl.when(j == 0)
    def _():
        for ref in (cf_ref, nf_ref, mf_ref, cb_ref, nb_ref, mb_ref):
            ref[...] = jnp.zeros_like(ref)

    t = lax.broadcasted_iota(jnp.int32, (TOK, 1), 0)
    row_mask = jnp.where(j == 0, TOK - 1, CHUNK - 1)
    pos = t & row_mask
    first = pos == 0
    last_ = pos == row_mask
    cwq = cwq_ref[:, 0, :]
    cwk = cwk_ref[:, 0, :]
    feats = []
    for hh in range(B_HPS):
        feats.append((_mlstm_state(bqf, bkf, bvf, grf, gcf, cwq, cwk, first, last_, cf_ref, nf_ref, mf_ref,
                                   hh, False),
                      _mlstm_state(bqb, bkb, bvb, grb, gcb, cwq, cwk, first, last_, cb_ref, nb_ref, mb_ref,
                                   hh, True)))
    for hh, (feats_f, feats_b) in enumerate(feats):
        _mlstm_output(feats_f, of_ref, hh, False)
        _mlstm_output(feats_b, ob_ref, hh, True)


def _mlstm_scan(p3, g_rows, g_cols, conv4, n_lat):
    Bn = p3.shape[0]
    A, Bh = A_HEADS, B_HEADS
    P = B_HPS
    G = Bh // P
    q0 = 5 * A // P
    k0 = (5 * A + Bh) // P
    v0 = (5 * A + 2 * Bh) // (2 * P)

    def specs(rev):
        d = int(rev)
        tb = lambda j: _tok_block(rev, j, n_lat)
        return [pl.BlockSpec((1, TOK, P * B_DQK), lambda b, h, j: (b, tb(j), q0 + h)),
                pl.BlockSpec((1, TOK, P * B_DQK), lambda b, h, j: (b, tb(j), k0 + h)),
                pl.BlockSpec((1, TOK, P * B_DV), lambda b, h, j: (b, tb(j), v0 + h)),
                pl.BlockSpec((1, 1, P, NCH, 2, CHUNK), lambda b, h, j: (b, d, h, tb(j), 0, 0)),
                pl.BlockSpec((1, 1, P, NCH, CHUNK, 2), lambda b, h, j: (b, d, h, tb(j), 0, 0))]

    def outp(rev):
        return pl.BlockSpec((1, TOK, P * B_DV), lambda b, h, j: (b, _out_block(rev, j, n_lat), h))

    osh = jax.ShapeDtypeStruct((Bn, n_lat * TOK, Bh * B_DV), BF16)
    state = [pltpu.VMEM((P, B_DQK, B_DV), F32), pltpu.VMEM((P, 1, B_DQK), F32),
             pltpu.VMEM((P, 1, LANE), F32)]
    return pl.pallas_call(
        _mlstm_kernel,
        grid=(Bn, G, n_lat + 1),
        in_specs=specs(False) + specs(True) + [
            pl.BlockSpec((4, 1, P * B_DQK), lambda b, h, j: (0, 0, h)),
            pl.BlockSpec((4, 1, P * B_DQK), lambda b, h, j: (0, 0, G + h))],
        out_specs=[outp(False), outp(True)],
        out_shape=[osh, osh],
        scratch_shapes=state + state,
        compiler_params=_cparams(("parallel", "parallel", "arbitrary")),
        name="mlstm_scan",
    )(p3, p3, p3, g_rows, g_cols, p3, p3, p3, g_rows, g_cols, conv4, conv4)


def _readout_kernel(oaf_ref, oab_ref, obf_ref, obb_ref, ag_ref, bo_ref, na_ref, nb_ref, ya_ref, yb_ref):
    for h in range(A_HEADS):
        sl = slice(h * A_D, (h + 1) * A_D)
        o = oaf_ref[0, :, sl].astype(F32) + oab_ref[0, :, sl].astype(F32)
        ya_ref[0, :, sl] = (_rms(o, na_ref[:, sl]) * _silu(ag_ref[0, :, sl])).astype(ya_ref.dtype)
    for h in range(B_HEADS):
        sl = slice(h * B_DV, (h + 1) * B_DV)
        o = obf_ref[0, :, sl].astype(F32) + obb_ref[0, :, sl].astype(F32)
        yb_ref[0, :, sl] = (_rms(o, nb_ref[:, sl]) * _sigmoid(bo_ref[0, :, sl])).astype(yb_ref.dtype)


def _readout(oa, ob, p3, norm_a, norm_b, n_lat):
    Bn = p3.shape[0]
    T = n_lat * TOK
    wa = A_HEADS * A_D
    wb = B_HEADS * B_DV
    ag_blk = 4
    bo_blk = (5 * A_HEADS + 4 * B_HEADS) * LANE // wb
    sa = pl.BlockSpec((1, TOK, wa), lambda b, i: (b, i, 0))
    sb = pl.BlockSpec((1, TOK, wb), lambda b, i: (b, i, 0))
    return pl.pallas_call(
        _readout_kernel,
        grid=(Bn, n_lat),
        in_specs=[sa, sa, sb, sb,
                  pl.BlockSpec((1, TOK, wa), lambda b, i: (b, i, ag_blk)),
                  pl.BlockSpec((1, TOK, wb), lambda b, i: (b, i, bo_blk)),
                  pl.BlockSpec((1, wa), lambda b, i: (0, 0)),
                  pl.BlockSpec((1, wb), lambda b, i: (0, 0))],
        out_specs=[sa, sb],
        out_shape=[jax.ShapeDtypeStruct((Bn, T, wa), BF16),
                   jax.ShapeDtypeStruct((Bn, T, wb), BF16)],
        compiler_params=_cparams(("parallel", "parallel")),
        name="readout",
    )(oa[0], oa[1], ob[0], ob[1], p3, p3, norm_a.reshape(1, wa), norm_b.reshape(1, wb))


def _merge_kernel(ya_ref, yb_ref, wa_ref, wb_ref, ga_ref, gb_ref, o_ref):
    pa = jnp.dot(ya_ref[0], wa_ref[...], preferred_element_type=F32)
    pb = jnp.dot(yb_ref[0], wb_ref[...], preferred_element_type=F32)
    o_ref[0] = (_sigmoid(ga_ref[0]) * pa + _sigmoid(gb_ref[0]) * pb).astype(o_ref.dtype)


def _merge(ya, yb, wa, wb, p3, D):
    Bn, T, ka = ya.shape
    kb = yb.shape[2]
    tm = _pick(T, 1024)
    tn = _pick(D, 512)
    ga0 = 0
    gb0 = D // tn
    return pl.pallas_call(
        _merge_kernel,
        grid=(Bn, T // tm, D // tn),
        in_specs=[pl.BlockSpec((1, tm, ka), lambda b, i, j: (b, i, 0)),
                  pl.BlockSpec((1, tm, kb), lambda b, i, j: (b, i, 0)),
                  pl.BlockSpec((ka, tn), lambda b, i, j: (0, j)),
                  pl.BlockSpec((kb, tn), lambda b, i, j: (0, j)),
                  pl.BlockSpec((1, tm, tn), lambda b, i, j: (b, i, ga0 + j)),
                  pl.BlockSpec((1, tm, tn), lambda b, i, j: (b, i, gb0 + j))],
        out_specs=pl.BlockSpec((1, tm, tn), lambda b, i, j: (b, i, j)),
        out_shape=jax.ShapeDtypeStruct((Bn, T, D), BF16),
        compiler_params=_cparams(("parallel", "parallel", "arbitrary")),
        name="merge",
    )(ya, yb, wa, wb, p3, p3)


def _resid_router_kernel(x_ref, mix_ref, g1_ref, g2_ref, gate_ref, sh_ref, sc_ref, wr_ref,
                         h_ref, v_ref, aff_ref):
    h = x_ref[0] + gate_ref[0] * _rms(mix_ref[0].astype(F32), g1_ref[...])
    h_ref[0] = h
    v = _rms(h, g2_ref[...]) * (1.0 + sc_ref[0]) + sh_ref[0]
    v_ref[0] = _pack_bf16_pairs(v)
    logits = jnp.dot(v, wr_ref[...], precision=lax.Precision.HIGHEST, preferred_element_type=F32)
    lane = lax.broadcasted_iota(jnp.int32, logits.shape, 1)
    logits = jnp.where(lane < N_EXPERTS, logits, -jnp.inf)
    e = jnp.exp(logits - jnp.max(logits, axis=-1, keepdims=True))
    aff_ref[0] = e / jnp.sum(e, axis=-1, keepdims=True)


def _resid_router(x, mix, g1, g2, mod3, w_router_pad):
    Bn, T, D = x.shape
    mspec = lambda k: pl.BlockSpec((1, 1, D), lambda b, i: (b, 0, k))
    tok = pl.BlockSpec((1, TOK, D), lambda b, i: (b, i, 0))
    vec = pl.BlockSpec((1, D), lambda b, i: (0, 0))
    return pl.pallas_call(
        _resid_router_kernel,
        grid=(Bn, T // TOK),
        in_specs=[tok, tok, vec, vec, mspec(2), mspec(3), mspec(4),
                  pl.BlockSpec((D, LANE), lambda b, i: (0, 0))],
        out_specs=[tok, pl.BlockSpec((1, TOK, D // 2), lambda b, i: (b, i, 0)),
                   pl.BlockSpec((1, TOK, LANE), lambda b, i: (b, i, 0))],
        out_shape=[jax.ShapeDtypeStruct((Bn, T, D), F32),
                   jax.ShapeDtypeStruct((Bn, T, D // 2), jnp.uint32),
                   jax.ShapeDtypeStruct((Bn, T, LANE), F32)],
        compiler_params=_cparams(("parallel", "parallel")),
        name="resid_router",
    )(x, mix, g1.reshape(1, D), g2.reshape(1, D), mod3, mod3, mod3, w_router_pad)


def _select_kernel(aff_ref, slot_ref, idx_ref, offs_ref, *, cap, tk):
    E, T = aff_ref.shape[1], aff_ref.shape[2]
    nck = T // LANE

    def key():
        return lax.bitcast_convert_type(aff_ref[0], jnp.int32)

    def bit_step(i, tau):
        cand = tau | jnp.left_shift(jnp.int32(1), 30 - i)
        cnt = jnp.sum(jnp.where(key() >= cand, 1.0, 0.0), axis=1, keepdims=True)
        return jnp.where(cnt >= cap, cand, tau)

    tau = lax.fori_loop(0, 31, bit_step, jnp.zeros((E, 1), jnp.int32))
    kk = key()
    gt = kk > tau
    eq = kk == tau
    need = cap - jnp.sum(jnp.where(gt, 1.0, 0.0), axis=1, keepdims=True)

    r = lax.broadcasted_iota(jnp.int32, (LANE, LANE), 0)
    c = lax.broadcasted_iota(jnp.int32, (LANE, LANE), 1)
    upper = jnp.where(r < c, 1.0, 0.0).astype(BF16)
    tr = jnp.right_shift(lax.broadcasted_iota(jnp.int32, (T, LANE), 0), LANE.bit_length() - 1)
    member = jnp.where(tr == lax.broadcasted_iota(jnp.int32, (T, LANE), 1), 1.0, 0.0).astype(BF16)

    def prefix(flags):
        x = jnp.where(flags, 1.0, 0.0).astype(BF16)
        tot = jnp.dot(x, member, preferred_element_type=F32)
        offs = jnp.dot(tot.astype(BF16), upper, preferred_element_type=F32)
        parts = [jnp.dot(x[:, j * LANE:(j + 1) * LANE], upper, preferred_element_type=F32)
                 + offs[:, j:j + 1] for j in range(nck)]
        return jnp.concatenate(parts, axis=1), offs

    tie_rank, _ = prefix(eq)
    sel = gt | (eq & (tie_rank < need))
    pos, offs = prefix(sel)
    slot = jnp.where(sel, pos, -1.0)
    slot_ref[0] = slot
    offs_ref[0] = offs.astype(jnp.int32)

    tt = lax.broadcasted_iota(jnp.int32, (8, T), 1)
    rr = lax.broadcasted_iota(jnp.int32, (8, T), 0)
    digits = jnp.where(rr == 0, jnp.right_shift(tt, 6), jnp.where(rr == 1, tt & 63, 0))
    digits = digits.astype(F32).astype(BF16)
    slot_iota = lax.broadcasted_iota(jnp.int32, (cap, tk), 0).astype(F32)
    for e in range(E):
        acc = jnp.zeros((8, cap), F32)
        for kc in range(T // tk):
            ks = slice(kc * tk, (kc + 1) * tk)
            oh = jnp.where(slot[e:e + 1, ks] == slot_iota, 1.0, 0.0).astype(BF16)
            acc = acc + lax.dot_general(digits[:, ks], oh, _NT, preferred_element_type=F32)
        idx_ref[0, e:e + 1, :] = (acc[0:1] * 64.0 + acc[1:2]).astype(jnp.int32)


def _select(aff_rows, cap):
    Bn, E, T = aff_rows.shape
    assert T // LANE < LANE and T <= 4096
    tk = _pick(T, 1024)
    row = lambda n: pl.BlockSpec((1, E, n), lambda b: (b, 0, 0))
    return pl.pallas_call(
        functools.partial(_select_kernel, cap=cap, tk=tk),
        grid=(Bn,),
        in_specs=[row(T)],
        out_specs=[row(T), row(cap), row(LANE)],
        out_shape=[jax.ShapeDtypeStruct((Bn, E, T), F32),
                   jax.ShapeDtypeStruct((Bn, E, cap), jnp.int32),
                   jax.ShapeDtypeStruct((Bn, E, LANE), jnp.int32)],
        compiler_params=_cparams(("parallel",)),
        name="ec_select",
    )(aff_rows)


def _pack_bf16_pairs(x):
    half = x.shape[1] // 2
    bits = lax.bitcast_convert_type(x.astype(BF16).astype(F32), jnp.uint32)
    return bits[:, half:] | (bits[:, :half] >> 16)


def _unpack_bf16_pairs(p):
    lo = lax.bitcast_convert_type(p << 16, F32).astype(BF16)
    hi = lax.bitcast_convert_type(p & jnp.uint32(0xFFFF0000), F32).astype(BF16)
    return lo, hi


def _gather_kernel(idx_ref, v_hbm, o_ref, buf, sem, *, cap, n_tok):
    b = pl.program_id(0)
    e = pl.program_id(1)
    base = (b * N_EXPERTS + e) * cap

    def row_copy(r, tok):
        return pltpu.make_async_copy(v_hbm.at[pl.ds(b * n_tok + tok, 1), :], buf.at[pl.ds(r, 1), :], sem)

    def issue(p, carry):
        for prio in range(2):
            r = 2 * p + prio
            row_copy(r, idx_ref[base + r]).start(priority=prio)
        return carry

    def drain(r, carry):
        row_copy(r, 0).wait()
        return carry

    lax.fori_loop(0, cap // 2, issue, 0)
    lax.fori_loop(0, cap, drain, 0)
    half = buf.shape[1]
    lo, hi = _unpack_bf16_pairs(buf[...])
    o_ref[0, 0, :, :half] = lo
    o_ref[0, 0, :, half:] = hi


def _gather(idx, v_pairs, cap):
    Bn, T, half = v_pairs.shape
    D = 2 * half
    return pl.pallas_call(
        functools.partial(_gather_kernel, cap=cap, n_tok=T),
        grid_spec=pltpu.PrefetchScalarGridSpec(
            num_scalar_prefetch=1,
            grid=(Bn, N_EXPERTS),
            in_specs=[pl.BlockSpec(memory_space=pl.ANY)],
            out_specs=pl.BlockSpec((1, 1, cap, D), lambda b, e, idx_ref: (e, b, 0, 0)),
            scratch_shapes=[pltpu.VMEM((cap, half), jnp.uint32), pltpu.SemaphoreType.DMA(())]),
        out_shape=jax.ShapeDtypeStruct((N_EXPERTS, Bn, cap, D), BF16),
        compiler_params=_cparams(("arbitrary", "arbitrary")),
        name="ec_gather",
    )(idx.reshape(-1), v_pairs.reshape(Bn * T, half))


def _ffn1_kernel(x_ref, wg_ref, wu_ref, o_ref):
    x = x_ref[0]
    g = jnp.dot(x, wg_ref[0].astype(BF16), preferred_element_type=F32)
    u = jnp.dot(x, wu_ref[0].astype(BF16), preferred_element_type=F32)
    o_ref[0] = (_silu(g) * u).astype(o_ref.dtype)


def _ffn1(xg, wg, wu):
    E, M, D = xg.shape
    F = wg.shape[2]
    tm = _pick(M, 1024)
    tn = _pick(F, 256)
    return pl.pallas_call(
        _ffn1_kernel,
        grid=(E, M // tm, F // tn),
        in_specs=[pl.BlockSpec((1, tm, D), lambda e, i, j: (e, i, 0)),
                  pl.BlockSpec((1, D, tn), lambda e, i, j: (e, 0, j)),
                  pl.BlockSpec((1, D, tn), lambda e, i, j: (e, 0, j))],
        out_specs=pl.BlockSpec((1, tm, tn), lambda e, i, j: (e, i, j)),
        out_shape=jax.ShapeDtypeStruct((E, M, F), BF16),
        compiler_params=_cparams(("parallel", "parallel", "arbitrary")),
        name="ec_ffn_up",
    )(xg, wg, wu)


def _ffn2_kernel(h_ref, wd_ref, o_ref):
    o_ref[0] = jnp.dot(h_ref[0], wd_ref[0].astype(BF16),
                       preferred_element_type=F32).astype(o_ref.dtype)


def _ffn2(hid, wd):
    E, M, F = hid.shape
    D = wd.shape[2]
    tn = _pick(D, 512)
    return pl.pallas_call(
        _ffn2_kernel,
        grid=(E, D // tn),
        in_specs=[pl.BlockSpec((1, M, F), lambda e, j: (e, 0, 0)),
                  pl.BlockSpec((1, F, tn), lambda e, j: (e, 0, j))],
        out_specs=pl.BlockSpec((1, M, tn), lambda e, j: (e, 0, j)),
        out_shape=jax.ShapeDtypeStruct((E, M, D), BF16),
        compiler_params=_cparams(("parallel", "arbitrary")),
        name="ec_ffn_down",
    )(hid, wd)


def _scatter_kernel(offs_ref, slot_ref, aff_ref, y_ref, h_ref, g_ref, gate_ref, o_ref, *, cap, win):
    b = pl.program_id(0)
    t = pl.program_id(1)
    e = pl.program_id(2)
    tq = o_ref.shape[1]

    @pl.when(e == 0)
    def _():
        o_ref[...] = jnp.zeros_like(o_ref)

    base = (b * N_EXPERTS + e) * LANE
    lo = offs_ref[base + t * (tq // LANE)]
    hi = offs_ref[base + (t + 1) * (tq // LANE)]
    sl = jnp.broadcast_to(slot_ref[0, 0], (LANE, tq)).T
    af = jnp.broadcast_to(aff_ref[0, 0], (LANE, tq)).T
    sl = jnp.concatenate([sl] * (win // LANE), axis=1)
    af = jnp.concatenate([af] * (win // LANE), axis=1)
    lane = lax.broadcasted_iota(jnp.int32, (tq, win), 1).astype(F32)
    for w0 in range(0, cap, win):
        @pl.when((lo < w0 + win) & (hi > w0))
        def _():
            ohw = jnp.where(sl == lane + float(w0), af, 0.0).astype(BF16)
            o_ref[0] += jnp.dot(ohw, y_ref[0, 0, w0:w0 + win, :], preferred_element_type=F32)

    @pl.when(e == pl.num_programs(2) - 1)
    def _():
        o_ref[0] = h_ref[0] + gate_ref[0] * _rms(o_ref[0], g_ref[...])


def _scatter_combine(offs, slot_rows, aff_rows, y, cap, h, g3, mod3):
    E, Bn, _, D = y.shape
    T = slot_rows.shape[3]
    tq = _pick(T, 512)
    win = _pick(cap, 256)
    assert win % LANE == 0 and tq % LANE == 0
    row = pl.BlockSpec((1, 1, 1, tq), lambda b, t, e, offs_ref: (b, e, 0, t))
    return pl.pallas_call(
        functools.partial(_scatter_kernel, cap=cap, win=win),
        grid_spec=pltpu.PrefetchScalarGridSpec(
            num_scalar_prefetch=1,
            grid=(Bn, T // tq, E),
            in_specs=[row, row,
                      pl.BlockSpec((1, 1, cap, D), lambda b, t, e, offs_ref: (e, b, 0, 0)),
                      pl.BlockSpec((1, tq, D), lambda b, t, e, offs_ref: (b, t, 0)),
                      pl.BlockSpec((1, D), lambda b, t, e, offs_ref: (0, 0)),
                      pl.BlockSpec((1, 1, D), lambda b, t, e, offs_ref: (b, 0, 5))],
            out_specs=pl.BlockSpec((1, tq, D), lambda b, t, e, offs_ref: (b, t, 0))),
        out_shape=jax.ShapeDtypeStruct((Bn, T, D), F32),
        compiler_params=_cparams(("parallel", "parallel", "arbitrary")),
        name="ec_scatter",
    )(offs.reshape(-1), slot_rows, aff_rows, y, h, g3.reshape(1, D), mod3)


def _layer(h_lat, ctx, c8, l, lb_l, w_ada, b_ada, g_norm, w_in, hgrn_f_bias, hgrn_norm,
           mlstm_conv_w, mlstm_conv_b, mlstm_gate_b, mlstm_norm, w_branch_a, w_branch_b, w_out,
           w_router, w_expert_gate, w_expert_up, w_expert_down):
    Bn, T, D = h_lat.shape
    n_lat = T // TOK
    n_tok = T + TOK
    A, Bh = A_HEADS, B_HEADS
    g0 = (5 * A + 6 * Bh) * LANE

    mod = _modulation(c8, w_ada[l], b_ada[l])
    mod3 = mod.reshape(8, 1, N_MOD * D)

    u = _prenorm(h_lat, ctx, g_norm[l, 0], mod3)

    w_in_t = jnp.swapaxes(w_in, 1, 2)
    w_merge_t, w_gate_t = _wsplit(w_in_t, l, g0, 4 * Bh, 2 * D)
    gate_bias = jnp.pad(mlstm_gate_b[l].reshape(4 * Bh), (0, LANE - 4 * Bh))
    u2 = u.reshape(Bn * n_tok, D)
    p3 = _matmul_t(u2, w_in_t, F32, n_cols=g0, layer=l, name="in_proj").reshape(Bn, n_tok, g0)
    p_merge = _matmul_t(u2, w_merge_t, F32, name="in_proj_merge").reshape(Bn, n_tok, 2 * D)
    gates = _matmul_t(u2, w_gate_t, F32, tn=LANE, bias=gate_bias, name="in_proj_gates")

    g6 = gates[:, :4 * Bh].reshape(Bn, n_tok // CHUNK, CHUNK, 2, 2, Bh)
    g_rows = g6.transpose(0, 3, 5, 1, 4, 2)
    g_cols = g6.transpose(0, 3, 5, 1, 2, 4)
    conv4 = jnp.concatenate([mlstm_conv_w[l], mlstm_conv_b[l][None]], axis=0)
    conv4 = conv4.reshape(4, 1, 2 * Bh * B_DQK)

    oa = _hgrn_scan(p3, hgrn_f_bias[l], lb_l, n_lat)
    ob = _mlstm_scan(p3, g_rows, g_cols, conv4, n_lat)
    ya, yb = _readout(oa, ob, p3, hgrn_norm[l], mlstm_norm[l], n_lat)
    merged = _merge(ya, yb, w_branch_a[l].astype(BF16), w_branch_b[l].astype(BF16), p_merge, D)
    mix = _matmul(merged.reshape(Bn * T, D), w_out[l].astype(BF16), BF16, name="out_proj")

    w_router_pad = jnp.pad(w_router[l], ((0, 0), (0, LANE - N_EXPERTS)))
    h_lat, v_lat, aff = _resid_router(h_lat, mix.reshape(Bn, T, D), g_norm[l, 1], g_norm[l, 2],
                                      mod3, w_router_pad)

    cap = CAPACITY * T // N_EXPERTS
    aff_rows = aff[:, :, :N_EXPERTS].transpose(0, 2, 1)
    slot, idx, offs = _select(aff_rows, cap)
    xg = _gather(idx, v_lat, cap)
    hid = _ffn1(xg.reshape(N_EXPERTS, Bn * cap, D), w_expert_gate[l], w_expert_up[l])
    y = _ffn2(hid, w_expert_down[l]).reshape(N_EXPERTS, Bn, cap, D)
    return _scatter_combine(offs, slot.reshape(Bn, N_EXPERTS, 1, T),
                            aff_rows.reshape(Bn, N_EXPERTS, 1, T), y, cap, h_lat, g_norm[l, 3], mod3)


def kernel(x, c, ctx, c_ctx, w_ada, b_ada, g_norm, w_in, hgrn_f_bias, hgrn_lb, hgrn_norm,
           mlstm_conv_w, mlstm_conv_b, mlstm_gate_b, mlstm_norm, w_branch_a, w_branch_b, w_out,
           w_router, w_expert_gate, w_expert_up, w_expert_down):
    Bn, T, D = x.shape
    depth = w_ada.shape[0]
    assert depth == 1, "context outputs are only produced for the state hand-off (single layer)"
    assert ctx.shape[1] == TOK and T % TOK == 0 and Bn < 8
    lb_all = jnp.cumsum(jax.nn.softmax(hgrn_lb.astype(F32), axis=1), axis=1)
    c8 = jnp.zeros((8, D), F32).at[:Bn].set(c).at[Bn].set(c_ctx)
    h_lat = x
    for l in range(depth):
        h_lat = _layer(h_lat, ctx, c8, l, lb_all[:, l], w_ada, b_ada, g_norm, w_in, hgrn_f_bias,
                       hgrn_norm, mlstm_conv_w, mlstm_conv_b, mlstm_gate_b, mlstm_norm,
                       w_branch_a, w_branch_b, w_out, w_router, w_expert_gate, w_expert_up,
                       w_expert_down)
    return h_lat.astype(x.dtype)
```

```python
import functools

import jax
import jax.numpy as jnp
from jax import lax
from jax.experimental import pallas as pl
from jax.experimental.pallas import tpu as pltpu

F32 = jnp.float32
BF16 = jnp.bfloat16

EPS = 1e-6
N_MOD = 6
A_HEADS = 16
A_D = 128
B_HEADS = 8
B_DQK = 128
B_DV = 256
N_EXPERTS = 16
CAPACITY = 2
CHUNK = 64
SUB = 16
A_HPS = 4
B_HPS = 4
TOK = 256
NCH = TOK // CHUNK
NSUB = CHUNK // SUB
LANE = 128
VMEM_LIMIT = 56 * 1024 * 1024

_NT = (((1,), (1,)), ((), ()))


def _pick(n, pref):
    t = min(n, pref)
    while n % t:
        t //= 2
    return t


def _cparams(sem):
    return pltpu.CompilerParams(dimension_semantics=sem, vmem_limit_bytes=VMEM_LIMIT)


def _silu(x):
    return x / (1.0 + jnp.exp(-x))


def _sigmoid(x):
    return 1.0 / (1.0 + jnp.exp(-x))


def _sig_pair(z):
    t = jnp.exp(-jnp.abs(z))
    r = 1.0 / (1.0 + t)
    tr = t * r
    pos = z >= 0
    return jnp.where(pos, r, tr), jnp.where(pos, tr, r)


def _log_sigmoid(x):
    return jnp.minimum(x, 0.0) - jnp.log(1.0 + jnp.exp(-jnp.abs(x)))


def _rms(xf, w):
    return xf * lax.rsqrt(jnp.mean(xf * xf, axis=-1, keepdims=True) + EPS) * w


def _rows(x, r, n):
    return jnp.broadcast_to(x[r:r + 1], (n, x.shape[1]))


def _mod_kernel(c_ref, w_ref, b_ref, o_ref):
    a = _silu(c_ref[...]).astype(BF16)
    o_ref[...] = jnp.dot(a, w_ref[...].astype(BF16), preferred_element_type=F32) + b_ref[...]


def _modulation(c8, w, b):
    D, N = w.shape
    tn = _pick(N, 512)
    return pl.pallas_call(
        _mod_kernel,
        grid=(N // tn,),
        in_specs=[pl.BlockSpec((8, D), lambda j: (0, 0)),
                  pl.BlockSpec((D, tn), lambda j: (0, j)),
                  pl.BlockSpec((1, tn), lambda j: (0, j))],
        out_specs=pl.BlockSpec((8, tn), lambda j: (0, j)),
        out_shape=jax.ShapeDtypeStruct((8, N), F32),
        compiler_params=_cparams(("parallel",)),
        name="adaln_mod",
    )(c8, w, b.reshape(1, N))


def _prenorm_kernel(x_ref, ctx_ref, g_ref, sh_ref, sc_ref, o_ref, *, n_lat):
    i = pl.program_id(1)
    g = g_ref[...]
    sh = sh_ref[0]
    sc = sc_ref[0]

    @pl.when(i < n_lat)
    def _():
        o_ref[0] = (_rms(x_ref[0], g) * (1.0 + sc) + sh).astype(o_ref.dtype)

    @pl.when(i == n_lat)
    def _():
        o_ref[0] = (_rms(ctx_ref[0], g) * (1.0 + sc) + sh).astype(o_ref.dtype)


def _prenorm(x, ctx, g, mod3):
    Bn, T, D = x.shape
    n_lat = T // TOK
    row = lambda b, i: jnp.where(i == n_lat, Bn, b)
    return pl.pallas_call(
        functools.partial(_prenorm_kernel, n_lat=n_lat),
        grid=(Bn, n_lat + 1),
        in_specs=[pl.BlockSpec((1, TOK, D), lambda b, i: (b, jnp.minimum(i, n_lat - 1), 0)),
                  pl.BlockSpec((1, TOK, D), lambda b, i: (b, 0, 0)),
                  pl.BlockSpec((1, D), lambda b, i: (0, 0)),
                  pl.BlockSpec((1, 1, D), lambda b, i: (row(b, i), 0, 0)),
                  pl.BlockSpec((1, 1, D), lambda b, i: (row(b, i), 0, 1))],
        out_specs=pl.BlockSpec((1, TOK, D), lambda b, i: (b, i, 0)),
        out_shape=jax.ShapeDtypeStruct((Bn, T + TOK, D), BF16),
        compiler_params=_cparams(("parallel", "arbitrary")),
        name="prenorm",
    )(x, ctx, g.reshape(1, D), mod3, mod3)


_NN = (((1,), (0,)), ((), ()))


def _mm_kernel(a_ref, b_ref, o_ref, *, dims):
    o_ref[...] = lax.dot_general(a_ref[...], b_ref[...].astype(BF16), dims,
                                 preferred_element_type=F32).astype(o_ref.dtype)


def _mm_bias_kernel(a_ref, b_ref, bias_ref, o_ref, *, dims):
    o_ref[...] = (lax.dot_general(a_ref[...], b_ref[...].astype(BF16), dims, preferred_element_type=F32)
                  + bias_ref[...]).astype(o_ref.dtype)


def _matmul(a, b, out_dtype, tm=1024, tn=512, bias=None, name="matmul"):
    M, K = a.shape
    N = b.shape[1]
    tm = _pick(M, tm)
    tn = _pick(N, tn)
    return _matmul_call(a, b, pl.BlockSpec((K, tn), lambda i, j: (0, j)), _NN, N, out_dtype, tm, tn,
                        bias, name)


def _matmul_t(a, bt, out_dtype, tm=1024, tn=512, bias=None, n_cols=None, layer=None, name="matmul_t"):
    M, K = a.shape
    N = bt.shape[-2] if n_cols is None else n_cols
    tm = _pick(M, tm)
    tn = _pick(N, tn)
    if layer is None:
        b_spec = pl.BlockSpec((tn, K), lambda i, j: (j, 0))
    else:
        b_spec = pl.BlockSpec((None, tn, K), lambda i, j: (layer, j, 0))
    return _matmul_call(a, bt, b_spec, _NT, N, out_dtype, tm, tn, bias, name)


def _matmul_call(a, b, b_spec, dims, N, out_dtype, tm, tn, bias, name):
    M, K = a.shape
    in_specs = [pl.BlockSpec((tm, K), lambda i, j: (i, 0)), b_spec]
    args = [a, b]
    kern = functools.partial(_mm_kernel, dims=dims)
    if bias is not None:
        in_specs.append(pl.BlockSpec((1, tn), lambda i, j: (0, j)))
        args.append(bias.reshape(1, N))
        kern = functools.partial(_mm_bias_kernel, dims=dims)
    return pl.pallas_call(
        kern,
        grid=(M // tm, N // tn),
        in_specs=in_specs,
        out_specs=pl.BlockSpec((tm, tn), lambda i, j: (i, j)),
        out_shape=jax.ShapeDtypeStruct((M, N), out_dtype),
        compiler_params=_cparams(("parallel", "arbitrary")),
        name=name,
    )(*args)


def _matmul_t_rows(a3, bt, n_rows, out_dtype, tm=1024, tn=512, name="matmul_t_rows"):
    Bn, _, K = a3.shape
    N = bt.shape[0]
    tm = _pick(n_rows, tm)
    tn = _pick(N, tn)
    return pl.pallas_call(
        functools.partial(_mm_kernel, dims=_NT),
        grid=(Bn, n_rows // tm, N // tn),
        in_specs=[pl.BlockSpec((None, tm, K), lambda b, i, j: (b, i, 0)),
                  pl.BlockSpec((tn, K), lambda b, i, j: (j, 0))],
        out_specs=pl.BlockSpec((None, tm, tn), lambda b, i, j: (b, i, j)),
        out_shape=jax.ShapeDtypeStruct((Bn, n_rows, N), out_dtype),
        compiler_params=_cparams(("parallel", "parallel", "arbitrary")),
        name=name,
    )(a3, bt)


def _wsplit_kernel(a_ref, b_ref, o_ref, g_ref, *, shift):
    j = pl.program_id(1)
    tn = o_ref.shape[0]
    cat = jnp.concatenate([a_ref[...], b_ref[...]], axis=0)
    o_ref[...] = cat[shift:shift + tn].astype(o_ref.dtype)

    @pl.when(j == 0)
    def _():
        row = lax.broadcasted_iota(jnp.int32, g_ref.shape, 0)
        g_ref[...] = jnp.where(row < shift, cat[:LANE], 0.0).astype(g_ref.dtype)


def _wsplit(wt, layer, row0, shift, n):
    K = wt.shape[2]
    tk = _pick(K, 1024)
    tn = _pick(n, 512)
    assert row0 % tn == 0 and 0 < shift < LANE and shift % 8 == 0 and row0 + shift + n <= wt.shape[1]
    r0 = row0 // tn
    return pl.pallas_call(
        functools.partial(_wsplit_kernel, shift=shift),
        grid=(K // tk, n // tn),
        in_specs=[pl.BlockSpec((None, tn, tk), lambda i, j: (layer, r0 + j, i)),
                  pl.BlockSpec((None, tn, tk), lambda i, j: (layer, r0 + j + 1, i))],
        out_specs=[pl.BlockSpec((tn, tk), lambda i, j: (j, i)),
                   pl.BlockSpec((LANE, tk), lambda i, j: (0, i))],
        out_shape=[jax.ShapeDtypeStruct((n, K), BF16), jax.ShapeDtypeStruct((LANE, K), BF16)],
        compiler_params=_cparams(("parallel", "arbitrary")),
        name="w_split",
    )(wt, wt)


def _tok_block(rev, j, n_lat):
    lat = n_lat - j if rev else j - 1
    return jnp.where(j == 0, n_lat, lat)


def _out_block(rev, j, n_lat):
    return jnp.clip(n_lat - j if rev else j - 1, 0, n_lat - 1)


def _chunk_masks(rev):
    row = lax.broadcasted_iota(jnp.int32, (CHUNK, CHUNK), 0)
    col = lax.broadcasted_iota(jnp.int32, (CHUNK, CHUNK), 1)
    return (col >= row, col <= row) if rev else (col <= row, col >= row)


def _hgrn_consts():
    t = jnp.arange(TOK)
    same = (t[:, None] // CHUNK) == (t[None, :] // CHUNK)
    sub_r, sub_c = t[:, None] // SUB, t[None, :] // SUB
    fwd = jnp.concatenate([same & (t[None, :] <= t[:, None]), same & (sub_c < sub_r)], axis=0)
    bwd = jnp.concatenate([same & (t[None, :] >= t[:, None]), same & (sub_c > sub_r)], axis=0)
    masks = jnp.stack([fwd, bwd]).astype(BF16)
    r = jnp.arange(SUB * A_D)
    emat = ((r[:, None] // A_D) == (jnp.arange(LANE)[None, :] % SUB)).astype(BF16)
    return masks, emat


def _hgrn_state(af, v, fb, lb, mask_ref, st_ref, rev):
    z = af + fb
    sp, sn = _sig_pair(z)
    k = (1.0 - lb) * sn
    lf2 = jnp.log2(lb + (1.0 - lb) * sp)
    lk2 = jnp.log2(k)

    hi = lf2.astype(BF16)
    r1 = lf2 - hi.astype(F32)
    mid = r1.astype(BF16)
    lo = (r1 - mid.astype(F32)).astype(BF16)
    cs = jnp.dot(mask_ref[0], jnp.concatenate([hi, mid, lo], axis=1), preferred_element_type=F32)
    cs = cs[:, :A_D] + cs[:, A_D:2 * A_D] + cs[:, 2 * A_D:]
    b = cs[:TOK]
    ent = cs[TOK:]
    c_all = lk2 - b

    last = [(c * CHUNK if rev else c * CHUNK + CHUNK - 1) for c in range(NCH)]
    tot_rows = jnp.concatenate([_rows(b, last[c], CHUNK) for c in range(NCH)], axis=0)
    khat = jnp.exp2(jnp.minimum(tot_rows + c_all, lk2)).astype(BF16)
    upd = [jnp.dot(v[c * CHUNK:(c + 1) * CHUNK].T.astype(BF16), khat[c * CHUNK:(c + 1) * CHUNK],
                   preferred_element_type=F32) for c in range(NCH)]

    order = range(NCH - 1, -1, -1) if rev else range(NCH)
    st = st_ref[...]
    st_in = [None] * NCH
    for c in order:
        st_in[c] = st
        st = st * jnp.exp2(b[last[c]:last[c] + 1]) + upd[c]
    st_ref[...] = st
    return v, b, ent, c_all, lk2, st_in


def _hgrn_output(aq, feats, emat_ref, o_ref, hs, rev):
    v, b, ent, c_all, lk2, st_in = feats
    q = _silu(aq)
    qb = q.astype(BF16)
    vb = v.astype(BF16)
    qt = (q * jnp.exp2(b - ent)).astype(BF16)
    row = lax.broadcasted_iota(jnp.int32, (CHUNK, CHUNK), 0)
    col = lax.broadcasted_iota(jnp.int32, (CHUNK, CHUNK), 1)
    rs = jnp.right_shift(row, SUB.bit_length() - 1)
    cs_ = jnp.right_shift(col, SUB.bit_length() - 1)
    prev_blk = (cs_ > rs) if rev else (cs_ < rs)
    diag_blk = (rs == cs_) & ((col >= row) if rev else (col <= row))

    off = [[None] * NSUB for _ in range(NCH)]
    for i in range(NSUB):
        e_i = jnp.concatenate([_rows(ent, c * CHUNK + i * SUB, CHUNK) for c in range(NCH)], axis=0)
        kt = jnp.exp2(jnp.minimum(e_i + c_all, lk2)).astype(BF16)
        for c in range(NCH):
            r0 = c * CHUNK + i * SUB
            off[c][i] = lax.dot_general(qt[r0:r0 + SUB], kt[c * CHUNK:(c + 1) * CHUNK], _NT,
                                        preferred_element_type=F32)

    half = SUB // 2
    nsb = TOK // SUB
    zeros = jnp.zeros((half, A_D), F32)
    ps = []
    for sl in range(SUB):
        n = half if (sl < half if rev else sl >= half) else SUB
        r0 = SUB - n if not rev else 0
        bq = b if n == SUB else jnp.concatenate(
            [b[g * SUB + r0:g * SUB + r0 + n] for g in range(nsb)], axis=0)
        cref = jnp.concatenate([_rows(c_all, g * SUB + sl, n) for g in range(nsb)], axis=0)
        kref = jnp.concatenate([_rows(lk2, g * SUB + sl, n) for g in range(nsb)], axis=0)
        e = jnp.exp2(jnp.minimum(bq + cref, kref))
        if n != SUB:
            parts = []
            for g in range(nsb):
                piece = e[g * half:(g + 1) * half]
                parts += [piece, zeros] if rev else [zeros, piece]
            e = jnp.concatenate(parts, axis=0)
        ps.append(e.astype(BF16) * qb)
    acc = jnp.dot(jnp.concatenate(ps, axis=1), emat_ref[...], preferred_element_type=F32)

    qhat = (q * jnp.exp2(b)).astype(BF16)
    for c in range(NCH):
        sl_c = slice(c * CHUNK, (c + 1) * CHUNK)
        attn = (jnp.where(prev_blk, jnp.concatenate(off[c], axis=0), 0.0)
                + jnp.where(diag_blk, acc[sl_c, :CHUNK], 0.0))
        o = jnp.dot(attn.astype(BF16), vb[sl_c], preferred_element_type=F32)
        o = o + lax.dot_general(qhat[sl_c], st_in[c].astype(BF16), _NT, preferred_element_type=F32)
        o_ref[0, sl_c, hs] = o.astype(o_ref.dtype)


def _hgrn_kernel(aqf, aff, aif, aqb, afb, aib, fb_ref, lb_ref, mask_ref, emat_ref, of_ref, ob_ref,
                 stf_ref, stb_ref):
    j = pl.program_id(2)

    @pl.when(j == 0)
    def _():
        stf_ref[...] = jnp.zeros_like(stf_ref)
        stb_ref[...] = jnp.zeros_like(stb_ref)

    feats = []
    for hh in range(A_HPS):
        hs = slice(hh * A_D, (hh + 1) * A_D)
        feats.append((hs,
                      _hgrn_state(aff[0, :, hs], aif[0, :, hs], fb_ref[0][:, hs], lb_ref[0][:, hs],
                                  mask_ref.at[0:1], stf_ref.at[hh], False),
                      _hgrn_state(afb[0, :, hs], aib[0, :, hs], fb_ref[1][:, hs], lb_ref[1][:, hs],
                                  mask_ref.at[1:2], stb_ref.at[hh], True)))
    for hs, feats_f, feats_b in feats:
        _hgrn_output(aqf[0, :, hs], feats_f, emat_ref, of_ref, hs, False)
        _hgrn_output(aqb[0, :, hs], feats_b, emat_ref, ob_ref, hs, True)


def _hgrn_scan(p3, f_bias, lb, n_lat):
    Bn = p3.shape[0]
    A = A_HEADS
    masks, emat = _hgrn_consts()

    W = A_HPS * A_D
    G = A // A_HPS

    def feat(rev, grp):
        return pl.BlockSpec((1, TOK, W), lambda b, h, j: (b, _tok_block(rev, j, n_lat), grp * G + h))

    def outp(rev):
        return pl.BlockSpec((1, TOK, W), lambda b, h, j: (b, _out_block(rev, j, n_lat), h))

    par = pl.BlockSpec((2, 1, W), lambda b, h, j: (0, 0, h))
    osh = jax.ShapeDtypeStruct((Bn, n_lat * TOK, A * A_D), BF16)
    return pl.pallas_call(
        _hgrn_kernel,
        grid=(Bn, G, n_lat + 1),
        in_specs=[feat(False, 0), feat(False, 1), feat(False, 3),
                  feat(True, 0), feat(True, 2), feat(True, 3),
                  par, par,
                  pl.BlockSpec((2, 2 * TOK, TOK), lambda b, h, j: (0, 0, 0)),
                  pl.BlockSpec((SUB * A_D, LANE), lambda b, h, j: (0, 0))],
        out_specs=[outp(False), outp(True)],
        out_shape=[osh, osh],
        scratch_shapes=[pltpu.VMEM((A_HPS, A_D, A_D), F32), pltpu.VMEM((A_HPS, A_D, A_D), F32)],
        compiler_params=_cparams(("parallel", "parallel", "arbitrary")),
        name="hgrn2_scan",
    )(p3, p3, p3, p3, p3, p3, f_bias.reshape(2, 1, A * A_D), lb.reshape(2, 1, A * A_D), masks, emat)


def _mlstm_state(bq_ref, bk_ref, bv_ref, gr_ref, gc_ref, cwq, cwk, first, last_, c_ref, n_ref, m_ref,
                 hh, rev):
    def conv(u, cw):
        up = jnp.where(first, 0.0, pltpu.roll(u, 1, 0))
        dn = jnp.where(last_, 0.0, pltpu.roll(u, TOK - 1, 0))
        return _silu(cw[3:4] + up * cw[0:1] + u * cw[1:2] + dn * cw[2:3])

    hs = slice(hh * B_DQK, (hh + 1) * B_DQK)
    vs = slice(hh * B_DV, (hh + 1) * B_DV)
    c_ref, n_ref, m_ref = c_ref.at[hh], n_ref.at[hh], m_ref.at[hh]
    q_all = conv(bq_ref[0, :, hs], cwq[:, hs])
    k_all = conv(bk_ref[0, :, hs], cwk[:, hs]) * (B_DQK ** -0.5)
    seen, seen_t = _chunk_masks(rev)

    pre = []
    for c in range(NCH):
        sl_c = slice(c * CHUNK, (c + 1) * CHUNK)
        k = k_all[sl_c]
        vb = bv_ref[0, sl_c, vs].astype(BF16)
        g_r = gr_ref[0, 0, hh, c]
        g_c = gc_ref[0, 0, hh, c]
        ii_r = g_r[0:1]
        lf_r = _log_sigmoid(g_r[1:2])
        ii_c = g_c[:, 0:1]
        lf_c = _log_sigmoid(g_c[:, 1:2])
        b_c = jnp.sum(jnp.where(seen, lf_r, 0.0), axis=1, keepdims=True)
        b_r = jnp.sum(jnp.where(seen_t, lf_c, 0.0), axis=0, keepdims=True)
        total = jnp.sum(lf_r, axis=1, keepdims=True)
        logs = total - b_c + ii_c
        ms = jnp.max(logs, axis=0, keepdims=True)
        kw = k * jnp.exp(logs - ms)
        upd = jnp.dot(kw.T.astype(BF16), vb, preferred_element_type=F32)
        nupd = jnp.sum(kw, axis=0, keepdims=True)
        pre.append((vb, ii_r, b_c, b_r, total, ms, upd, nupd))

    cmat, nvec, m = c_ref[...], n_ref[...], m_ref[:, 0:1]
    st_in = [None] * NCH
    for c in (range(NCH - 1, -1, -1) if rev else range(NCH)):
        _, _, _, _, total, ms, upd, nupd = pre[c]
        st_in[c] = (cmat, nvec, m)
        m_new = jnp.maximum(total + m, ms)
        dec = jnp.exp(total + m - m_new)
        sc = jnp.exp(ms - m_new)
        cmat = dec * cmat + sc * upd
        nvec = dec * nvec + sc * nupd
        m = m_new
    c_ref[...] = cmat
    n_ref[...] = nvec
    m_ref[...] = jnp.broadcast_to(m, m_ref.shape)
    return q_all, k_all, pre, st_in


def _mlstm_output(feats, o_ref, hh, rev):
    q_all, k_all, pre, st_in = feats
    seen, _ = _chunk_masks(rev)
    for c in range(NCH):
        sl_c = slice(c * CHUNK, (c + 1) * CHUNK)
        vb, ii_r, b_c, b_r, _, _, _, _ = pre[c]
        cm, nv, m0 = st_in[c]
        q = q_all[sl_c]
        qb = q.astype(BF16)
        kb = k_all[sl_c].astype(BF16)
        logw = jnp.where(seen, b_c - b_r + ii_r, -jnp.inf)
        mw = jnp.max(logw, axis=1, keepdims=True)
        qk = lax.dot_general(qb, kb, _NT, preferred_element_type=F32) * jnp.exp(logw - mw)
        num0 = jnp.dot(qk.astype(BF16), vb, preferred_element_type=F32)
        den0 = jnp.sum(qk, axis=1, keepdims=True)
        log_inter = b_c + m0
        m_t = jnp.maximum(mw, log_inter)
        r = jnp.exp(mw - m_t)
        a = jnp.exp(log_inter - m_t)
        num = r * num0 + a * jnp.dot(qb, cm.astype(BF16), preferred_element_type=F32)
        den = r * den0 + a * jnp.sum(q * nv, axis=1, keepdims=True)
        h = num / jnp.maximum(jnp.abs(den), jnp.exp(-m_t))
        o_ref[0, sl_c, hh * B_DV:(hh + 1) * B_DV] = h.astype(o_ref.dtype)


def _mlstm_kernel(bqf, bkf, bvf, grf, gcf, bqb, bkb, bvb, grb, gcb, cwq_ref, cwk_ref, of_ref, ob_ref,
                  cf_ref, nf_ref, mf_ref, cb_ref, nb_ref, mb_ref):
    j = pl.program_id(2)

    @pl.when(j == 0)
    def _():
        for ref in (cf_ref, nf_ref, mf_ref, cb_ref, nb_ref, mb_ref):
            ref[...] = jnp.zeros_like(ref)

    t = lax.broadcasted_iota(jnp.int32, (TOK, 1), 0)
    row_mask = jnp.where(j == 0, TOK - 1, CHUNK - 1)
    pos = t & row_mask
    first = pos == 0
    last_ = pos == row_mask
    cwq = cwq_ref[:, 0, :]
    cwk = cwk_ref[:, 0, :]
    feats = []
    for hh in range(B_HPS):
        feats.append((_mlstm_state(bqf, bkf, bvf, grf, gcf, cwq, cwk, first, last_, cf_ref, nf_ref, mf_ref,
                                   hh, False),
                      _mlstm_state(bqb, bkb, bvb, grb, gcb, cwq, cwk, first, last_, cb_ref, nb_ref, mb_ref,
                                   hh, True)))
    for hh, (feats_f, feats_b) in enumerate(feats):
        _mlstm_output(feats_f, of_ref, hh, False)
        _mlstm_output(feats_b, ob_ref, hh, True)


def _mlstm_scan(p3, g_rows, g_cols, conv4, n_lat):
    Bn = p3.shape[0]
    A, Bh = A_HEADS, B_HEADS
    P = B_HPS
    G = Bh // P
    q0 = 5 * A // P
    k0 = (5 * A + Bh) // P
    v0 = (5 * A + 2 * Bh) // (2 * P)

    def specs(rev):
        d = int(rev)
        tb = lambda j: _tok_block(rev, j, n_lat)
        return [pl.BlockSpec((1, TOK, P * B_DQK), lambda b, h, j: (b, tb(j), q0 + h)),
                pl.BlockSpec((1, TOK, P * B_DQK), lambda b, h, j: (b, tb(j), k0 + h)),
                pl.BlockSpec((1, TOK, P * B_DV), lambda b, h, j: (b, tb(j), v0 + h)),
                pl.BlockSpec((1, 1, P, NCH, 2, CHUNK), lambda b, h, j: (b, d, h, tb(j), 0, 0)),
                pl.BlockSpec((1, 1, P, NCH, CHUNK, 2), lambda b, h, j: (b, d, h, tb(j), 0, 0))]

    def outp(rev):
        return pl.BlockSpec((1, TOK, P * B_DV), lambda b, h, j: (b, _out_block(rev, j, n_lat), h))

    osh = jax.ShapeDtypeStruct((Bn, n_lat * TOK, Bh * B_DV), BF16)
    state = [pltpu.VMEM((P, B_DQK, B_DV), F32), pltpu.VMEM((P, 1, B_DQK), F32),
             pltpu.VMEM((P, 1, LANE), F32)]
    return pl.pallas_call(
        _mlstm_kernel,
        grid=(Bn, G, n_lat + 1),
        in_specs=specs(False) + specs(True) + [
            pl.BlockSpec((4, 1, P * B_DQK), lambda b, h, j: (0, 0, h)),
            pl.BlockSpec((4, 1, P * B_DQK), lambda b, h, j: (0, 0, G + h))],
        out_specs=[outp(False), outp(True)],
        out_shape=[osh, osh],
        scratch_shapes=state + state,
        compiler_params=_cparams(("parallel", "parallel", "arbitrary")),
        name="mlstm_scan",
    )(p3, p3, p3, g_rows, g_cols, p3, p3, p3, g_rows, g_cols, conv4, conv4)


def _readout_kernel(oaf_ref, oab_ref, obf_ref, obb_ref, ag_ref, bo_ref, na_ref, nb_ref, ya_ref, yb_ref):
    for h in range(A_HEADS):
        sl = slice(h * A_D, (h + 1) * A_D)
        o = oaf_ref[0, :, sl].astype(F32) + oab_ref[0, :, sl].astype(F32)
        ya_ref[0, :, sl] = (_rms(o, na_ref[:, sl]) * _silu(ag_ref[0, :, sl])).astype(ya_ref.dtype)
    for h in range(B_HEADS):
        sl = slice(h * B_DV, (h + 1) * B_DV)
        o = obf_ref[0, :, sl].astype(F32) + obb_ref[0, :, sl].astype(F32)
        yb_ref[0, :, sl] = (_rms(o, nb_ref[:, sl]) * _sigmoid(bo_ref[0, :, sl])).astype(yb_ref.dtype)


def _readout(oa, ob, p3, norm_a, norm_b, n_lat):
    Bn = p3.shape[0]
    T = n_lat * TOK
    wa = A_HEADS * A_D
    wb = B_HEADS * B_DV
    ag_blk = 4
    bo_blk = (5 * A_HEADS + 4 * B_HEADS) * LANE // wb
    sa = pl.BlockSpec((1, TOK, wa), lambda b, i: (b, i, 0))
    sb = pl.BlockSpec((1, TOK, wb), lambda b, i: (b, i, 0))
    return pl.pallas_call(
        _readout_kernel,
        grid=(Bn, n_lat),
        in_specs=[sa, sa, sb, sb,
                  pl.BlockSpec((1, TOK, wa), lambda b, i: (b, i, ag_blk)),
                  pl.BlockSpec((1, TOK, wb), lambda b, i: (b, i, bo_blk)),
                  pl.BlockSpec((1, wa), lambda b, i: (0, 0)),
                  pl.BlockSpec((1, wb), lambda b, i: (0, 0))],
        out_specs=[sa, sb],
        out_shape=[jax.ShapeDtypeStruct((Bn, T, wa), BF16),
                   jax.ShapeDtypeStruct((Bn, T, wb), BF16)],
        compiler_params=_cparams(("parallel", "parallel")),
        name="readout",
    )(oa[0], oa[1], ob[0], ob[1], p3, p3, norm_a.reshape(1, wa), norm_b.reshape(1, wb))


def _merge_kernel(ya_ref, yb_ref, wa_ref, wb_ref, ga_ref, gb_ref, o_ref):
    pa = jnp.dot(ya_ref[0], wa_ref[...], preferred_element_type=F32)
    pb = jnp.dot(yb_ref[0], wb_ref[...], preferred_element_type=F32)
    o_ref[0] = (_sigmoid(ga_ref[0]) * pa + _sigmoid(gb_ref[0]) * pb).astype(o_ref.dtype)


def _merge(ya, yb, wa, wb, p3, D):
    Bn, T, ka = ya.shape
    kb = yb.shape[2]
    tm = _pick(T, 1024)
    tn = _pick(D, 512)
    ga0 = 0
    gb0 = D // tn
    return pl.pallas_call(
        _merge_kernel,
        grid=(Bn, T // tm, D // tn),
        in_specs=[pl.BlockSpec((1, tm, ka), lambda b, i, j: (b, i, 0)),
                  pl.BlockSpec((1, tm, kb), lambda b, i, j: (b, i, 0)),
                  pl.BlockSpec((ka, tn), lambda b, i, j: (0, j)),
                  pl.BlockSpec((kb, tn), lambda b, i, j: (0, j)),
                  pl.BlockSpec((1, tm, tn), lambda b, i, j: (b, i, ga0 + j)),
                  pl.BlockSpec((1, tm, tn), lambda b, i, j: (b, i, gb0 + j))],
        out_specs=pl.BlockSpec((1, tm, tn), lambda b, i, j: (b, i, j)),
        out_shape=jax.ShapeDtypeStruct((Bn, T, D), BF16),
        compiler_params=_cparams(("parallel", "parallel", "arbitrary")),
        name="merge",
    )(ya, yb, wa, wb, p3, p3)


def _resid_router_kernel(x_ref, mix_ref, g1_ref, g2_ref, gate_ref, sh_ref, sc_ref, wr_ref,
                         h_ref, v_ref, aff_ref):
    h = x_ref[0] + gate_ref[0] * _rms(mix_ref[0].astype(F32), g1_ref[...])
    h_ref[0] = h
    v = _rms(h, g2_ref[...]) * (1.0 + sc_ref[0]) + sh_ref[0]
    v_ref[0] = _pack_bf16_pairs(v)
    logits = jnp.dot(v, wr_ref[...], precision=lax.Precision.HIGHEST, preferred_element_type=F32)
    lane = lax.broadcasted_iota(jnp.int32, logits.shape, 1)
    logits = jnp.where(lane < N_EXPERTS, logits, -jnp.inf)
    e = jnp.exp(logits - jnp.max(logits, axis=-1, keepdims=True))
    aff_ref[0] = e / jnp.sum(e, axis=-1, keepdims=True)


def _resid_router(x, mix, g1, g2, mod3, w_router_pad):
    Bn, T, D = x.shape
    mspec = lambda k: pl.BlockSpec((1, 1, D), lambda b, i: (b, 0, k))
    tok = pl.BlockSpec((1, TOK, D), lambda b, i: (b, i, 0))
    vec = pl.BlockSpec((1, D), lambda b, i: (0, 0))
    return pl.pallas_call(
        _resid_router_kernel,
        grid=(Bn, T // TOK),
        in_specs=[tok, tok, vec, vec, mspec(2), mspec(3), mspec(4),
                  pl.BlockSpec((D, LANE), lambda b, i: (0, 0))],
        out_specs=[tok, pl.BlockSpec((1, TOK, D // 2), lambda b, i: (b, i, 0)),
                   pl.BlockSpec((1, TOK, LANE), lambda b, i: (b, i, 0))],
        out_shape=[jax.ShapeDtypeStruct((Bn, T, D), F32),
                   jax.ShapeDtypeStruct((Bn, T, D // 2), jnp.uint32),
                   jax.ShapeDtypeStruct((Bn, T, LANE), F32)],
        compiler_params=_cparams(("parallel", "parallel")),
        name="resid_router",
    )(x, mix, g1.reshape(1, D), g2.reshape(1, D), mod3, mod3, mod3, w_router_pad)


def _select_kernel(aff_ref, slot_ref, idx_ref, offs_ref, *, cap, tk):
    E, T = aff_ref.shape[1], aff_ref.shape[2]
    nck = T // LANE

    def key():
        return lax.bitcast_convert_type(aff_ref[0], jnp.int32)

    def bit_step(i, tau):
        cand = tau | jnp.left_shift(jnp.int32(1), 30 - i)
        cnt = jnp.sum(jnp.where(key() >= cand, 1.0, 0.0), axis=1, keepdims=True)
        return jnp.where(cnt >= cap, cand, tau)

    tau = lax.fori_loop(0, 31, bit_step, jnp.zeros((E, 1), jnp.int32))
    kk = key()
    gt = kk > tau
    eq = kk == tau
    need = cap - jnp.sum(jnp.where(gt, 1.0, 0.0), axis=1, keepdims=True)

    r = lax.broadcasted_iota(jnp.int32, (LANE, LANE), 0)
    c = lax.broadcasted_iota(jnp.int32, (LANE, LANE), 1)
    upper = jnp.where(r < c, 1.0, 0.0).astype(BF16)
    tr = jnp.right_shift(lax.broadcasted_iota(jnp.int32, (T, LANE), 0), LANE.bit_length() - 1)
    member = jnp.where(tr == lax.broadcasted_iota(jnp.int32, (T, LANE), 1), 1.0, 0.0).astype(BF16)

    def prefix(flags):
        x = jnp.where(flags, 1.0, 0.0).astype(BF16)
        tot = jnp.dot(x, member, preferred_element_type=F32)
        offs = jnp.dot(tot.astype(BF16), upper, preferred_element_type=F32)
        parts = [jnp.dot(x[:, j * LANE:(j + 1) * LANE], upper, preferred_element_type=F32)
                 + offs[:, j:j + 1] for j in range(nck)]
        return jnp.concatenate(parts, axis=1), offs

    tie_rank, _ = prefix(eq)
    sel = gt | (eq & (tie_rank < need))
    pos, offs = prefix(sel)
    slot = jnp.where(sel, pos, -1.0)
    slot_ref[0] = slot
    offs_ref[0] = offs.astype(jnp.int32)

    tt = lax.broadcasted_iota(jnp.int32, (8, T), 1)
    rr = lax.broadcasted_iota(jnp.int32, (8, T), 0)
    digits = jnp.where(rr == 0, jnp.right_shift(tt, 6), jnp.where(rr == 1, tt & 63, 0))
    digits = digits.astype(F32).astype(BF16)
    slot_iota = lax.broadcasted_iota(jnp.int32, (cap, tk), 0).astype(F32)
    for e in range(E):
        acc = jnp.zeros((8, cap), F32)
        for kc in range(T // tk):
            ks = slice(kc * tk, (kc + 1) * tk)
            oh = jnp.where(slot[e:e + 1, ks] == slot_iota, 1.0, 0.0).astype(BF16)
            acc = acc + lax.dot_general(digits[:, ks], oh, _NT, preferred_element_type=F32)
        idx_ref[0, e:e + 1, :] = (acc[0:1] * 64.0 + acc[1:2]).astype(jnp.int32)


def _select(aff_rows, cap):
    Bn, E, T = aff_rows.shape
    assert T // LANE < LANE and T <= 4096
    tk = _pick(T, 1024)
    row = lambda n: pl.BlockSpec((1, E, n), lambda b: (b, 0, 0))
    return pl.pallas_call(
        functools.partial(_select_kernel, cap=cap, tk=tk),
        grid=(Bn,),
        in_specs=[row(T)],
        out_specs=[row(T), row(cap), row(LANE)],
        out_shape=[jax.ShapeDtypeStruct((Bn, E, T), F32),
                   jax.ShapeDtypeStruct((Bn, E, cap), jnp.int32),
                   jax.ShapeDtypeStruct((Bn, E, LANE), jnp.int32)],
        compiler_params=_cparams(("parallel",)),
        name="ec_select",
    )(aff_rows)


def _pack_bf16_pairs(x):
    half = x.shape[1] // 2
    bits = lax.bitcast_convert_type(x.astype(BF16).astype(F32), jnp.uint32)
    return bits[:, half:] | (bits[:, :half] >> 16)


def _unpack_bf16_pairs(p):
    lo = lax.bitcast_convert_type(p << 16, F32).astype(BF16)
    hi = lax.bitcast_convert_type(p & jnp.uint32(0xFFFF0000), F32).astype(BF16)
    return lo, hi


def _gather_kernel(idx_ref, v_hbm, o_ref, buf, sem, *, cap, n_tok):
    b = pl.program_id(0)
    e = pl.program_id(1)
    base = (b * N_EXPERTS + e) * cap

    def row_copy(r, tok):
        return pltpu.make_async_copy(v_hbm.at[pl.ds(b * n_tok + tok, 1), :], buf.at[pl.ds(r, 1), :], sem)

    def issue(p, carry):
        for prio in range(2):
            r = 2 * p + prio
            row_copy(r, idx_ref[base + r]).start(priority=prio)
        return carry

    def drain(r, carry):
        row_copy(r, 0).wait()
        return carry

    lax.fori_loop(0, cap // 2, issue, 0)
    lax.fori_loop(0, cap, drain, 0)
    half = buf.shape[1]
    lo, hi = _unpack_bf16_pairs(buf[...])
    o_ref[0, 0, :, :half] = lo
    o_ref[0, 0, :, half:] = hi


def _gather(idx, v_pairs, cap):
    Bn, T, half = v_pairs.shape
    D = 2 * half
    return pl.pallas_call(
        functools.partial(_gather_kernel, cap=cap, n_tok=T),
        grid_spec=pltpu.PrefetchScalarGridSpec(
            num_scalar_prefetch=1,
            grid=(Bn, N_EXPERTS),
            in_specs=[pl.BlockSpec(memory_space=pl.ANY)],
            out_specs=pl.BlockSpec((1, 1, cap, D), lambda b, e, idx_ref: (e, b, 0, 0)),
            scratch_shapes=[pltpu.VMEM((cap, half), jnp.uint32), pltpu.SemaphoreType.DMA(())]),
        out_shape=jax.ShapeDtypeStruct((N_EXPERTS, Bn, cap, D), BF16),
        compiler_params=_cparams(("arbitrary", "arbitrary")),
        name="ec_gather",
    )(idx.reshape(-1), v_pairs.reshape(Bn * T, half))


def _ffn1_kernel(x_ref, wg_ref, wu_ref, o_ref):
    x = x_ref[0]
    g = jnp.dot(x, wg_ref[0].astype(BF16), preferred_element_type=F32)
    u = jnp.dot(x, wu_ref[0].astype(BF16), preferred_element_type=F32)
    o_ref[0] = (_silu(g) * u).astype(o_ref.dtype)


def _ffn1(xg, wg, wu):
    E, M, D = xg.shape
    F = wg.shape[2]
    tm = _pick(M, 1024)
    tn = _pick(F, 256)
    return pl.pallas_call(
        _ffn1_kernel,
        grid=(E, M // tm, F // tn),
        in_specs=[pl.BlockSpec((1, tm, D), lambda e, i, j: (e, i, 0)),
                  pl.BlockSpec((1, D, tn), lambda e, i, j: (e, 0, j)),
                  pl.BlockSpec((1, D, tn), lambda e, i, j: (e, 0, j))],
        out_specs=pl.BlockSpec((1, tm, tn), lambda e, i, j: (e, i, j)),
        out_shape=jax.ShapeDtypeStruct((E, M, F), BF16),
        compiler_params=_cparams(("parallel", "parallel", "arbitrary")),
        name="ec_ffn_up",
    )(xg, wg, wu)


def _ffn2_kernel(h_ref, wd_ref, o_ref):
    o_ref[0] = jnp.dot(h_ref[0], wd_ref[0].astype(BF16),
                       preferred_element_type=F32).astype(o_ref.dtype)


def _ffn2(hid, wd):
    E, M, F = hid.shape
    D = wd.shape[2]
    tn = _pick(D, 512)
    return pl.pallas_call(
        _ffn2_kernel,
        grid=(E, D // tn),
        in_specs=[pl.BlockSpec((1, M, F), lambda e, j: (e, 0, 0)),
                  pl.BlockSpec((1, F, tn), lambda e, j: (e, 0, j))],
        out_specs=pl.BlockSpec((1, M, tn), lambda e, j: (e, 0, j)),
        out_shape=jax.ShapeDtypeStruct((E, M, D), BF16),
        compiler_params=_cparams(("parallel", "arbitrary")),
        name="ec_ffn_down",
    )(hid, wd)


def _scatter_kernel(offs_ref, slot_ref, aff_ref, y_ref, h_ref, g_ref, gate_ref, o_ref, *, cap, win):
    b = pl.program_id(0)
    t = pl.program_id(1)
    e = pl.program_id(2)
    tq = o_ref.shape[1]

    @pl.when(e == 0)
    def _():
        o_ref[...] = jnp.zeros_like(o_ref)

    base = (b * N_EXPERTS + e) * LANE
    lo = offs_ref[base + t * (tq // LANE)]
    hi = offs_ref[base + (t + 1) * (tq // LANE)]
    sl = jnp.broadcast_to(slot_ref[0, 0], (LANE, tq)).T
    af = jnp.broadcast_to(aff_ref[0, 0], (LANE, tq)).T
    sl = jnp.concatenate([sl] * (win // LANE), axis=1)
    af = jnp.concatenate([af] * (win // LANE), axis=1)
    lane = lax.broadcasted_iota(jnp.int32, (tq, win), 1).astype(F32)
    for w0 in range(0, cap, win):
        @pl.when((lo < w0 + win) & (hi > w0))
        def _():
            ohw = jnp.where(sl == lane + float(w0), af, 0.0).astype(BF16)
            o_ref[0] += jnp.dot(ohw, y_ref[0, 0, w0:w0 + win, :], preferred_element_type=F32)

    @pl.when(e == pl.num_programs(2) - 1)
    def _():
        o_ref[0] = h_ref[0] + gate_ref[0] * _rms(o_ref[0], g_ref[...])


def _scatter_combine(offs, slot_rows, aff_rows, y, cap, h, g3, mod3):
    E, Bn, _, D = y.shape
    T = slot_rows.shape[3]
    tq = _pick(T, 512)
    win = _pick(cap, 256)
    assert win % LANE == 0 and tq % LANE == 0
    row = pl.BlockSpec((1, 1, 1, tq), lambda b, t, e, offs_ref: (b, e, 0, t))
    return pl.pallas_call(
        functools.partial(_scatter_kernel, cap=cap, win=win),
        grid_spec=pltpu.PrefetchScalarGridSpec(
            num_scalar_prefetch=1,
            grid=(Bn, T // tq, E),
            in_specs=[row, row,
                      pl.BlockSpec((1, 1, cap, D), lambda b, t, e, offs_ref: (e, b, 0, 0)),
                      pl.BlockSpec((1, tq, D), lambda b, t, e, offs_ref: (b, t, 0)),
                      pl.BlockSpec((1, D), lambda b, t, e, offs_ref: (0, 0)),
                      pl.BlockSpec((1, 1, D), lambda b, t, e, offs_ref: (b, 0, 5))],
            out_specs=pl.BlockSpec((1, tq, D), lambda b, t, e, offs_ref: (b, t, 0))),
        out_shape=jax.ShapeDtypeStruct((Bn, T, D), F32),
        compiler_params=_cparams(("parallel", "parallel", "arbitrary")),
        name="ec_scatter",
    )(offs.reshape(-1), slot_rows, aff_rows, y, h, g3.reshape(1, D), mod3)


def _layer(h_lat, ctx, c8, l, lb_l, w_ada, b_ada, g_norm, w_in, hgrn_f_bias, hgrn_norm,
           mlstm_conv_w, mlstm_conv_b, mlstm_gate_b, mlstm_norm, w_branch_a, w_branch_b, w_out,
           w_router, w_expert_gate, w_expert_up, w_expert_down):
    Bn, T, D = h_lat.shape
    n_lat = T // TOK
    n_tok = T + TOK
    A, Bh = A_HEADS, B_HEADS
    g0 = (5 * A + 6 * Bh) * LANE

    mod = _modulation(c8, w_ada[l], b_ada[l])
    mod3 = mod.reshape(8, 1, N_MOD * D)

    u = _prenorm(h_lat, ctx, g_norm[l, 0], mod3)

    w_in_t = jnp.swapaxes(w_in, 1, 2)
    w_merge_t, w_gate_t = _wsplit(w_in_t, l, g0, 4 * Bh, 2 * D)
    gate_bias = jnp.pad(mlstm_gate_b[l].reshape(4 * Bh), (0, LANE - 4 * Bh))
    u2 = u.reshape(Bn * n_tok, D)
    p3 = _matmul_t(u2, w_in_t, F32, n_cols=g0, layer=l, name="in_proj").reshape(Bn, n_tok, g0)
    p_merge = _matmul_t_rows(u, w_merge_t, T, F32, name="in_proj_merge")
    gates = _matmul_t(u2, w_gate_t, F32, tn=LANE, bias=gate_bias, name="in_proj_gates")

    g6 = gates[:, :4 * Bh].reshape(Bn, n_tok // CHUNK, CHUNK, 2, 2, Bh)
    g_rows = g6.transpose(0, 3, 5, 1, 4, 2)
    g_cols = g6.transpose(0, 3, 5, 1, 2, 4)
    conv4 = jnp.concatenate([mlstm_conv_w[l], mlstm_conv_b[l][None]], axis=0)
    conv4 = conv4.reshape(4, 1, 2 * Bh * B_DQK)

    oa = _hgrn_scan(p3, hgrn_f_bias[l], lb_l, n_lat)
    ob = _mlstm_scan(p3, g_rows, g_cols, conv4, n_lat)
    ya, yb = _readout(oa, ob, p3, hgrn_norm[l], mlstm_norm[l], n_lat)
    merged = _merge(ya, yb, w_branch_a[l].astype(BF16), w_branch_b[l].astype(BF16), p_merge, D)
    mix = _matmul(merged.reshape(Bn * T, D), w_out[l].astype(BF16), BF16, name="out_proj")

    w_router_pad = jnp.pad(w_router[l], ((0, 0), (0, LANE - N_EXPERTS)))
    h_lat, v_lat, aff = _resid_router(h_lat, mix.reshape(Bn, T, D), g_norm[l, 1], g_norm[l, 2],
                                      mod3, w_router_pad)

    cap = CAPACITY * T // N_EXPERTS
    aff_rows = aff[:, :, :N_EXPERTS].transpose(0, 2, 1)
    slot, idx, offs = _select(aff_rows, cap)
    xg = _gather(idx, v_lat, cap)
    hid = _ffn1(xg.reshape(N_EXPERTS, Bn * cap, D), w_expert_gate[l], w_expert_up[l])
    y = _ffn2(hid, w_expert_down[l]).reshape(N_EXPERTS, Bn, cap, D)
    return _scatter_combine(offs, slot.reshape(Bn, N_EXPERTS, 1, T),
                            aff_rows.reshape(Bn, N_EXPERTS, 1, T), y, cap, h_lat, g_norm[l, 3], mod3)


def kernel(x, c, ctx, c_ctx, w_ada, b_ada, g_norm, w_in, hgrn_f_bias, hgrn_lb, hgrn_norm,
           mlstm_conv_w, mlstm_conv_b, mlstm_gate_b, mlstm_norm, w_branch_a, w_branch_b, w_out,
           w_router, w_expert_gate, w_expert_up, w_expert_down):
    Bn, T, D = x.shape
    depth = w_ada.shape[0]
    assert depth == 1, "context outputs are only produced for the state hand-off (single layer)"
    assert ctx.shape[1] == TOK and T % TOK == 0 and Bn < 8
    lb_all = jnp.cumsum(jax.nn.softmax(hgrn_lb.astype(F32), axis=1), axis=1)
    c8 = jnp.zeros((8, D), F32).at[:Bn].set(c).at[Bn].set(c_ctx)
    h_lat = x
    for l in range(depth):
        h_lat = _layer(h_lat, ctx, c8, l, lb_all[:, l], w_ada, b_ada, g_norm, w_in, hgrn_f_bias,
                       hgrn_norm, mlstm_conv_w, mlstm_conv_b, mlstm_gate_b, mlstm_norm,
                       w_branch_a, w_branch_b, w_out, w_router, w_expert_gate, w_expert_up,
                       w_expert_down)
    return h_lat.astype(x.dtype)
```

```python
import functools

import jax
import jax.numpy as jnp
from jax import lax
from jax.experimental import pallas as pl
from jax.experimental.pallas import tpu as pltpu

F32 = jnp.float32
BF16 = jnp.bfloat16

EPS = 1e-6
N_MOD = 6
A_HEADS = 16
A_D = 128
B_HEADS = 8
B_DQK = 128
B_DV = 256
N_EXPERTS = 16
CAPACITY = 2
CHUNK = 64
SUB = 16
A_HPS = 4
B_HPS = 4
TOK = 256
NCH = TOK // CHUNK
NSUB = CHUNK // SUB
LANE = 128
VMEM_LIMIT = 56 * 1024 * 1024

_NT = (((1,), (1,)), ((), ()))


def _pick(n, pref):
    t = min(n, pref)
    while n % t:
        t //= 2
    return t


def _cparams(sem):
    return pltpu.CompilerParams(dimension_semantics=sem, vmem_limit_bytes=VMEM_LIMIT)


def _silu(x):
    return x / (1.0 + jnp.exp(-x))


def _sigmoid(x):
    return 1.0 / (1.0 + jnp.exp(-x))


def _sig_pair(z):
    t = jnp.exp(-jnp.abs(z))
    r = 1.0 / (1.0 + t)
    tr = t * r
    pos = z >= 0
    return jnp.where(pos, r, tr), jnp.where(pos, tr, r)


def _log_sigmoid(x):
    return jnp.minimum(x, 0.0) - jnp.log(1.0 + jnp.exp(-jnp.abs(x)))


def _rms(xf, w):
    return xf * lax.rsqrt(jnp.mean(xf * xf, axis=-1, keepdims=True) + EPS) * w


def _rows(x, r, n):
    return jnp.broadcast_to(x[r:r + 1], (n, x.shape[1]))


def _mod_kernel(c_ref, w_ref, b_ref, o_ref):
    a = _silu(c_ref[...]).astype(BF16)
    o_ref[...] = jnp.dot(a, w_ref[...].astype(BF16), preferred_element_type=F32) + b_ref[...]


def _modulation(c8, w, b):
    D, N = w.shape
    tn = _pick(N, 512)
    return pl.pallas_call(
        _mod_kernel,
        grid=(N // tn,),
        in_specs=[pl.BlockSpec((8, D), lambda j: (0, 0)),
                  pl.BlockSpec((D, tn), lambda j: (0, j)),
                  pl.BlockSpec((1, tn), lambda j: (0, j))],
        out_specs=pl.BlockSpec((8, tn), lambda j: (0, j)),
        out_shape=jax.ShapeDtypeStruct((8, N), F32),
        compiler_params=_cparams(("parallel",)),
        name="adaln_mod",
    )(c8, w, b.reshape(1, N))


def _prenorm_kernel(x_ref, ctx_ref, g_ref, sh_ref, sc_ref, o_ref, *, n_lat):
    i = pl.program_id(1)
    g = g_ref[...]
    sh = sh_ref[0]
    sc = sc_ref[0]

    @pl.when(i < n_lat)
    def _():
        o_ref[0] = (_rms(x_ref[0], g) * (1.0 + sc) + sh).astype(o_ref.dtype)

    @pl.when(i == n_lat)
    def _():
        o_ref[0] = (_rms(ctx_ref[0], g) * (1.0 + sc) + sh).astype(o_ref.dtype)


def _prenorm(x, ctx, g, mod3):
    Bn, T, D = x.shape
    n_lat = T // TOK
    row = lambda b, i: jnp.where(i == n_lat, Bn, b)
    return pl.pallas_call(
        functools.partial(_prenorm_kernel, n_lat=n_lat),
        grid=(Bn, n_lat + 1),
        in_specs=[pl.BlockSpec((1, TOK, D), lambda b, i: (b, jnp.minimum(i, n_lat - 1), 0)),
                  pl.BlockSpec((1, TOK, D), lambda b, i: (b, 0, 0)),
                  pl.BlockSpec((1, D), lambda b, i: (0, 0)),
                  pl.BlockSpec((1, 1, D), lambda b, i: (row(b, i), 0, 0)),
                  pl.BlockSpec((1, 1, D), lambda b, i: (row(b, i), 0, 1))],
        out_specs=pl.BlockSpec((1, TOK, D), lambda b, i: (b, i, 0)),
        out_shape=jax.ShapeDtypeStruct((Bn, T + TOK, D), BF16),
        compiler_params=_cparams(("parallel", "arbitrary")),
        name="prenorm",
    )(x, ctx, g.reshape(1, D), mod3, mod3)


_NN = (((1,), (0,)), ((), ()))


def _mm_kernel(a_ref, b_ref, o_ref, *, dims):
    o_ref[...] = lax.dot_general(a_ref[...], b_ref[...].astype(BF16), dims,
                                 preferred_element_type=F32).astype(o_ref.dtype)


def _mm_bias_kernel(a_ref, b_ref, bias_ref, o_ref, *, dims):
    o_ref[...] = (lax.dot_general(a_ref[...], b_ref[...].astype(BF16), dims, preferred_element_type=F32)
                  + bias_ref[...]).astype(o_ref.dtype)


def _matmul(a, b, out_dtype, tm=1024, tn=512, bias=None, name="matmul"):
    M, K = a.shape
    N = b.shape[1]
    tm = _pick(M, tm)
    tn = _pick(N, tn)
    return _matmul_call(a, b, pl.BlockSpec((K, tn), lambda i, j: (0, j)), _NN, N, out_dtype, tm, tn,
                        bias, name)


def _matmul_t(a, bt, out_dtype, tm=1024, tn=512, bias=None, n_cols=None, layer=None, name="matmul_t"):
    M, K = a.shape
    N = bt.shape[-2] if n_cols is None else n_cols
    tm = _pick(M, tm)
    tn = _pick(N, tn)
    if layer is None:
        b_spec = pl.BlockSpec((tn, K), lambda i, j: (j, 0))
    else:
        b_spec = pl.BlockSpec((None, tn, K), lambda i, j: (layer, j, 0))
    return _matmul_call(a, bt, b_spec, _NT, N, out_dtype, tm, tn, bias, name)


def _matmul_call(a, b, b_spec, dims, N, out_dtype, tm, tn, bias, name):
    M, K = a.shape
    in_specs = [pl.BlockSpec((tm, K), lambda i, j: (i, 0)), b_spec]
    args = [a, b]
    kern = functools.partial(_mm_kernel, dims=dims)
    if bias is not None:
        in_specs.append(pl.BlockSpec((1, tn), lambda i, j: (0, j)))
        args.append(bias.reshape(1, N))
        kern = functools.partial(_mm_bias_kernel, dims=dims)
    return pl.pallas_call(
        kern,
        grid=(M // tm, N // tn),
        in_specs=in_specs,
        out_specs=pl.BlockSpec((tm, tn), lambda i, j: (i, j)),
        out_shape=jax.ShapeDtypeStruct((M, N), out_dtype),
        compiler_params=_cparams(("parallel", "arbitrary")),
        name=name,
    )(*args)


def _matmul_t_rows(a3, bt, n_rows, out_dtype, tm=1024, tn=512, name="matmul_t_rows"):
    Bn, _, K = a3.shape
    N = bt.shape[0]
    tm = _pick(n_rows, tm)
    tn = _pick(N, tn)
    return pl.pallas_call(
        functools.partial(_mm_kernel, dims=_NT),
        grid=(Bn, n_rows // tm, N // tn),
        in_specs=[pl.BlockSpec((None, tm, K), lambda b, i, j: (b, i, 0)),
                  pl.BlockSpec((tn, K), lambda b, i, j: (j, 0))],
        out_specs=pl.BlockSpec((None, tm, tn), lambda b, i, j: (b, i, j)),
        out_shape=jax.ShapeDtypeStruct((Bn, n_rows, N), out_dtype),
        compiler_params=_cparams(("parallel", "parallel", "arbitrary")),
        name=name,
    )(a3, bt)


def _wsplit_kernel(a_ref, b_ref, o_ref, g_ref, *, shift):
    j = pl.program_id(1)
    tn = o_ref.shape[0]
    cat = jnp.concatenate([a_ref[...], b_ref[...]], axis=0)
    o_ref[...] = cat[shift:shift + tn].astype(o_ref.dtype)

    @pl.when(j == 0)
    def _():
        row = lax.broadcasted_iota(jnp.int32, g_ref.shape, 0)
        g_ref[...] = jnp.where(row < shift, cat[:LANE], 0.0).astype(g_ref.dtype)


def _wsplit(wt, layer, row0, shift, n):
    K = wt.shape[2]
    tk = _pick(K, 1024)
    tn = _pick(n, 512)
    assert row0 % tn == 0 and 0 < shift < LANE and shift % 8 == 0 and row0 + shift + n <= wt.shape[1]
    r0 = row0 // tn
    return pl.pallas_call(
        functools.partial(_wsplit_kernel, shift=shift),
        grid=(K // tk, n // tn),
        in_specs=[pl.BlockSpec((None, tn, tk), lambda i, j: (layer, r0 + j, i)),
                  pl.BlockSpec((None, tn, tk), lambda i, j: (layer, r0 + j + 1, i))],
        out_specs=[pl.BlockSpec((tn, tk), lambda i, j: (j, i)),
                   pl.BlockSpec((LANE, tk), lambda i, j: (0, i))],
        out_shape=[jax.ShapeDtypeStruct((n, K), BF16), jax.ShapeDtypeStruct((LANE, K), BF16)],
        compiler_params=_cparams(("parallel", "arbitrary")),
        name="w_split",
    )(wt, wt)


def _tok_block(rev, j, n_lat):
    lat = n_lat - j if rev else j - 1
    return jnp.where(j == 0, n_lat, lat)


def _out_block(rev, j, n_lat):
    return jnp.clip(n_lat - j if rev else j - 1, 0, n_lat - 1)


def _chunk_masks(rev):
    row = lax.broadcasted_iota(jnp.int32, (CHUNK, CHUNK), 0)
    col = lax.broadcasted_iota(jnp.int32, (CHUNK, CHUNK), 1)
    return (col >= row, col <= row) if rev else (col <= row, col >= row)


def _hgrn_consts():
    t = jnp.arange(TOK)
    same = (t[:, None] // CHUNK) == (t[None, :] // CHUNK)
    sub_r, sub_c = t[:, None] // SUB, t[None, :] // SUB
    fwd = jnp.concatenate([same & (t[None, :] <= t[:, None]), same & (sub_c < sub_r)], axis=0)
    bwd = jnp.concatenate([same & (t[None, :] >= t[:, None]), same & (sub_c > sub_r)], axis=0)
    masks = jnp.stack([fwd, bwd]).astype(BF16)
    r = jnp.arange(SUB * A_D)
    emat = ((r[:, None] // A_D) == (jnp.arange(LANE)[None, :] % SUB)).astype(BF16)
    return masks, emat


def _hgrn_state(af, v, fb, lb, mask_ref, st_ref, rev):
    z = af + fb
    sp, sn = _sig_pair(z)
    k = (1.0 - lb) * sn
    lf2 = jnp.log2(lb + (1.0 - lb) * sp)
    lk2 = jnp.log2(k)

    hi = lf2.astype(BF16)
    r1 = lf2 - hi.astype(F32)
    mid = r1.astype(BF16)
    lo = (r1 - mid.astype(F32)).astype(BF16)
    cs = jnp.dot(mask_ref[0], jnp.concatenate([hi, mid, lo], axis=1), preferred_element_type=F32)
    cs = cs[:, :A_D] + cs[:, A_D:2 * A_D] + cs[:, 2 * A_D:]
    b = cs[:TOK]
    ent = cs[TOK:]
    c_all = lk2 - b

    last = [(c * CHUNK if rev else c * CHUNK + CHUNK - 1) for c in range(NCH)]
    tot_rows = jnp.concatenate([_rows(b, last[c], CHUNK) for c in range(NCH)], axis=0)
    khat = jnp.exp2(jnp.minimum(tot_rows + c_all, lk2)).astype(BF16)
    upd = [jnp.dot(v[c * CHUNK:(c + 1) * CHUNK].T.astype(BF16), khat[c * CHUNK:(c + 1) * CHUNK],
                   preferred_element_type=F32) for c in range(NCH)]

    order = range(NCH - 1, -1, -1) if rev else range(NCH)
    st = st_ref[...]
    st_in = [None] * NCH
    for c in order:
        st_in[c] = st
        st = st * jnp.exp2(b[last[c]:last[c] + 1]) + upd[c]
    st_ref[...] = st
    return v, b, ent, c_all, lk2, st_in


def _hgrn_output(aq, feats, emat_ref, o_ref, hs, rev):
    v, b, ent, c_all, lk2, st_in = feats
    q = _silu(aq)
    qb = q.astype(BF16)
    vb = v.astype(BF16)
    qt = (q * jnp.exp2(b - ent)).astype(BF16)
    row = lax.broadcasted_iota(jnp.int32, (CHUNK, CHUNK), 0)
    col = lax.broadcasted_iota(jnp.int32, (CHUNK, CHUNK), 1)
    rs = jnp.right_shift(row, SUB.bit_length() - 1)
    cs_ = jnp.right_shift(col, SUB.bit_length() - 1)
    prev_blk = (cs_ > rs) if rev else (cs_ < rs)
    diag_blk = (rs == cs_) & ((col >= row) if rev else (col <= row))

    off = [[None] * NSUB for _ in range(NCH)]
    for i in range(NSUB):
        e_i = jnp.concatenate([_rows(ent, c * CHUNK + i * SUB, CHUNK) for c in range(NCH)], axis=0)
        kt = jnp.exp2(jnp.minimum(e_i + c_all, lk2)).astype(BF16)
        for c in range(NCH):
            r0 = c * CHUNK + i * SUB
            off[c][i] = lax.dot_general(qt[r0:r0 + SUB], kt[c * CHUNK:(c + 1) * CHUNK], _NT,
                                        preferred_element_type=F32)

    half = SUB // 2
    nsb = TOK // SUB
    zeros = jnp.zeros((half, A_D), F32)
    ps = []
    for sl in range(SUB):
        n = half if (sl < half if rev else sl >= half) else SUB
        r0 = SUB - n if not rev else 0
        bq = b if n == SUB else jnp.concatenate(
            [b[g * SUB + r0:g * SUB + r0 + n] for g in range(nsb)], axis=0)
        cref = jnp.concatenate([_rows(c_all, g * SUB + sl, n) for g in range(nsb)], axis=0)
        kref = jnp.concatenate([_rows(lk2, g * SUB + sl, n) for g in range(nsb)], axis=0)
        e = jnp.exp2(jnp.minimum(bq + cref, kref))
        if n != SUB:
            parts = []
            for g in range(nsb):
                piece = e[g * half:(g + 1) * half]
                parts += [piece, zeros] if rev else [zeros, piece]
            e = jnp.concatenate(parts, axis=0)
        ps.append(e.astype(BF16) * qb)
    acc = jnp.dot(jnp.concatenate(ps, axis=1), emat_ref[...], preferred_element_type=F32)

    qhat = (q * jnp.exp2(b)).astype(BF16)
    for c in range(NCH):
        sl_c = slice(c * CHUNK, (c + 1) * CHUNK)
        attn = (jnp.where(prev_blk, jnp.concatenate(off[c], axis=0), 0.0)
                + jnp.where(diag_blk, acc[sl_c, :CHUNK], 0.0))
        o = jnp.dot(attn.astype(BF16), vb[sl_c], preferred_element_type=F32)
        o = o + lax.dot_general(qhat[sl_c], st_in[c].astype(BF16), _NT, preferred_element_type=F32)
        o_ref[0, sl_c, hs] = o.astype(o_ref.dtype)


def _hgrn_kernel(aqf, aff, aif, aqb, afb, aib, fb_ref, lb_ref, mask_ref, emat_ref, of_ref, ob_ref,
                 stf_ref, stb_ref):
    j = pl.program_id(2)

    @pl.when(j == 0)
    def _():
        stf_ref[...] = jnp.zeros_like(stf_ref)
        stb_ref[...] = jnp.zeros_like(stb_ref)

    feats = []
    for hh in range(A_HPS):
        hs = slice(hh * A_D, (hh + 1) * A_D)
        feats.append((hs,
                      _hgrn_state(aff[0, :, hs], aif[0, :, hs], fb_ref[0][:, hs], lb_ref[0][:, hs],
                                  mask_ref.at[0:1], stf_ref.at[hh], False),
                      _hgrn_state(afb[0, :, hs], aib[0, :, hs], fb_ref[1][:, hs], lb_ref[1][:, hs],
                                  mask_ref.at[1:2], stb_ref.at[hh], True)))
    for hs, feats_f, feats_b in feats:
        _hgrn_output(aqf[0, :, hs], feats_f, emat_ref, of_ref, hs, False)
        _hgrn_output(aqb[0, :, hs], feats_b, emat_ref, ob_ref, hs, True)


def _hgrn_scan(p3, f_bias, lb, n_lat):
    Bn = p3.shape[0]
    A = A_HEADS
    masks, emat = _hgrn_consts()

    W = A_HPS * A_D
    G = A // A_HPS

    def feat(rev, grp):
        return pl.BlockSpec((1, TOK, W), lambda b, h, j: (b, _tok_block(rev, j, n_lat), grp * G + h))

    def outp(rev):
        return pl.BlockSpec((1, TOK, W), lambda b, h, j: (b, _out_block(rev, j, n_lat), h))

    par = pl.BlockSpec((2, 1, W), lambda b, h, j: (0, 0, h))
    osh = jax.ShapeDtypeStruct((Bn, n_lat * TOK, A * A_D), BF16)
    return pl.pallas_call(
        _hgrn_kernel,
        grid=(Bn, G, n_lat + 1),
        in_specs=[feat(False, 0), feat(False, 1), feat(False, 3),
                  feat(True, 0), feat(True, 2), feat(True, 3),
                  par, par,
                  pl.BlockSpec((2, 2 * TOK, TOK), lambda b, h, j: (0, 0, 0)),
                  pl.BlockSpec((SUB * A_D, LANE), lambda b, h, j: (0, 0))],
        out_specs=[outp(False), outp(True)],
        out_shape=[osh, osh],
        scratch_shapes=[pltpu.VMEM((A_HPS, A_D, A_D), F32), pltpu.VMEM((A_HPS, A_D, A_D), F32)],
        compiler_params=_cparams(("parallel", "parallel", "arbitrary")),
        name="hgrn2_scan",
    )(p3, p3, p3, p3, p3, p3, f_bias.reshape(2, 1, A * A_D), lb.reshape(2, 1, A * A_D), masks, emat)


def _mlstm_state(bq_ref, bk_ref, bv_ref, gr_ref, gc_ref, cwq, cwk, first, last_, c_ref, n_ref, m_ref,
                 hh, rev):
    def conv(u, cw):
        up = jnp.where(first, 0.0, pltpu.roll(u, 1, 0))
        dn = jnp.where(last_, 0.0, pltpu.roll(u, TOK - 1, 0))
        return _silu(cw[3:4] + up * cw[0:1] + u * cw[1:2] + dn * cw[2:3])

    hs = slice(hh * B_DQK, (hh + 1) * B_DQK)
    vs = slice(hh * B_DV, (hh + 1) * B_DV)
    c_ref, n_ref, m_ref = c_ref.at[hh], n_ref.at[hh], m_ref.at[hh]
    q_all = conv(bq_ref[0, :, hs], cwq[:, hs])
    k_all = conv(bk_ref[0, :, hs], cwk[:, hs]) * (B_DQK ** -0.5)
    seen, seen_t = _chunk_masks(rev)

    pre = []
    for c in range(NCH):
        sl_c = slice(c * CHUNK, (c + 1) * CHUNK)
        k = k_all[sl_c]
        vb = bv_ref[0, sl_c, vs].astype(BF16)
        g_r = gr_ref[0, 0, hh, c]
        g_c = gc_ref[0, 0, hh, c]
        ii_r = g_r[0:1]
        lf_r = _log_sigmoid(g_r[1:2])
        ii_c = g_c[:, 0:1]
        lf_c = _log_sigmoid(g_c[:, 1:2])
        b_c = jnp.sum(jnp.where(seen, lf_r, 0.0), axis=1, keepdims=True)
        b_r = jnp.sum(jnp.where(seen_t, lf_c, 0.0), axis=0, keepdims=True)
        total = jnp.sum(lf_r, axis=1, keepdims=True)
        logs = total - b_c + ii_c
        ms = jnp.max(logs, axis=0, keepdims=True)
        kw = k * jnp.exp(logs - ms)
        upd = jnp.dot(kw.T.astype(BF16), vb, preferred_element_type=F32)
        nupd = jnp.sum(kw, axis=0, keepdims=True)
        pre.append((vb, ii_r, b_c, b_r, total, ms, upd, nupd))

    cmat, nvec, m = c_ref[...], n_ref[...], m_ref[:, 0:1]
    st_in = [None] * NCH
    for c in (range(NCH - 1, -1, -1) if rev else range(NCH)):
        _, _, _, _, total, ms, upd, nupd = pre[c]
        st_in[c] = (cmat, nvec, m)
        m_new = jnp.maximum(total + m, ms)
        dec = jnp.exp(total + m - m_new)
        sc = jnp.exp(ms - m_new)
        cmat = dec * cmat + sc * upd
        nvec = dec * nvec + sc * nupd
        m = m_new
    c_ref[...] = cmat
    n_ref[...] = nvec
    m_ref[...] = jnp.broadcast_to(m, m_ref.shape)
    return q_all, k_all, pre, st_in


def _mlstm_output(feats, o_ref, hh, rev):
    q_all, k_all, pre, st_in = feats
    seen, _ = _chunk_masks(rev)
    for c in range(NCH):
        sl_c = slice(c * CHUNK, (c + 1) * CHUNK)
        vb, ii_r, b_c, b_r, _, _, _, _ = pre[c]
        cm, nv, m0 = st_in[c]
        q = q_all[sl_c]
        qb = q.astype(BF16)
        kb = k_all[sl_c].astype(BF16)
        logw = jnp.where(seen, b_c - b_r + ii_r, -jnp.inf)
        mw = jnp.max(logw, axis=1, keepdims=True)
        qk = lax.dot_general(qb, kb, _NT, preferred_element_type=F32) * jnp.exp(logw - mw)
        num0 = jnp.dot(qk.astype(BF16), vb, preferred_element_type=F32)
        den0 = jnp.sum(qk, axis=1, keepdims=True)
        log_inter = b_c + m0
        m_t = jnp.maximum(mw, log_inter)
        r = jnp.exp(mw - m_t)
        a = jnp.exp(log_inter - m_t)
        num = r * num0 + a * jnp.dot(qb, cm.astype(BF16), preferred_element_type=F32)
        den = r * den0 + a * jnp.sum(q * nv, axis=1, keepdims=True)
        h = num / jnp.maximum(jnp.abs(den), jnp.exp(-m_t))
        o_ref[0, sl_c, hh * B_DV:(hh + 1) * B_DV] = h.astype(o_ref.dtype)


def _mlstm_kernel(bqf, bkf, bvf, grf, gcf, bqb, bkb, bvb, grb, gcb, cwq_ref, cwk_ref, of_ref, ob_ref,
                  cf_ref, nf_ref, mf_ref, cb_ref, nb_ref, mb_ref):
    j = pl.program_id(2)

    @pl.when(j == 0)
    def _():
        for ref in (cf_ref, nf_ref, mf_ref, cb_ref, nb_ref, mb_ref):
            ref[...] = jnp.zeros_like(ref)

    t = lax.broadcasted_iota(jnp.int32, (TOK, 1), 0)
    row_mask = jnp.where(j == 0, TOK - 1, CHUNK - 1)
    pos = t & row_mask
    first = pos == 0
    last_ = pos == row_mask
    cwq = cwq_ref[:, 0, :]
    cwk = cwk_ref[:, 0, :]
    feats = []
    for hh in range(B_HPS):
        feats.append((_mlstm_state(bqf, bkf, bvf, grf, gcf, cwq, cwk, first, last_, cf_ref, nf_ref, mf_ref,
                                   hh, False),
                      _mlstm_state(bqb, bkb, bvb, grb, gcb, cwq, cwk, first, last_, cb_ref, nb_ref, mb_ref,
                                   hh, True)))
    for hh, (feats_f, feats_b) in enumerate(feats):
        _mlstm_output(feats_f, of_ref, hh, False)
        _mlstm_output(feats_b, ob_ref, hh, True)


def _mlstm_scan(p3, g_rows, g_cols, conv4, n_lat):
    Bn = p3.shape[0]
    A, Bh = A_HEADS, B_HEADS
    P = B_HPS
    G = Bh // P
    q0 = 5 * A // P
    k0 = (5 * A + Bh) // P
    v0 = (5 * A + 2 * Bh) // (2 * P)

    def specs(rev):
        d = int(rev)
        tb = lambda j: _tok_block(rev, j, n_lat)
        return [pl.BlockSpec((1, TOK, P * B_DQK), lambda b, h, j: (b, tb(j), q0 + h)),
                pl.BlockSpec((1, TOK, P * B_DQK), lambda b, h, j: (b, tb(j), k0 + h)),
                pl.BlockSpec((1, TOK, P * B_DV), lambda b, h, j: (b, tb(j), v0 + h)),
                pl.BlockSpec((1, 1, P, NCH, 2, CHUNK), lambda b, h, j: (b, d, h, tb(j), 0, 0)),
                pl.BlockSpec((1, 1, P, NCH, CHUNK, 2), lambda b, h, j: (b, d, h, tb(j), 0, 0))]

    def outp(rev):
        return pl.BlockSpec((1, TOK, P * B_DV), lambda b, h, j: (b, _out_block(rev, j, n_lat), h))

    osh = jax.ShapeDtypeStruct((Bn, n_lat * TOK, Bh * B_DV), BF16)
    state = [pltpu.VMEM((P, B_DQK, B_DV), F32), pltpu.VMEM((P, 1, B_DQK), F32),
             pltpu.VMEM((P, 1, LANE), F32)]
    return pl.pallas_call(
        _mlstm_kernel,
        grid=(Bn, G, n_lat + 1),
        in_specs=specs(False) + specs(True) + [
            pl.BlockSpec((4, 1, P * B_DQK), lambda b, h, j: (0, 0, h)),
            pl.BlockSpec((4, 1, P * B_DQK), lambda b, h, j: (0, 0, G + h))],
        out_specs=[outp(False), outp(True)],
        out_shape=[osh, osh],
        scratch_shapes=state + state,
        compiler_params=_cparams(("parallel", "parallel", "arbitrary")),
        name="mlstm_scan",
    )(p3, p3, p3, g_rows, g_cols, p3, p3, p3, g_rows, g_cols, conv4, conv4)


def _readout_kernel(oaf_ref, oab_ref, obf_ref, obb_ref, ag_ref, bo_ref, na_ref, nb_ref, ya_ref, yb_ref):
    for h in range(A_HEADS):
        sl = slice(h * A_D, (h + 1) * A_D)
        o = oaf_ref[0, :, sl].astype(F32) + oab_ref[0, :, sl].astype(F32)
        ya_ref[0, :, sl] = (_rms(o, na_ref[:, sl]) * _silu(ag_ref[0, :, sl])).astype(ya_ref.dtype)
    for h in range(B_HEADS):
        sl = slice(h * B_DV, (h + 1) * B_DV)
        o = obf_ref[0, :, sl].astype(F32) + obb_ref[0, :, sl].astype(F32)
        yb_ref[0, :, sl] = (_rms(o, nb_ref[:, sl]) * _sigmoid(bo_ref[0, :, sl])).astype(yb_ref.dtype)


def _readout(oa, ob, p3, norm_a, norm_b, n_lat):
    Bn = p3.shape[0]
    T = n_lat * TOK
    wa = A_HEADS * A_D
    wb = B_HEADS * B_DV
    ag_blk = 4
    bo_blk = (5 * A_HEADS + 4 * B_HEADS) * LANE // wb
    sa = pl.BlockSpec((1, TOK, wa), lambda b, i: (b, i, 0))
    sb = pl.BlockSpec((1, TOK, wb), lambda b, i: (b, i, 0))
    return pl.pallas_call(
        _readout_kernel,
        grid=(Bn, n_lat),
        in_specs=[sa, sa, sb, sb,
                  pl.BlockSpec((1, TOK, wa), lambda b, i: (b, i, ag_blk)),
                  pl.BlockSpec((1, TOK, wb), lambda b, i: (b, i, bo_blk)),
                  pl.BlockSpec((1, wa), lambda b, i: (0, 0)),
                  pl.BlockSpec((1, wb), lambda b, i: (0, 0))],
        out_specs=[sa, sb],
        out_shape=[jax.ShapeDtypeStruct((Bn, T, wa), BF16),
                   jax.ShapeDtypeStruct((Bn, T, wb), BF16)],
        compiler_params=_cparams(("parallel", "parallel")),
        name="readout",
    )(oa[0], oa[1], ob[0], ob[1], p3, p3, norm_a.reshape(1, wa), norm_b.reshape(1, wb))


def _merge_kernel(ya_ref, yb_ref, wa_ref, wb_ref, ga_ref, gb_ref, o_ref):
    pa = jnp.dot(ya_ref[0], wa_ref[...].astype(BF16), preferred_element_type=F32)
    pb = jnp.dot(yb_ref[0], wb_ref[...].astype(BF16), preferred_element_type=F32)
    o_ref[0] = (_sigmoid(ga_ref[0]) * pa + _sigmoid(gb_ref[0]) * pb).astype(o_ref.dtype)


def _merge(ya, yb, wa, wb, p3, D):
    Bn, T, ka = ya.shape
    kb = yb.shape[2]
    tm = _pick(T, 1024)
    tn = _pick(D, 512)
    ga0 = 0
    gb0 = D // tn
    return pl.pallas_call(
        _merge_kernel,
        grid=(Bn, T // tm, D // tn),
        in_specs=[pl.BlockSpec((1, tm, ka), lambda b, i, j: (b, i, 0)),
                  pl.BlockSpec((1, tm, kb), lambda b, i, j: (b, i, 0)),
                  pl.BlockSpec((ka, tn), lambda b, i, j: (0, j)),
                  pl.BlockSpec((kb, tn), lambda b, i, j: (0, j)),
                  pl.BlockSpec((1, tm, tn), lambda b, i, j: (b, i, ga0 + j)),
                  pl.BlockSpec((1, tm, tn), lambda b, i, j: (b, i, gb0 + j))],
        out_specs=pl.BlockSpec((1, tm, tn), lambda b, i, j: (b, i, j)),
        out_shape=jax.ShapeDtypeStruct((Bn, T, D), BF16),
        compiler_params=_cparams(("parallel", "parallel", "arbitrary")),
        name="merge",
    )(ya, yb, wa, wb, p3, p3)


def _resid_router_kernel(x_ref, mix_ref, g1_ref, g2_ref, gate_ref, sh_ref, sc_ref, wr_ref,
                         h_ref, v_ref, aff_ref):
    h = x_ref[0] + gate_ref[0] * _rms(mix_ref[0].astype(F32), g1_ref[...])
    h_ref[0] = h
    v = _rms(h, g2_ref[...]) * (1.0 + sc_ref[0]) + sh_ref[0]
    v_ref[0] = _pack_bf16_pairs(v)
    logits = jnp.dot(v, wr_ref[...], precision=lax.Precision.HIGHEST, preferred_element_type=F32)
    lane = lax.broadcasted_iota(jnp.int32, logits.shape, 1)
    logits = jnp.where(lane < N_EXPERTS, logits, -jnp.inf)
    e = jnp.exp(logits - jnp.max(logits, axis=-1, keepdims=True))
    aff_ref[0] = e / jnp.sum(e, axis=-1, keepdims=True)


def _resid_router(x, mix, g1, g2, mod3, w_router_pad):
    Bn, T, D = x.shape
    mspec = lambda k: pl.BlockSpec((1, 1, D), lambda b, i: (b, 0, k))
    tok = pl.BlockSpec((1, TOK, D), lambda b, i: (b, i, 0))
    vec = pl.BlockSpec((1, D), lambda b, i: (0, 0))
    return pl.pallas_call(
        _resid_router_kernel,
        grid=(Bn, T // TOK),
        in_specs=[tok, tok, vec, vec, mspec(2), mspec(3), mspec(4),
                  pl.BlockSpec((D, LANE), lambda b, i: (0, 0))],
        out_specs=[tok, pl.BlockSpec((1, TOK, D // 2), lambda b, i: (b, i, 0)),
                   pl.BlockSpec((1, TOK, LANE), lambda b, i: (b, i, 0))],
        out_shape=[jax.ShapeDtypeStruct((Bn, T, D), F32),
                   jax.ShapeDtypeStruct((Bn, T, D // 2), jnp.uint32),
                   jax.ShapeDtypeStruct((Bn, T, LANE), F32)],
        compiler_params=_cparams(("parallel", "parallel")),
        name="resid_router",
    )(x, mix, g1.reshape(1, D), g2.reshape(1, D), mod3, mod3, mod3, w_router_pad)


def _select_kernel(aff_ref, slot_ref, idx_ref, offs_ref, *, cap, tk):
    E, T = aff_ref.shape[1], aff_ref.shape[2]
    nck = T // LANE

    def key():
        return lax.bitcast_convert_type(aff_ref[0], jnp.int32)

    def bit_step(i, tau):
        cand = tau | jnp.left_shift(jnp.int32(1), 30 - i)
        cnt = jnp.sum(jnp.where(key() >= cand, 1.0, 0.0), axis=1, keepdims=True)
        return jnp.where(cnt >= cap, cand, tau)

    tau = lax.fori_loop(0, 31, bit_step, jnp.zeros((E, 1), jnp.int32))
    kk = key()
    gt = kk > tau
    eq = kk == tau
    need = cap - jnp.sum(jnp.where(gt, 1.0, 0.0), axis=1, keepdims=True)

    r = lax.broadcasted_iota(jnp.int32, (LANE, LANE), 0)
    c = lax.broadcasted_iota(jnp.int32, (LANE, LANE), 1)
    upper = jnp.where(r < c, 1.0, 0.0).astype(BF16)
    tr = jnp.right_shift(lax.broadcasted_iota(jnp.int32, (T, LANE), 0), LANE.bit_length() - 1)
    member = jnp.where(tr == lax.broadcasted_iota(jnp.int32, (T, LANE), 1), 1.0, 0.0).astype(BF16)

    def prefix(flags):
        x = jnp.where(flags, 1.0, 0.0).astype(BF16)
        tot = jnp.dot(x, member, preferred_element_type=F32)
        offs = jnp.dot(tot.astype(BF16), upper, preferred_element_type=F32)
        parts = [jnp.dot(x[:, j * LANE:(j + 1) * LANE], upper, preferred_element_type=F32)
                 + offs[:, j:j + 1] for j in range(nck)]
        return jnp.concatenate(parts, axis=1), offs

    tie_rank, _ = prefix(eq)
    sel = gt | (eq & (tie_rank < need))
    pos, offs = prefix(sel)
    slot = jnp.where(sel, pos, -1.0)
    slot_ref[0] = slot
    offs_ref[0] = offs.astype(jnp.int32)

    tt = lax.broadcasted_iota(jnp.int32, (8, T), 1)
    rr = lax.broadcasted_iota(jnp.int32, (8, T), 0)
    digits = jnp.where(rr == 0, jnp.right_shift(tt, 6), jnp.where(rr == 1, tt & 63, 0))
    digits = digits.astype(F32).astype(BF16)
    slot_iota = lax.broadcasted_iota(jnp.int32, (cap, tk), 0).astype(F32)
    for e in range(E):
        acc = jnp.zeros((8, cap), F32)
        for kc in range(T // tk):
            ks = slice(kc * tk, (kc + 1) * tk)
            oh = jnp.where(slot[e:e + 1, ks] == slot_iota, 1.0, 0.0).astype(BF16)
            acc = acc + lax.dot_general(digits[:, ks], oh, _NT, preferred_element_type=F32)
        idx_ref[0, e:e + 1, :] = (acc[0:1] * 64.0 + acc[1:2]).astype(jnp.int32)


def _select(aff_rows, cap):
    Bn, E, T = aff_rows.shape
    assert T // LANE < LANE and T <= 4096
    tk = _pick(T, 1024)
    row = lambda n: pl.BlockSpec((1, E, n), lambda b: (b, 0, 0))
    return pl.pallas_call(
        functools.partial(_select_kernel, cap=cap, tk=tk),
        grid=(Bn,),
        in_specs=[row(T)],
        out_specs=[row(T), row(cap), row(LANE)],
        out_shape=[jax.ShapeDtypeStruct((Bn, E, T), F32),
                   jax.ShapeDtypeStruct((Bn, E, cap), jnp.int32),
                   jax.ShapeDtypeStruct((Bn, E, LANE), jnp.int32)],
        compiler_params=_cparams(("parallel",)),
        name="ec_select",
    )(aff_rows)


def _pack_bf16_pairs(x):
    half = x.shape[1] // 2
    bits = lax.bitcast_convert_type(x.astype(BF16).astype(F32), jnp.uint32)
    return bits[:, half:] | (bits[:, :half] >> 16)


def _unpack_bf16_pairs(p):
    lo = lax.bitcast_convert_type(p << 16, F32).astype(BF16)
    hi = lax.bitcast_convert_type(p & jnp.uint32(0xFFFF0000), F32).astype(BF16)
    return lo, hi


def _gather_kernel(idx_ref, v_hbm, o_ref, buf, sem, *, cap, n_tok):
    b = pl.program_id(0)
    e = pl.program_id(1)
    base = (b * N_EXPERTS + e) * cap

    hc = cap // 2

    def row_copy(r, tok):
        return pltpu.make_async_copy(v_hbm.at[pl.ds(b * n_tok + tok, 1), :], buf.at[pl.ds(r, 1), :],
                                     sem.at[r // hc])

    def issue(p, carry):
        for prio in range(2):
            r = 2 * p + prio
            row_copy(r, idx_ref[base + r]).start(priority=prio)
        return carry

    def drain(r, carry):
        row_copy(r, 0).wait()
        return carry

    lax.fori_loop(0, cap // 2, issue, 0)
    half = buf.shape[1]
    for h in range(2):
        rows = slice(h * hc, (h + 1) * hc)
        lax.fori_loop(h * hc, (h + 1) * hc, drain, 0)
        lo, hi = _unpack_bf16_pairs(buf[rows])
        o_ref[0, 0, rows, :half] = lo
        o_ref[0, 0, rows, half:] = hi


def _gather(idx, v_pairs, cap):
    Bn, T, half = v_pairs.shape
    D = 2 * half
    return pl.pallas_call(
        functools.partial(_gather_kernel, cap=cap, n_tok=T),
        grid_spec=pltpu.PrefetchScalarGridSpec(
            num_scalar_prefetch=1,
            grid=(Bn, N_EXPERTS),
            in_specs=[pl.BlockSpec(memory_space=pl.ANY)],
            out_specs=pl.BlockSpec((1, 1, cap, D), lambda b, e, idx_ref: (e, b, 0, 0)),
            scratch_shapes=[pltpu.VMEM((cap, half), jnp.uint32), pltpu.SemaphoreType.DMA((2,))]),
        out_shape=jax.ShapeDtypeStruct((N_EXPERTS, Bn, cap, D), BF16),
        compiler_params=_cparams(("arbitrary", "arbitrary")),
        name="ec_gather",
    )(idx.reshape(-1), v_pairs.reshape(Bn * T, half))


def _ffn1_kernel(x_ref, wg_ref, wu_ref, o_ref):
    x = x_ref[0]
    g = jnp.dot(x, wg_ref[0].astype(BF16), preferred_element_type=F32)
    u = jnp.dot(x, wu_ref[0].astype(BF16), preferred_element_type=F32)
    o_ref[0] = (_silu(g) * u).astype(o_ref.dtype)


def _ffn1(xg, wg, wu):
    E, M, D = xg.shape
    F = wg.shape[2]
    tm = _pick(M, 1024)
    tn = _pick(F, 256)
    return pl.pallas_call(
        _ffn1_kernel,
        grid=(E, M // tm, F // tn),
        in_specs=[pl.BlockSpec((1, tm, D), lambda e, i, j: (e, i, 0)),
                  pl.BlockSpec((1, D, tn), lambda e, i, j: (e, 0, j)),
                  pl.BlockSpec((1, D, tn), lambda e, i, j: (e, 0, j))],
        out_specs=pl.BlockSpec((1, tm, tn), lambda e, i, j: (e, i, j)),
        out_shape=jax.ShapeDtypeStruct((E, M, F), BF16),
        compiler_params=_cparams(("parallel", "parallel", "arbitrary")),
        name="ec_ffn_up",
    )(xg, wg, wu)


def _ffn2_kernel(h_ref, wd_ref, o_ref):
    o_ref[0] = jnp.dot(h_ref[0], wd_ref[0].astype(BF16),
                       preferred_element_type=F32).astype(o_ref.dtype)


def _ffn2(hid, wd):
    E, M, F = hid.shape
    D = wd.shape[2]
    tn = _pick(D, 512)
    return pl.pallas_call(
        _ffn2_kernel,
        grid=(E, D // tn),
        in_specs=[pl.BlockSpec((1, M, F), lambda e, j: (e, 0, 0)),
                  pl.BlockSpec((1, F, tn), lambda e, j: (e, 0, j))],
        out_specs=pl.BlockSpec((1, M, tn), lambda e, j: (e, 0, j)),
        out_shape=jax.ShapeDtypeStruct((E, M, D), BF16),
        compiler_params=_cparams(("parallel", "arbitrary")),
        name="ec_ffn_down",
    )(hid, wd)


def _scatter_kernel(offs_ref, slot_ref, aff_ref, y_ref, h_ref, g_ref, gate_ref, o_ref, *, cap, win):
    b = pl.program_id(0)
    t = pl.program_id(1)
    e = pl.program_id(2)
    tq = o_ref.shape[1]

    @pl.when(e == 0)
    def _():
        o_ref[...] = jnp.zeros_like(o_ref)

    base = (b * N_EXPERTS + e) * LANE
    lo = offs_ref[base + t * (tq // LANE)]
    hi = offs_ref[base + (t + 1) * (tq // LANE)]
    sl = jnp.broadcast_to(slot_ref[0, 0], (LANE, tq)).T
    af = jnp.broadcast_to(aff_ref[0, 0], (LANE, tq)).T
    sl = jnp.concatenate([sl] * (win // LANE), axis=1)
    af = jnp.concatenate([af] * (win // LANE), axis=1)
    lane = lax.broadcasted_iota(jnp.int32, (tq, win), 1).astype(F32)
    for w0 in range(0, cap, win):
        @pl.when((lo < w0 + win) & (hi > w0))
        def _():
            ohw = jnp.where(sl == lane + float(w0), af, 0.0).astype(BF16)
            o_ref[0] += jnp.dot(ohw, y_ref[0, 0, w0:w0 + win, :], preferred_element_type=F32)

    @pl.when(e == pl.num_programs(2) - 1)
    def _():
        o_ref[0] = h_ref[0] + gate_ref[0] * _rms(o_ref[0], g_ref[...])


def _scatter_combine(offs, slot_rows, aff_rows, y, cap, h, g3, mod3):
    E, Bn, _, D = y.shape
    T = slot_rows.shape[3]
    tq = _pick(T, 512)
    win = _pick(cap, 256)
    assert win % LANE == 0 and tq % LANE == 0
    row = pl.BlockSpec((1, 1, 1, tq), lambda b, t, e, offs_ref: (b, e, 0, t))
    return pl.pallas_call(
        functools.partial(_scatter_kernel, cap=cap, win=win),
        grid_spec=pltpu.PrefetchScalarGridSpec(
            num_scalar_prefetch=1,
            grid=(Bn, T // tq, E),
            in_specs=[row, row,
                      pl.BlockSpec((1, 1, cap, D), lambda b, t, e, offs_ref: (e, b, 0, 0)),
                      pl.BlockSpec((1, tq, D), lambda b, t, e, offs_ref: (b, t, 0)),
                      pl.BlockSpec((1, D), lambda b, t, e, offs_ref: (0, 0)),
                      pl.BlockSpec((1, 1, D), lambda b, t, e, offs_ref: (b, 0, 5))],
            out_specs=pl.BlockSpec((1, tq, D), lambda b, t, e, offs_ref: (b, t, 0))),
        out_shape=jax.ShapeDtypeStruct((Bn, T, D), F32),
        compiler_params=_cparams(("parallel", "parallel", "arbitrary")),
        name="ec_scatter",
    )(offs.reshape(-1), slot_rows, aff_rows, y, h, g3.reshape(1, D), mod3)


def _layer(h_lat, ctx, c8, l, lb_l, w_ada, b_ada, g_norm, w_in, hgrn_f_bias, hgrn_norm,
           mlstm_conv_w, mlstm_conv_b, mlstm_gate_b, mlstm_norm, w_branch_a, w_branch_b, w_out,
           w_router, w_expert_gate, w_expert_up, w_expert_down):
    Bn, T, D = h_lat.shape
    n_lat = T // TOK
    n_tok = T + TOK
    A, Bh = A_HEADS, B_HEADS
    g0 = (5 * A + 6 * Bh) * LANE

    mod = _modulation(c8, w_ada[l], b_ada[l])
    mod3 = mod.reshape(8, 1, N_MOD * D)

    u = _prenorm(h_lat, ctx, g_norm[l, 0], mod3)

    w_in_t = jnp.swapaxes(w_in, 1, 2)
    w_merge_t, w_gate_t = _wsplit(w_in_t, l, g0, 4 * Bh, 2 * D)
    gate_bias = jnp.pad(mlstm_gate_b[l].reshape(4 * Bh), (0, LANE - 4 * Bh))
    u2 = u.reshape(Bn * n_tok, D)
    p3 = _matmul_t(u2, w_in_t, F32, n_cols=g0, layer=l, name="in_proj").reshape(Bn, n_tok, g0)
    p_merge = _matmul_t_rows(u, w_merge_t, T, F32, name="in_proj_merge")
    gates = _matmul_t(u2, w_gate_t, F32, tn=LANE, bias=gate_bias, name="in_proj_gates")

    g6 = gates[:, :4 * Bh].reshape(Bn, n_tok // CHUNK, CHUNK, 2, 2, Bh)
    g_rows = g6.transpose(0, 3, 5, 1, 4, 2)
    g_cols = g6.transpose(0, 3, 5, 1, 2, 4)
    conv4 = jnp.concatenate([mlstm_conv_w[l], mlstm_conv_b[l][None]], axis=0)
    conv4 = conv4.reshape(4, 1, 2 * Bh * B_DQK)

    oa = _hgrn_scan(p3, hgrn_f_bias[l], lb_l, n_lat)
    ob = _mlstm_scan(p3, g_rows, g_cols, conv4, n_lat)
    ya, yb = _readout(oa, ob, p3, hgrn_norm[l], mlstm_norm[l], n_lat)
    merged = _merge(ya, yb, w_branch_a[l], w_branch_b[l], p_merge, D)
    mix = _matmul(merged.reshape(Bn * T, D), w_out[l], BF16, name="out_proj")

    w_router_pad = jnp.pad(w_router[l], ((0, 0), (0, LANE - N_EXPERTS)))
    h_lat, v_lat, aff = _resid_router(h_lat, mix.reshape(Bn, T, D), g_norm[l, 1], g_norm[l, 2],
                                      mod3, w_router_pad)

    cap = CAPACITY * T // N_EXPERTS
    aff_rows = aff[:, :, :N_EXPERTS].transpose(0, 2, 1)
    slot, idx, offs = _select(aff_rows, cap)
    xg = _gather(idx, v_lat, cap)
    hid = _ffn1(xg.reshape(N_EXPERTS, Bn * cap, D), w_expert_gate[l], w_expert_up[l])
    y = _ffn2(hid, w_expert_down[l]).reshape(N_EXPERTS, Bn, cap, D)
    return _scatter_combine(offs, slot.reshape(Bn, N_EXPERTS, 1, T),
                            aff_rows.reshape(Bn, N_EXPERTS, 1, T), y, cap, h_lat, g_norm[l, 3], mod3)


def kernel(x, c, ctx, c_ctx, w_ada, b_ada, g_norm, w_in, hgrn_f_bias, hgrn_lb, hgrn_norm,
           mlstm_conv_w, mlstm_conv_b, mlstm_gate_b, mlstm_norm, w_branch_a, w_branch_b, w_out,
           w_router, w_expert_gate, w_expert_up, w_expert_down):
    Bn, T, D = x.shape
    depth = w_ada.shape[0]
    assert depth == 1, "context outputs are only produced for the state hand-off (single layer)"
    assert ctx.shape[1] == TOK and T % TOK == 0 and Bn < 8
    lb_all = jnp.cumsum(jax.nn.softmax(hgrn_lb.astype(F32), axis=1), axis=1)
    c8 = jnp.zeros((8, D), F32).at[:Bn].set(c).at[Bn].set(c_ctx)
    h_lat = x
    for l in range(depth):
        h_lat = _layer(h_lat, ctx, c8, l, lb_all[:, l], w_ada, b_ada, g_norm, w_in, hgrn_f_bias,
                       hgrn_norm, mlstm_conv_w, mlstm_conv_b, mlstm_gate_b, mlstm_norm,
                       w_branch_a, w_branch_b, w_out, w_router, w_expert_gate, w_expert_up,
                       w_expert_down)
    return h_lat.astype(x.dtype)
```

```python
import functools

import jax
import jax.numpy as jnp
from jax import lax
from jax.experimental import pallas as pl
from jax.experimental.pallas import tpu as pltpu

F32 = jnp.float32
BF16 = jnp.bfloat16

EPS = 1e-6
N_MOD = 6
A_HEADS = 16
A_D = 128
B_HEADS = 8
B_DQK = 128
B_DV = 256
N_EXPERTS = 16
CAPACITY = 2
CHUNK = 64
SUB = 16
A_HPS = 4
B_HPS = 4
TOK = 256
NCH = TOK // CHUNK
NSUB = CHUNK // SUB
LANE = 128
VMEM_LIMIT = 56 * 1024 * 1024

_NT = (((1,), (1,)), ((), ()))


def _pick(n, pref):
    t = min(n, pref)
    while n % t:
        t //= 2
    return t


def _cparams(sem):
    return pltpu.CompilerParams(dimension_semantics=sem, vmem_limit_bytes=VMEM_LIMIT)


def _silu(x):
    return x / (1.0 + jnp.exp(-x))


def _sigmoid(x):
    return 1.0 / (1.0 + jnp.exp(-x))


def _sig_pair(z):
    t = jnp.exp(-jnp.abs(z))
    r = 1.0 / (1.0 + t)
    tr = t * r
    pos = z >= 0
    return jnp.where(pos, r, tr), jnp.where(pos, tr, r)


def _log_sigmoid(x):
    return jnp.minimum(x, 0.0) - jnp.log(1.0 + jnp.exp(-jnp.abs(x)))


def _rms(xf, w):
    return xf * lax.rsqrt(jnp.mean(xf * xf, axis=-1, keepdims=True) + EPS) * w


def _rows(x, r, n):
    return jnp.broadcast_to(x[r:r + 1], (n, x.shape[1]))


def _mod_kernel(c_ref, w_ref, b_ref, o_ref):
    a = _silu(c_ref[...]).astype(BF16)
    o_ref[...] = jnp.dot(a, w_ref[...].astype(BF16), preferred_element_type=F32) + b_ref[...]


def _modulation(c8, w, b):
    D, N = w.shape
    tn = _pick(N, 512)
    return pl.pallas_call(
        _mod_kernel,
        grid=(N // tn,),
        in_specs=[pl.BlockSpec((8, D), lambda j: (0, 0)),
                  pl.BlockSpec((D, tn), lambda j: (0, j)),
                  pl.BlockSpec((1, tn), lambda j: (0, j))],
        out_specs=pl.BlockSpec((8, tn), lambda j: (0, j)),
        out_shape=jax.ShapeDtypeStruct((8, N), F32),
        compiler_params=_cparams(("parallel",)),
        name="adaln_mod",
    )(c8, w, b.reshape(1, N))


def _prenorm_kernel(x_ref, ctx_ref, g_ref, sh_ref, sc_ref, o_ref, *, n_lat):
    i = pl.program_id(1)
    g = g_ref[...]
    sh = sh_ref[0]
    sc = sc_ref[0]

    @pl.when(i < n_lat)
    def _():
        o_ref[0] = (_rms(x_ref[0], g) * (1.0 + sc) + sh).astype(o_ref.dtype)

    @pl.when(i == n_lat)
    def _():
        o_ref[0] = (_rms(ctx_ref[0], g) * (1.0 + sc) + sh).astype(o_ref.dtype)


def _prenorm(x, ctx, g, mod3):
    Bn, T, D = x.shape
    n_lat = T // TOK
    row = lambda b, i: jnp.where(i == n_lat, Bn, b)
    return pl.pallas_call(
        functools.partial(_prenorm_kernel, n_lat=n_lat),
        grid=(Bn, n_lat + 1),
        in_specs=[pl.BlockSpec((1, TOK, D), lambda b, i: (b, jnp.minimum(i, n_lat - 1), 0)),
                  pl.BlockSpec((1, TOK, D), lambda b, i: (b, 0, 0)),
                  pl.BlockSpec((1, D), lambda b, i: (0, 0)),
                  pl.BlockSpec((1, 1, D), lambda b, i: (row(b, i), 0, 0)),
                  pl.BlockSpec((1, 1, D), lambda b, i: (row(b, i), 0, 1))],
        out_specs=pl.BlockSpec((1, TOK, D), lambda b, i: (b, i, 0)),
        out_shape=jax.ShapeDtypeStruct((Bn, T + TOK, D), BF16),
        compiler_params=_cparams(("parallel", "arbitrary")),
        name="prenorm",
    )(x, ctx, g.reshape(1, D), mod3, mod3)


_NN = (((1,), (0,)), ((), ()))


def _mm_kernel(a_ref, b_ref, o_ref, *, dims):
    o_ref[...] = lax.dot_general(a_ref[...], b_ref[...].astype(BF16), dims,
                                 preferred_element_type=F32).astype(o_ref.dtype)


def _mm_bias_kernel(a_ref, b_ref, bias_ref, o_ref, *, dims):
    o_ref[...] = (lax.dot_general(a_ref[...], b_ref[...].astype(BF16), dims, preferred_element_type=F32)
                  + bias_ref[...]).astype(o_ref.dtype)


def _matmul(a, b, out_dtype, tm=1024, tn=512, bias=None, name="matmul"):
    M, K = a.shape
    N = b.shape[1]
    tm = _pick(M, tm)
    tn = _pick(N, tn)
    return _matmul_call(a, b, pl.BlockSpec((K, tn), lambda i, j: (0, j)), _NN, N, out_dtype, tm, tn,
                        bias, name)


def _matmul_t(a, bt, out_dtype, tm=1024, tn=512, bias=None, n_cols=None, layer=None, name="matmul_t"):
    M, K = a.shape
    N = bt.shape[-2] if n_cols is None else n_cols
    tm = _pick(M, tm)
    tn = _pick(N, tn)
    if layer is None:
        b_spec = pl.BlockSpec((tn, K), lambda i, j: (j, 0))
    else:
        b_spec = pl.BlockSpec((None, tn, K), lambda i, j: (layer, j, 0))
    return _matmul_call(a, bt, b_spec, _NT, N, out_dtype, tm, tn, bias, name)


def _matmul_call(a, b, b_spec, dims, N, out_dtype, tm, tn, bias, name):
    M, K = a.shape
    in_specs = [pl.BlockSpec((tm, K), lambda i, j: (i, 0)), b_spec]
    args = [a, b]
    kern = functools.partial(_mm_kernel, dims=dims)
    if bias is not None:
        in_specs.append(pl.BlockSpec((1, tn), lambda i, j: (0, j)))
        args.append(bias.reshape(1, N))
        kern = functools.partial(_mm_bias_kernel, dims=dims)
    return pl.pallas_call(
        kern,
        grid=(M // tm, N // tn),
        in_specs=in_specs,
        out_specs=pl.BlockSpec((tm, tn), lambda i, j: (i, j)),
        out_shape=jax.ShapeDtypeStruct((M, N), out_dtype),
        compiler_params=_cparams(("parallel", "arbitrary")),
        name=name,
    )(*args)


def _matmul_t_rows(a3, bt, n_rows, out_dtype, tm=1024, tn=512, name="matmul_t_rows"):
    Bn, _, K = a3.shape
    N = bt.shape[0]
    tm = _pick(n_rows, tm)
    tn = _pick(N, tn)
    return pl.pallas_call(
        functools.partial(_mm_kernel, dims=_NT),
        grid=(Bn, n_rows // tm, N // tn),
        in_specs=[pl.BlockSpec((None, tm, K), lambda b, i, j: (b, i, 0)),
                  pl.BlockSpec((tn, K), lambda b, i, j: (j, 0))],
        out_specs=pl.BlockSpec((None, tm, tn), lambda b, i, j: (b, i, j)),
        out_shape=jax.ShapeDtypeStruct((Bn, n_rows, N), out_dtype),
        compiler_params=_cparams(("parallel", "parallel", "arbitrary")),
        name=name,
    )(a3, bt)


def _wsplit_kernel(a_ref, b_ref, o_ref, g_ref, *, shift):
    j = pl.program_id(1)
    tn = o_ref.shape[0]
    cat = jnp.concatenate([a_ref[...], b_ref[...]], axis=0)
    o_ref[...] = cat[shift:shift + tn].astype(o_ref.dtype)

    @pl.when(j == 0)
    def _():
        row = lax.broadcasted_iota(jnp.int32, g_ref.shape, 0)
        g_ref[...] = jnp.where(row < shift, cat[:LANE], 0.0).astype(g_ref.dtype)


def _wsplit(wt, layer, row0, shift, n):
    K = wt.shape[2]
    tk = _pick(K, 1024)
    tn = _pick(n, 512)
    assert row0 % tn == 0 and 0 < shift < LANE and shift % 8 == 0 and row0 + shift + n <= wt.shape[1]
    r0 = row0 // tn
    return pl.pallas_call(
        functools.partial(_wsplit_kernel, shift=shift),
        grid=(K // tk, n // tn),
        in_specs=[pl.BlockSpec((None, tn, tk), lambda i, j: (layer, r0 + j, i)),
                  pl.BlockSpec((None, tn, tk), lambda i, j: (layer, r0 + j + 1, i))],
        out_specs=[pl.BlockSpec((tn, tk), lambda i, j: (j, i)),
                   pl.BlockSpec((LANE, tk), lambda i, j: (0, i))],
        out_shape=[jax.ShapeDtypeStruct((n, K), BF16), jax.ShapeDtypeStruct((LANE, K), BF16)],
        compiler_params=_cparams(("parallel", "arbitrary")),
        name="w_split",
    )(wt, wt)


def _tok_block(rev, j, n_lat):
    lat = n_lat - j if rev else j - 1
    return jnp.where(j == 0, n_lat, lat)


def _out_block(rev, j, n_lat):
    return jnp.clip(n_lat - j if rev else j - 1, 0, n_lat - 1)


def _chunk_masks(rev):
    row = lax.broadcasted_iota(jnp.int32, (CHUNK, CHUNK), 0)
    col = lax.broadcasted_iota(jnp.int32, (CHUNK, CHUNK), 1)
    return (col >= row, col <= row) if rev else (col <= row, col >= row)


def _hgrn_consts():
    t = jnp.arange(TOK)
    same = (t[:, None] // CHUNK) == (t[None, :] // CHUNK)
    sub_r, sub_c = t[:, None] // SUB, t[None, :] // SUB
    fwd = jnp.concatenate([same & (t[None, :] <= t[:, None]), same & (sub_c < sub_r)], axis=0)
    bwd = jnp.concatenate([same & (t[None, :] >= t[:, None]), same & (sub_c > sub_r)], axis=0)
    masks = jnp.stack([fwd, bwd]).astype(BF16)
    r = jnp.arange(SUB * A_D)
    emat = ((r[:, None] // A_D) == (jnp.arange(LANE)[None, :] % SUB)).astype(BF16)
    return masks, emat


def _hgrn_state(af, v, fb, lb, mask_ref, st_ref, rev):
    z = af + fb
    sp, sn = _sig_pair(z)
    k = (1.0 - lb) * sn
    lf2 = jnp.log2(lb + (1.0 - lb) * sp)
    lk2 = jnp.log2(k)

    hi = lf2.astype(BF16)
    r1 = lf2 - hi.astype(F32)
    mid = r1.astype(BF16)
    lo = (r1 - mid.astype(F32)).astype(BF16)
    cs = jnp.dot(mask_ref[0], jnp.concatenate([hi, mid, lo], axis=1), preferred_element_type=F32)
    cs = cs[:, :A_D] + cs[:, A_D:2 * A_D] + cs[:, 2 * A_D:]
    b = cs[:TOK]
    ent = cs[TOK:]
    c_all = lk2 - b

    last = [(c * CHUNK if rev else c * CHUNK + CHUNK - 1) for c in range(NCH)]
    tot_rows = jnp.concatenate([_rows(b, last[c], CHUNK) for c in range(NCH)], axis=0)
    khat = jnp.exp2(jnp.minimum(tot_rows + c_all, lk2)).astype(BF16)
    upd = [jnp.dot(v[c * CHUNK:(c + 1) * CHUNK].T.astype(BF16), khat[c * CHUNK:(c + 1) * CHUNK],
                   preferred_element_type=F32) for c in range(NCH)]

    order = range(NCH - 1, -1, -1) if rev else range(NCH)
    st = st_ref[...]
    st_in = [None] * NCH
    for c in order:
        st_in[c] = st
        st = st * jnp.exp2(b[last[c]:last[c] + 1]) + upd[c]
    st_ref[...] = st
    return v, b, ent, c_all, lk2, st_in


def _hgrn_output(aq, feats, emat_ref, o_ref, hs, rev):
    v, b, ent, c_all, lk2, st_in = feats
    q = _silu(aq)
    qb = q.astype(BF16)
    vb = v.astype(BF16)
    qt = (q * jnp.exp2(b - ent)).astype(BF16)
    row = lax.broadcasted_iota(jnp.int32, (CHUNK, CHUNK), 0)
    col = lax.broadcasted_iota(jnp.int32, (CHUNK, CHUNK), 1)
    rs = jnp.right_shift(row, SUB.bit_length() - 1)
    cs_ = jnp.right_shift(col, SUB.bit_length() - 1)
    prev_blk = (cs_ > rs) if rev else (cs_ < rs)
    diag_blk = (rs == cs_) & ((col >= row) if rev else (col <= row))

    off = [[None] * NSUB for _ in range(NCH)]
    for i in range(NSUB):
        e_i = jnp.concatenate([_rows(ent, c * CHUNK + i * SUB, CHUNK) for c in range(NCH)], axis=0)
        kt = jnp.exp2(jnp.minimum(e_i + c_all, lk2)).astype(BF16)
        for c in range(NCH):
            r0 = c * CHUNK + i * SUB
            off[c][i] = lax.dot_general(qt[r0:r0 + SUB], kt[c * CHUNK:(c + 1) * CHUNK], _NT,
                                        preferred_element_type=F32)

    half = SUB // 2
    nsb = TOK // SUB
    zeros = jnp.zeros((half, A_D), F32)
    ps = []
    for sl in range(SUB):
        n = half if (sl < half if rev else sl >= half) else SUB
        r0 = SUB - n if not rev else 0
        bq = b if n == SUB else jnp.concatenate(
            [b[g * SUB + r0:g * SUB + r0 + n] for g in range(nsb)], axis=0)
        cref = jnp.concatenate([_rows(c_all, g * SUB + sl, n) for g in range(nsb)], axis=0)
        kref = jnp.concatenate([_rows(lk2, g * SUB + sl, n) for g in range(nsb)], axis=0)
        e = jnp.exp2(jnp.minimum(bq + cref, kref))
        if n != SUB:
            parts = []
            for g in range(nsb):
                piece = e[g * half:(g + 1) * half]
                parts += [piece, zeros] if rev else [zeros, piece]
            e = jnp.concatenate(parts, axis=0)
        ps.append(e.astype(BF16) * qb)
    acc = jnp.dot(jnp.concatenate(ps, axis=1), emat_ref[...], preferred_element_type=F32)

    qhat = (q * jnp.exp2(b)).astype(BF16)
    for c in range(NCH):
        sl_c = slice(c * CHUNK, (c + 1) * CHUNK)
        attn = (jnp.where(prev_blk, jnp.concatenate(off[c], axis=0), 0.0)
                + jnp.where(diag_blk, acc[sl_c, :CHUNK], 0.0))
        o = jnp.dot(attn.astype(BF16), vb[sl_c], preferred_element_type=F32)
        o = o + lax.dot_general(qhat[sl_c], st_in[c].astype(BF16), _NT, preferred_element_type=F32)
        o_ref[0, sl_c, hs] = o.astype(o_ref.dtype)


def _hgrn_kernel(aqf, aff, aif, aqb, afb, aib, fb_ref, lb_ref, mask_ref, emat_ref, of_ref, ob_ref,
                 stf_ref, stb_ref):
    j = pl.program_id(2)

    @pl.when(j == 0)
    def _():
        stf_ref[...] = jnp.zeros_like(stf_ref)
        stb_ref[...] = jnp.zeros_like(stb_ref)

    feats = []
    for hh in range(A_HPS):
        hs = slice(hh * A_D, (hh + 1) * A_D)
        feats.append((hs,
                      _hgrn_state(aff[0, :, hs], aif[0, :, hs], fb_ref[0][:, hs], lb_ref[0][:, hs],
                                  mask_ref.at[0:1], stf_ref.at[hh], False),
                      _hgrn_state(afb[0, :, hs], aib[0, :, hs], fb_ref[1][:, hs], lb_ref[1][:, hs],
                                  mask_ref.at[1:2], stb_ref.at[hh], True)))
    for hs, feats_f, feats_b in feats:
        _hgrn_output(aqf[0, :, hs], feats_f, emat_ref, of_ref, hs, False)
        _hgrn_output(aqb[0, :, hs], feats_b, emat_ref, ob_ref, hs, True)


def _hgrn_scan(p3, f_bias, lb, n_lat):
    Bn = p3.shape[0]
    A = A_HEADS
    masks, emat = _hgrn_consts()

    W = A_HPS * A_D
    G = A // A_HPS

    def feat(rev, grp):
        return pl.BlockSpec((1, TOK, W), lambda b, h, j: (b, _tok_block(rev, j, n_lat), grp * G + h))

    def outp(rev):
        return pl.BlockSpec((1, TOK, W), lambda b, h, j: (b, _out_block(rev, j, n_lat), h))

    par = pl.BlockSpec((2, 1, W), lambda b, h, j: (0, 0, h))
    osh = jax.ShapeDtypeStruct((Bn, n_lat * TOK, A * A_D), BF16)
    return pl.pallas_call(
        _hgrn_kernel,
        grid=(Bn, G, n_lat + 1),
        in_specs=[feat(False, 0), feat(False, 1), feat(False, 3),
                  feat(True, 0), feat(True, 2), feat(True, 3),
                  par, par,
                  pl.BlockSpec((2, 2 * TOK, TOK), lambda b, h, j: (0, 0, 0)),
                  pl.BlockSpec((SUB * A_D, LANE), lambda b, h, j: (0, 0))],
        out_specs=[outp(False), outp(True)],
        out_shape=[osh, osh],
        scratch_shapes=[pltpu.VMEM((A_HPS, A_D, A_D), F32), pltpu.VMEM((A_HPS, A_D, A_D), F32)],
        compiler_params=_cparams(("parallel", "parallel", "arbitrary")),
        name="hgrn2_scan",
    )(p3, p3, p3, p3, p3, p3, f_bias.reshape(2, 1, A * A_D), lb.reshape(2, 1, A * A_D), masks, emat)


def _mlstm_state(bq_ref, bk_ref, bv_ref, gr_ref, gc_ref, cwq, cwk, first, last_, c_ref, n_ref, m_ref,
                 hh, rev):
    def conv(u, cw):
        up = jnp.where(first, 0.0, pltpu.roll(u, 1, 0))
        dn = jnp.where(last_, 0.0, pltpu.roll(u, TOK - 1, 0))
        return _silu(cw[3:4] + up * cw[0:1] + u * cw[1:2] + dn * cw[2:3])

    hs = slice(hh * B_DQK, (hh + 1) * B_DQK)
    vs = slice(hh * B_DV, (hh + 1) * B_DV)
    c_ref, n_ref, m_ref = c_ref.at[hh], n_ref.at[hh], m_ref.at[hh]
    q_all = conv(bq_ref[0, :, hs], cwq[:, hs])
    k_all = conv(bk_ref[0, :, hs], cwk[:, hs]) * (B_DQK ** -0.5)
    seen, seen_t = _chunk_masks(rev)

    pre = []
    for c in range(NCH):
        sl_c = slice(c * CHUNK, (c + 1) * CHUNK)
        k = k_all[sl_c]
        vb = bv_ref[0, sl_c, vs].astype(BF16)
        g_r = gr_ref[0, 0, hh, c]
        g_c = gc_ref[0, 0, hh, c]
        ii_r = g_r[0:1]
        lf_r = _log_sigmoid(g_r[1:2])
        ii_c = g_c[:, 0:1]
        lf_c = _log_sigmoid(g_c[:, 1:2])
        b_c = jnp.sum(jnp.where(seen, lf_r, 0.0), axis=1, keepdims=True)
        b_r = jnp.sum(jnp.where(seen_t, lf_c, 0.0), axis=0, keepdims=True)
        total = jnp.sum(lf_r, axis=1, keepdims=True)
        logs = total - b_c + ii_c
        ms = jnp.max(logs, axis=0, keepdims=True)
        kw = k * jnp.exp(logs - ms)
        upd = jnp.dot(kw.T.astype(BF16), vb, preferred_element_type=F32)
        nupd = jnp.sum(kw, axis=0, keepdims=True)
        pre.append((vb, ii_r, b_c, b_r, total, ms, upd, nupd))

    cmat, nvec, m = c_ref[...], n_ref[...], m_ref[:, 0:1]
    st_in = [None] * NCH
    for c in (range(NCH - 1, -1, -1) if rev else range(NCH)):
        _, _, _, _, total, ms, upd, nupd = pre[c]
        st_in[c] = (cmat, nvec, m)
        m_new = jnp.maximum(total + m, ms)
        dec = jnp.exp(total + m - m_new)
        sc = jnp.exp(ms - m_new)
        cmat = dec * cmat + sc * upd
        nvec = dec * nvec + sc * nupd
        m = m_new
    c_ref[...] = cmat
    n_ref[...] = nvec
    m_ref[...] = jnp.broadcast_to(m, m_ref.shape)
    return q_all, k_all, pre, st_in


def _mlstm_output(feats, o_ref, hh, rev):
    q_all, k_all, pre, st_in = feats
    seen, _ = _chunk_masks(rev)
    for c in range(NCH):
        sl_c = slice(c * CHUNK, (c + 1) * CHUNK)
        vb, ii_r, b_c, b_r, _, _, _, _ = pre[c]
        cm, nv, m0 = st_in[c]
        q = q_all[sl_c]
        qb = q.astype(BF16)
        kb = k_all[sl_c].astype(BF16)
        logw = jnp.where(seen, b_c - b_r + ii_r, -jnp.inf)
        mw = jnp.max(logw, axis=1, keepdims=True)
        qk = lax.dot_general(qb, kb, _NT, preferred_element_type=F32) * jnp.exp(logw - mw)
        num0 = jnp.dot(qk.astype(BF16), vb, preferred_element_type=F32)
        den0 = jnp.sum(qk, axis=1, keepdims=True)
        log_inter = b_c + m0
        m_t = jnp.maximum(mw, log_inter)
        r = jnp.exp(mw - m_t)
        a = jnp.exp(log_inter - m_t)
        num = r * num0 + a * jnp.dot(qb, cm.astype(BF16), preferred_element_type=F32)
        den = r * den0 + a * jnp.sum(q * nv, axis=1, keepdims=True)
        h = num / jnp.maximum(jnp.abs(den), jnp.exp(-m_t))
        o_ref[0, sl_c, hh * B_DV:(hh + 1) * B_DV] = h.astype(o_ref.dtype)


def _mlstm_kernel(bqf, bkf, bvf, grf, gcf, bqb, bkb, bvb, grb, gcb, cwq_ref, cwk_ref, of_ref, ob_ref,
                  cf_ref, nf_ref, mf_ref, cb_ref, nb_ref, mb_ref):
    j = pl.program_id(2)

    @pl.when(j == 0)
    def _():
        for ref in (cf_ref, nf_ref, mf_ref, cb_ref, nb_ref, mb_ref):
            ref[...] = jnp.zeros_like(ref)

    t = lax.broadcasted_iota(jnp.int32, (TOK, 1), 0)
    row_mask = jnp.where(j == 0, TOK - 1, CHUNK - 1)
    pos = t & row_mask
    first = pos == 0
    last_ = pos == row_mask
    cwq = cwq_ref[:, 0, :]
    cwk = cwk_ref[:, 0, :]
    feats = []
    for hh in range(B_HPS):
        feats.append((_mlstm_state(bqf, bkf, bvf, grf, gcf, cwq, cwk, first, last_, cf_ref, nf_ref, mf_ref,
                                   hh, False),
                      _mlstm_state(bqb, bkb, bvb, grb, gcb, cwq, cwk, first, last_, cb_ref, nb_ref, mb_ref,
                                   hh, True)))
    for hh, (feats_f, feats_b) in enumerate(feats):
        _mlstm_output(feats_f, of_ref, hh, False)
        _mlstm_output(feats_b, ob_ref, hh, True)


def _mlstm_scan(p3, g_rows, g_cols, conv4, n_lat):
    Bn = p3.shape[0]
    A, Bh = A_HEADS, B_HEADS
    P = B_HPS
    G = Bh // P
    q0 = 5 * A // P
    k0 = (5 * A + Bh) // P
    v0 = (5 * A + 2 * Bh) // (2 * P)

    def specs(rev):
        d = int(rev)
        tb = lambda j: _tok_block(rev, j, n_lat)
        return [pl.BlockSpec((1, TOK, P * B_DQK), lambda b, h, j: (b, tb(j), q0 + h)),
                pl.BlockSpec((1, TOK, P * B_DQK), lambda b, h, j: (b, tb(j), k0 + h)),
                pl.BlockSpec((1, TOK, P * B_DV), lambda b, h, j: (b, tb(j), v0 + h)),
                pl.BlockSpec((1, 1, P, NCH, 2, CHUNK), lambda b, h, j: (b, d, h, tb(j), 0, 0)),
                pl.BlockSpec((1, 1, P, NCH, CHUNK, 2), lambda b, h, j: (b, d, h, tb(j), 0, 0))]

    def outp(rev):
        return pl.BlockSpec((1, TOK, P * B_DV), lambda b, h, j: (b, _out_block(rev, j, n_lat), h))

    osh = jax.ShapeDtypeStruct((Bn, n_lat * TOK, Bh * B_DV), BF16)
    state = [pltpu.VMEM((P, B_DQK, B_DV), F32), pltpu.VMEM((P, 1, B_DQK), F32),
             pltpu.VMEM((P, 1, LANE), F32)]
    return pl.pallas_call(
        _mlstm_kernel,
        grid=(Bn, G, n_lat + 1),
        in_specs=specs(False) + specs(True) + [
            pl.BlockSpec((4, 1, P * B_DQK), lambda b, h, j: (0, 0, h)),
            pl.BlockSpec((4, 1, P * B_DQK), lambda b, h, j: (0, 0, G + h))],
        out_specs=[outp(False), outp(True)],
        out_shape=[osh, osh],
        scratch_shapes=state + state,
        compiler_params=_cparams(("parallel", "parallel", "arbitrary")),
        name="mlstm_scan",
    )(p3, p3, p3, g_rows, g_cols, p3, p3, p3, g_rows, g_cols, conv4, conv4)


def _readout_kernel(oaf_ref, oab_ref, obf_ref, obb_ref, ag_ref, bo_ref, na_ref, nb_ref, ya_ref, yb_ref):
    for h in range(A_HEADS):
        sl = slice(h * A_D, (h + 1) * A_D)
        o = oaf_ref[0, :, sl].astype(F32) + oab_ref[0, :, sl].astype(F32)
        ya_ref[0, :, sl] = (_rms(o, na_ref[:, sl]) * _silu(ag_ref[0, :, sl])).astype(ya_ref.dtype)
    for h in range(B_HEADS):
        sl = slice(h * B_DV, (h + 1) * B_DV)
        o = obf_ref[0, :, sl].astype(F32) + obb_ref[0, :, sl].astype(F32)
        yb_ref[0, :, sl] = (_rms(o, nb_ref[:, sl]) * _sigmoid(bo_ref[0, :, sl])).astype(yb_ref.dtype)


def _readout(oa, ob, p3, norm_a, norm_b, n_lat):
    Bn = p3.shape[0]
    T = n_lat * TOK
    wa = A_HEADS * A_D
    wb = B_HEADS * B_DV
    ag_blk = 4
    bo_blk = (5 * A_HEADS + 4 * B_HEADS) * LANE // wb
    sa = pl.BlockSpec((1, TOK, wa), lambda b, i: (b, i, 0))
    sb = pl.BlockSpec((1, TOK, wb), lambda b, i: (b, i, 0))
    return pl.pallas_call(
        _readout_kernel,
        grid=(Bn, n_lat),
        in_specs=[sa, sa, sb, sb,
                  pl.BlockSpec((1, TOK, wa), lambda b, i: (b, i, ag_blk)),
                  pl.BlockSpec((1, TOK, wb), lambda b, i: (b, i, bo_blk)),
                  pl.BlockSpec((1, wa), lambda b, i: (0, 0)),
                  pl.BlockSpec((1, wb), lambda b, i: (0, 0))],
        out_specs=[sa, sb],
        out_shape=[jax.ShapeDtypeStruct((Bn, T, wa), BF16),
                   jax.ShapeDtypeStruct((Bn, T, wb), BF16)],
        compiler_params=_cparams(("parallel", "parallel")),
        name="readout",
    )(oa[0], oa[1], ob[0], ob[1], p3, p3, norm_a.reshape(1, wa), norm_b.reshape(1, wb))


def _merge_kernel(ya_ref, yb_ref, wa_ref, wb_ref, ga_ref, gb_ref, o_ref):
    pa = jnp.dot(ya_ref[0], wa_ref[...], preferred_element_type=F32)
    pb = jnp.dot(yb_ref[0], wb_ref[...], preferred_element_type=F32)
    o_ref[0] = (_sigmoid(ga_ref[0]) * pa + _sigmoid(gb_ref[0]) * pb).astype(o_ref.dtype)


def _merge(ya, yb, wa, wb, p3, D):
    Bn, T, ka = ya.shape
    kb = yb.shape[2]
    tm = _pick(T, 1024)
    tn = _pick(D, 512)
    ga0 = 0
    gb0 = D // tn
    return pl.pallas_call(
        _merge_kernel,
        grid=(Bn, T // tm, D // tn),
        in_specs=[pl.BlockSpec((1, tm, ka), lambda b, i, j: (b, i, 0)),
                  pl.BlockSpec((1, tm, kb), lambda b, i, j: (b, i, 0)),
                  pl.BlockSpec((ka, tn), lambda b, i, j: (0, j)),
                  pl.BlockSpec((kb, tn), lambda b, i, j: (0, j)),
                  pl.BlockSpec((1, tm, tn), lambda b, i, j: (b, i, ga0 + j)),
                  pl.BlockSpec((1, tm, tn), lambda b, i, j: (b, i, gb0 + j))],
        out_specs=pl.BlockSpec((1, tm, tn), lambda b, i, j: (b, i, j)),
        out_shape=jax.ShapeDtypeStruct((Bn, T, D), BF16),
        compiler_params=_cparams(("parallel", "parallel", "arbitrary")),
        name="merge",
    )(ya, yb, wa, wb, p3, p3)


def _resid_router_kernel(x_ref, mix_ref, g1_ref, g2_ref, gate_ref, sh_ref, sc_ref, wr_ref,
                         h_ref, v_ref, aff_ref):
    h = x_ref[0] + gate_ref[0] * _rms(mix_ref[0].astype(F32), g1_ref[...])
    h_ref[0] = h
    v = _rms(h, g2_ref[...]) * (1.0 + sc_ref[0]) + sh_ref[0]
    v_ref[0] = _pack_bf16_pairs(v)
    logits = jnp.dot(v, wr_ref[...], precision=lax.Precision.HIGHEST, preferred_element_type=F32)
    lane = lax.broadcasted_iota(jnp.int32, logits.shape, 1)
    logits = jnp.where(lane < N_EXPERTS, logits, -jnp.inf)
    e = jnp.exp(logits - jnp.max(logits, axis=-1, keepdims=True))
    aff_ref[0] = e / jnp.sum(e, axis=-1, keepdims=True)


def _resid_router(x, mix, g1, g2, mod3, w_router_pad):
    Bn, T, D = x.shape
    mspec = lambda k: pl.BlockSpec((1, 1, D), lambda b, i: (b, 0, k))
    tok = pl.BlockSpec((1, TOK, D), lambda b, i: (b, i, 0))
    vec = pl.BlockSpec((1, D), lambda b, i: (0, 0))
    return pl.pallas_call(
        _resid_router_kernel,
        grid=(Bn, T // TOK),
        in_specs=[tok, tok, vec, vec, mspec(2), mspec(3), mspec(4),
                  pl.BlockSpec((D, LANE), lambda b, i: (0, 0))],
        out_specs=[tok, pl.BlockSpec((1, TOK, D // 2), lambda b, i: (b, i, 0)),
                   pl.BlockSpec((1, TOK, LANE), lambda b, i: (b, i, 0))],
        out_shape=[jax.ShapeDtypeStruct((Bn, T, D), F32),
                   jax.ShapeDtypeStruct((Bn, T, D // 2), jnp.uint32),
                   jax.ShapeDtypeStruct((Bn, T, LANE), F32)],
        compiler_params=_cparams(("parallel", "parallel")),
        name="resid_router",
    )(x, mix, g1.reshape(1, D), g2.reshape(1, D), mod3, mod3, mod3, w_router_pad)


def _select_kernel(aff_ref, slot_ref, idx_ref, offs_ref, *, cap, tk):
    E, T = aff_ref.shape[1], aff_ref.shape[2]
    nck = T // LANE

    def key():
        return lax.bitcast_convert_type(aff_ref[0], jnp.int32)

    def bit_step(i, tau):
        cand = tau | jnp.left_shift(jnp.int32(1), 30 - i)
        cnt = jnp.sum(jnp.where(key() >= cand, 1.0, 0.0), axis=1, keepdims=True)
        return jnp.where(cnt >= cap, cand, tau)

    tau = lax.fori_loop(0, 31, bit_step, jnp.zeros((E, 1), jnp.int32))
    kk = key()
    gt = kk > tau
    eq = kk == tau
    need = cap - jnp.sum(jnp.where(gt, 1.0, 0.0), axis=1, keepdims=True)

    r = lax.broadcasted_iota(jnp.int32, (LANE, LANE), 0)
    c = lax.broadcasted_iota(jnp.int32, (LANE, LANE), 1)
    upper = jnp.where(r < c, 1.0, 0.0).astype(BF16)
    tr = jnp.right_shift(lax.broadcasted_iota(jnp.int32, (T, LANE), 0), LANE.bit_length() - 1)
    member = jnp.where(tr == lax.broadcasted_iota(jnp.int32, (T, LANE), 1), 1.0, 0.0).astype(BF16)

    def prefix(flags):
        x = jnp.where(flags, 1.0, 0.0).astype(BF16)
        tot = jnp.dot(x, member, preferred_element_type=F32)
        offs = jnp.dot(tot.astype(BF16), upper, preferred_element_type=F32)
        parts = [jnp.dot(x[:, j * LANE:(j + 1) * LANE], upper, preferred_element_type=F32)
                 + offs[:, j:j + 1] for j in range(nck)]
        return jnp.concatenate(parts, axis=1), offs

    tie_rank, _ = prefix(eq)
    sel = gt | (eq & (tie_rank < need))
    pos, offs = prefix(sel)
    slot = jnp.where(sel, pos, -1.0)
    slot_ref[0] = slot
    offs_ref[0] = offs.astype(jnp.int32)

    tt = lax.broadcasted_iota(jnp.int32, (8, T), 1)
    rr = lax.broadcasted_iota(jnp.int32, (8, T), 0)
    digits = jnp.where(rr == 0, jnp.right_shift(tt, 6), jnp.where(rr == 1, tt & 63, 0))
    digits = digits.astype(F32).astype(BF16)
    slot_iota = lax.broadcasted_iota(jnp.int32, (cap, tk), 0).astype(F32)
    for e in range(E):
        acc = jnp.zeros((8, cap), F32)
        for kc in range(T // tk):
            ks = slice(kc * tk, (kc + 1) * tk)
            oh = jnp.where(slot[e:e + 1, ks] == slot_iota, 1.0, 0.0).astype(BF16)
            acc = acc + lax.dot_general(digits[:, ks], oh, _NT, preferred_element_type=F32)
        idx_ref[0, e:e + 1, :] = (acc[0:1] * 64.0 + acc[1:2]).astype(jnp.int32)


def _select(aff_rows, cap):
    Bn, E, T = aff_rows.shape
    assert T // LANE < LANE and T <= 4096
    tk = _pick(T, 1024)
    row = lambda n: pl.BlockSpec((1, E, n), lambda b: (b, 0, 0))
    return pl.pallas_call(
        functools.partial(_select_kernel, cap=cap, tk=tk),
        grid=(Bn,),
        in_specs=[row(T)],
        out_specs=[row(T), row(cap), row(LANE)],
        out_shape=[jax.ShapeDtypeStruct((Bn, E, T), F32),
                   jax.ShapeDtypeStruct((Bn, E, cap), jnp.int32),
                   jax.ShapeDtypeStruct((Bn, E, LANE), jnp.int32)],
        compiler_params=_cparams(("parallel",)),
        name="ec_select",
    )(aff_rows)


def _pack_bf16_pairs(x):
    half = x.shape[1] // 2
    bits = lax.bitcast_convert_type(x.astype(BF16).astype(F32), jnp.uint32)
    return bits[:, half:] | (bits[:, :half] >> 16)


def _unpack_bf16_pairs(p):
    lo = lax.bitcast_convert_type(p << 16, F32).astype(BF16)
    hi = lax.bitcast_convert_type(p & jnp.uint32(0xFFFF0000), F32).astype(BF16)
    return lo, hi


def _gather_kernel(idx_ref, v_hbm, o_ref, buf, sem, *, cap, n_tok):
    b = pl.program_id(0)
    e = pl.program_id(1)
    base = (b * N_EXPERTS + e) * cap

    def row_copy(r, tok, part):
        return pltpu.make_async_copy(v_hbm.at[pl.ds(b * n_tok + tok, 1), :], buf.at[pl.ds(r, 1), :],
                                     sem.at[part])

    hc = cap // 2
    for part in range(2):
        def issue(p, carry, part=part):
            for prio in range(2):
                r = part * hc + 2 * p + prio
                row_copy(r, idx_ref[base + r], part).start(priority=prio)
            return carry

        lax.fori_loop(0, hc // 2, issue, 0)

    half = buf.shape[1]
    for part in range(2):
        def drain(r, carry, part=part):
            row_copy(r, 0, part).wait()
            return carry

        rows = slice(part * hc, (part + 1) * hc)
        lax.fori_loop(part * hc, (part + 1) * hc, drain, 0)
        lo, hi = _unpack_bf16_pairs(buf[rows])
        o_ref[0, 0, rows, :half] = lo
        o_ref[0, 0, rows, half:] = hi


def _gather(idx, v_pairs, cap):
    Bn, T, half = v_pairs.shape
    D = 2 * half
    return pl.pallas_call(
        functools.partial(_gather_kernel, cap=cap, n_tok=T),
        grid_spec=pltpu.PrefetchScalarGridSpec(
            num_scalar_prefetch=1,
            grid=(Bn, N_EXPERTS),
            in_specs=[pl.BlockSpec(memory_space=pl.ANY)],
            out_specs=pl.BlockSpec((1, 1, cap, D), lambda b, e, idx_ref: (e, b, 0, 0)),
            scratch_shapes=[pltpu.VMEM((cap, half), jnp.uint32), pltpu.SemaphoreType.DMA((2,))]),
        out_shape=jax.ShapeDtypeStruct((N_EXPERTS, Bn, cap, D), BF16),
        compiler_params=_cparams(("arbitrary", "arbitrary")),
        name="ec_gather",
    )(idx.reshape(-1), v_pairs.reshape(Bn * T, half))


def _ffn1_kernel(x_ref, wg_ref, wu_ref, o_ref):
    x = x_ref[0]
    g = jnp.dot(x, wg_ref[0].astype(BF16), preferred_element_type=F32)
    u = jnp.dot(x, wu_ref[0].astype(BF16), preferred_element_type=F32)
    o_ref[0] = (_silu(g) * u).astype(o_ref.dtype)


def _ffn1(xg, wg, wu):
    E, M, D = xg.shape
    F = wg.shape[2]
    tm = _pick(M, 1024)
    tn = _pick(F, 256)
    return pl.pallas_call(
        _ffn1_kernel,
        grid=(E, M // tm, F // tn),
        in_specs=[pl.BlockSpec((1, tm, D), lambda e, i, j: (e, i, 0)),
                  pl.BlockSpec((1, D, tn), lambda e, i, j: (e, 0, j)),
                  pl.BlockSpec((1, D, tn), lambda e, i, j: (e, 0, j))],
        out_specs=pl.BlockSpec((1, tm, tn), lambda e, i, j: (e, i, j)),
        out_shape=jax.ShapeDtypeStruct((E, M, F), BF16),
        compiler_params=_cparams(("parallel", "parallel", "arbitrary")),
        name="ec_ffn_up",
    )(xg, wg, wu)


def _ffn2_kernel(h_ref, wd_ref, o_ref):
    o_ref[0] = jnp.dot(h_ref[0], wd_ref[0].astype(BF16),
                       preferred_element_type=F32).astype(o_ref.dtype)


def _ffn2(hid, wd):
    E, M, F = hid.shape
    D = wd.shape[2]
    tn = _pick(D, 512)
    return pl.pallas_call(
        _ffn2_kernel,
        grid=(E, D // tn),
        in_specs=[pl.BlockSpec((1, M, F), lambda e, j: (e, 0, 0)),
                  pl.BlockSpec((1, F, tn), lambda e, j: (e, 0, j))],
        out_specs=pl.BlockSpec((1, M, tn), lambda e, j: (e, 0, j)),
        out_shape=jax.ShapeDtypeStruct((E, M, D), BF16),
        compiler_params=_cparams(("parallel", "arbitrary")),
        name="ec_ffn_down",
    )(hid, wd)


def _scatter_kernel(offs_ref, slot_ref, aff_ref, y_ref, h_ref, g_ref, gate_ref, o_ref, *, cap, win):
    b = pl.program_id(0)
    t = pl.program_id(1)
    e = pl.program_id(2)
    tq = o_ref.shape[1]

    @pl.when(e == 0)
    def _():
        o_ref[...] = jnp.zeros_like(o_ref)

    base = (b * N_EXPERTS + e) * LANE
    lo = offs_ref[base + t * (tq // LANE)]
    hi = offs_ref[base + (t + 1) * (tq // LANE)]
    sl = jnp.broadcast_to(slot_ref[0, 0], (LANE, tq)).T
    af = jnp.broadcast_to(aff_ref[0, 0], (LANE, tq)).T
    sl = jnp.concatenate([sl] * (win // LANE), axis=1)
    af = jnp.concatenate([af] * (win // LANE), axis=1)
    lane = lax.broadcasted_iota(jnp.int32, (tq, win), 1).astype(F32)
    for w0 in range(0, cap, win):
        @pl.when((lo < w0 + win) & (hi > w0))
        def _():
            ohw = jnp.where(sl == lane + float(w0), af, 0.0).astype(BF16)
            o_ref[0] += jnp.dot(ohw, y_ref[0, 0, w0:w0 + win, :], preferred_element_type=F32)

    @pl.when(e == pl.num_programs(2) - 1)
    def _():
        o_ref[0] = h_ref[0] + gate_ref[0] * _rms(o_ref[0], g_ref[...])


def _scatter_combine(offs, slot_rows, aff_rows, y, cap, h, g3, mod3):
    E, Bn, _, D = y.shape
    T = slot_rows.shape[3]
    tq = _pick(T, 512)
    win = _pick(cap, 256)
    assert win % LANE == 0 and tq % LANE == 0
    row = pl.BlockSpec((1, 1, 1, tq), lambda b, t, e, offs_ref: (b, e, 0, t))
    return pl.pallas_call(
        functools.partial(_scatter_kernel, cap=cap, win=win),
        grid_spec=pltpu.PrefetchScalarGridSpec(
            num_scalar_prefetch=1,
            grid=(Bn, T // tq, E),
            in_specs=[row, row,
                      pl.BlockSpec((1, 1, cap, D), lambda b, t, e, offs_ref: (e, b, 0, 0)),
                      pl.BlockSpec((1, tq, D), lambda b, t, e, offs_ref: (b, t, 0)),
                      pl.BlockSpec((1, D), lambda b, t, e, offs_ref: (0, 0)),
                      pl.BlockSpec((1, 1, D), lambda b, t, e, offs_ref: (b, 0, 5))],
            out_specs=pl.BlockSpec((1, tq, D), lambda b, t, e, offs_ref: (b, t, 0))),
        out_shape=jax.ShapeDtypeStruct((Bn, T, D), F32),
        compiler_params=_cparams(("parallel", "parallel", "arbitrary")),
        name="ec_scatter",
    )(offs.reshape(-1), slot_rows, aff_rows, y, h, g3.reshape(1, D), mod3)


def _layer(h_lat, ctx, c8, l, lb_l, w_ada, b_ada, g_norm, w_in, hgrn_f_bias, hgrn_norm,
           mlstm_conv_w, mlstm_conv_b, mlstm_gate_b, mlstm_norm, w_branch_a, w_branch_b, w_out,
           w_router, w_expert_gate, w_expert_up, w_expert_down):
    Bn, T, D = h_lat.shape
    n_lat = T // TOK
    n_tok = T + TOK
    A, Bh = A_HEADS, B_HEADS
    g0 = (5 * A + 6 * Bh) * LANE

    mod = _modulation(c8, w_ada[l], b_ada[l])
    mod3 = mod.reshape(8, 1, N_MOD * D)

    u = _prenorm(h_lat, ctx, g_norm[l, 0], mod3)

    w_in_t = jnp.swapaxes(w_in, 1, 2)
    w_merge_t, w_gate_t = _wsplit(w_in_t, l, g0, 4 * Bh, 2 * D)
    gate_bias = jnp.pad(mlstm_gate_b[l].reshape(4 * Bh), (0, LANE - 4 * Bh))
    u2 = u.reshape(Bn * n_tok, D)
    p3 = _matmul_t(u2, w_in_t, F32, n_cols=g0, layer=l, name="in_proj").reshape(Bn, n_tok, g0)
    p_merge = _matmul_t_rows(u, w_merge_t, T, F32, name="in_proj_merge")
    gates = _matmul_t(u2, w_gate_t, F32, tn=LANE, bias=gate_bias, name="in_proj_gates")

    g6 = gates[:, :4 * Bh].reshape(Bn, n_tok // CHUNK, CHUNK, 2, 2, Bh)
    g_rows = g6.transpose(0, 3, 5, 1, 4, 2)
    g_cols = g6.transpose(0, 3, 5, 1, 2, 4)
    conv4 = jnp.concatenate([mlstm_conv_w[l], mlstm_conv_b[l][None]], axis=0)
    conv4 = conv4.reshape(4, 1, 2 * Bh * B_DQK)

    oa = _hgrn_scan(p3, hgrn_f_bias[l], lb_l, n_lat)
    ob = _mlstm_scan(p3, g_rows, g_cols, conv4, n_lat)
    ya, yb = _readout(oa, ob, p3, hgrn_norm[l], mlstm_norm[l], n_lat)
    merged = _merge(ya, yb, w_branch_a[l].astype(BF16), w_branch_b[l].astype(BF16), p_merge, D)
    mix = _matmul(merged.reshape(Bn * T, D), w_out[l].astype(BF16), BF16, name="out_proj")

    w_router_pad = jnp.pad(w_router[l], ((0, 0), (0, LANE - N_EXPERTS)))
    h_lat, v_lat, aff = _resid_router(h_lat, mix.reshape(Bn, T, D), g_norm[l, 1], g_norm[l, 2],
                                      mod3, w_router_pad)

    cap = CAPACITY * T // N_EXPERTS
    aff_rows = aff[:, :, :N_EXPERTS].transpose(0, 2, 1)
    slot, idx, offs = _select(aff_rows, cap)
    xg = _gather(idx, v_lat, cap)
    hid = _ffn1(xg.reshape(N_EXPERTS, Bn * cap, D), w_expert_gate[l], w_expert_up[l])
    y = _ffn2(hid, w_expert_down[l]).reshape(N_EXPERTS, Bn, cap, D)
    return _scatter_combine(offs, slot.reshape(Bn, N_EXPERTS, 1, T),
                            aff_rows.reshape(Bn, N_EXPERTS, 1, T), y, cap, h_lat, g_norm[l, 3], mod3)


def kernel(x, c, ctx, c_ctx, w_ada, b_ada, g_norm, w_in, hgrn_f_bias, hgrn_lb, hgrn_norm,
           mlstm_conv_w, mlstm_conv_b, mlstm_gate_b, mlstm_norm, w_branch_a, w_branch_b, w_out,
           w_router, w_expert_gate, w_expert_up, w_expert_down):
    Bn, T, D = x.shape
    depth = w_ada.shape[0]
    assert depth == 1, "context outputs are only produced for the state hand-off (single layer)"
    assert ctx.shape[1] == TOK and T % TOK == 0 and Bn < 8
    lb_all = jnp.cumsum(jax.nn.softmax(hgrn_lb.astype(F32), axis=1), axis=1)
    c8 = jnp.zeros((8, D), F32).at[:Bn].set(c).at[Bn].set(c_ctx)
    h_lat = x
    for l in range(depth):
        h_lat = _layer(h_lat, ctx, c8, l, lb_all[:, l], w_ada, b_ada, g_norm, w_in, hgrn_f_bias,
                       hgrn_norm, mlstm_conv_w, mlstm_conv_b, mlstm_gate_b, mlstm_norm,
                       w_branch_a, w_branch_b, w_out, w_router, w_expert_gate, w_expert_up,
                       w_expert_down)
    return h_lat.astype(x.dtype)
```

```python
import functools

import jax
import jax.numpy as jnp
from jax import lax
from jax.experimental import pallas as pl
from jax.experimental.pallas import tpu as pltpu

F32 = jnp.float32
BF16 = jnp.bfloat16

EPS = 1e-6
N_MOD = 6
A_HEADS = 16
A_D = 128
B_HEADS = 8
B_DQK = 128
B_DV = 256
N_EXPERTS = 16
CAPACITY = 2
CHUNK = 64
SUB = 16
A_HPS = 8
B_HPS = 8
TOK = 256
NCH = TOK // CHUNK
NSUB = CHUNK // SUB
LANE = 128
VMEM_LIMIT = 56 * 1024 * 1024

_NT = (((1,), (1,)), ((), ()))


def _pick(n, pref):
    t = min(n, pref)
    while n % t:
        t //= 2
    return t


def _cparams(sem):
    return pltpu.CompilerParams(dimension_semantics=sem, vmem_limit_bytes=VMEM_LIMIT)


def _silu(x):
    return x / (1.0 + jnp.exp(-x))


def _sigmoid(x):
    return 1.0 / (1.0 + jnp.exp(-x))


def _sig_pair(z):
    t = jnp.exp(-jnp.abs(z))
    r = 1.0 / (1.0 + t)
    tr = t * r
    pos = z >= 0
    return jnp.where(pos, r, tr), jnp.where(pos, tr, r)


def _log_sigmoid(x):
    return jnp.minimum(x, 0.0) - jnp.log(1.0 + jnp.exp(-jnp.abs(x)))


def _rms(xf, w):
    return xf * lax.rsqrt(jnp.mean(xf * xf, axis=-1, keepdims=True) + EPS) * w


def _rows(x, r, n):
    return jnp.broadcast_to(x[r:r + 1], (n, x.shape[1]))


def _mod_kernel(c_ref, w_ref, b_ref, o_ref):
    a = _silu(c_ref[...]).astype(BF16)
    o_ref[...] = jnp.dot(a, w_ref[...].astype(BF16), preferred_element_type=F32) + b_ref[...]


def _modulation(c8, w, b):
    D, N = w.shape
    tn = _pick(N, 512)
    return pl.pallas_call(
        _mod_kernel,
        grid=(N // tn,),
        in_specs=[pl.BlockSpec((8, D), lambda j: (0, 0)),
                  pl.BlockSpec((D, tn), lambda j: (0, j)),
                  pl.BlockSpec((1, tn), lambda j: (0, j))],
        out_specs=pl.BlockSpec((8, tn), lambda j: (0, j)),
        out_shape=jax.ShapeDtypeStruct((8, N), F32),
        compiler_params=_cparams(("parallel",)),
        name="adaln_mod",
    )(c8, w, b.reshape(1, N))


def _prenorm_kernel(x_ref, ctx_ref, g_ref, sh_ref, sc_ref, o_ref, *, n_lat):
    i = pl.program_id(1)
    g = g_ref[...]
    sh = sh_ref[0]
    sc = sc_ref[0]

    @pl.when(i < n_lat)
    def _():
        o_ref[0] = (_rms(x_ref[0], g) * (1.0 + sc) + sh).astype(o_ref.dtype)

    @pl.when(i == n_lat)
    def _():
        o_ref[0] = (_rms(ctx_ref[0], g) * (1.0 + sc) + sh).astype(o_ref.dtype)


def _prenorm(x, ctx, g, mod3):
    Bn, T, D = x.shape
    n_lat = T // TOK
    row = lambda b, i: jnp.where(i == n_lat, Bn, b)
    return pl.pallas_call(
        functools.partial(_prenorm_kernel, n_lat=n_lat),
        grid=(Bn, n_lat + 1),
        in_specs=[pl.BlockSpec((1, TOK, D), lambda b, i: (b, jnp.minimum(i, n_lat - 1), 0)),
                  pl.BlockSpec((1, TOK, D), lambda b, i: (b, 0, 0)),
                  pl.BlockSpec((1, D), lambda b, i: (0, 0)),
                  pl.BlockSpec((1, 1, D), lambda b, i: (row(b, i), 0, 0)),
                  pl.BlockSpec((1, 1, D), lambda b, i: (row(b, i), 0, 1))],
        out_specs=pl.BlockSpec((1, TOK, D), lambda b, i: (b, i, 0)),
        out_shape=jax.ShapeDtypeStruct((Bn, T + TOK, D), BF16),
        compiler_params=_cparams(("parallel", "arbitrary")),
        name="prenorm",
    )(x, ctx, g.reshape(1, D), mod3, mod3)


_NN = (((1,), (0,)), ((), ()))


def _mm_kernel(a_ref, b_ref, o_ref, *, dims):
    o_ref[...] = lax.dot_general(a_ref[...], b_ref[...].astype(BF16), dims,
                                 preferred_element_type=F32).astype(o_ref.dtype)


def _mm_bias_kernel(a_ref, b_ref, bias_ref, o_ref, *, dims):
    o_ref[...] = (lax.dot_general(a_ref[...], b_ref[...].astype(BF16), dims, preferred_element_type=F32)
                  + bias_ref[...]).astype(o_ref.dtype)


def _matmul(a, b, out_dtype, tm=1024, tn=512, bias=None, name="matmul"):
    M, K = a.shape
    N = b.shape[1]
    tm = _pick(M, tm)
    tn = _pick(N, tn)
    return _matmul_call(a, b, pl.BlockSpec((K, tn), lambda i, j: (0, j)), _NN, N, out_dtype, tm, tn,
                        bias, name)


def _matmul_t(a, bt, out_dtype, tm=1024, tn=512, bias=None, n_cols=None, layer=None, name="matmul_t"):
    M, K = a.shape
    N = bt.shape[-2] if n_cols is None else n_cols
    tm = _pick(M, tm)
    tn = _pick(N, tn)
    if layer is None:
        b_spec = pl.BlockSpec((tn, K), lambda i, j: (j, 0))
    else:
        b_spec = pl.BlockSpec((None, tn, K), lambda i, j: (layer, j, 0))
    return _matmul_call(a, bt, b_spec, _NT, N, out_dtype, tm, tn, bias, name)


def _matmul_call(a, b, b_spec, dims, N, out_dtype, tm, tn, bias, name):
    M, K = a.shape
    in_specs = [pl.BlockSpec((tm, K), lambda i, j: (i, 0)), b_spec]
    args = [a, b]
    kern = functools.partial(_mm_kernel, dims=dims)
    if bias is not None:
        in_specs.append(pl.BlockSpec((1, tn), lambda i, j: (0, j)))
        args.append(bias.reshape(1, N))
        kern = functools.partial(_mm_bias_kernel, dims=dims)
    return pl.pallas_call(
        kern,
        grid=(M // tm, N // tn),
        in_specs=in_specs,
        out_specs=pl.BlockSpec((tm, tn), lambda i, j: (i, j)),
        out_shape=jax.ShapeDtypeStruct((M, N), out_dtype),
        compiler_params=_cparams(("parallel", "arbitrary")),
        name=name,
    )(*args)


def _matmul_t_rows(a3, bt, n_rows, out_dtype, tm=1024, tn=512, name="matmul_t_rows"):
    Bn, _, K = a3.shape
    N = bt.shape[0]
    tm = _pick(n_rows, tm)
    tn = _pick(N, tn)
    return pl.pallas_call(
        functools.partial(_mm_kernel, dims=_NT),
        grid=(Bn, n_rows // tm, N // tn),
        in_specs=[pl.BlockSpec((None, tm, K), lambda b, i, j: (b, i, 0)),
                  pl.BlockSpec((tn, K), lambda b, i, j: (j, 0))],
        out_specs=pl.BlockSpec((None, tm, tn), lambda b, i, j: (b, i, j)),
        out_shape=jax.ShapeDtypeStruct((Bn, n_rows, N), out_dtype),
        compiler_params=_cparams(("parallel", "parallel", "arbitrary")),
        name=name,
    )(a3, bt)


def _wsplit_kernel(a_ref, b_ref, o_ref, g_ref, *, shift):
    j = pl.program_id(1)
    tn = o_ref.shape[0]
    cat = jnp.concatenate([a_ref[...], b_ref[...]], axis=0)
    o_ref[...] = cat[shift:shift + tn].astype(o_ref.dtype)

    @pl.when(j == 0)
    def _():
        row = lax.broadcasted_iota(jnp.int32, g_ref.shape, 0)
        g_ref[...] = jnp.where(row < shift, cat[:LANE], 0.0).astype(g_ref.dtype)


def _wsplit(wt, layer, row0, shift, n):
    K = wt.shape[2]
    tk = _pick(K, 1024)
    tn = _pick(n, 512)
    assert row0 % tn == 0 and 0 < shift < LANE and shift % 8 == 0 and row0 + shift + n <= wt.shape[1]
    r0 = row0 // tn
    return pl.pallas_call(
        functools.partial(_wsplit_kernel, shift=shift),
        grid=(K // tk, n // tn),
        in_specs=[pl.BlockSpec((None, tn, tk), lambda i, j: (layer, r0 + j, i)),
                  pl.BlockSpec((None, tn, tk), lambda i, j: (layer, r0 + j + 1, i))],
        out_specs=[pl.BlockSpec((tn, tk), lambda i, j: (j, i)),
                   pl.BlockSpec((LANE, tk), lambda i, j: (0, i))],
        out_shape=[jax.ShapeDtypeStruct((n, K), BF16), jax.ShapeDtypeStruct((LANE, K), BF16)],
        compiler_params=_cparams(("parallel", "arbitrary")),
        name="w_split",
    )(wt, wt)


def _tok_block(rev, j, n_lat):
    lat = n_lat - j if rev else j - 1
    return jnp.where(j == 0, n_lat, lat)


def _out_block(rev, j, n_lat):
    return jnp.clip(n_lat - j if rev else j - 1, 0, n_lat - 1)


def _chunk_masks(rev):
    row = lax.broadcasted_iota(jnp.int32, (CHUNK, CHUNK), 0)
    col = lax.broadcasted_iota(jnp.int32, (CHUNK, CHUNK), 1)
    return (col >= row, col <= row) if rev else (col <= row, col >= row)


def _hgrn_consts():
    t = jnp.arange(TOK)
    same = (t[:, None] // CHUNK) == (t[None, :] // CHUNK)
    sub_r, sub_c = t[:, None] // SUB, t[None, :] // SUB
    fwd = jnp.concatenate([same & (t[None, :] <= t[:, None]), same & (sub_c < sub_r)], axis=0)
    bwd = jnp.concatenate([same & (t[None, :] >= t[:, None]), same & (sub_c > sub_r)], axis=0)
    masks = jnp.stack([fwd, bwd]).astype(BF16)
    r = jnp.arange(SUB * A_D)
    emat = ((r[:, None] // A_D) == (jnp.arange(LANE)[None, :] % SUB)).astype(BF16)
    return masks, emat


def _hgrn_state(af, v, fb, lb, mask_ref, st_ref, rev):
    z = af + fb
    sp, sn = _sig_pair(z)
    k = (1.0 - lb) * sn
    lf2 = jnp.log2(lb + (1.0 - lb) * sp)
    lk2 = jnp.log2(k)

    hi = lf2.astype(BF16)
    r1 = lf2 - hi.astype(F32)
    mid = r1.astype(BF16)
    lo = (r1 - mid.astype(F32)).astype(BF16)
    cs = jnp.dot(mask_ref[0], jnp.concatenate([hi, mid, lo], axis=1), preferred_element_type=F32)
    cs = cs[:, :A_D] + cs[:, A_D:2 * A_D] + cs[:, 2 * A_D:]
    b = cs[:TOK]
    ent = cs[TOK:]
    c_all = lk2 - b

    last = [(c * CHUNK if rev else c * CHUNK + CHUNK - 1) for c in range(NCH)]
    tot_rows = jnp.concatenate([_rows(b, last[c], CHUNK) for c in range(NCH)], axis=0)
    khat = jnp.exp2(jnp.minimum(tot_rows + c_all, lk2)).astype(BF16)
    upd = [jnp.dot(v[c * CHUNK:(c + 1) * CHUNK].T.astype(BF16), khat[c * CHUNK:(c + 1) * CHUNK],
                   preferred_element_type=F32) for c in range(NCH)]

    order = range(NCH - 1, -1, -1) if rev else range(NCH)
    st = st_ref[...]
    st_in = [None] * NCH
    for c in order:
        st_in[c] = st
        st = st * jnp.exp2(b[last[c]:last[c] + 1]) + upd[c]
    st_ref[...] = st
    return v, b, ent, c_all, lk2, st_in


def _hgrn_output(aq, feats, emat_ref, o_ref, hs, rev):
    v, b, ent, c_all, lk2, st_in = feats
    q = _silu(aq)
    qb = q.astype(BF16)
    vb = v.astype(BF16)
    qt = (q * jnp.exp2(b - ent)).astype(BF16)
    row = lax.broadcasted_iota(jnp.int32, (CHUNK, CHUNK), 0)
    col = lax.broadcasted_iota(jnp.int32, (CHUNK, CHUNK), 1)
    rs = jnp.right_shift(row, SUB.bit_length() - 1)
    cs_ = jnp.right_shift(col, SUB.bit_length() - 1)
    prev_blk = (cs_ > rs) if rev else (cs_ < rs)
    diag_blk = (rs == cs_) & ((col >= row) if rev else (col <= row))

    off = [[None] * NSUB for _ in range(NCH)]
    for i in range(NSUB):
        e_i = jnp.concatenate([_rows(ent, c * CHUNK + i * SUB, CHUNK) for c in range(NCH)], axis=0)
        kt = jnp.exp2(jnp.minimum(e_i + c_all, lk2)).astype(BF16)
        for c in range(NCH):
            r0 = c * CHUNK + i * SUB
            off[c][i] = lax.dot_general(qt[r0:r0 + SUB], kt[c * CHUNK:(c + 1) * CHUNK], _NT,
                                        preferred_element_type=F32)

    half = SUB // 2
    nsb = TOK // SUB
    zeros = jnp.zeros((half, A_D), F32)
    ps = []
    for sl in range(SUB):
        n = half if (sl < half if rev else sl >= half) else SUB
        r0 = SUB - n if not rev else 0
        bq = b if n == SUB else jnp.concatenate(
            [b[g * SUB + r0:g * SUB + r0 + n] for g in range(nsb)], axis=0)
        cref = jnp.concatenate([_rows(c_all, g * SUB + sl, n) for g in range(nsb)], axis=0)
        kref = jnp.concatenate([_rows(lk2, g * SUB + sl, n) for g in range(nsb)], axis=0)
        e = jnp.exp2(jnp.minimum(bq + cref, kref))
        if n != SUB:
            parts = []
            for g in range(nsb):
                piece = e[g * half:(g + 1) * half]
                parts += [piece, zeros] if rev else [zeros, piece]
            e = jnp.concatenate(parts, axis=0)
        ps.append(e.astype(BF16) * qb)
    acc = jnp.dot(jnp.concatenate(ps, axis=1), emat_ref[...], preferred_element_type=F32)

    qhat = (q * jnp.exp2(b)).astype(BF16)
    for c in range(NCH):
        sl_c = slice(c * CHUNK, (c + 1) * CHUNK)
        attn = (jnp.where(prev_blk, jnp.concatenate(off[c], axis=0), 0.0)
                + jnp.where(diag_blk, acc[sl_c, :CHUNK], 0.0))
        o = jnp.dot(attn.astype(BF16), vb[sl_c], preferred_element_type=F32)
        o = o + lax.dot_general(qhat[sl_c], st_in[c].astype(BF16), _NT, preferred_element_type=F32)
        o_ref[0, sl_c, hs] = o.astype(o_ref.dtype)


def _hgrn_kernel(aqf, aff, aif, aqb, afb, aib, fb_ref, lb_ref, mask_ref, emat_ref, of_ref, ob_ref,
                 stf_ref, stb_ref):
    j = pl.program_id(2)

    @pl.when(j == 0)
    def _():
        stf_ref[...] = jnp.zeros_like(stf_ref)
        stb_ref[...] = jnp.zeros_like(stb_ref)

    feats = []
    for hh in range(A_HPS):
        hs = slice(hh * A_D, (hh + 1) * A_D)
        feats.append((hs,
                      _hgrn_state(aff[0, :, hs], aif[0, :, hs], fb_ref[0][:, hs], lb_ref[0][:, hs],
                                  mask_ref.at[0:1], stf_ref.at[hh], False),
                      _hgrn_state(afb[0, :, hs], aib[0, :, hs], fb_ref[1][:, hs], lb_ref[1][:, hs],
                                  mask_ref.at[1:2], stb_ref.at[hh], True)))
    for hs, feats_f, feats_b in feats:
        _hgrn_output(aqf[0, :, hs], feats_f, emat_ref, of_ref, hs, False)
        _hgrn_output(aqb[0, :, hs], feats_b, emat_ref, ob_ref, hs, True)


def _hgrn_scan(p3, f_bias, lb, n_lat):
    Bn = p3.shape[0]
    A = A_HEADS
    masks, emat = _hgrn_consts()

    W = A_HPS * A_D
    G = A // A_HPS

    def feat(rev, grp):
        return pl.BlockSpec((1, TOK, W), lambda b, h, j: (b, _tok_block(rev, j, n_lat), grp * G + h))

    def outp(rev):
        return pl.BlockSpec((1, TOK, W), lambda b, h, j: (b, _out_block(rev, j, n_lat), h))

    par = pl.BlockSpec((2, 1, W), lambda b, h, j: (0, 0, h))
    osh = jax.ShapeDtypeStruct((Bn, n_lat * TOK, A * A_D), BF16)
    return pl.pallas_call(
        _hgrn_kernel,
        grid=(Bn, G, n_lat + 1),
        in_specs=[feat(False, 0), feat(False, 1), feat(False, 3),
                  feat(True, 0), feat(True, 2), feat(True, 3),
                  par, par,
                  pl.BlockSpec((2, 2 * TOK, TOK), lambda b, h, j: (0, 0, 0)),
                  pl.BlockSpec((SUB * A_D, LANE), lambda b, h, j: (0, 0))],
        out_specs=[outp(False), outp(True)],
        out_shape=[osh, osh],
        scratch_shapes=[pltpu.VMEM((A_HPS, A_D, A_D), F32), pltpu.VMEM((A_HPS, A_D, A_D), F32)],
        compiler_params=_cparams(("parallel", "parallel", "arbitrary")),
        name="hgrn2_scan",
    )(p3, p3, p3, p3, p3, p3, f_bias.reshape(2, 1, A * A_D), lb.reshape(2, 1, A * A_D), masks, emat)


def _mlstm_state(bq_ref, bk_ref, bv_ref, gr_ref, gc_ref, cwq, cwk, first, last_, c_ref, n_ref, m_ref,
                 hh, rev):
    def conv(u, cw):
        up = jnp.where(first, 0.0, pltpu.roll(u, 1, 0))
        dn = jnp.where(last_, 0.0, pltpu.roll(u, TOK - 1, 0))
        return _silu(cw[3:4] + up * cw[0:1] + u * cw[1:2] + dn * cw[2:3])

    hs = slice(hh * B_DQK, (hh + 1) * B_DQK)
    vs = slice(hh * B_DV, (hh + 1) * B_DV)
    c_ref, n_ref, m_ref = c_ref.at[hh], n_ref.at[hh], m_ref.at[hh]
    q_all = conv(bq_ref[0, :, hs], cwq[:, hs])
    k_all = conv(bk_ref[0, :, hs], cwk[:, hs]) * (B_DQK ** -0.5)
    seen, seen_t = _chunk_masks(rev)

    pre = []
    for c in range(NCH):
        sl_c = slice(c * CHUNK, (c + 1) * CHUNK)
        k = k_all[sl_c]
        vb = bv_ref[0, sl_c, vs].astype(BF16)
        g_r = gr_ref[0, 0, hh, c]
        g_c = gc_ref[0, 0, hh, c]
        ii_r = g_r[0:1]
        lf_r = _log_sigmoid(g_r[1:2])
        ii_c = g_c[:, 0:1]
        lf_c = _log_sigmoid(g_c[:, 1:2])
        b_c = jnp.sum(jnp.where(seen, lf_r, 0.0), axis=1, keepdims=True)
        b_r = jnp.sum(jnp.where(seen_t, lf_c, 0.0), axis=0, keepdims=True)
        total = jnp.sum(lf_r, axis=1, keepdims=True)
        logs = total - b_c + ii_c
        ms = jnp.max(logs, axis=0, keepdims=True)
        kw = k * jnp.exp(logs - ms)
        upd = jnp.dot(kw.T.astype(BF16), vb, preferred_element_type=F32)
        nupd = jnp.sum(kw, axis=0, keepdims=True)
        pre.append((vb, ii_r, b_c, b_r, total, ms, upd, nupd))

    cmat, nvec, m = c_ref[...], n_ref[...], m_ref[:, 0:1]
    st_in = [None] * NCH
    for c in (range(NCH - 1, -1, -1) if rev else range(NCH)):
        _, _, _, _, total, ms, upd, nupd = pre[c]
        st_in[c] = (cmat, nvec, m)
        m_new = jnp.maximum(total + m, ms)
        dec = jnp.exp(total + m - m_new)
        sc = jnp.exp(ms - m_new)
        cmat = dec * cmat + sc * upd
        nvec = dec * nvec + sc * nupd
        m = m_new
    c_ref[...] = cmat
    n_ref[...] = nvec
    m_ref[...] = jnp.broadcast_to(m, m_ref.shape)
    return q_all, k_all, pre, st_in


def _mlstm_output(feats, o_ref, hh, rev):
    q_all, k_all, pre, st_in = feats
    seen, _ = _chunk_masks(rev)
    for c in range(NCH):
        sl_c = slice(c * CHUNK, (c + 1) * CHUNK)
        vb, ii_r, b_c, b_r, _, _, _, _ = pre[c]
        cm, nv, m0 = st_in[c]
        q = q_all[sl_c]
        qb = q.astype(BF16)
        kb = k_all[sl_c].astype(BF16)
        logw = jnp.where(seen, b_c - b_r + ii_r, -jnp.inf)
        mw = jnp.max(logw, axis=1, keepdims=True)
        qk = lax.dot_general(qb, kb, _NT, preferred_element_type=F32) * jnp.exp(logw - mw)
        num0 = jnp.dot(qk.astype(BF16), vb, preferred_element_type=F32)
        den0 = jnp.sum(qk, axis=1, keepdims=True)
        log_inter = b_c + m0
        m_t = jnp.maximum(mw, log_inter)
        r = jnp.exp(mw - m_t)
        a = jnp.exp(log_inter - m_t)
        num = r * num0 + a * jnp.dot(qb, cm.astype(BF16), preferred_element_type=F32)
        den = r * den0 + a * jnp.sum(q * nv, axis=1, keepdims=True)
        h = num / jnp.maximum(jnp.abs(den), jnp.exp(-m_t))
        o_ref[0, sl_c, hh * B_DV:(hh + 1) * B_DV] = h.astype(o_ref.dtype)


def _mlstm_kernel(bqf, bkf, bvf, grf, gcf, bqb, bkb, bvb, grb, gcb, cwq_ref, cwk_ref, of_ref, ob_ref,
                  cf_ref, nf_ref, mf_ref, cb_ref, nb_ref, mb_ref):
    j = pl.program_id(2)

    @pl.when(j == 0)
    def _():
        for ref in (cf_ref, nf_ref, mf_ref, cb_ref, nb_ref, mb_ref):
            ref[...] = jnp.zeros_like(ref)

    t = lax.broadcasted_iota(jnp.int32, (TOK, 1), 0)
    row_mask = jnp.where(j == 0, TOK - 1, CHUNK - 1)
    pos = t & row_mask
    first = pos == 0
    last_ = pos == row_mask
    cwq = cwq_ref[:, 0, :]
    cwk = cwk_ref[:, 0, :]
    feats = []
    for hh in range(B_HPS):
        feats.append((_mlstm_state(bqf, bkf, bvf, grf, gcf, cwq, cwk, first, last_, cf_ref, nf_ref, mf_ref,
                                   hh, False),
                      _mlstm_state(bqb, bkb, bvb, grb, gcb, cwq, cwk, first, last_, cb_ref, nb_ref, mb_ref,
                                   hh, True)))
    for hh, (feats_f, feats_b) in enumerate(feats):
        _mlstm_output(feats_f, of_ref, hh, False)
        _mlstm_output(feats_b, ob_ref, hh, True)


def _mlstm_scan(p3, g_rows, g_cols, conv4, n_lat):
    Bn = p3.shape[0]
    A, Bh = A_HEADS, B_HEADS
    P = B_HPS
    G = Bh // P
    q0 = 5 * A // P
    k0 = (5 * A + Bh) // P
    v0 = (5 * A + 2 * Bh) // (2 * P)

    def specs(rev):
        d = int(rev)
        tb = lambda j: _tok_block(rev, j, n_lat)
        return [pl.BlockSpec((1, TOK, P * B_DQK), lambda b, h, j: (b, tb(j), q0 + h)),
                pl.BlockSpec((1, TOK, P * B_DQK), lambda b, h, j: (b, tb(j), k0 + h)),
                pl.BlockSpec((1, TOK, P * B_DV), lambda b, h, j: (b, tb(j), v0 + h)),
                pl.BlockSpec((1, 1, P, NCH, 2, CHUNK), lambda b, h, j: (b, d, h, tb(j), 0, 0)),
                pl.BlockSpec((1, 1, P, NCH, CHUNK, 2), lambda b, h, j: (b, d, h, tb(j), 0, 0))]

    def outp(rev):
        return pl.BlockSpec((1, TOK, P * B_DV), lambda b, h, j: (b, _out_block(rev, j, n_lat), h))

    osh = jax.ShapeDtypeStruct((Bn, n_lat * TOK, Bh * B_DV), BF16)
    state = [pltpu.VMEM((P, B_DQK, B_DV), F32), pltpu.VMEM((P, 1, B_DQK), F32),
             pltpu.VMEM((P, 1, LANE), F32)]
    return pl.pallas_call(
        _mlstm_kernel,
        grid=(Bn, G, n_lat + 1),
        in_specs=specs(False) + specs(True) + [
            pl.BlockSpec((4, 1, P * B_DQK), lambda b, h, j: (0, 0, h)),
            pl.BlockSpec((4, 1, P * B_DQK), lambda b, h, j: (0, 0, G + h))],
        out_specs=[outp(False), outp(True)],
        out_shape=[osh, osh],
        scratch_shapes=state + state,
        compiler_params=_cparams(("parallel", "parallel", "arbitrary")),
        name="mlstm_scan",
    )(p3, p3, p3, g_rows, g_cols, p3, p3, p3, g_rows, g_cols, conv4, conv4)


def _readout_kernel(oaf_ref, oab_ref, obf_ref, obb_ref, ag_ref, bo_ref, na_ref, nb_ref, ya_ref, yb_ref):
    for h in range(A_HEADS):
        sl = slice(h * A_D, (h + 1) * A_D)
        o = oaf_ref[0, :, sl].astype(F32) + oab_ref[0, :, sl].astype(F32)
        ya_ref[0, :, sl] = (_rms(o, na_ref[:, sl]) * _silu(ag_ref[0, :, sl])).astype(ya_ref.dtype)
    for h in range(B_HEADS):
        sl = slice(h * B_DV, (h + 1) * B_DV)
        o = obf_ref[0, :, sl].astype(F32) + obb_ref[0, :, sl].astype(F32)
        yb_ref[0, :, sl] = (_rms(o, nb_ref[:, sl]) * _sigmoid(bo_ref[0, :, sl])).astype(yb_ref.dtype)


def _readout(oa, ob, p3, norm_a, norm_b, n_lat):
    Bn = p3.shape[0]
    T = n_lat * TOK
    wa = A_HEADS * A_D
    wb = B_HEADS * B_DV
    ag_blk = 4
    bo_blk = (5 * A_HEADS + 4 * B_HEADS) * LANE // wb
    sa = pl.BlockSpec((1, TOK, wa), lambda b, i: (b, i, 0))
    sb = pl.BlockSpec((1, TOK, wb), lambda b, i: (b, i, 0))
    return pl.pallas_call(
        _readout_kernel,
        grid=(Bn, n_lat),
        in_specs=[sa, sa, sb, sb,
                  pl.BlockSpec((1, TOK, wa), lambda b, i: (b, i, ag_blk)),
                  pl.BlockSpec((1, TOK, wb), lambda b, i: (b, i, bo_blk)),
                  pl.BlockSpec((1, wa), lambda b, i: (0, 0)),
                  pl.BlockSpec((1, wb), lambda b, i: (0, 0))],
        out_specs=[sa, sb],
        out_shape=[jax.ShapeDtypeStruct((Bn, T, wa), BF16),
                   jax.ShapeDtypeStruct((Bn, T, wb), BF16)],
        compiler_params=_cparams(("parallel", "parallel")),
        name="readout",
    )(oa[0], oa[1], ob[0], ob[1], p3, p3, norm_a.reshape(1, wa), norm_b.reshape(1, wb))


def _merge_kernel(ya_ref, yb_ref, wa_ref, wb_ref, ga_ref, gb_ref, o_ref):
    pa = jnp.dot(ya_ref[0], wa_ref[...], preferred_element_type=F32)
    pb = jnp.dot(yb_ref[0], wb_ref[...], preferred_element_type=F32)
    o_ref[0] = (_sigmoid(ga_ref[0]) * pa + _sigmoid(gb_ref[0]) * pb).astype(o_ref.dtype)


def _merge(ya, yb, wa, wb, p3, D):
    Bn, T, ka = ya.shape
    kb = yb.shape[2]
    tm = _pick(T, 1024)
    tn = _pick(D, 512)
    ga0 = 0
    gb0 = D // tn
    return pl.pallas_call(
        _merge_kernel,
        grid=(Bn, T // tm, D // tn),
        in_specs=[pl.BlockSpec((1, tm, ka), lambda b, i, j: (b, i, 0)),
                  pl.BlockSpec((1, tm, kb), lambda b, i, j: (b, i, 0)),
                  pl.BlockSpec((ka, tn), lambda b, i, j: (0, j)),
                  pl.BlockSpec((kb, tn), lambda b, i, j: (0, j)),
                  pl.BlockSpec((1, tm, tn), lambda b, i, j: (b, i, ga0 + j)),
                  pl.BlockSpec((1, tm, tn), lambda b, i, j: (b, i, gb0 + j))],
        out_specs=pl.BlockSpec((1, tm, tn), lambda b, i, j: (b, i, j)),
        out_shape=jax.ShapeDtypeStruct((Bn, T, D), BF16),
        compiler_params=_cparams(("parallel", "parallel", "arbitrary")),
        name="merge",
    )(ya, yb, wa, wb, p3, p3)


def _resid_router_kernel(x_ref, mix_ref, g1_ref, g2_ref, gate_ref, sh_ref, sc_ref, wr_ref,
                         h_ref, v_ref, aff_ref):
    h = x_ref[0] + gate_ref[0] * _rms(mix_ref[0].astype(F32), g1_ref[...])
    h_ref[0] = h
    v = _rms(h, g2_ref[...]) * (1.0 + sc_ref[0]) + sh_ref[0]
    v_ref[0] = _pack_bf16_pairs(v)
    logits = jnp.dot(v, wr_ref[...], precision=lax.Precision.HIGHEST, preferred_element_type=F32)
    lane = lax.broadcasted_iota(jnp.int32, logits.shape, 1)
    logits = jnp.where(lane < N_EXPERTS, logits, -jnp.inf)
    e = jnp.exp(logits - jnp.max(logits, axis=-1, keepdims=True))
    aff_ref[0] = e / jnp.sum(e, axis=-1, keepdims=True)


def _resid_router(x, mix, g1, g2, mod3, w_router_pad):
    Bn, T, D = x.shape
    mspec = lambda k: pl.BlockSpec((1, 1, D), lambda b, i: (b, 0, k))
    tok = pl.BlockSpec((1, TOK, D), lambda b, i: (b, i, 0))
    vec = pl.BlockSpec((1, D), lambda b, i: (0, 0))
    return pl.pallas_call(
        _resid_router_kernel,
        grid=(Bn, T // TOK),
        in_specs=[tok, tok, vec, vec, mspec(2), mspec(3), mspec(4),
                  pl.BlockSpec((D, LANE), lambda b, i: (0, 0))],
        out_specs=[tok, pl.BlockSpec((1, TOK, D // 2), lambda b, i: (b, i, 0)),
                   pl.BlockSpec((1, TOK, LANE), lambda b, i: (b, i, 0))],
        out_shape=[jax.ShapeDtypeStruct((Bn, T, D), F32),
                   jax.ShapeDtypeStruct((Bn, T, D // 2), jnp.uint32),
                   jax.ShapeDtypeStruct((Bn, T, LANE), F32)],
        compiler_params=_cparams(("parallel", "parallel")),
        name="resid_router",
    )(x, mix, g1.reshape(1, D), g2.reshape(1, D), mod3, mod3, mod3, w_router_pad)


def _select_kernel(aff_ref, slot_ref, idx_ref, offs_ref, *, cap, tk):
    E, T = aff_ref.shape[1], aff_ref.shape[2]
    nck = T // LANE

    def key():
        return lax.bitcast_convert_type(aff_ref[0], jnp.int32)

    def bit_step(i, tau):
        cand = tau | jnp.left_shift(jnp.int32(1), 30 - i)
        cnt = jnp.sum(jnp.where(key() >= cand, 1.0, 0.0), axis=1, keepdims=True)
        return jnp.where(cnt >= cap, cand, tau)

    tau = lax.fori_loop(0, 31, bit_step, jnp.zeros((E, 1), jnp.int32))
    kk = key()
    gt = kk > tau
    eq = kk == tau
    need = cap - jnp.sum(jnp.where(gt, 1.0, 0.0), axis=1, keepdims=True)

    r = lax.broadcasted_iota(jnp.int32, (LANE, LANE), 0)
    c = lax.broadcasted_iota(jnp.int32, (LANE, LANE), 1)
    upper = jnp.where(r < c, 1.0, 0.0).astype(BF16)
    tr = jnp.right_shift(lax.broadcasted_iota(jnp.int32, (T, LANE), 0), LANE.bit_length() - 1)
    member = jnp.where(tr == lax.broadcasted_iota(jnp.int32, (T, LANE), 1), 1.0, 0.0).astype(BF16)

    def prefix(flags):
        x = jnp.where(flags, 1.0, 0.0).astype(BF16)
        tot = jnp.dot(x, member, preferred_element_type=F32)
        offs = jnp.dot(tot.astype(BF16), upper, preferred_element_type=F32)
        parts = [jnp.dot(x[:, j * LANE:(j + 1) * LANE], upper, preferred_element_type=F32)
                 + offs[:, j:j + 1] for j in range(nck)]
        return jnp.concatenate(parts, axis=1), offs

    tie_rank, _ = prefix(eq)
    sel = gt | (eq & (tie_rank < need))
    pos, offs = prefix(sel)
    slot = jnp.where(sel, pos, -1.0)
    slot_ref[0] = slot
    offs_ref[0] = offs.astype(jnp.int32)

    tt = lax.broadcasted_iota(jnp.int32, (8, T), 1)
    rr = lax.broadcasted_iota(jnp.int32, (8, T), 0)
    digits = jnp.where(rr == 0, jnp.right_shift(tt, 6), jnp.where(rr == 1, tt & 63, 0))
    digits = digits.astype(F32).astype(BF16)
    slot_iota = lax.broadcasted_iota(jnp.int32, (cap, tk), 0).astype(F32)
    for e in range(E):
        acc = jnp.zeros((8, cap), F32)
        for kc in range(T // tk):
            ks = slice(kc * tk, (kc + 1) * tk)
            oh = jnp.where(slot[e:e + 1, ks] == slot_iota, 1.0, 0.0).astype(BF16)
            acc = acc + lax.dot_general(digits[:, ks], oh, _NT, preferred_element_type=F32)
        idx_ref[0, e:e + 1, :] = (acc[0:1] * 64.0 + acc[1:2]).astype(jnp.int32)


def _select(aff_rows, cap):
    Bn, E, T = aff_rows.shape
    assert T // LANE < LANE and T <= 4096
    tk = _pick(T, 1024)
    row = lambda n: pl.BlockSpec((1, E, n), lambda b: (b, 0, 0))
    return pl.pallas_call(
        functools.partial(_select_kernel, cap=cap, tk=tk),
        grid=(Bn,),
        in_specs=[row(T)],
        out_specs=[row(T), row(cap), row(LANE)],
        out_shape=[jax.ShapeDtypeStruct((Bn, E, T), F32),
                   jax.ShapeDtypeStruct((Bn, E, cap), jnp.int32),
                   jax.ShapeDtypeStruct((Bn, E, LANE), jnp.int32)],
        compiler_params=_cparams(("parallel",)),
        name="ec_select",
    )(aff_rows)


def _pack_bf16_pairs(x):
    half = x.shape[1] // 2
    bits = lax.bitcast_convert_type(x.astype(BF16).astype(F32), jnp.uint32)
    return bits[:, half:] | (bits[:, :half] >> 16)


def _unpack_bf16_pairs(p):
    lo = lax.bitcast_convert_type(p << 16, F32).astype(BF16)
    hi = lax.bitcast_convert_type(p & jnp.uint32(0xFFFF0000), F32).astype(BF16)
    return lo, hi


def _gather_kernel(idx_ref, v_hbm, o_ref, buf, sem, *, cap, n_tok):
    b = pl.program_id(0)
    e = pl.program_id(1)
    base = (b * N_EXPERTS + e) * cap

    def row_copy(r, tok):
        return pltpu.make_async_copy(v_hbm.at[pl.ds(b * n_tok + tok, 1), :], buf.at[pl.ds(r, 1), :], sem)

    def issue(p, carry):
        for prio in range(2):
            r = 2 * p + prio
            row_copy(r, idx_ref[base + r]).start(priority=prio)
        return carry

    def drain(r, carry):
        row_copy(r, 0).wait()
        return carry

    lax.fori_loop(0, cap // 2, issue, 0)
    lax.fori_loop(0, cap, drain, 0)
    half = buf.shape[1]
    lo, hi = _unpack_bf16_pairs(buf[...])
    o_ref[0, 0, :, :half] = lo
    o_ref[0, 0, :, half:] = hi


def _gather(idx, v_pairs, cap):
    Bn, T, half = v_pairs.shape
    D = 2 * half
    return pl.pallas_call(
        functools.partial(_gather_kernel, cap=cap, n_tok=T),
        grid_spec=pltpu.PrefetchScalarGridSpec(
            num_scalar_prefetch=1,
            grid=(Bn, N_EXPERTS),
            in_specs=[pl.BlockSpec(memory_space=pl.ANY)],
            out_specs=pl.BlockSpec((1, 1, cap, D), lambda b, e, idx_ref: (e, b, 0, 0)),
            scratch_shapes=[pltpu.VMEM((cap, half), jnp.uint32), pltpu.SemaphoreType.DMA(())]),
        out_shape=jax.ShapeDtypeStruct((N_EXPERTS, Bn, cap, D), BF16),
        compiler_params=_cparams(("arbitrary", "arbitrary")),
        name="ec_gather",
    )(idx.reshape(-1), v_pairs.reshape(Bn * T, half))


def _ffn1_kernel(x_ref, wg_ref, wu_ref, o_ref):
    x = x_ref[0]
    g = jnp.dot(x, wg_ref[0].astype(BF16), preferred_element_type=F32)
    u = jnp.dot(x, wu_ref[0].astype(BF16), preferred_element_type=F32)
    o_ref[0] = (_silu(g) * u).astype(o_ref.dtype)


def _ffn1(xg, wg, wu):
    E, M, D = xg.shape
    F = wg.shape[2]
    tm = _pick(M, 1024)
    tn = _pick(F, 256)
    return pl.pallas_call(
        _ffn1_kernel,
        grid=(E, M // tm, F // tn),
        in_specs=[pl.BlockSpec((1, tm, D), lambda e, i, j: (e, i, 0)),
                  pl.BlockSpec((1, D, tn), lambda e, i, j: (e, 0, j)),
                  pl.BlockSpec((1, D, tn), lambda e, i, j: (e, 0, j))],
        out_specs=pl.BlockSpec((1, tm, tn), lambda e, i, j: (e, i, j)),
        out_shape=jax.ShapeDtypeStruct((E, M, F), BF16),
        compiler_params=_cparams(("parallel", "parallel", "arbitrary")),
        name="ec_ffn_up",
    )(xg, wg, wu)


def _ffn2_kernel(h_ref, wd_ref, o_ref):
    o_ref[0] = jnp.dot(h_ref[0], wd_ref[0].astype(BF16),
                       preferred_element_type=F32).astype(o_ref.dtype)


def _ffn2(hid, wd):
    E, M, F = hid.shape
    D = wd.shape[2]
    tn = _pick(D, 512)
    return pl.pallas_call(
        _ffn2_kernel,
        grid=(E, D // tn),
        in_specs=[pl.BlockSpec((1, M, F), lambda e, j: (e, 0, 0)),
                  pl.BlockSpec((1, F, tn), lambda e, j: (e, 0, j))],
        out_specs=pl.BlockSpec((1, M, tn), lambda e, j: (e, 0, j)),
        out_shape=jax.ShapeDtypeStruct((E, M, D), BF16),
        compiler_params=_cparams(("parallel", "arbitrary")),
        name="ec_ffn_down",
    )(hid, wd)


def _scatter_kernel(offs_ref, slot_ref, aff_ref, y_ref, h_ref, g_ref, gate_ref, o_ref, *, cap, win):
    b = pl.program_id(0)
    t = pl.program_id(1)
    e = pl.program_id(2)
    tq = o_ref.shape[1]

    @pl.when(e == 0)
    def _():
        o_ref[...] = jnp.zeros_like(o_ref)

    base = (b * N_EXPERTS + e) * LANE
    lo = offs_ref[base + t * (tq // LANE)]
    hi = offs_ref[base + (t + 1) * (tq // LANE)]
    sl = jnp.broadcast_to(slot_ref[0, 0], (LANE, tq)).T
    af = jnp.broadcast_to(aff_ref[0, 0], (LANE, tq)).T
    sl = jnp.concatenate([sl] * (win // LANE), axis=1)
    af = jnp.concatenate([af] * (win // LANE), axis=1)
    lane = lax.broadcasted_iota(jnp.int32, (tq, win), 1).astype(F32)
    for w0 in range(0, cap, win):
        @pl.when((lo < w0 + win) & (hi > w0))
        def _():
            ohw = jnp.where(sl == lane + float(w0), af, 0.0).astype(BF16)
            o_ref[0] += jnp.dot(ohw, y_ref[0, 0, w0:w0 + win, :], preferred_element_type=F32)

    @pl.when(e == pl.num_programs(2) - 1)
    def _():
        o_ref[0] = h_ref[0] + gate_ref[0] * _rms(o_ref[0], g_ref[...])


def _scatter_combine(offs, slot_rows, aff_rows, y, cap, h, g3, mod3):
    E, Bn, _, D = y.shape
    T = slot_rows.shape[3]
    tq = _pick(T, 512)
    win = _pick(cap, 256)
    assert win % LANE == 0 and tq % LANE == 0
    row = pl.BlockSpec((1, 1, 1, tq), lambda b, t, e, offs_ref: (b, e, 0, t))
    return pl.pallas_call(
        functools.partial(_scatter_kernel, cap=cap, win=win),
        grid_spec=pltpu.PrefetchScalarGridSpec(
            num_scalar_prefetch=1,
            grid=(Bn, T // tq, E),
            in_specs=[row, row,
                      pl.BlockSpec((1, 1, cap, D), lambda b, t, e, offs_ref: (e, b, 0, 0)),
                      pl.BlockSpec((1, tq, D), lambda b, t, e, offs_ref: (b, t, 0)),
                      pl.BlockSpec((1, D), lambda b, t, e, offs_ref: (0, 0)),
                      pl.BlockSpec((1, 1, D), lambda b, t, e, offs_ref: (b, 0, 5))],
            out_specs=pl.BlockSpec((1, tq, D), lambda b, t, e, offs_ref: (b, t, 0))),
        out_shape=jax.ShapeDtypeStruct((Bn, T, D), F32),
        compiler_params=_cparams(("parallel", "parallel", "arbitrary")),
        name="ec_scatter",
    )(offs.reshape(-1), slot_rows, aff_rows, y, h, g3.reshape(1, D), mod3)


def _layer(h_lat, ctx, c8, l, lb_l, w_ada, b_ada, g_norm, w_in, hgrn_f_bias, hgrn_norm,
           mlstm_conv_w, mlstm_conv_b, mlstm_gate_b, mlstm_norm, w_branch_a, w_branch_b, w_out,
           w_router, w_expert_gate, w_expert_up, w_expert_down):
    Bn, T, D = h_lat.shape
    n_lat = T // TOK
    n_tok = T + TOK
    A, Bh = A_HEADS, B_HEADS
    g0 = (5 * A + 6 * Bh) * LANE

    mod = _modulation(c8, w_ada[l], b_ada[l])
    mod3 = mod.reshape(8, 1, N_MOD * D)

    u = _prenorm(h_lat, ctx, g_norm[l, 0], mod3)

    w_in_t = jnp.swapaxes(w_in, 1, 2)
    w_merge_t, w_gate_t = _wsplit(w_in_t, l, g0, 4 * Bh, 2 * D)
    gate_bias = jnp.pad(mlstm_gate_b[l].reshape(4 * Bh), (0, LANE - 4 * Bh))
    u2 = u.reshape(Bn * n_tok, D)
    p3 = _matmul_t(u2, w_in_t, F32, n_cols=g0, layer=l, name="in_proj").reshape(Bn, n_tok, g0)
    p_merge = _matmul_t_rows(u, w_merge_t, T, F32, name="in_proj_merge")
    gates = _matmul_t(u2, w_gate_t, F32, tn=LANE, bias=gate_bias, name="in_proj_gates")

    g6 = gates[:, :4 * Bh].reshape(Bn, n_tok // CHUNK, CHUNK, 2, 2, Bh)
    g_rows = g6.transpose(0, 3, 5, 1, 4, 2)
    g_cols = g6.transpose(0, 3, 5, 1, 2, 4)
    conv4 = jnp.concatenate([mlstm_conv_w[l], mlstm_conv_b[l][None]], axis=0)
    conv4 = conv4.reshape(4, 1, 2 * Bh * B_DQK)

    oa = _hgrn_scan(p3, hgrn_f_bias[l], lb_l, n_lat)
    ob = _mlstm_scan(p3, g_rows, g_cols, conv4, n_lat)
    ya, yb = _readout(oa, ob, p3, hgrn_norm[l], mlstm_norm[l], n_lat)
    merged = _merge(ya, yb, w_branch_a[l].astype(BF16), w_branch_b[l].astype(BF16), p_merge, D)
    mix = _matmul(merged.reshape(Bn * T, D), w_out[l].astype(BF16), BF16, name="out_proj")

    w_router_pad = jnp.pad(w_router[l], ((0, 0), (0, LANE - N_EXPERTS)))
    h_lat, v_lat, aff = _resid_router(h_lat, mix.reshape(Bn, T, D), g_norm[l, 1], g_norm[l, 2],
                                      mod3, w_router_pad)

    cap = CAPACITY * T // N_EXPERTS
    aff_rows = aff[:, :, :N_EXPERTS].transpose(0, 2, 1)
    slot, idx, offs = _select(aff_rows, cap)
    xg = _gather(idx, v_lat, cap)
    hid = _ffn1(xg.reshape(N_EXPERTS, Bn * cap, D), w_expert_gate[l], w_expert_up[l])
    y = _ffn2(hid, w_expert_down[l]).reshape(N_EXPERTS, Bn, cap, D)
    return _scatter_combine(offs, slot.reshape(Bn, N_EXPERTS, 1, T),
                            aff_rows.reshape(Bn, N_EXPERTS, 1, T), y, cap, h_lat, g_norm[l, 3], mod3)


def kernel(x, c, ctx, c_ctx, w_ada, b_ada, g_norm, w_in, hgrn_f_bias, hgrn_lb, hgrn_norm,
           mlstm_conv_w, mlstm_conv_b, mlstm_gate_b, mlstm_norm, w_branch_a, w_branch_b, w_out,
           w_router, w_expert_gate, w_expert_up, w_expert_down):
    Bn, T, D = x.shape
    depth = w_ada.shape[0]
    assert depth == 1, "context outputs are only produced for the state hand-off (single layer)"
    assert ctx.shape[1] == TOK and T % TOK == 0 and Bn < 8
    lb_all = jnp.cumsum(jax.nn.softmax(hgrn_lb.astype(F32), axis=1), axis=1)
    c8 = jnp.zeros((8, D), F32).at[:Bn].set(c).at[Bn].set(c_ctx)
    h_lat = x
    for l in range(depth):
        h_lat = _layer(h_lat, ctx, c8, l, lb_all[:, l], w_ada, b_ada, g_norm, w_in, hgrn_f_bias,
                       hgrn_norm, mlstm_conv_w, mlstm_conv_b, mlstm_gate_b, mlstm_norm,
                       w_branch_a, w_branch_b, w_out, w_router, w_expert_gate, w_expert_up,
                       w_expert_down)
    return h_lat.astype(x.dtype)
```
